```python
import math
import jax, jax.numpy as jnp
from jax import lax
import numpy as np

D_MODEL = 2048
BATCH = 4
SEQ = 4096
DEPTH = 1

MIX_WIDTH = D_MODEL
BLOCK = 128
EPS = 1e-6

SWA_HEADS = 16
SWA_KV_HEADS = 2
SWA_HEAD_DIM = 64
SWA_GROUP = SWA_HEADS // SWA_KV_HEADS
WINDOW = 128

REL_BUCKETS = 32
REL_MAX_DIST = 128

MLA_HEADS = 8
MLA_Q_RANK = 384
MLA_KV_RANK = 128
MLA_NOPE_DIM = 128
MLA_ROPE_DIM = 64
MLA_V_DIM = 128
MLA_QK_DIM = MLA_NOPE_DIM + MLA_ROPE_DIM
ROPE_THETA = 10000.0

D_FF = 4 * D_MODEL

SWA_Q_COLS = SWA_HEADS * SWA_HEAD_DIM
SWA_KV_COLS = SWA_KV_HEADS * SWA_HEAD_DIM
OFF_SWA_Q = 0
OFF_SWA_K = OFF_SWA_Q + SWA_Q_COLS
OFF_SWA_V = OFF_SWA_K + SWA_KV_COLS
OFF_MLA_CQ = OFF_SWA_V + SWA_KV_COLS
OFF_MLA_CKV = OFF_MLA_CQ + MLA_Q_RANK
OFF_MLA_KR = OFF_MLA_CKV + MLA_KV_RANK
IN_COLS = OFF_MLA_KR + MLA_ROPE_DIM
SWA_OUT = SWA_HEADS * SWA_HEAD_DIM
MLA_OUT = MLA_HEADS * MLA_V_DIM

kernel_name = "hymba_swa_sink_mla_adaln_layer"


def rmsnorm(x, g):
    x32 = x.astype(jnp.float32)
    y = x32 * lax.rsqrt(jnp.mean(x32 * x32, axis=-1, keepdims=True) + EPS)
    return y.astype(x.dtype) * g


def t5_causal_bucket(dist):
    n = jnp.maximum(dist, 0)
    max_exact = REL_BUCKETS // 2
    is_small = n < max_exact
    nf = jnp.maximum(n, 1).astype(jnp.float32)
    large = max_exact + (jnp.log(nf / max_exact) / math.log(REL_MAX_DIST / max_exact)
                         * (REL_BUCKETS - max_exact)).astype(jnp.int32)
    large = jnp.minimum(large, REL_BUCKETS - 1)
    return jnp.where(is_small, n, large)


def rope(x, positions):
    half = x.shape[-1] // 2
    inv_freq = ROPE_THETA ** (-jnp.arange(half, dtype=jnp.float32) / half)
    ang = positions.astype(jnp.float32)[:, None] * inv_freq[None, :]
    extra = x.ndim - 3
    ang = ang.reshape((1, ang.shape[0]) + (1,) * extra + (half,))
    cos = jnp.cos(ang).astype(x.dtype)
    sin = jnp.sin(ang).astype(x.dtype)
    x1, x2 = x[..., :half], x[..., half:]
    return jnp.concatenate([x1 * cos - x2 * sin, x2 * cos + x1 * sin], axis=-1)


def swa_sink_attention(q, k, v, sinks, rel_bias):
    B, S = q.shape[0], q.shape[1]
    nb = S // BLOCK
    qb = q.reshape(B, nb, BLOCK, SWA_KV_HEADS, SWA_GROUP, SWA_HEAD_DIM)
    kb = k.reshape(B, nb, BLOCK, SWA_KV_HEADS, SWA_HEAD_DIM)
    vb = v.reshape(B, nb, BLOCK, SWA_KV_HEADS, SWA_HEAD_DIM)
    zero = jnp.zeros_like(kb[:, :1])
    k_band = jnp.concatenate([jnp.concatenate([zero, kb[:, :-1]], axis=1), kb], axis=2)
    v_band = jnp.concatenate([jnp.concatenate([zero, vb[:, :-1]], axis=1), vb], axis=2)
    s = jnp.einsum('bnqhgd,bnkhd->bnhgqk', qb, k_band).astype(jnp.float32) * (SWA_HEAD_DIM ** -0.5)

    q_loc = jnp.arange(BLOCK)[:, None]
    k_loc = jnp.arange(2 * BLOCK)[None, :]
    dist = q_loc + BLOCK - k_loc
    in_window = (dist >= 0) & (dist < WINDOW)
    blk = jnp.arange(nb)[:, None]
    key_valid = (blk * BLOCK - BLOCK + k_loc) >= 0
    mask = in_window[None] & key_valid[:, None, :]

    bias = rel_bias.astype(jnp.float32)[t5_causal_bucket(dist)]
    bias = bias.transpose(2, 0, 1).reshape(SWA_KV_HEADS, SWA_GROUP, BLOCK, 2 * BLOCK)
    s = s + bias[None, None]
    s = jnp.where(mask[None, :, None, None], s, -jnp.inf)

    sink = sinks.astype(jnp.float32).reshape(SWA_KV_HEADS, SWA_GROUP)[None, None, :, :, None, None]
    m = jnp.maximum(jnp.max(s, axis=-1, keepdims=True), sink)
    p = jnp.exp(s - m)
    p = p / (jnp.sum(p, axis=-1, keepdims=True) + jnp.exp(sink - m))
    o = jnp.einsum('bnhgqk,bnkhd->bnqhgd', p.astype(v.dtype), v_band)
    return o.reshape(B, S, SWA_OUT)


def mla_attention(q_nope, q_rope, k_nope, k_rope, v):
    B, S = q_nope.shape[0], q_nope.shape[1]
    nb = S // BLOCK
    qn = q_nope.reshape(B, nb, BLOCK, MLA_HEADS, MLA_NOPE_DIM).transpose(1, 0, 2, 3, 4)
    qr = q_rope.reshape(B, nb, BLOCK, MLA_HEADS, MLA_ROPE_DIM).transpose(1, 0, 2, 3, 4)
    key_pos = jnp.arange(S)
    scale = MLA_QK_DIM ** -0.5

    def one_block(args):
        i, qn_i, qr_i = args
        s = (jnp.einsum('bqhd,bkhd->bhqk', qn_i, k_nope)
             + jnp.einsum('bqhd,bkd->bhqk', qr_i, k_rope)).astype(jnp.float32) * scale
        q_pos = i * BLOCK + jnp.arange(BLOCK)
        causal = key_pos[None, :] <= q_pos[:, None]
        s = jnp.where(causal[None, None], s, -jnp.inf)
        p = jax.nn.softmax(s, axis=-1)
        return jnp.einsum('bhqk,bkhd->bqhd', p.astype(v.dtype), v)

    o = lax.map(one_block, (jnp.arange(nb), qn, qr))
    return o.transpose(1, 0, 2, 3, 4).reshape(B, S, MLA_OUT)


def setup_inputs(seed: int = 0) -> dict:
    key = jax.random.key(seed)
    ks = jax.random.split(key, 20)
    f32 = jnp.float32
    nrm = lambda k, shape, s: jax.random.normal(k, shape, f32) * s
    return {
        "x": nrm(ks[0], (BATCH, SEQ, D_MODEL), 1.0),
        "c": nrm(ks[1], (BATCH, D_MODEL), 1.0),
        "w_mod": nrm(ks[2], (DEPTH, D_MODEL, 6 * D_MODEL), 0.5 * D_MODEL ** -0.5),
        "b_mod": nrm(ks[3], (DEPTH, 6 * D_MODEL), 0.01),
        "attn_norm_g": 1.0 + nrm(ks[4], (DEPTH, D_MODEL), 0.02),
        "w_in": nrm(ks[5], (DEPTH, D_MODEL, IN_COLS), D_MODEL ** -0.5),
        "swa_sinks": nrm(ks[6], (DEPTH, SWA_HEADS), 1.0),
        "rel_bias": nrm(ks[7], (REL_BUCKETS, SWA_HEADS), 0.5),
        "mla_q_norm_g": 1.0 + nrm(ks[8], (DEPTH, MLA_Q_RANK), 0.02),
        "w_uq": nrm(ks[9], (DEPTH, MLA_Q_RANK, MLA_HEADS * MLA_QK_DIM), MLA_Q_RANK ** -0.5),
        "mla_kv_norm_g": 1.0 + nrm(ks[10], (DEPTH, MLA_KV_RANK), 0.02),
        "w_ukv": nrm(ks[11], (DEPTH, MLA_KV_RANK, MLA_HEADS * (MLA_NOPE_DIM + MLA_V_DIM)), MLA_KV_RANK ** -0.5),
        "w_out": nrm(ks[12], (DEPTH, MIX_WIDTH, D_MODEL), MIX_WIDTH ** -0.5),
        "mlp_norm_g": 1.0 + nrm(ks[13], (DEPTH, D_MODEL), 0.02),
        "w_ff1": nrm(ks[14], (DEPTH, D_MODEL, D_FF), D_MODEL ** -0.5),
        "w_ff2": nrm(ks[15], (DEPTH, D_FF, D_MODEL), D_FF ** -0.5),
        "final_norm_g": 1.0 + nrm(ks[16], (D_MODEL,), 0.02),
    }


def reference(x, c, w_mod, b_mod, attn_norm_g, w_in, swa_sinks, rel_bias, mla_q_norm_g, w_uq,
              mla_kv_norm_g, w_ukv, w_out, mlp_norm_g, w_ff1, w_ff2, final_norm_g):
    B, S, _ = x.shape
    positions = jnp.arange(S)
    c_act = jax.nn.silu(c)
    for l in range(DEPTH):
        mod = c_act @ w_mod[l] + b_mod[l]
        sh1, sc1, g1, sh2, sc2, g2 = [m[:, None, :] for m in jnp.split(mod, 6, axis=-1)]

        h = rmsnorm(x, attn_norm_g[l]) * (1.0 + sc1) + sh1
        proj = jnp.einsum('bsd,df->bsf', h, w_in[l])

        q_a = proj[..., OFF_SWA_Q:OFF_SWA_K].reshape(B, S, SWA_HEADS, SWA_HEAD_DIM)
        k_a = proj[..., OFF_SWA_K:OFF_SWA_V].reshape(B, S, SWA_KV_HEADS, SWA_HEAD_DIM)
        v_a = proj[..., OFF_SWA_V:OFF_MLA_CQ].reshape(B, S, SWA_KV_HEADS, SWA_HEAD_DIM)
        o_a = swa_sink_attention(q_a, k_a, v_a, swa_sinks[l], rel_bias)

        c_q = rmsnorm(proj[..., OFF_MLA_CQ:OFF_MLA_CKV], mla_q_norm_g[l])
        c_kv = rmsnorm(proj[..., OFF_MLA_CKV:OFF_MLA_KR], mla_kv_norm_g[l])
        k_rope = rope(proj[..., OFF_MLA_KR:IN_COLS], positions)
        q_b = jnp.einsum('bsr,rf->bsf', c_q, w_uq[l]).reshape(B, S, MLA_HEADS, MLA_QK_DIM)
        q_nope = q_b[..., :MLA_NOPE_DIM]
        q_rope = rope(q_b[..., MLA_NOPE_DIM:], positions)
        kv_b = jnp.einsum('bsr,rf->bsf', c_kv, w_ukv[l]).reshape(B, S, MLA_HEADS, MLA_NOPE_DIM + MLA_V_DIM)
        k_nope = kv_b[..., :MLA_NOPE_DIM]
        v_b = kv_b[..., MLA_NOPE_DIM:]
        o_b = mla_attention(q_nope, q_rope, k_nope, k_rope, v_b)

        mix = jnp.concatenate([o_a, o_b], axis=-1)
        x = x + g1 * jnp.einsum('bsm,md->bsd', mix, w_out[l])

        h = rmsnorm(x, mlp_norm_g[l]) * (1.0 + sc2) + sh2
        u = jax.nn.relu(jnp.einsum('bsd,df->bsf', h, w_ff1[l]))
        x = x + g2 * jnp.einsum('bsf,fd->bsd', u * u, w_ff2[l])
    return rmsnorm(x, final_norm_g)
```

```python
import functools
import math

import jax
import jax.numpy as jnp
from jax import lax
from jax.experimental import pallas as pl
from jax.experimental.pallas import tpu as pltpu

F32 = jnp.float32
BF16 = jnp.bfloat16

D_MODEL = 2048
BLOCK = 128
EPS = 1e-6

SWA_HEADS = 16
SWA_KV_HEADS = 2
SWA_HEAD_DIM = 64
SWA_GROUP = SWA_HEADS // SWA_KV_HEADS
WINDOW = 128
REL_BUCKETS = 32
REL_MAX_DIST = 128

MLA_HEADS = 8
MLA_Q_RANK = 384
MLA_KV_RANK = 128
MLA_NOPE_DIM = 128
MLA_ROPE_DIM = 64
MLA_V_DIM = 128
MLA_QK_DIM = MLA_NOPE_DIM + MLA_ROPE_DIM
MLA_QK_PAD = 256
ROPE_THETA = 10000.0
D_FF = 4 * D_MODEL

SWA_Q_COLS = SWA_HEADS * SWA_HEAD_DIM
SWA_KV_COLS = SWA_KV_HEADS * SWA_HEAD_DIM
OFF_SWA_K = SWA_Q_COLS
OFF_SWA_V = OFF_SWA_K + SWA_KV_COLS
OFF_MLA_CQ = OFF_SWA_V + SWA_KV_COLS
OFF_MLA_CKV = OFF_MLA_CQ + MLA_Q_RANK
OFF_MLA_KR = OFF_MLA_CKV + MLA_KV_RANK
OFF_MLA_KR_ROT = OFF_MLA_KR + 128
IN_COLS_EXT = OFF_MLA_KR_ROT + 128

VMEM_LIMIT_BYTES = 56 * 1024 * 1024

TM_PROJ = 512
TQ_MLA = 512
TM_OUT = 512
TM_MLP = 512
TF_MLP = 1024
TN_MOD = 1536


def _params(sem):
    return pltpu.CompilerParams(dimension_semantics=sem, vmem_limit_bytes=VMEM_LIMIT_BYTES)


def _const_spec(shape):
    nd = len(shape)
    return pl.BlockSpec(shape, lambda *_: (0,) * nd, pipeline_mode=pl.Buffered(1))


def _rms(x):
    return x * lax.rsqrt(jnp.mean(x * x, axis=-1, keepdims=True) + EPS)


def _mod_kernel(c_ref, w_ref, b_ref, o_ref):
    c = c_ref[...]
    ca = c * (1.0 / (1.0 + jnp.exp(-c)))
    o_ref[...] = jnp.dot(ca.astype(BF16), w_ref[...].astype(BF16),
                         preferred_element_type=F32) + b_ref[...]


def _mod_call(c8, w_mod, b_mod):
    n = w_mod.shape[1]
    return pl.pallas_call(
        _mod_kernel,
        out_shape=jax.ShapeDtypeStruct((8, n), F32),
        grid=(n // TN_MOD,),
        in_specs=[
            pl.BlockSpec((8, D_MODEL), lambda j: (0, 0)),
            pl.BlockSpec((D_MODEL, TN_MOD), lambda j: (0, j)),
            pl.BlockSpec((1, TN_MOD), lambda j: (0, j)),
        ],
        out_specs=pl.BlockSpec((8, TN_MOD), lambda j: (0, j)),
        compiler_params=_params(("arbitrary",)),
        name="mod",
    )(c8, w_mod, b_mod)


def _bias_kernel(rel_ref, bucket_ref, o_ref):
    h = pl.program_id(0)
    bucket = bucket_ref[...]
    acc = jnp.zeros(bucket.shape, F32)
    for k in range(REL_BUCKETS):
        acc = jnp.where(bucket == k, rel_ref[k, h], acc)
    q_loc = lax.broadcasted_iota(jnp.int32, bucket.shape, 0)
    k_loc = lax.broadcasted_iota(jnp.int32, bucket.shape, 1)
    dist = q_loc + BLOCK - k_loc
    o_ref[0] = jnp.where((dist >= 0) & (dist < WINDOW), acc, -jnp.inf)


def _bias_call(rel_bias, bucket):
    return pl.pallas_call(
        _bias_kernel,
        out_shape=jax.ShapeDtypeStruct((SWA_HEADS, BLOCK, 2 * BLOCK), F32),
        grid=(SWA_HEADS,),
        in_specs=[
            pl.BlockSpec(memory_space=pltpu.SMEM),
            pl.BlockSpec((BLOCK, 2 * BLOCK), lambda h: (0, 0)),
        ],
        out_specs=pl.BlockSpec((1, BLOCK, 2 * BLOCK), lambda h: (h, 0, 0)),
        compiler_params=_params(("arbitrary",)),
        name="t5_bias",
    )(rel_bias, bucket)


def _proj_kernel(x_ref, mod_ref, g_ref, win_ref, gq_ref, gkv_ref, wqm_ref, wqr_ref,
                 wuk_ref, wvt_ref, cosq_ref, sinq_ref, cosk_ref, sink_ref,
                 qa_ref, ka_ref, va_ref, qb_ref, kc_ref, vt_ref, *, q_scale):
    x = x_ref[...]
    mod = mod_ref[0]
    sh1 = mod[0:1]
    sc1 = mod[1:2]
    h = (_rms(x) * g_ref[...]) * (1.0 + sc1) + sh1
    proj = jnp.dot(h.astype(BF16), win_ref[...], preferred_element_type=F32)

    qa_ref[...] = (proj[:, :SWA_Q_COLS] * (SWA_HEAD_DIM ** -0.5)).astype(BF16)
    ka_ref[...] = proj[:, OFF_SWA_K:OFF_SWA_V].astype(BF16)
    va_ref[...] = proj[:, OFF_SWA_V:OFF_MLA_CQ].astype(BF16)

    cq = (_rms(proj[:, OFF_MLA_CQ:OFF_MLA_CKV]) * gq_ref[...]).astype(BF16)
    ckv = (_rms(proj[:, OFF_MLA_CKV:OFF_MLA_KR]) * gkv_ref[...]).astype(BF16)
    krope = (proj[:, OFF_MLA_KR:OFF_MLA_KR_ROT] * cosk_ref[...]
             + proj[:, OFF_MLA_KR_ROT:IN_COLS_EXT] * sink_ref[...]).astype(BF16)

    qmain = jnp.dot(cq, wqm_ref[...], preferred_element_type=F32)
    qrot = jnp.dot(cq, wqr_ref[...], preferred_element_type=F32)
    knope = jnp.dot(ckv, wuk_ref[...], preferred_element_type=F32)
    vt = lax.dot_general(wvt_ref[...], ckv, (((1,), (1,)), ((), ())),
                         preferred_element_type=F32)

    cosq = cosq_ref[...]
    sinq = sinq_ref[...]
    for hd in range(MLA_HEADS):
        lo = hd * MLA_QK_PAD
        mid = lo + MLA_NOPE_DIM
        hi = lo + MLA_QK_PAD
        qb_ref[:, lo:mid] = (qmain[:, lo:mid] * q_scale).astype(BF16)
        qb_ref[:, mid:hi] = (qmain[:, mid:hi] * cosq
                             + qrot[:, hd * 128:(hd + 1) * 128] * sinq).astype(BF16)
        kc_ref[:, lo:mid] = knope[:, hd * MLA_NOPE_DIM:(hd + 1) * MLA_NOPE_DIM].astype(BF16)
        kc_ref[:, mid:hi] = krope
        vt_ref[0, hd, 0] = vt[hd * MLA_V_DIM:(hd + 1) * MLA_V_DIM].astype(BF16)


def _proj_call(x2d, mod3, g_attn, w_in_ext, gq, gkv, wq_main, wq_rot, w_uk, w_vt,
               cosq, sinq, cosk, sink, *, batch, seq):
    tm = TM_PROJ
    tiles_per_seq = seq // tm
    t = batch * seq
    tok = lambda i: (i, 0)
    pos = lambda i: (i % tiles_per_seq, 0)
    out_shape = (
        jax.ShapeDtypeStruct((t, SWA_Q_COLS), BF16),
        jax.ShapeDtypeStruct((t, SWA_KV_COLS), BF16),
        jax.ShapeDtypeStruct((t, SWA_KV_COLS), BF16),
        jax.ShapeDtypeStruct((t, MLA_HEADS * MLA_QK_PAD), BF16),
        jax.ShapeDtypeStruct((t, MLA_HEADS * MLA_QK_PAD), BF16),
        jax.ShapeDtypeStruct((batch, MLA_HEADS, tiles_per_seq, MLA_V_DIM, tm), BF16),
    )
    return pl.pallas_call(
        functools.partial(_proj_kernel, q_scale=MLA_QK_DIM ** -0.5),
        out_shape=out_shape,
        grid=(t // tm,),
        in_specs=[
            pl.BlockSpec((tm, D_MODEL), tok),
            pl.BlockSpec((1, 6, D_MODEL), lambda i: (i // tiles_per_seq, 0, 0)),
            _const_spec((1, D_MODEL)),
            _const_spec((D_MODEL, IN_COLS_EXT)),
            _const_spec((1, MLA_Q_RANK)),
            _const_spec((1, MLA_KV_RANK)),
            _const_spec((MLA_Q_RANK, MLA_HEADS * MLA_QK_PAD)),
            _const_spec((MLA_Q_RANK, MLA_HEADS * 128)),
            _const_spec((MLA_KV_RANK, MLA_HEADS * MLA_NOPE_DIM)),
            _const_spec((MLA_HEADS * MLA_V_DIM, MLA_KV_RANK)),
            pl.BlockSpec((tm, 128), pos),
            pl.BlockSpec((tm, 128), pos),
            pl.BlockSpec((tm, 128), pos),
            pl.BlockSpec((tm, 128), pos),
        ],
        out_specs=(
            pl.BlockSpec((tm, SWA_Q_COLS), tok),
            pl.BlockSpec((tm, SWA_KV_COLS), tok),
            pl.BlockSpec((tm, SWA_KV_COLS), tok),
            pl.BlockSpec((tm, MLA_HEADS * MLA_QK_PAD), tok),
            pl.BlockSpec((tm, MLA_HEADS * MLA_QK_PAD), tok),
            pl.BlockSpec((1, MLA_HEADS, 1, MLA_V_DIM, tm),
                         lambda i: (i // tiles_per_seq, 0, i % tiles_per_seq, 0, 0)),
        ),
        compiler_params=_params(("parallel",)),
        name="in_proj",
    )(x2d, mod3, g_attn, w_in_ext, gq, gkv, wq_main, wq_rot, w_uk, w_vt,
      cosq, sinq, cosk, sink)


def _swa_kernel(sinks_ref, q_ref, kp_ref, kc_ref, vp_ref, vc_ref, bias_ref, o_ref):
    n = pl.program_id(1)
    first = jnp.where(n == 0, -jnp.inf, 0.0).astype(F32)
    lane = lax.broadcasted_iota(jnp.int32, (BLOCK, 2 * BLOCK), 1)
    prev_mask = jnp.where(lane < BLOCK, first, 0.0)
    kband = jnp.concatenate([kp_ref[...], kc_ref[...]], axis=0)
    vband = jnp.concatenate([vp_ref[...], vc_ref[...]], axis=0)
    q = q_ref[...]
    outs = []
    for hd in range(SWA_HEADS):
        g = hd // SWA_GROUP
        qh = q[:, hd * SWA_HEAD_DIM:(hd + 1) * SWA_HEAD_DIM]
        kg = kband[:, g * SWA_HEAD_DIM:(g + 1) * SWA_HEAD_DIM]
        vg = vband[:, g * SWA_HEAD_DIM:(g + 1) * SWA_HEAD_DIM]
        s = lax.dot_general(qh, kg, (((1,), (1,)), ((), ())), preferred_element_type=F32)
        s = s + bias_ref[hd] + prev_mask
        sink = sinks_ref[hd]
        m = jnp.maximum(jnp.max(s, axis=-1, keepdims=True), sink)
        p = jnp.exp(s - m)
        denom = jnp.sum(p, axis=-1, keepdims=True) + jnp.exp(sink - m)
        o = jnp.dot(p.astype(BF16), vg, preferred_element_type=F32)
        outs.append(o / denom)
    o_ref[...] = jnp.concatenate(outs, axis=1).astype(BF16)


def _swa_call(sinks, qa, ka, va, bias_tab, *, batch, seq):
    nb = seq // BLOCK
    cur = lambda b, n: (b * nb + n, 0)
    prev = lambda b, n: (b * nb + jnp.maximum(n - 1, 0), 0)
    return pl.pallas_call(
        _swa_kernel,
        out_shape=jax.ShapeDtypeStruct((batch * seq, SWA_Q_COLS), BF16),
        grid=(batch, nb),
        in_specs=[
            pl.BlockSpec(memory_space=pltpu.SMEM),
            pl.BlockSpec((BLOCK, SWA_Q_COLS), cur),
            pl.BlockSpec((BLOCK, SWA_KV_COLS), prev),
            pl.BlockSpec((BLOCK, SWA_KV_COLS), cur),
            pl.BlockSpec((BLOCK, SWA_KV_COLS), prev),
            pl.BlockSpec((BLOCK, SWA_KV_COLS), cur),
            _const_spec((SWA_HEADS, BLOCK, 2 * BLOCK)),
        ],
        out_specs=pl.BlockSpec((BLOCK, SWA_Q_COLS), cur),
        compiler_params=_params(("parallel", "arbitrary")),
        name="swa",
    )(sinks, qa, ka, ka, va, va, bias_tab)


def _mla_kernel(q_ref, k_ref, vt_ref, o_ref, m_ref, l_ref, acc_ref, *, tk):
    qi = pl.program_id(2)
    q = q_ref[...]
    m_ref[...] = jnp.full(m_ref.shape, -jnp.inf, F32)
    l_ref[...] = jnp.zeros(l_ref.shape, F32)
    acc_ref[...] = jnp.zeros(acc_ref.shape, F32)

    def step(j, masked):
        k = k_ref[pl.ds(pl.multiple_of(j * tk, tk), tk), :]
        st = lax.dot_general(k, q, (((1,), (1,)), ((), ())), preferred_element_type=F32)
        if masked:
            kk = lax.broadcasted_iota(jnp.int32, st.shape, 0)
            qq = lax.broadcasted_iota(jnp.int32, st.shape, 1)
            st = jnp.where(kk <= qq, st, -jnp.inf)
        m_old = m_ref[...]
        m_new = jnp.maximum(m_old, jnp.max(st, axis=0, keepdims=True))
        alpha = jnp.exp(m_old - m_new)
        p = jnp.exp(st - m_new)
        l_ref[...] = alpha * l_ref[...] + jnp.sum(p, axis=0, keepdims=True)
        pv = jnp.dot(vt_ref[0, 0, j], p.astype(BF16), preferred_element_type=F32)
        acc_ref[...] = alpha * acc_ref[...] + pv
        m_ref[...] = m_new

    def body(j, carry):
        step(j, False)
        return carry

    lax.fori_loop(0, qi, body, 0)
    step(qi, True)
    o_ref[...] = (acc_ref[...] / l_ref[...]).T.astype(BF16)


def _mla_call(qb, kcat, vt, *, batch, seq):
    tq = TQ_MLA
    tk = TM_PROJ
    assert tq == tk
    nq = seq // tq
    return pl.pallas_call(
        functools.partial(_mla_kernel, tk=tk),
        out_shape=jax.ShapeDtypeStruct((batch * seq, MLA_HEADS * MLA_V_DIM), BF16),
        grid=(batch, MLA_HEADS, nq),
        in_specs=[
            pl.BlockSpec((tq, MLA_QK_PAD), lambda b, h, i: (b * nq + i, h)),
            pl.BlockSpec((seq, MLA_QK_PAD), lambda b, h, i: (b, h)),
            pl.BlockSpec((1, 1, seq // tk, MLA_V_DIM, tk), lambda b, h, i: (b, h, 0, 0, 0)),
        ],
        out_specs=pl.BlockSpec((tq, MLA_V_DIM), lambda b, h, i: (b * nq + i, h)),
        scratch_shapes=[
            pltpu.VMEM((1, tq), F32),
            pltpu.VMEM((1, tq), F32),
            pltpu.VMEM((MLA_V_DIM, tq), F32),
        ],
        compiler_params=_params(("parallel", "parallel", "arbitrary")),
        name="mla",
    )(qb, kcat, vt)


def _out_kernel(x_ref, oa_ref, ob_ref, mod_ref, wa_ref, wb_ref, o_ref):
    g1 = mod_ref[0][2:3]
    y = (jnp.dot(oa_ref[...], wa_ref[...], preferred_element_type=F32)
         + jnp.dot(ob_ref[...], wb_ref[...], preferred_element_type=F32))
    o_ref[...] = x_ref[...] + g1 * y


def _out_call(x2d, oa, ob, mod3, w_out_a, w_out_b, *, seq):
    tm = TM_OUT
    t = x2d.shape[0]
    tiles_per_seq = seq // tm
    tok = lambda i: (i, 0)
    return pl.pallas_call(
        _out_kernel,
        out_shape=jax.ShapeDtypeStruct((t, D_MODEL), F32),
        grid=(t // tm,),
        in_specs=[
            pl.BlockSpec((tm, D_MODEL), tok),
            pl.BlockSpec((tm, SWA_Q_COLS), tok),
            pl.BlockSpec((tm, MLA_HEADS * MLA_V_DIM), tok),
            pl.BlockSpec((1, 6, D_MODEL), lambda i: (i // tiles_per_seq, 0, 0)),
            _const_spec((SWA_Q_COLS, D_MODEL)),
            _const_spec((MLA_HEADS * MLA_V_DIM, D_MODEL)),
        ],
        out_specs=pl.BlockSpec((tm, D_MODEL), tok),
        compiler_params=_params(("parallel",)),
        name="out_proj",
    )(x2d, oa, ob, mod3, w_out_a, w_out_b)


def _mlp_kernel(x_ref, mod_ref, g_ref, gf_ref, w1_ref, w2_ref, o_ref, h_ref):
    j = pl.program_id(1)
    mod = mod_ref[0]

    @pl.when(j == 0)
    def _():
        sh2 = mod[3:4]
        sc2 = mod[4:5]
        h = (_rms(x_ref[...]) * g_ref[...]) * (1.0 + sc2) + sh2
        h_ref[...] = h.astype(BF16)

    u = jnp.maximum(jnp.dot(h_ref[...], w1_ref[...], preferred_element_type=F32), 0.0)
    y = jnp.dot((u * u).astype(BF16), w2_ref[...], preferred_element_type=F32)

    @pl.when(j == 0)
    def _():
        o_ref[...] = y

    @pl.when(j > 0)
    def _():
        o_ref[...] += y

    @pl.when(j == pl.num_programs(1) - 1)
    def _():
        g2 = mod[5:6]
        x2 = x_ref[...] + g2 * o_ref[...]
        o_ref[...] = _rms(x2) * gf_ref[...]


def _mlp_call(x1, mod3, g_mlp, g_final, w1, w2, *, seq):
    tm = TM_MLP
    tf = TF_MLP
    t = x1.shape[0]
    tiles_per_seq = seq // tm
    return pl.pallas_call(
        _mlp_kernel,
        out_shape=jax.ShapeDtypeStruct((t, D_MODEL), F32),
        grid=(t // tm, D_FF // tf),
        in_specs=[
            pl.BlockSpec((tm, D_MODEL), lambda i, j: (i, 0)),
            pl.BlockSpec((1, 6, D_MODEL), lambda i, j: (i // tiles_per_seq, 0, 0)),
            _const_spec((1, D_MODEL)),
            _const_spec((1, D_MODEL)),
            pl.BlockSpec((D_MODEL, tf), lambda i, j: (0, j)),
            pl.BlockSpec((tf, D_MODEL), lambda i, j: (j, 0)),
        ],
        out_specs=pl.BlockSpec((tm, D_MODEL), lambda i, j: (i, 0)),
        scratch_shapes=[pltpu.VMEM((tm, D_MODEL), BF16)],
        compiler_params=_params(("parallel", "arbitrary")),
        name="mlp",
    )(x1, mod3, g_mlp, g_final, w1, w2)


def _t5_bucket_table():
    q_loc = jnp.arange(BLOCK)[:, None]
    k_loc = jnp.arange(2 * BLOCK)[None, :]
    n = jnp.maximum(q_loc + BLOCK - k_loc, 0)
    max_exact = REL_BUCKETS // 2
    nf = jnp.maximum(n, 1).astype(F32)
    large = max_exact + (jnp.log(nf / max_exact) / math.log(REL_MAX_DIST / max_exact)
                         * (REL_BUCKETS - max_exact)).astype(jnp.int32)
    large = jnp.minimum(large, REL_BUCKETS - 1)
    return jnp.where(n < max_exact, n, large).astype(jnp.int32)


def _rope_tables(seq, q_scale):
    half = MLA_ROPE_DIM // 2
    inv_freq = ROPE_THETA ** (-jnp.arange(half, dtype=F32) / half)
    ang = jnp.arange(seq).astype(F32)[:, None] * inv_freq[None, :]
    zeros = jnp.zeros((seq, 128 - MLA_ROPE_DIM), F32)
    cos = jnp.cos(ang)
    sin = jnp.sin(ang)
    cosk = jnp.concatenate([cos, cos, zeros], axis=1)
    sink = jnp.concatenate([sin, sin, zeros], axis=1)
    return cosk * q_scale, sink * q_scale, cosk, sink


def _rot_cols(w):
    half = w.shape[-1] // 2
    return jnp.concatenate([-w[..., half:], w[..., :half]], axis=-1)


def kernel(x, c, w_mod, b_mod, attn_norm_g, w_in, swa_sinks, rel_bias, mla_q_norm_g, w_uq,
           mla_kv_norm_g, w_ukv, w_out, mlp_norm_g, w_ff1, w_ff2, final_norm_g):
    batch, seq, _ = x.shape
    depth = w_mod.shape[0]
    assert depth == 1
    t = batch * seq
    x2d = x.reshape(t, D_MODEL)
    l = 0

    w_kr = w_in[l][:, OFF_MLA_KR:OFF_MLA_KR + MLA_ROPE_DIM]
    zpad = jnp.zeros((D_MODEL, 128 - MLA_ROPE_DIM), F32)
    w_in_ext = jnp.concatenate(
        [w_in[l][:, :OFF_MLA_KR], w_kr, zpad, _rot_cols(w_kr), zpad], axis=1).astype(BF16)

    wq = w_uq[l].reshape(MLA_Q_RANK, MLA_HEADS, MLA_QK_DIM)
    wq_nope = wq[..., :MLA_NOPE_DIM]
    wq_rope = wq[..., MLA_NOPE_DIM:]
    zq = jnp.zeros((MLA_Q_RANK, MLA_HEADS, 128 - MLA_ROPE_DIM), F32)
    wq_main = jnp.concatenate([wq_nope, wq_rope, zq], axis=-1).reshape(
        MLA_Q_RANK, MLA_HEADS * MLA_QK_PAD).astype(BF16)
    wq_rot = jnp.concatenate([_rot_cols(wq_rope), zq], axis=-1).reshape(
        MLA_Q_RANK, MLA_HEADS * 128).astype(BF16)

    wkv = w_ukv[l].reshape(MLA_KV_RANK, MLA_HEADS, MLA_NOPE_DIM + MLA_V_DIM)
    w_uk = wkv[..., :MLA_NOPE_DIM].reshape(MLA_KV_RANK, MLA_HEADS * MLA_NOPE_DIM).astype(BF16)
    w_vt = wkv[..., MLA_NOPE_DIM:].reshape(MLA_KV_RANK, MLA_HEADS * MLA_V_DIM).T.astype(BF16)

    w_out_a = w_out[l][:SWA_Q_COLS].astype(BF16)
    w_out_b = w_out[l][SWA_Q_COLS:].astype(BF16)
    w1 = w_ff1[l].astype(BF16)
    w2 = w_ff2[l].astype(BF16)

    cosq, sinq, cosk, sink = _rope_tables(seq, MLA_QK_DIM ** -0.5)

    c8 = jnp.pad(c, ((0, 8 - batch), (0, 0)))
    mod = _mod_call(c8, w_mod[l], b_mod[l].reshape(1, -1))[:batch]
    mod3 = mod.reshape(batch, 6, D_MODEL)

    bias_tab = _bias_call(rel_bias, _t5_bucket_table())

    qa, ka, va, qb, kcat, vt = _proj_call(
        x2d, mod3, attn_norm_g[l].reshape(1, -1), w_in_ext,
        mla_q_norm_g[l].reshape(1, -1), mla_kv_norm_g[l].reshape(1, -1),
        wq_main, wq_rot, w_uk, w_vt, cosq, sinq, cosk, sink, batch=batch, seq=seq)

    oa = _swa_call(swa_sinks[l], qa, ka, va, bias_tab, batch=batch, seq=seq)
    ob = _mla_call(qb, kcat, vt, batch=batch, seq=seq)

    x1 = _out_call(x2d, oa, ob, mod3, w_out_a, w_out_b, seq=seq)
    out = _mlp_call(x1, mod3, mlp_norm_g[l].reshape(1, -1), final_norm_g.reshape(1, -1),
                    w1, w2, seq=seq)
    return out.reshape(batch, seq, D_MODEL)
```

```python
import functools
import math

import jax
import jax.numpy as jnp
from jax import lax
from jax.experimental import pallas as pl
from jax.experimental.pallas import tpu as pltpu

F32 = jnp.float32
BF16 = jnp.bfloat16

D_MODEL = 2048
BLOCK = 128
EPS = 1e-6

SWA_HEADS = 16
SWA_KV_HEADS = 2
SWA_HEAD_DIM = 64
SWA_GROUP = SWA_HEADS // SWA_KV_HEADS
WINDOW = 128
REL_BUCKETS = 32
REL_MAX_DIST = 128

MLA_HEADS = 8
MLA_Q_RANK = 384
MLA_KV_RANK = 128
MLA_NOPE_DIM = 128
MLA_ROPE_DIM = 64
MLA_V_DIM = 128
MLA_QK_DIM = MLA_NOPE_DIM + MLA_ROPE_DIM
MLA_QK_PAD = 256
ROPE_THETA = 10000.0
MLA_Q_SCALE = MLA_QK_DIM ** -0.5 * math.log2(math.e)
D_FF = 4 * D_MODEL

SWA_Q_COLS = SWA_HEADS * SWA_HEAD_DIM
SWA_KV_COLS = SWA_KV_HEADS * SWA_HEAD_DIM
OFF_SWA_K = SWA_Q_COLS
OFF_SWA_V = OFF_SWA_K + SWA_KV_COLS
OFF_MLA_CQ = OFF_SWA_V + SWA_KV_COLS
OFF_MLA_CKV = OFF_MLA_CQ + MLA_Q_RANK
OFF_MLA_KR = OFF_MLA_CKV + MLA_KV_RANK
OFF_MLA_KR_ROT = OFF_MLA_KR + 128
IN_COLS_EXT = OFF_MLA_KR_ROT + 128

VMEM_LIMIT_BYTES = 56 * 1024 * 1024

TM_PROJ = 512
TM_OUT = 512
TM_MLP = 512
TF_MLP = 1024
TN_MOD = 1536


def _params(sem):
    return pltpu.CompilerParams(dimension_semantics=sem, vmem_limit_bytes=VMEM_LIMIT_BYTES)


def _const_spec(shape):
    nd = len(shape)
    return pl.BlockSpec(shape, lambda *_: (0,) * nd, pipeline_mode=pl.Buffered(1))


def _rms(x):
    return x * lax.rsqrt(jnp.mean(x * x, axis=-1, keepdims=True) + EPS)


def _mod_kernel(c_ref, w_ref, b_ref, o_ref):
    c = c_ref[...]
    ca = c * (1.0 / (1.0 + jnp.exp(-c)))
    o_ref[...] = jnp.dot(ca.astype(BF16), w_ref[...].astype(BF16),
                         preferred_element_type=F32) + b_ref[...]


def _mod_call(c8, w_mod, b_mod):
    n = w_mod.shape[1]
    return pl.pallas_call(
        _mod_kernel,
        out_shape=jax.ShapeDtypeStruct((8, n), F32),
        grid=(n // TN_MOD,),
        in_specs=[
            pl.BlockSpec((8, D_MODEL), lambda j: (0, 0)),
            pl.BlockSpec((D_MODEL, TN_MOD), lambda j: (0, j)),
            pl.BlockSpec((1, TN_MOD), lambda j: (0, j)),
        ],
        out_specs=pl.BlockSpec((8, TN_MOD), lambda j: (0, j)),
        compiler_params=_params(("arbitrary",)),
        name="mod",
    )(c8, w_mod, b_mod)


def _bias_kernel(rel_ref, bucket_ref, o_ref):
    h = pl.program_id(0)
    bucket = bucket_ref[...]
    acc = jnp.zeros(bucket.shape, F32)
    for k in range(REL_BUCKETS):
        acc = jnp.where(bucket == k, rel_ref[k, h], acc)
    q_loc = lax.broadcasted_iota(jnp.int32, bucket.shape, 0)
    k_loc = lax.broadcasted_iota(jnp.int32, bucket.shape, 1)
    dist = q_loc + BLOCK - k_loc
    o_ref[0] = jnp.where((dist >= 0) & (dist < WINDOW), acc, -jnp.inf)


def _bias_call(rel_bias, bucket):
    return pl.pallas_call(
        _bias_kernel,
        out_shape=jax.ShapeDtypeStruct((SWA_HEADS, BLOCK, 2 * BLOCK), F32),
        grid=(SWA_HEADS,),
        in_specs=[
            pl.BlockSpec(memory_space=pltpu.SMEM),
            pl.BlockSpec((BLOCK, 2 * BLOCK), lambda h: (0, 0)),
        ],
        out_specs=pl.BlockSpec((1, BLOCK, 2 * BLOCK), lambda h: (h, 0, 0)),
        compiler_params=_params(("arbitrary",)),
        name="t5_bias",
    )(rel_bias, bucket)


def _proj_kernel(x_ref, mod_ref, g_ref, win_ref, gq_ref, gkv_ref, wqm_ref, wqr_ref,
                 wuk_ref, wvt_ref, cosq_ref, sinq_ref, cosk_ref, sink_ref,
                 qa_ref, ka_ref, va_ref, qb_ref, kc_ref, vt_ref, *, q_scale):
    x = x_ref[...]
    mod = mod_ref[0]
    sh1 = mod[0:1]
    sc1 = mod[1:2]
    h = (_rms(x) * g_ref[...]) * (1.0 + sc1) + sh1
    proj = jnp.dot(h.astype(BF16), win_ref[...], preferred_element_type=F32)

    qa_ref[...] = (proj[:, :SWA_Q_COLS] * (SWA_HEAD_DIM ** -0.5)).astype(BF16)
    ka_ref[...] = proj[:, OFF_SWA_K:OFF_SWA_V].astype(BF16)
    va_ref[...] = proj[:, OFF_SWA_V:OFF_MLA_CQ].astype(BF16)

    cq = (_rms(proj[:, OFF_MLA_CQ:OFF_MLA_CKV]) * gq_ref[...]).astype(BF16)
    ckv = (_rms(proj[:, OFF_MLA_CKV:OFF_MLA_KR]) * gkv_ref[...]).astype(BF16)
    krope = (proj[:, OFF_MLA_KR:OFF_MLA_KR_ROT] * cosk_ref[...]
             + proj[:, OFF_MLA_KR_ROT:IN_COLS_EXT] * sink_ref[...]).astype(BF16)

    qmain = jnp.dot(cq, wqm_ref[...], preferred_element_type=F32)
    qrot = jnp.dot(cq, wqr_ref[...], preferred_element_type=F32)
    knope = jnp.dot(ckv, wuk_ref[...], preferred_element_type=F32)
    vt = lax.dot_general(wvt_ref[...], ckv, (((1,), (1,)), ((), ())),
                         preferred_element_type=F32)

    cosq = cosq_ref[...]
    sinq = sinq_ref[...]
    for hd in range(MLA_HEADS):
        lo = hd * MLA_QK_PAD
        mid = lo + MLA_NOPE_DIM
        hi = lo + MLA_QK_PAD
        qb_ref[:, lo:mid] = (qmain[:, lo:mid] * q_scale).astype(BF16)
        qb_ref[:, mid:hi] = (qmain[:, mid:hi] * cosq
                             + qrot[:, hd * 128:(hd + 1) * 128] * sinq).astype(BF16)
        kc_ref[:, lo:mid] = knope[:, hd * MLA_NOPE_DIM:(hd + 1) * MLA_NOPE_DIM].astype(BF16)
        kc_ref[:, mid:hi] = krope
        vt_ref[0, hd, 0] = vt[hd * MLA_V_DIM:(hd + 1) * MLA_V_DIM].astype(BF16)


def _proj_call(x2d, mod3, g_attn, w_in_ext, gq, gkv, wq_main, wq_rot, w_uk, w_vt,
               cosq, sinq, cosk, sink, *, batch, seq):
    tm = TM_PROJ
    tiles_per_seq = seq // tm
    t = batch * seq
    tok = lambda i: (i, 0)
    pos = lambda i: (i % tiles_per_seq, 0)
    out_shape = (
        jax.ShapeDtypeStruct((t, SWA_Q_COLS), BF16),
        jax.ShapeDtypeStruct((t, SWA_KV_COLS), BF16),
        jax.ShapeDtypeStruct((t, SWA_KV_COLS), BF16),
        jax.ShapeDtypeStruct((t, MLA_HEADS * MLA_QK_PAD), BF16),
        jax.ShapeDtypeStruct((t, MLA_HEADS * MLA_QK_PAD), BF16),
        jax.ShapeDtypeStruct((batch, MLA_HEADS, tiles_per_seq, MLA_V_DIM, tm), BF16),
    )
    return pl.pallas_call(
        functools.partial(_proj_kernel, q_scale=MLA_Q_SCALE),
        out_shape=out_shape,
        grid=(t // tm,),
        in_specs=[
            pl.BlockSpec((tm, D_MODEL), tok),
            pl.BlockSpec((1, 6, D_MODEL), lambda i: (i // tiles_per_seq, 0, 0)),
            _const_spec((1, D_MODEL)),
            _const_spec((D_MODEL, IN_COLS_EXT)),
            _const_spec((1, MLA_Q_RANK)),
            _const_spec((1, MLA_KV_RANK)),
            _const_spec((MLA_Q_RANK, MLA_HEADS * MLA_QK_PAD)),
            _const_spec((MLA_Q_RANK, MLA_HEADS * 128)),
            _const_spec((MLA_KV_RANK, MLA_HEADS * MLA_NOPE_DIM)),
            _const_spec((MLA_HEADS * MLA_V_DIM, MLA_KV_RANK)),
            pl.BlockSpec((tm, 128), pos),
            pl.BlockSpec((tm, 128), pos),
            pl.BlockSpec((tm, 128), pos),
            pl.BlockSpec((tm, 128), pos),
        ],
        out_specs=(
            pl.BlockSpec((tm, SWA_Q_COLS), tok),
            pl.BlockSpec((tm, SWA_KV_COLS), tok),
            pl.BlockSpec((tm, SWA_KV_COLS), tok),
            pl.BlockSpec((tm, MLA_HEADS * MLA_QK_PAD), tok),
            pl.BlockSpec((tm, MLA_HEADS * MLA_QK_PAD), tok),
            pl.BlockSpec((1, MLA_HEADS, 1, MLA_V_DIM, tm),
                         lambda i: (i // tiles_per_seq, 0, i % tiles_per_seq, 0, 0)),
        ),
        compiler_params=_params(("parallel",)),
        name="in_proj",
    )(x2d, mod3, g_attn, w_in_ext, gq, gkv, wq_main, wq_rot, w_uk, w_vt,
      cosq, sinq, cosk, sink)


def _swa_kernel(sinks_ref, q_ref, kp_ref, kc_ref, vp_ref, vc_ref, bias_ref, o_ref):
    n = pl.program_id(1)
    first = jnp.where(n == 0, -jnp.inf, 0.0).astype(F32)
    lane = lax.broadcasted_iota(jnp.int32, (BLOCK, 2 * BLOCK), 1)
    prev_mask = jnp.where(lane < BLOCK, first, 0.0)
    kband = jnp.concatenate([kp_ref[...], kc_ref[...]], axis=0)
    vband = jnp.concatenate([vp_ref[...], vc_ref[...]], axis=0)
    q = q_ref[...]
    outs = []
    for hd in range(SWA_HEADS):
        g = hd // SWA_GROUP
        qh = q[:, hd * SWA_HEAD_DIM:(hd + 1) * SWA_HEAD_DIM]
        kg = kband[:, g * SWA_HEAD_DIM:(g + 1) * SWA_HEAD_DIM]
        vg = vband[:, g * SWA_HEAD_DIM:(g + 1) * SWA_HEAD_DIM]
        s = lax.dot_general(qh, kg, (((1,), (1,)), ((), ())), preferred_element_type=F32)
        s = s + bias_ref[hd] + prev_mask
        sink = sinks_ref[hd]
        m = jnp.maximum(jnp.max(s, axis=-1, keepdims=True), sink)
        p = jnp.exp(s - m)
        denom = jnp.sum(p, axis=-1, keepdims=True) + jnp.exp(sink - m)
        o = jnp.dot(p.astype(BF16), vg, preferred_element_type=F32)
        outs.append(o / denom)
    o_ref[...] = jnp.concatenate(outs, axis=1).astype(BF16)


def _swa_call(sinks, qa, ka, va, bias_tab, *, batch, seq):
    nb = seq // BLOCK
    cur = lambda b, n: (b * nb + n, 0)
    prev = lambda b, n: (b * nb + jnp.maximum(n - 1, 0), 0)
    return pl.pallas_call(
        _swa_kernel,
        out_shape=jax.ShapeDtypeStruct((batch * seq, SWA_Q_COLS), BF16),
        grid=(batch, nb),
        in_specs=[
            pl.BlockSpec(memory_space=pltpu.SMEM),
            pl.BlockSpec((BLOCK, SWA_Q_COLS), cur),
            pl.BlockSpec((BLOCK, SWA_KV_COLS), prev),
            pl.BlockSpec((BLOCK, SWA_KV_COLS), cur),
            pl.BlockSpec((BLOCK, SWA_KV_COLS), prev),
            pl.BlockSpec((BLOCK, SWA_KV_COLS), cur),
            _const_spec((SWA_HEADS, BLOCK, 2 * BLOCK)),
        ],
        out_specs=pl.BlockSpec((BLOCK, SWA_Q_COLS), cur),
        compiler_params=_params(("parallel", "arbitrary")),
        name="swa",
    )(sinks, qa, ka, ka, va, va, bias_tab)


def _mla_kernel(q_ref, k_ref, vt_ref, o_ref, s_ref, p_ref, al_ref, m_ref, l_ref, acc_ref,
                *, tile, n_tiles):
    pairs = [(qi, j) for qi in range(n_tiles) for j in range(qi + 1)]

    def stage_scores(u):
        qi, j = pairs[u]
        k = k_ref[j * tile:(j + 1) * tile, :]
        q = q_ref[qi * tile:(qi + 1) * tile, :]
        s_ref[u % 2] = lax.dot_general(k, q, (((1,), (1,)), ((), ())),
                                       preferred_element_type=F32)

    def stage_softmax(u):
        qi, j = pairs[u]
        st = s_ref[u % 2]
        if j == qi:
            kk = lax.broadcasted_iota(jnp.int32, st.shape, 0)
            qq = lax.broadcasted_iota(jnp.int32, st.shape, 1)
            st = jnp.where(kk <= qq, st, -jnp.inf)
        cmax = jnp.max(st, axis=0, keepdims=True)
        if j == 0:
            m_new = cmax
            p = jnp.exp2(st - m_new)
            l_new = jnp.sum(p, axis=0, keepdims=True)
        else:
            m_old = m_ref[qi % 2]
            m_new = jnp.maximum(m_old, cmax)
            alpha = jnp.exp2(m_old - m_new)
            p = jnp.exp2(st - m_new)
            l_new = alpha * l_ref[qi % 2] + jnp.sum(p, axis=0, keepdims=True)
            al_ref[u % 2] = alpha
        m_ref[qi % 2] = m_new
        l_ref[qi % 2] = l_new
        p_ref[u % 2] = p.astype(BF16)

    def stage_pv(u):
        qi, j = pairs[u]
        pv = jnp.dot(vt_ref[0, 0, j], p_ref[u % 2], preferred_element_type=F32)
        if j == 0:
            acc = pv
        else:
            acc = al_ref[u % 2] * acc_ref[...] + pv
        if j == qi:
            o_ref[qi * tile:(qi + 1) * tile, :] = (acc / l_ref[qi % 2]).T.astype(BF16)
        else:
            acc_ref[...] = acc

    n = len(pairs)
    for t in range(n + 2):
        if 0 <= t - 2 < n:
            stage_pv(t - 2)
        if 0 <= t - 1 < n:
            stage_softmax(t - 1)
        if t < n:
            stage_scores(t)


def _mla_call(qb, kcat, vt, *, batch, seq):
    tile = TM_PROJ
    n_tiles = seq // tile
    return pl.pallas_call(
        functools.partial(_mla_kernel, tile=tile, n_tiles=n_tiles),
        out_shape=jax.ShapeDtypeStruct((batch * seq, MLA_HEADS * MLA_V_DIM), BF16),
        grid=(batch, MLA_HEADS),
        in_specs=[
            pl.BlockSpec((seq, MLA_QK_PAD), lambda b, h: (b, h)),
            pl.BlockSpec((seq, MLA_QK_PAD), lambda b, h: (b, h)),
            pl.BlockSpec((1, 1, n_tiles, MLA_V_DIM, tile), lambda b, h: (b, h, 0, 0, 0)),
        ],
        out_specs=pl.BlockSpec((seq, MLA_V_DIM), lambda b, h: (b, h)),
        scratch_shapes=[
            pltpu.VMEM((2, tile, tile), F32),
            pltpu.VMEM((2, tile, tile), BF16),
            pltpu.VMEM((2, 1, tile), F32),
            pltpu.VMEM((2, 1, tile), F32),
            pltpu.VMEM((2, 1, tile), F32),
            pltpu.VMEM((MLA_V_DIM, tile), F32),
        ],
        compiler_params=_params(("parallel", "parallel")),
        name="mla",
    )(qb, kcat, vt)


def _out_kernel(x_ref, oa_ref, ob_ref, mod_ref, wa_ref, wb_ref, o_ref):
    g1 = mod_ref[0][2:3]
    y = (jnp.dot(oa_ref[...], wa_ref[...], preferred_element_type=F32)
         + jnp.dot(ob_ref[...], wb_ref[...], preferred_element_type=F32))
    o_ref[...] = x_ref[...] + g1 * y


def _out_call(x2d, oa, ob, mod3, w_out_a, w_out_b, *, seq):
    tm = TM_OUT
    t = x2d.shape[0]
    tiles_per_seq = seq // tm
    tok = lambda i: (i, 0)
    return pl.pallas_call(
        _out_kernel,
        out_shape=jax.ShapeDtypeStruct((t, D_MODEL), F32),
        grid=(t // tm,),
        in_specs=[
            pl.BlockSpec((tm, D_MODEL), tok),
            pl.BlockSpec((tm, SWA_Q_COLS), tok),
            pl.BlockSpec((tm, MLA_HEADS * MLA_V_DIM), tok),
            pl.BlockSpec((1, 6, D_MODEL), lambda i: (i // tiles_per_seq, 0, 0)),
            _const_spec((SWA_Q_COLS, D_MODEL)),
            _const_spec((MLA_HEADS * MLA_V_DIM, D_MODEL)),
        ],
        out_specs=pl.BlockSpec((tm, D_MODEL), tok),
        compiler_params=_params(("parallel",)),
        name="out_proj",
    )(x2d, oa, ob, mod3, w_out_a, w_out_b)


def _mlp_kernel(x_ref, mod_ref, g_ref, gf_ref, w1_ref, w2_ref, o_ref, h_ref):
    j = pl.program_id(1)
    mod = mod_ref[0]

    @pl.when(j == 0)
    def _():
        sh2 = mod[3:4]
        sc2 = mod[4:5]
        h = (_rms(x_ref[...]) * g_ref[...]) * (1.0 + sc2) + sh2
        h_ref[...] = h.astype(BF16)

    u = jnp.maximum(jnp.dot(h_ref[...], w1_ref[...], preferred_element_type=F32), 0.0)
    y = jnp.dot((u * u).astype(BF16), w2_ref[...], preferred_element_type=F32)

    @pl.when(j == 0)
    def _():
        o_ref[...] = y

    @pl.when(j > 0)
    def _():
        o_ref[...] += y

    @pl.when(j == pl.num_programs(1) - 1)
    def _():
        g2 = mod[5:6]
        x2 = x_ref[...] + g2 * o_ref[...]
        o_ref[...] = _rms(x2) * gf_ref[...]


def _mlp_call(x1, mod3, g_mlp, g_final, w1, w2, *, seq):
    tm = TM_MLP
    tf = TF_MLP
    t = x1.shape[0]
    tiles_per_seq = seq // tm
    return pl.pallas_call(
        _mlp_kernel,
        out_shape=jax.ShapeDtypeStruct((t, D_MODEL), F32),
        grid=(t // tm, D_FF // tf),
        in_specs=[
            pl.BlockSpec((tm, D_MODEL), lambda i, j: (i, 0)),
            pl.BlockSpec((1, 6, D_MODEL), lambda i, j: (i // tiles_per_seq, 0, 0)),
            _const_spec((1, D_MODEL)),
            _const_spec((1, D_MODEL)),
            pl.BlockSpec((D_MODEL, tf), lambda i, j: (0, j)),
            pl.BlockSpec((tf, D_MODEL), lambda i, j: (j, 0)),
        ],
        out_specs=pl.BlockSpec((tm, D_MODEL), lambda i, j: (i, 0)),
        scratch_shapes=[pltpu.VMEM((tm, D_MODEL), BF16)],
        compiler_params=_params(("parallel", "arbitrary")),
        name="mlp",
    )(x1, mod3, g_mlp, g_final, w1, w2)


def _t5_bucket_table():
    q_loc = jnp.arange(BLOCK)[:, None]
    k_loc = jnp.arange(2 * BLOCK)[None, :]
    n = jnp.maximum(q_loc + BLOCK - k_loc, 0)
    max_exact = REL_BUCKETS // 2
    nf = jnp.maximum(n, 1).astype(F32)
    large = max_exact + (jnp.log(nf / max_exact) / math.log(REL_MAX_DIST / max_exact)
                         * (REL_BUCKETS - max_exact)).astype(jnp.int32)
    large = jnp.minimum(large, REL_BUCKETS - 1)
    return jnp.where(n < max_exact, n, large).astype(jnp.int32)


def _rope_tables(seq, q_scale):
    half = MLA_ROPE_DIM // 2
    inv_freq = ROPE_THETA ** (-jnp.arange(half, dtype=F32) / half)
    ang = jnp.arange(seq).astype(F32)[:, None] * inv_freq[None, :]
    zeros = jnp.zeros((seq, 128 - MLA_ROPE_DIM), F32)
    cos = jnp.cos(ang)
    sin = jnp.sin(ang)
    cosk = jnp.concatenate([cos, cos, zeros], axis=1)
    sink = jnp.concatenate([sin, sin, zeros], axis=1)
    return cosk * q_scale, sink * q_scale, cosk, sink


def _rot_cols(w):
    half = w.shape[-1] // 2
    return jnp.concatenate([-w[..., half:], w[..., :half]], axis=-1)


def kernel(x, c, w_mod, b_mod, attn_norm_g, w_in, swa_sinks, rel_bias, mla_q_norm_g, w_uq,
           mla_kv_norm_g, w_ukv, w_out, mlp_norm_g, w_ff1, w_ff2, final_norm_g):
    batch, seq, _ = x.shape
    depth = w_mod.shape[0]
    assert depth == 1
    t = batch * seq
    x2d = x.reshape(t, D_MODEL)
    l = 0

    w_kr = w_in[l][:, OFF_MLA_KR:OFF_MLA_KR + MLA_ROPE_DIM]
    zpad = jnp.zeros((D_MODEL, 128 - MLA_ROPE_DIM), F32)
    w_in_ext = jnp.concatenate(
        [w_in[l][:, :OFF_MLA_KR], w_kr, zpad, _rot_cols(w_kr), zpad], axis=1).astype(BF16)

    wq = w_uq[l].reshape(MLA_Q_RANK, MLA_HEADS, MLA_QK_DIM)
    wq_nope = wq[..., :MLA_NOPE_DIM]
    wq_rope = wq[..., MLA_NOPE_DIM:]
    zq = jnp.zeros((MLA_Q_RANK, MLA_HEADS, 128 - MLA_ROPE_DIM), F32)
    wq_main = jnp.concatenate([wq_nope, wq_rope, zq], axis=-1).reshape(
        MLA_Q_RANK, MLA_HEADS * MLA_QK_PAD).astype(BF16)
    wq_rot = jnp.concatenate([_rot_cols(wq_rope), zq], axis=-1).reshape(
        MLA_Q_RANK, MLA_HEADS * 128).astype(BF16)

    wkv = w_ukv[l].reshape(MLA_KV_RANK, MLA_HEADS, MLA_NOPE_DIM + MLA_V_DIM)
    w_uk = wkv[..., :MLA_NOPE_DIM].reshape(MLA_KV_RANK, MLA_HEADS * MLA_NOPE_DIM).astype(BF16)
    w_vt = wkv[..., MLA_NOPE_DIM:].reshape(MLA_KV_RANK, MLA_HEADS * MLA_V_DIM).T.astype(BF16)

    w_out_a = w_out[l][:SWA_Q_COLS].astype(BF16)
    w_out_b = w_out[l][SWA_Q_COLS:].astype(BF16)
    w1 = w_ff1[l].astype(BF16)
    w2 = w_ff2[l].astype(BF16)

    cosq, sinq, cosk, sink = _rope_tables(seq, MLA_Q_SCALE)

    c8 = jnp.pad(c, ((0, 8 - batch), (0, 0)))
    mod = _mod_call(c8, w_mod[l], b_mod[l].reshape(1, -1))[:batch]
    mod3 = mod.reshape(batch, 6, D_MODEL)

    bias_tab = _bias_call(rel_bias, _t5_bucket_table())

    qa, ka, va, qb, kcat, vt = _proj_call(
        x2d, mod3, attn_norm_g[l].reshape(1, -1), w_in_ext,
        mla_q_norm_g[l].reshape(1, -1), mla_kv_norm_g[l].reshape(1, -1),
        wq_main, wq_rot, w_uk, w_vt, cosq, sinq, cosk, sink, batch=batch, seq=seq)

    oa = _swa_call(swa_sinks[l], qa, ka, va, bias_tab, batch=batch, seq=seq)
    ob = _mla_call(qb, kcat, vt, batch=batch, seq=seq)

    x1 = _out_call(x2d, oa, ob, mod3, w_out_a, w_out_b, seq=seq)
    out = _mlp_call(x1, mod3, mlp_norm_g[l].reshape(1, -1), final_norm_g.reshape(1, -1),
                    w1, w2, seq=seq)
    return out.reshape(batch, seq, D_MODEL)
```

```python
import functools
import math

import jax
import jax.numpy as jnp
from jax import lax
from jax.experimental import pallas as pl
from jax.experimental.pallas import tpu as pltpu

F32 = jnp.float32
BF16 = jnp.bfloat16

D_MODEL = 2048
BLOCK = 128
EPS = 1e-6

SWA_HEADS = 16
SWA_KV_HEADS = 2
SWA_HEAD_DIM = 64
SWA_GROUP = SWA_HEADS // SWA_KV_HEADS
WINDOW = 128
REL_BUCKETS = 32
REL_MAX_DIST = 128

MLA_HEADS = 8
MLA_Q_RANK = 384
MLA_KV_RANK = 128
MLA_NOPE_DIM = 128
MLA_ROPE_DIM = 64
MLA_V_DIM = 128
MLA_QK_DIM = MLA_NOPE_DIM + MLA_ROPE_DIM
MLA_QK_PAD = 256
ROPE_THETA = 10000.0
MLA_Q_SCALE = MLA_QK_DIM ** -0.5 * math.log2(math.e)
D_FF = 4 * D_MODEL

SWA_Q_COLS = SWA_HEADS * SWA_HEAD_DIM
SWA_KV_COLS = SWA_KV_HEADS * SWA_HEAD_DIM
OFF_SWA_K = SWA_Q_COLS
OFF_SWA_V = OFF_SWA_K + SWA_KV_COLS
OFF_MLA_CQ = OFF_SWA_V + SWA_KV_COLS
OFF_MLA_CKV = OFF_MLA_CQ + MLA_Q_RANK
OFF_MLA_KR = OFF_MLA_CKV + MLA_KV_RANK
P_CQ = SWA_KV_COLS
P_CKV = P_CQ + MLA_Q_RANK
P_KR = P_CKV + MLA_KV_RANK
P_KR_ROT = P_KR + 128
P_COLS = P_KR_ROT + 128
PT_ROWS = SWA_Q_COLS + SWA_KV_COLS
LOG2E = math.log2(math.e)

VMEM_LIMIT_BYTES = 56 * 1024 * 1024

TM_PROJ = 512
TM_SWA = 512
TM_OUT = 512
TM_MLP = 512
TF_MLP = 1024
TN_MOD = 1536


def _params(sem):
    return pltpu.CompilerParams(dimension_semantics=sem, vmem_limit_bytes=VMEM_LIMIT_BYTES)


def _const_spec(shape):
    nd = len(shape)
    return pl.BlockSpec(shape, lambda *_: (0,) * nd, pipeline_mode=pl.Buffered(1))


def _rms(x):
    return x * lax.rsqrt(jnp.mean(x * x, axis=-1, keepdims=True) + EPS)


def _mod_kernel(c_ref, w_ref, b_ref, o_ref):
    c = c_ref[...]
    ca = c * (1.0 / (1.0 + jnp.exp(-c)))
    o_ref[...] = jnp.dot(ca.astype(BF16), w_ref[...].astype(BF16),
                         preferred_element_type=F32) + b_ref[...]


def _mod_call(c8, w_mod, b_mod):
    n = w_mod.shape[1]
    return pl.pallas_call(
        _mod_kernel,
        out_shape=jax.ShapeDtypeStruct((8, n), F32),
        grid=(n // TN_MOD,),
        in_specs=[
            pl.BlockSpec((8, D_MODEL), lambda j: (0, 0)),
            pl.BlockSpec((D_MODEL, TN_MOD), lambda j: (0, j)),
            pl.BlockSpec((1, TN_MOD), lambda j: (0, j)),
        ],
        out_specs=pl.BlockSpec((8, TN_MOD), lambda j: (0, j)),
        compiler_params=_params(("arbitrary",)),
        name="mod",
    )(c8, w_mod, b_mod)


def _bias_kernel(rel_ref, bucket_ref, o_ref):
    variant = pl.program_id(0)
    h = pl.program_id(1)
    bucket = bucket_ref[...]
    acc = jnp.zeros(bucket.shape, F32)
    for k in range(REL_BUCKETS):
        acc = jnp.where(bucket == k, rel_ref[k, h], acc)
    k_loc = lax.broadcasted_iota(jnp.int32, bucket.shape, 0)
    q_loc = lax.broadcasted_iota(jnp.int32, bucket.shape, 1)
    dist = q_loc + BLOCK - k_loc
    valid = (dist >= 0) & (dist < WINDOW) & ((k_loc >= BLOCK) | (variant == 1))
    o_ref[0, 0] = jnp.where(valid, acc * LOG2E, -jnp.inf)


def _bias_call(rel_bias, bucket_t):
    return pl.pallas_call(
        _bias_kernel,
        out_shape=jax.ShapeDtypeStruct((2, SWA_HEADS, 2 * BLOCK, BLOCK), F32),
        grid=(2, SWA_HEADS),
        in_specs=[
            pl.BlockSpec(memory_space=pltpu.SMEM),
            pl.BlockSpec((2 * BLOCK, BLOCK), lambda v, h: (0, 0)),
        ],
        out_specs=pl.BlockSpec((1, 1, 2 * BLOCK, BLOCK), lambda v, h: (v, h, 0, 0)),
        compiler_params=_params(("arbitrary", "arbitrary")),
        name="t5_bias",
    )(rel_bias, bucket_t)


def _proj_kernel(x_ref, mod_ref, g_ref, win_ref, wt_ref, gq_ref, gkv_ref, wqm_ref, wqr_ref,
                 wuk_ref, wvt_ref, cosq_ref, sinq_ref, cosk_ref, sink_ref,
                 qat_ref, ka_ref, vat_ref, qb_ref, kc_ref, vt_ref, *, q_scale):
    x = x_ref[...]
    mod = mod_ref[0]
    sh1 = mod[0:1]
    sc1 = mod[1:2]
    h = ((_rms(x) * g_ref[...]) * (1.0 + sc1) + sh1).astype(BF16)
    proj = jnp.dot(h, win_ref[...], preferred_element_type=F32)
    proj_t = lax.dot_general(wt_ref[...], h, (((1,), (1,)), ((), ())),
                             preferred_element_type=F32)

    qat_ref[...] = (proj_t[:SWA_Q_COLS] * (SWA_HEAD_DIM ** -0.5 * LOG2E)).astype(BF16)
    vat_ref[...] = proj_t[SWA_Q_COLS:].astype(BF16)
    ka_ref[...] = proj[:, :P_CQ].astype(BF16)

    cq = (_rms(proj[:, P_CQ:P_CKV]) * gq_ref[...]).astype(BF16)
    ckv = (_rms(proj[:, P_CKV:P_KR]) * gkv_ref[...]).astype(BF16)
    krope = (proj[:, P_KR:P_KR_ROT] * cosk_ref[...]
             + proj[:, P_KR_ROT:P_COLS] * sink_ref[...]).astype(BF16)

    qmain = jnp.dot(cq, wqm_ref[...], preferred_element_type=F32)
    qrot = jnp.dot(cq, wqr_ref[...], preferred_element_type=F32)
    knope = jnp.dot(ckv, wuk_ref[...], preferred_element_type=F32)
    vt = lax.dot_general(wvt_ref[...], ckv, (((1,), (1,)), ((), ())),
                         preferred_element_type=F32)

    cosq = cosq_ref[...]
    sinq = sinq_ref[...]
    for hd in range(MLA_HEADS):
        lo = hd * MLA_QK_PAD
        mid = lo + MLA_NOPE_DIM
        hi = lo + MLA_QK_PAD
        qb_ref[:, lo:mid] = (qmain[:, lo:mid] * q_scale).astype(BF16)
        qb_ref[:, mid:hi] = (qmain[:, mid:hi] * cosq
                             + qrot[:, hd * 128:(hd + 1) * 128] * sinq).astype(BF16)
        kc_ref[:, lo:mid] = knope[:, hd * MLA_NOPE_DIM:(hd + 1) * MLA_NOPE_DIM].astype(BF16)
        kc_ref[:, mid:hi] = krope
        vt_ref[0, hd, 0] = vt[hd * MLA_V_DIM:(hd + 1) * MLA_V_DIM].astype(BF16)


def _proj_call(x2d, mod3, g_attn, w_main, w_t, gq, gkv, wq_main, wq_rot, w_uk, w_vt,
               cosq, sinq, cosk, sink, *, batch, seq):
    tm = TM_PROJ
    tiles_per_seq = seq // tm
    t = batch * seq
    tok = lambda i: (i, 0)
    tok_t = lambda i: (0, i)
    pos = lambda i: (i % tiles_per_seq, 0)
    out_shape = (
        jax.ShapeDtypeStruct((SWA_Q_COLS, t), BF16),
        jax.ShapeDtypeStruct((t, SWA_KV_COLS), BF16),
        jax.ShapeDtypeStruct((SWA_KV_COLS, t), BF16),
        jax.ShapeDtypeStruct((t, MLA_HEADS * MLA_QK_PAD), BF16),
        jax.ShapeDtypeStruct((t, MLA_HEADS * MLA_QK_PAD), BF16),
        jax.ShapeDtypeStruct((batch, MLA_HEADS, tiles_per_seq, MLA_V_DIM, tm), BF16),
    )
    return pl.pallas_call(
        functools.partial(_proj_kernel, q_scale=MLA_Q_SCALE),
        out_shape=out_shape,
        grid=(t // tm,),
        in_specs=[
            pl.BlockSpec((tm, D_MODEL), tok),
            pl.BlockSpec((1, 6, D_MODEL), lambda i: (i // tiles_per_seq, 0, 0)),
            _const_spec((1, D_MODEL)),
            _const_spec((D_MODEL, P_COLS)),
            _const_spec((PT_ROWS, D_MODEL)),
            _const_spec((1, MLA_Q_RANK)),
            _const_spec((1, MLA_KV_RANK)),
            _const_spec((MLA_Q_RANK, MLA_HEADS * MLA_QK_PAD)),
            _const_spec((MLA_Q_RANK, MLA_HEADS * 128)),
            _const_spec((MLA_KV_RANK, MLA_HEADS * MLA_NOPE_DIM)),
            _const_spec((MLA_HEADS * MLA_V_DIM, MLA_KV_RANK)),
            pl.BlockSpec((tm, 128), pos),
            pl.BlockSpec((tm, 128), pos),
            pl.BlockSpec((tm, 128), pos),
            pl.BlockSpec((tm, 128), pos),
        ],
        out_specs=(
            pl.BlockSpec((SWA_Q_COLS, tm), tok_t),
            pl.BlockSpec((tm, SWA_KV_COLS), tok),
            pl.BlockSpec((SWA_KV_COLS, tm), tok_t),
            pl.BlockSpec((tm, MLA_HEADS * MLA_QK_PAD), tok),
            pl.BlockSpec((tm, MLA_HEADS * MLA_QK_PAD), tok),
            pl.BlockSpec((1, MLA_HEADS, 1, MLA_V_DIM, tm),
                         lambda i: (i // tiles_per_seq, 0, i % tiles_per_seq, 0, 0)),
        ),
        compiler_params=_params(("parallel",)),
        name="in_proj",
    )(x2d, mod3, g_attn, w_main, w_t, gq, gkv, wq_main, wq_rot, w_uk, w_vt,
      cosq, sinq, cosk, sink)


def _swa_kernel(sinks_ref, qt_ref, kp_ref, kc_ref, vtp_ref, vtc_ref, bias_ref, o_ref,
                *, blocks_per_step):
    first_step = pl.program_id(1) == 0
    k_all = jnp.concatenate([kp_ref[...], kc_ref[...]], axis=0)
    vt_all = jnp.concatenate([vtp_ref[...], vtc_ref[...]], axis=1)
    zeros_q = jnp.zeros((SWA_HEAD_DIM, SWA_GROUP * BLOCK), BF16)
    for i in range(blocks_per_step):
        k_band = k_all[i * BLOCK:(i + 2) * BLOCK]
        vt_band = vt_all[:, i * BLOCK:(i + 2) * BLOCK]
        qt = qt_ref[:, i * BLOCK:(i + 1) * BLOCK]
        variant = jnp.where(first_step, 0, 1) if i == 0 else 1
        for g in range(SWA_KV_HEADS):
            q_g = jnp.concatenate(
                [qt[(g * SWA_GROUP + hh) * SWA_HEAD_DIM:(g * SWA_GROUP + hh + 1) * SWA_HEAD_DIM]
                 for hh in range(SWA_GROUP)], axis=1)
            rhs = jnp.concatenate([q_g, zeros_q] if g == 0 else [zeros_q, q_g], axis=0)
            st_g = jnp.dot(k_band, rhs, preferred_element_type=F32)
            p_list = []
            inv_list = []
            for hh in range(SWA_GROUP):
                hd = g * SWA_GROUP + hh
                st = st_g[:, hh * BLOCK:(hh + 1) * BLOCK] + bias_ref[variant, hd]
                sink = sinks_ref[hd] * LOG2E
                m = jnp.maximum(jnp.max(st, axis=0, keepdims=True), sink)
                p = jnp.exp2(st - m)
                denom = jnp.sum(p, axis=0, keepdims=True) + jnp.exp2(sink - m)
                inv_list.append(1.0 / denom)
                p_list.append(p.astype(BF16))
            pt_g = jnp.concatenate(p_list, axis=1)
            ot_g = jnp.dot(vt_band[g * SWA_HEAD_DIM:(g + 1) * SWA_HEAD_DIM], pt_g,
                           preferred_element_type=F32)
            for pair in range(SWA_GROUP // 2):
                h0 = 2 * pair
                slab = jnp.concatenate(
                    [ot_g[:, (h0 + e) * BLOCK:(h0 + e + 1) * BLOCK] * inv_list[h0 + e]
                     for e in range(2)], axis=0)
                col = (g * SWA_GROUP + h0) * SWA_HEAD_DIM
                o_ref[i * BLOCK:(i + 1) * BLOCK, col:col + 2 * SWA_HEAD_DIM] = (
                    slab.T.astype(BF16))


def _swa_call(sinks, qat, ka, vat, bias_tab, *, batch, seq):
    step = TM_SWA
    blocks_per_step = step // BLOCK
    steps_per_seq = seq // step
    nb = seq // BLOCK
    prev_blk = lambda b, s: b * nb + jnp.maximum(s * blocks_per_step - 1, 0)
    return pl.pallas_call(
        functools.partial(_swa_kernel, blocks_per_step=blocks_per_step),
        out_shape=jax.ShapeDtypeStruct((batch * seq, SWA_Q_COLS), BF16),
        grid=(batch, steps_per_seq),
        in_specs=[
            pl.BlockSpec(memory_space=pltpu.SMEM),
            pl.BlockSpec((SWA_Q_COLS, step), lambda b, s: (0, b * steps_per_seq + s)),
            pl.BlockSpec((BLOCK, SWA_KV_COLS), lambda b, s: (prev_blk(b, s), 0)),
            pl.BlockSpec((step, SWA_KV_COLS), lambda b, s: (b * steps_per_seq + s, 0)),
            pl.BlockSpec((SWA_KV_COLS, BLOCK), lambda b, s: (0, prev_blk(b, s))),
            pl.BlockSpec((SWA_KV_COLS, step), lambda b, s: (0, b * steps_per_seq + s)),
            _const_spec((2, SWA_HEADS, 2 * BLOCK, BLOCK)),
        ],
        out_specs=pl.BlockSpec((step, SWA_Q_COLS), lambda b, s: (b * steps_per_seq + s, 0)),
        compiler_params=_params(("parallel", "arbitrary")),
        name="swa",
    )(sinks, qat, ka, ka, vat, vat, bias_tab)


def _mla_kernel(q_ref, k_ref, vt_ref, o_ref, s_ref, p_ref, al_ref, m_ref, l_ref, acc_ref,
                *, tile, n_tiles):
    pairs = [(qi, j) for qi in range(n_tiles) for j in range(qi + 1)]

    def stage_scores(u):
        qi, j = pairs[u]
        k = k_ref[j * tile:(j + 1) * tile, :]
        q = q_ref[qi * tile:(qi + 1) * tile, :]
        s_ref[u % 2] = lax.dot_general(k, q, (((1,), (1,)), ((), ())),
                                       preferred_element_type=F32)

    def stage_softmax(u):
        qi, j = pairs[u]
        st = s_ref[u % 2]
        if j == qi:
            kk = lax.broadcasted_iota(jnp.int32, st.shape, 0)
            qq = lax.broadcasted_iota(jnp.int32, st.shape, 1)
            st = jnp.where(kk <= qq, st, -jnp.inf)
        cmax = jnp.max(st, axis=0, keepdims=True)
        if j == 0:
            m_new = cmax
            p = jnp.exp2(st - m_new)
            l_new = jnp.sum(p, axis=0, keepdims=True)
        else:
            m_old = m_ref[qi % 2]
            m_new = jnp.maximum(m_old, cmax)
            alpha = jnp.exp2(m_old - m_new)
            p = jnp.exp2(st - m_new)
            l_new = alpha * l_ref[qi % 2] + jnp.sum(p, axis=0, keepdims=True)
            al_ref[u % 2] = alpha
        m_ref[qi % 2] = m_new
        l_ref[qi % 2] = l_new
        p_ref[u % 2] = p.astype(BF16)

    def stage_pv(u):
        qi, j = pairs[u]
        pv = jnp.dot(vt_ref[0, 0, j], p_ref[u % 2], preferred_element_type=F32)
        if j == 0:
            acc = pv
        else:
            acc = al_ref[u % 2] * acc_ref[...] + pv
        if j == qi:
            o_ref[qi * tile:(qi + 1) * tile, :] = (acc / l_ref[qi % 2]).T.astype(BF16)
        else:
            acc_ref[...] = acc

    n = len(pairs)
    for t in range(n + 2):
        if 0 <= t - 2 < n:
            stage_pv(t - 2)
        if 0 <= t - 1 < n:
            stage_softmax(t - 1)
        if t < n:
            stage_scores(t)


def _mla_call(qb, kcat, vt, *, batch, seq):
    tile = TM_PROJ
    n_tiles = seq // tile
    return pl.pallas_call(
        functools.partial(_mla_kernel, tile=tile, n_tiles=n_tiles),
        out_shape=jax.ShapeDtypeStruct((batch * seq, MLA_HEADS * MLA_V_DIM), BF16),
        grid=(batch, MLA_HEADS),
        in_specs=[
            pl.BlockSpec((seq, MLA_QK_PAD), lambda b, h: (b, h)),
            pl.BlockSpec((seq, MLA_QK_PAD), lambda b, h: (b, h)),
            pl.BlockSpec((1, 1, n_tiles, MLA_V_DIM, tile), lambda b, h: (b, h, 0, 0, 0)),
        ],
        out_specs=pl.BlockSpec((seq, MLA_V_DIM), lambda b, h: (b, h)),
        scratch_shapes=[
            pltpu.VMEM((2, tile, tile), F32),
            pltpu.VMEM((2, tile, tile), BF16),
            pltpu.VMEM((2, 1, tile), F32),
            pltpu.VMEM((2, 1, tile), F32),
            pltpu.VMEM((2, 1, tile), F32),
            pltpu.VMEM((MLA_V_DIM, tile), F32),
        ],
        compiler_params=_params(("parallel", "parallel")),
        name="mla",
    )(qb, kcat, vt)


def _out_kernel(x_ref, oa_ref, ob_ref, mod_ref, g_ref, wa_ref, wb_ref, x1_ref, h_ref):
    mod = mod_ref[0]
    g1 = mod[2:3]
    sh2 = mod[3:4]
    sc2 = mod[4:5]
    y = (jnp.dot(oa_ref[...], wa_ref[...], preferred_element_type=F32)
         + jnp.dot(ob_ref[...], wb_ref[...], preferred_element_type=F32))
    x1 = x_ref[...] + g1 * y
    x1_ref[...] = x1
    h_ref[...] = ((_rms(x1) * g_ref[...]) * (1.0 + sc2) + sh2).astype(BF16)


def _out_call(x2d, oa, ob, mod3, g_mlp, w_out_a, w_out_b, *, seq):
    tm = TM_OUT
    t = x2d.shape[0]
    tiles_per_seq = seq // tm
    tok = lambda i: (i, 0)
    return pl.pallas_call(
        _out_kernel,
        out_shape=(jax.ShapeDtypeStruct((t, D_MODEL), F32),
                   jax.ShapeDtypeStruct((t, D_MODEL), BF16)),
        grid=(t // tm,),
        in_specs=[
            pl.BlockSpec((tm, D_MODEL), tok),
            pl.BlockSpec((tm, SWA_Q_COLS), tok),
            pl.BlockSpec((tm, MLA_HEADS * MLA_V_DIM), tok),
            pl.BlockSpec((1, 6, D_MODEL), lambda i: (i // tiles_per_seq, 0, 0)),
            _const_spec((1, D_MODEL)),
            _const_spec((SWA_Q_COLS, D_MODEL)),
            _const_spec((MLA_HEADS * MLA_V_DIM, D_MODEL)),
        ],
        out_specs=(pl.BlockSpec((tm, D_MODEL), tok), pl.BlockSpec((tm, D_MODEL), tok)),
        compiler_params=_params(("parallel",)),
        name="out_proj",
    )(x2d, oa, ob, mod3, g_mlp, w_out_a, w_out_b)


def _mlp_kernel(x_ref, h_ref, mod_ref, gf_ref, w1_ref, w2_ref, o_ref):
    j = pl.program_id(1)

    @pl.when(j == 0)
    def _():
        o_ref[...] = jnp.zeros(o_ref.shape, F32)

    u = jnp.maximum(jnp.dot(h_ref[...], w1_ref[...], preferred_element_type=F32), 0.0)
    o_ref[...] += jnp.dot((u * u).astype(BF16), w2_ref[...], preferred_element_type=F32)

    @pl.when(j == pl.num_programs(1) - 1)
    def _():
        g2 = mod_ref[0][5:6]
        x2 = x_ref[...] + g2 * o_ref[...]
        o_ref[...] = _rms(x2) * gf_ref[...]


def _mlp_call(x1, h2, mod3, g_final, w1, w2, *, seq):
    tm = TM_MLP
    tf = TF_MLP
    t = x1.shape[0]
    tiles_per_seq = seq // tm
    return pl.pallas_call(
        _mlp_kernel,
        out_shape=jax.ShapeDtypeStruct((t, D_MODEL), F32),
        grid=(t // tm, D_FF // tf),
        in_specs=[
            pl.BlockSpec((tm, D_MODEL), lambda i, j: (i, 0)),
            pl.BlockSpec((tm, D_MODEL), lambda i, j: (i, 0)),
            pl.BlockSpec((1, 6, D_MODEL), lambda i, j: (i // tiles_per_seq, 0, 0)),
            _const_spec((1, D_MODEL)),
            pl.BlockSpec((D_MODEL, tf), lambda i, j: (0, j)),
            pl.BlockSpec((tf, D_MODEL), lambda i, j: (j, 0)),
        ],
        out_specs=pl.BlockSpec((tm, D_MODEL), lambda i, j: (i, 0)),
        compiler_params=_params(("parallel", "arbitrary")),
        name="mlp",
    )(x1, h2, mod3, g_final, w1, w2)


def _t5_bucket_table():
    q_loc = jnp.arange(BLOCK)[:, None]
    k_loc = jnp.arange(2 * BLOCK)[None, :]
    n = jnp.maximum(q_loc + BLOCK - k_loc, 0)
    max_exact = REL_BUCKETS // 2
    nf = jnp.maximum(n, 1).astype(F32)
    large = max_exact + (jnp.log(nf / max_exact) / math.log(REL_MAX_DIST / max_exact)
                         * (REL_BUCKETS - max_exact)).astype(jnp.int32)
    large = jnp.minimum(large, REL_BUCKETS - 1)
    return jnp.where(n < max_exact, n, large).astype(jnp.int32)


def _rope_tables(seq, q_scale):
    half = MLA_ROPE_DIM // 2
    inv_freq = ROPE_THETA ** (-jnp.arange(half, dtype=F32) / half)
    ang = jnp.arange(seq).astype(F32)[:, None] * inv_freq[None, :]
    zeros = jnp.zeros((seq, 128 - MLA_ROPE_DIM), F32)
    cos = jnp.cos(ang)
    sin = jnp.sin(ang)
    cosk = jnp.concatenate([cos, cos, zeros], axis=1)
    sink = jnp.concatenate([sin, sin, zeros], axis=1)
    return cosk * q_scale, sink * q_scale, cosk, sink


def _rot_cols(w):
    half = w.shape[-1] // 2
    return jnp.concatenate([-w[..., half:], w[..., :half]], axis=-1)


def kernel(x, c, w_mod, b_mod, attn_norm_g, w_in, swa_sinks, rel_bias, mla_q_norm_g, w_uq,
           mla_kv_norm_g, w_ukv, w_out, mlp_norm_g, w_ff1, w_ff2, final_norm_g):
    batch, seq, _ = x.shape
    depth = w_mod.shape[0]
    assert depth == 1
    t = batch * seq
    x2d = x.reshape(t, D_MODEL)
    l = 0

    w_kr = w_in[l][:, OFF_MLA_KR:OFF_MLA_KR + MLA_ROPE_DIM]
    zpad = jnp.zeros((D_MODEL, 128 - MLA_ROPE_DIM), F32)
    w_main = jnp.concatenate(
        [w_in[l][:, OFF_SWA_K:OFF_SWA_V], w_in[l][:, OFF_MLA_CQ:OFF_MLA_KR],
         w_kr, zpad, _rot_cols(w_kr), zpad], axis=1).astype(BF16)
    w_t = jnp.concatenate(
        [w_in[l][:, :SWA_Q_COLS], w_in[l][:, OFF_SWA_V:OFF_MLA_CQ]], axis=1).T.astype(BF16)

    wq = w_uq[l].reshape(MLA_Q_RANK, MLA_HEADS, MLA_QK_DIM)
    wq_nope = wq[..., :MLA_NOPE_DIM]
    wq_rope = wq[..., MLA_NOPE_DIM:]
    zq = jnp.zeros((MLA_Q_RANK, MLA_HEADS, 128 - MLA_ROPE_DIM), F32)
    wq_main = jnp.concatenate([wq_nope, wq_rope, zq], axis=-1).reshape(
        MLA_Q_RANK, MLA_HEADS * MLA_QK_PAD).astype(BF16)
    wq_rot = jnp.concatenate([_rot_cols(wq_rope), zq], axis=-1).reshape(
        MLA_Q_RANK, MLA_HEADS * 128).astype(BF16)

    wkv = w_ukv[l].reshape(MLA_KV_RANK, MLA_HEADS, MLA_NOPE_DIM + MLA_V_DIM)
    w_uk = wkv[..., :MLA_NOPE_DIM].reshape(MLA_KV_RANK, MLA_HEADS * MLA_NOPE_DIM).astype(BF16)
    w_vt = wkv[..., MLA_NOPE_DIM:].reshape(MLA_KV_RANK, MLA_HEADS * MLA_V_DIM).T.astype(BF16)

    w_out_a = w_out[l][:SWA_Q_COLS].astype(BF16)
    w_out_b = w_out[l][SWA_Q_COLS:].astype(BF16)
    w1 = w_ff1[l].astype(BF16)
    w2 = w_ff2[l].astype(BF16)

    cosq, sinq, cosk, sink = _rope_tables(seq, MLA_Q_SCALE)

    c8 = jnp.pad(c, ((0, 8 - batch), (0, 0)))
    mod = _mod_call(c8, w_mod[l], b_mod[l].reshape(1, -1))[:batch]
    mod3 = mod.reshape(batch, 6, D_MODEL)

    bias_tab = _bias_call(rel_bias, _t5_bucket_table().T)

    qat, ka, vat, qb, kcat, vt = _proj_call(
        x2d, mod3, attn_norm_g[l].reshape(1, -1), w_main, w_t,
        mla_q_norm_g[l].reshape(1, -1), mla_kv_norm_g[l].reshape(1, -1),
        wq_main, wq_rot, w_uk, w_vt, cosq, sinq, cosk, sink, batch=batch, seq=seq)

    oa = _swa_call(swa_sinks[l], qat, ka, vat, bias_tab, batch=batch, seq=seq)
    ob = _mla_call(qb, kcat, vt, batch=batch, seq=seq)

    x1, h2 = _out_call(x2d, oa, ob, mod3, mlp_norm_g[l].reshape(1, -1), w_out_a, w_out_b,
                       seq=seq)
    out = _mlp_call(x1, h2, mod3, final_norm_g.reshape(1, -1), w1, w2, seq=seq)
    return out.reshape(batch, seq, D_MODEL)
```

```python
import functools
import math

import jax
import jax.numpy as jnp
import numpy as np
from jax import lax
from jax.experimental import pallas as pl
from jax.experimental.pallas import tpu as pltpu

F32 = jnp.float32
BF16 = jnp.bfloat16

D_MODEL = 2048
BLOCK = 128
EPS = 1e-6

SWA_HEADS = 16
SWA_KV_HEADS = 2
SWA_HEAD_DIM = 64
SWA_GROUP = SWA_HEADS // SWA_KV_HEADS
WINDOW = 128
REL_BUCKETS = 32
REL_MAX_DIST = 128

MLA_HEADS = 8
MLA_Q_RANK = 384
MLA_KV_RANK = 128
MLA_NOPE_DIM = 128
MLA_ROPE_DIM = 64
MLA_V_DIM = 128
MLA_QK_DIM = MLA_NOPE_DIM + MLA_ROPE_DIM
MLA_QK_PAD = 256
ROPE_THETA = 10000.0
MLA_Q_SCALE = MLA_QK_DIM ** -0.5 * math.log2(math.e)
D_FF = 4 * D_MODEL

SWA_Q_COLS = SWA_HEADS * SWA_HEAD_DIM
SWA_KV_COLS = SWA_KV_HEADS * SWA_HEAD_DIM
OFF_SWA_K = SWA_Q_COLS
OFF_SWA_V = OFF_SWA_K + SWA_KV_COLS
OFF_MLA_CQ = OFF_SWA_V + SWA_KV_COLS
OFF_MLA_CKV = OFF_MLA_CQ + MLA_Q_RANK
OFF_MLA_KR = OFF_MLA_CKV + MLA_KV_RANK
P_CQ = SWA_KV_COLS
P_CKV = P_CQ + MLA_Q_RANK
P_KR = P_CKV + MLA_KV_RANK
P_COLS = P_KR + 128
PT_ROWS = SWA_Q_COLS + SWA_KV_COLS
LOG2E = math.log2(math.e)

VMEM_LIMIT_BYTES = 62 * 1024 * 1024

TM_PROJ = 512
TM_SWA = 1024
TM_OUT = 512
TM_MLP = 512
TF_MLP = 2048
TN_MOD = 1536


def _params(sem):
    return pltpu.CompilerParams(dimension_semantics=sem, vmem_limit_bytes=VMEM_LIMIT_BYTES)


def _const_spec(shape):
    nd = len(shape)
    return pl.BlockSpec(shape, lambda *_: (0,) * nd, pipeline_mode=pl.Buffered(1))


def _rms(x):
    return x * lax.rsqrt(jnp.mean(x * x, axis=-1, keepdims=True) + EPS)


def _mod_kernel(c_ref, w_ref, b_ref, o_ref):
    c = c_ref[...]
    ca = c * (1.0 / (1.0 + jnp.exp(-c)))
    o_ref[...] = jnp.dot(ca.astype(BF16), w_ref[...].astype(BF16),
                         preferred_element_type=F32) + b_ref[...]


def _mod_call(c8, w_mod, b_mod):
    n = w_mod.shape[1]
    return pl.pallas_call(
        _mod_kernel,
        out_shape=jax.ShapeDtypeStruct((8, n), F32),
        grid=(n // TN_MOD,),
        in_specs=[
            pl.BlockSpec((8, D_MODEL), lambda j: (0, 0)),
            pl.BlockSpec((D_MODEL, TN_MOD), lambda j: (0, j)),
            pl.BlockSpec((1, TN_MOD), lambda j: (0, j)),
        ],
        out_specs=pl.BlockSpec((8, TN_MOD), lambda j: (0, j)),
        compiler_params=_params(("arbitrary",)),
        name="mod",
    )(c8, w_mod, b_mod)


def _bias_kernel(rel_ref, bucket_ref, o_ref):
    variant = pl.program_id(0)
    h = pl.program_id(1)
    bucket = bucket_ref[...]
    acc = jnp.zeros(bucket.shape, F32)
    for k in range(REL_BUCKETS):
        acc = jnp.where(bucket == k, rel_ref[k, h], acc)
    k_loc = lax.broadcasted_iota(jnp.int32, bucket.shape, 0)
    q_loc = lax.broadcasted_iota(jnp.int32, bucket.shape, 1)
    dist = q_loc + BLOCK - k_loc
    valid = (dist >= 0) & (dist < WINDOW) & ((k_loc >= BLOCK) | (variant == 1))
    o_ref[0, 0] = jnp.where(valid, acc * LOG2E, -jnp.inf)


def _bias_call(rel_bias, bucket_t):
    return pl.pallas_call(
        _bias_kernel,
        out_shape=jax.ShapeDtypeStruct((2, SWA_HEADS, 2 * BLOCK, BLOCK), F32),
        grid=(2, SWA_HEADS),
        in_specs=[
            pl.BlockSpec(memory_space=pltpu.SMEM),
            pl.BlockSpec((2 * BLOCK, BLOCK), lambda v, h: (0, 0)),
        ],
        out_specs=pl.BlockSpec((1, 1, 2 * BLOCK, BLOCK), lambda v, h: (v, h, 0, 0)),
        compiler_params=_params(("arbitrary", "arbitrary")),
        name="t5_bias",
    )(rel_bias, bucket_t)


def _proj_kernel(x_ref, mod_ref, g_ref, win_ref, wt_ref, gq_ref, gkv_ref, wqm_ref, wqr_ref,
                 wuk_ref, wvt_ref, cos_ref, sin_ref,
                 qat_ref, ka_ref, vat_ref, qb_ref, kc_ref, vt_ref, *, q_scale):
    x = x_ref[...]
    mod = mod_ref[0]
    sh1 = mod[0:1]
    sc1 = mod[1:2]
    h = ((_rms(x) * g_ref[...]) * (1.0 + sc1) + sh1).astype(BF16)
    proj = jnp.dot(h, win_ref[...], preferred_element_type=F32)
    proj_t = lax.dot_general(wt_ref[...], h, (((1,), (1,)), ((), ())),
                             preferred_element_type=F32)

    qat_ref[...] = (proj_t[:SWA_Q_COLS] * (SWA_HEAD_DIM ** -0.5 * LOG2E)).astype(BF16)
    vat_ref[...] = proj_t[SWA_Q_COLS:].astype(BF16)
    ka_ref[...] = proj[:, :P_CQ].astype(BF16)

    cq = (_rms(proj[:, P_CQ:P_CKV]) * gq_ref[...]).astype(BF16)
    ckv = (_rms(proj[:, P_CKV:P_KR]) * gkv_ref[...]).astype(BF16)
    cos_lo = cos_ref[...]
    sin_lo = sin_ref[...]
    cos_hi = pltpu.roll(cos_lo, MLA_ROPE_DIM, 1)
    sin_hi = pltpu.roll(sin_lo, MLA_ROPE_DIM, 1)
    kr = proj[:, P_KR:P_COLS]
    kr_sw = pltpu.roll(kr, MLA_ROPE_DIM, 1)
    krope_lo = (kr * cos_lo + kr_sw * sin_lo).astype(BF16)
    krope_hi = (kr_sw * cos_hi + kr * sin_hi).astype(BF16)

    qmain = jnp.dot(cq, wqm_ref[...], preferred_element_type=F32)
    qrot = jnp.dot(cq, wqr_ref[...], preferred_element_type=F32)
    knope = jnp.dot(ckv, wuk_ref[...], preferred_element_type=F32)
    vt = lax.dot_general(wvt_ref[...], ckv, (((1,), (1,)), ((), ())),
                         preferred_element_type=F32)

    half_heads = MLA_HEADS // 2
    for hd in range(MLA_HEADS):
        lo = hd * MLA_QK_PAD
        mid = lo + MLA_NOPE_DIM
        hi = lo + MLA_QK_PAD
        low = hd < half_heads
        cos_t, sin_t = (cos_lo, sin_lo) if low else (cos_hi, sin_hi)
        rot = qrot[:, (hd % half_heads) * 128:(hd % half_heads + 1) * 128]
        qb_ref[:, lo:mid] = (qmain[:, lo:mid] * q_scale).astype(BF16)
        qb_ref[:, mid:hi] = ((qmain[:, mid:hi] * cos_t + rot * sin_t) * q_scale).astype(BF16)
        kc_ref[:, lo:mid] = knope[:, hd * MLA_NOPE_DIM:(hd + 1) * MLA_NOPE_DIM].astype(BF16)
        kc_ref[:, mid:hi] = krope_lo if low else krope_hi
        vt_ref[0, hd, 0] = vt[hd * MLA_V_DIM:(hd + 1) * MLA_V_DIM].astype(BF16)


def _proj_call(x2d, mod3, g_attn, w_main, w_t, gq, gkv, wq_main, wq_rot, w_uk, w_vt,
               cos_tab, sin_tab, *, batch, seq):
    tm = TM_PROJ
    tiles_per_seq = seq // tm
    t = batch * seq
    tok = lambda i: (i, 0)
    tok_t = lambda i: (0, i)
    pos = lambda i: (i % tiles_per_seq, 0)
    out_shape = (
        jax.ShapeDtypeStruct((SWA_Q_COLS, t), BF16),
        jax.ShapeDtypeStruct((t, SWA_KV_COLS), BF16),
        jax.ShapeDtypeStruct((SWA_KV_COLS, t), BF16),
        jax.ShapeDtypeStruct((t, MLA_HEADS * MLA_QK_PAD), BF16),
        jax.ShapeDtypeStruct((t, MLA_HEADS * MLA_QK_PAD), BF16),
        jax.ShapeDtypeStruct((batch, MLA_HEADS, tiles_per_seq, MLA_V_DIM, tm), BF16),
    )
    return pl.pallas_call(
        functools.partial(_proj_kernel, q_scale=MLA_Q_SCALE),
        out_shape=out_shape,
        grid=(t // tm,),
        in_specs=[
            pl.BlockSpec((tm, D_MODEL), tok),
            pl.BlockSpec((1, 6, D_MODEL), lambda i: (i // tiles_per_seq, 0, 0)),
            _const_spec((1, D_MODEL)),
            _const_spec((D_MODEL, P_COLS)),
            _const_spec((PT_ROWS, D_MODEL)),
            _const_spec((1, MLA_Q_RANK)),
            _const_spec((1, MLA_KV_RANK)),
            _const_spec((MLA_Q_RANK, MLA_HEADS * MLA_QK_PAD)),
            _const_spec((MLA_Q_RANK, MLA_HEADS // 2 * 128)),
            _const_spec((MLA_KV_RANK, MLA_HEADS * MLA_NOPE_DIM)),
            _const_spec((MLA_HEADS * MLA_V_DIM, MLA_KV_RANK)),
            pl.BlockSpec((tm, 128), pos),
            pl.BlockSpec((tm, 128), pos),
        ],
        out_specs=(
            pl.BlockSpec((SWA_Q_COLS, tm), tok_t),
            pl.BlockSpec((tm, SWA_KV_COLS), tok),
            pl.BlockSpec((SWA_KV_COLS, tm), tok_t),
            pl.BlockSpec((tm, MLA_HEADS * MLA_QK_PAD), tok),
            pl.BlockSpec((tm, MLA_HEADS * MLA_QK_PAD), tok),
            pl.BlockSpec((1, MLA_HEADS, 1, MLA_V_DIM, tm),
                         lambda i: (i // tiles_per_seq, 0, i % tiles_per_seq, 0, 0)),
        ),
        compiler_params=_params(("parallel",)),
        name="in_proj",
    )(x2d, mod3, g_attn, w_main, w_t, gq, gkv, wq_main, wq_rot, w_uk, w_vt,
      cos_tab, sin_tab)


def _swa_kernel(sinks_ref, qt_ref, kp_ref, kc_ref, vtp_ref, vtc_ref, bias_ref, o_ref,
                s_ref, p_ref, inv_ref, *, blocks_per_step):
    first_step = pl.program_id(1) == 0
    k_all = jnp.concatenate([kp_ref[...], kc_ref[...]], axis=0)
    vt_all = jnp.concatenate([vtp_ref[...], vtc_ref[...]], axis=1)
    zeros_q = jnp.zeros((SWA_HEAD_DIM, SWA_GROUP * BLOCK), BF16)
    units = [(i, g) for i in range(blocks_per_step) for g in range(SWA_KV_HEADS)]

    def stage_scores(u):
        i, g = units[u]
        k_band = k_all[i * BLOCK:(i + 2) * BLOCK]
        qt = qt_ref[:, i * BLOCK:(i + 1) * BLOCK]
        q_g = jnp.concatenate(
            [qt[(g * SWA_GROUP + hh) * SWA_HEAD_DIM:(g * SWA_GROUP + hh + 1) * SWA_HEAD_DIM]
             for hh in range(SWA_GROUP)], axis=1)
        rhs = jnp.concatenate([q_g, zeros_q] if g == 0 else [zeros_q, q_g], axis=0)
        s_ref[u % 2] = jnp.dot(k_band, rhs, preferred_element_type=F32)

    def stage_softmax(u):
        i, g = units[u]
        variant = jnp.where(first_step, 0, 1) if i == 0 else 1
        for hh in range(SWA_GROUP):
            hd = g * SWA_GROUP + hh
            st = s_ref[u % 2, :, hh * BLOCK:(hh + 1) * BLOCK] + bias_ref[variant, hd]
            sink = sinks_ref[hd] * LOG2E
            m = jnp.maximum(jnp.max(st, axis=0, keepdims=True), sink)
            p = jnp.exp2(st - m)
            denom = jnp.sum(p, axis=0, keepdims=True) + jnp.exp2(sink - m)
            inv_ref[u % 2, hh] = 1.0 / denom
            p_ref[u % 2, :, hh * BLOCK:(hh + 1) * BLOCK] = p.astype(BF16)

    def stage_pv(u):
        i, g = units[u]
        vt_band = vt_all[g * SWA_HEAD_DIM:(g + 1) * SWA_HEAD_DIM, i * BLOCK:(i + 2) * BLOCK]
        ot_g = jnp.dot(vt_band, p_ref[u % 2], preferred_element_type=F32)
        for pair in range(SWA_GROUP // 2):
            h0 = 2 * pair
            slab = jnp.concatenate(
                [ot_g[:, (h0 + e) * BLOCK:(h0 + e + 1) * BLOCK] * inv_ref[u % 2, h0 + e]
                 for e in range(2)], axis=0)
            col = (g * SWA_GROUP + h0) * SWA_HEAD_DIM
            o_ref[i * BLOCK:(i + 1) * BLOCK, col:col + 2 * SWA_HEAD_DIM] = (
                slab.T.astype(BF16))

    n = len(units)
    for t in range(n + 2):
        if 0 <= t - 2 < n:
            stage_pv(t - 2)
        if 0 <= t - 1 < n:
            stage_softmax(t - 1)
        if t < n:
            stage_scores(t)


def _swa_call(sinks, qat, ka, vat, bias_tab, *, batch, seq):
    step = TM_SWA
    blocks_per_step = step // BLOCK
    steps_per_seq = seq // step
    nb = seq // BLOCK
    prev_blk = lambda b, s: b * nb + jnp.maximum(s * blocks_per_step - 1, 0)
    return pl.pallas_call(
        functools.partial(_swa_kernel, blocks_per_step=blocks_per_step),
        out_shape=jax.ShapeDtypeStruct((batch * seq, SWA_Q_COLS), BF16),
        grid=(batch, steps_per_seq),
        in_specs=[
            pl.BlockSpec(memory_space=pltpu.SMEM),
            pl.BlockSpec((SWA_Q_COLS, step), lambda b, s: (0, b * steps_per_seq + s)),
            pl.BlockSpec((BLOCK, SWA_KV_COLS), lambda b, s: (prev_blk(b, s), 0)),
            pl.BlockSpec((step, SWA_KV_COLS), lambda b, s: (b * steps_per_seq + s, 0)),
            pl.BlockSpec((SWA_KV_COLS, BLOCK), lambda b, s: (0, prev_blk(b, s))),
            pl.BlockSpec((SWA_KV_COLS, step), lambda b, s: (0, b * steps_per_seq + s)),
            _const_spec((2, SWA_HEADS, 2 * BLOCK, BLOCK)),
        ],
        out_specs=pl.BlockSpec((step, SWA_Q_COLS), lambda b, s: (b * steps_per_seq + s, 0)),
        scratch_shapes=[
            pltpu.VMEM((2, 2 * BLOCK, SWA_GROUP * BLOCK), F32),
            pltpu.VMEM((2, 2 * BLOCK, SWA_GROUP * BLOCK), BF16),
            pltpu.VMEM((2, SWA_GROUP, 1, BLOCK), F32),
        ],
        compiler_params=_params(("parallel", "arbitrary")),
        name="swa",
    )(sinks, qat, ka, ka, vat, vat, bias_tab)


def _mla_kernel(q_ref, k_ref, vt_ref, o_ref, s_ref, p_ref, al_ref, m_ref, l_ref, acc_ref,
                *, tile, n_tiles):
    pairs = [(qi, j) for qi in range(n_tiles) for j in range(qi + 1)]

    def stage_scores(u):
        qi, j = pairs[u]
        k = k_ref[j * tile:(j + 1) * tile, :]
        q = q_ref[qi * tile:(qi + 1) * tile, :]
        s_ref[u % 2] = lax.dot_general(k, q, (((1,), (1,)), ((), ())),
                                       preferred_element_type=F32)

    def stage_softmax(u):
        qi, j = pairs[u]
        st = s_ref[u % 2]
        if j == qi:
            kk = lax.broadcasted_iota(jnp.int32, st.shape, 0)
            qq = lax.broadcasted_iota(jnp.int32, st.shape, 1)
            st = jnp.where(kk <= qq, st, -jnp.inf)
        cmax = jnp.max(st, axis=0, keepdims=True)
        if j == 0:
            m_new = cmax
            p = jnp.exp2(st - m_new)
            l_new = jnp.sum(p, axis=0, keepdims=True)
        else:
            m_old = m_ref[qi % 2]
            m_new = jnp.maximum(m_old, cmax)
            alpha = jnp.exp2(m_old - m_new)
            p = jnp.exp2(st - m_new)
            l_new = alpha * l_ref[qi % 2] + jnp.sum(p, axis=0, keepdims=True)
            al_ref[u % 2] = alpha
        m_ref[qi % 2] = m_new
        l_ref[qi % 2] = l_new
        p_ref[u % 2] = p.astype(BF16)

    def stage_pv(u):
        qi, j = pairs[u]
        pv = jnp.dot(vt_ref[0, 0, j], p_ref[u % 2], preferred_element_type=F32)
        if j == 0:
            acc = pv
        else:
            acc = al_ref[u % 2] * acc_ref[...] + pv
        if j == qi:
            o_ref[qi * tile:(qi + 1) * tile, :] = (acc / l_ref[qi % 2]).T.astype(BF16)
        else:
            acc_ref[...] = acc

    n = len(pairs)
    for t in range(n + 2):
        if 0 <= t - 2 < n:
            stage_pv(t - 2)
        if 0 <= t - 1 < n:
            stage_softmax(t - 1)
        if t < n:
            stage_scores(t)


def _mla_call(qb, kcat, vt, *, batch, seq):
    tile = TM_PROJ
    n_tiles = seq // tile
    return pl.pallas_call(
        functools.partial(_mla_kernel, tile=tile, n_tiles=n_tiles),
        out_shape=jax.ShapeDtypeStruct((batch * seq, MLA_HEADS * MLA_V_DIM), BF16),
        grid=(batch, MLA_HEADS),
        in_specs=[
            pl.BlockSpec((seq, MLA_QK_PAD), lambda b, h: (b, h)),
            pl.BlockSpec((seq, MLA_QK_PAD), lambda b, h: (b, h)),
            pl.BlockSpec((1, 1, n_tiles, MLA_V_DIM, tile), lambda b, h: (b, h, 0, 0, 0)),
        ],
        out_specs=pl.BlockSpec((seq, MLA_V_DIM), lambda b, h: (b, h)),
        scratch_shapes=[
            pltpu.VMEM((2, tile, tile), F32),
            pltpu.VMEM((2, tile, tile), BF16),
            pltpu.VMEM((2, 1, tile), F32),
            pltpu.VMEM((2, 1, tile), F32),
            pltpu.VMEM((2, 1, tile), F32),
            pltpu.VMEM((MLA_V_DIM, tile), F32),
        ],
        compiler_params=_params(("parallel", "parallel")),
        name="mla",
    )(qb, kcat, vt)


def _out_kernel(x_ref, oa_ref, ob_ref, mod_ref, g_ref, wa_ref, wb_ref, x1_ref, h_ref):
    mod = mod_ref[0]
    g1 = mod[2:3]
    sh2 = mod[3:4]
    sc2 = mod[4:5]
    y = (jnp.dot(oa_ref[...], wa_ref[...], preferred_element_type=F32)
         + jnp.dot(ob_ref[...], wb_ref[...], preferred_element_type=F32))
    x1 = x_ref[...] + g1 * y
    x1_ref[...] = x1
    h_ref[...] = ((_rms(x1) * g_ref[...]) * (1.0 + sc2) + sh2).astype(BF16)


def _out_call(x2d, oa, ob, mod3, g_mlp, w_out_a, w_out_b, *, seq):
    tm = TM_OUT
    t = x2d.shape[0]
    tiles_per_seq = seq // tm
    tok = lambda i: (i, 0)
    return pl.pallas_call(
        _out_kernel,
        out_shape=(jax.ShapeDtypeStruct((t, D_MODEL), F32),
                   jax.ShapeDtypeStruct((t, D_MODEL), BF16)),
        grid=(t // tm,),
        in_specs=[
            pl.BlockSpec((tm, D_MODEL), tok),
            pl.BlockSpec((tm, SWA_Q_COLS), tok),
            pl.BlockSpec((tm, MLA_HEADS * MLA_V_DIM), tok),
            pl.BlockSpec((1, 6, D_MODEL), lambda i: (i // tiles_per_seq, 0, 0)),
            _const_spec((1, D_MODEL)),
            _const_spec((SWA_Q_COLS, D_MODEL)),
            _const_spec((MLA_HEADS * MLA_V_DIM, D_MODEL)),
        ],
        out_specs=(pl.BlockSpec((tm, D_MODEL), tok), pl.BlockSpec((tm, D_MODEL), tok)),
        compiler_params=_params(("parallel",)),
        name="out_proj",
    )(x2d, oa, ob, mod3, g_mlp, w_out_a, w_out_b)


def _mlp_kernel(x_ref, h_ref, mod_ref, gf_ref, w1_ref, w2_ref, o_ref):
    j = pl.program_id(1)

    @pl.when(j == 0)
    def _():
        o_ref[...] = jnp.zeros(o_ref.shape, F32)

    u = jnp.maximum(jnp.dot(h_ref[...], w1_ref[...], preferred_element_type=F32), 0.0)
    o_ref[...] += jnp.dot((u * u).astype(BF16), w2_ref[...], preferred_element_type=F32)

    @pl.when(j == pl.num_programs(1) - 1)
    def _():
        g2 = mod_ref[0][5:6]
        x2 = x_ref[...] + g2 * o_ref[...]
        o_ref[...] = _rms(x2) * gf_ref[...]


def _mlp_call(x1, h2, mod3, g_final, w1, w2, *, seq):
    tm = TM_MLP
    tf = TF_MLP
    t = x1.shape[0]
    tiles_per_seq = seq // tm
    return pl.pallas_call(
        _mlp_kernel,
        out_shape=jax.ShapeDtypeStruct((t, D_MODEL), F32),
        grid=(t // tm, D_FF // tf),
        in_specs=[
            pl.BlockSpec((tm, D_MODEL), lambda i, j: (i, 0)),
            pl.BlockSpec((tm, D_MODEL), lambda i, j: (i, 0)),
            pl.BlockSpec((1, 6, D_MODEL), lambda i, j: (i // tiles_per_seq, 0, 0)),
            _const_spec((1, D_MODEL)),
            pl.BlockSpec((D_MODEL, tf), lambda i, j: (0, j)),
            pl.BlockSpec((tf, D_MODEL), lambda i, j: (j, 0)),
        ],
        out_specs=pl.BlockSpec((tm, D_MODEL), lambda i, j: (i, 0)),
        compiler_params=_params(("parallel", "arbitrary")),
        name="mlp",
    )(x1, h2, mod3, g_final, w1, w2)


def _t5_bucket_table():
    q_loc = np.arange(BLOCK)[:, None]
    k_loc = np.arange(2 * BLOCK)[None, :]
    n = np.maximum(q_loc + BLOCK - k_loc, 0)
    max_exact = REL_BUCKETS // 2
    nf = np.maximum(n, 1).astype(np.float64)
    large = max_exact + (np.log(nf / max_exact) / math.log(REL_MAX_DIST / max_exact)
                         * (REL_BUCKETS - max_exact)).astype(np.int32)
    large = np.minimum(large, REL_BUCKETS - 1)
    return np.where(n < max_exact, n, large).astype(np.int32)


def _rope_tables(seq):
    half = MLA_ROPE_DIM // 2
    inv_freq = ROPE_THETA ** (-np.arange(half, dtype=np.float64) / half)
    ang = np.arange(seq, dtype=np.float64)[:, None] * inv_freq[None, :]
    zeros = np.zeros((seq, 128 - MLA_ROPE_DIM))
    cos_tab = np.concatenate([np.cos(ang), np.cos(ang), zeros], axis=1)
    sin_tab = np.concatenate([np.sin(ang), np.sin(ang), zeros], axis=1)
    return cos_tab.astype(np.float32), sin_tab.astype(np.float32)


def _rot_cols(w):
    half = w.shape[-1] // 2
    return jnp.concatenate([-w[..., half:], w[..., :half]], axis=-1)


def kernel(x, c, w_mod, b_mod, attn_norm_g, w_in, swa_sinks, rel_bias, mla_q_norm_g, w_uq,
           mla_kv_norm_g, w_ukv, w_out, mlp_norm_g, w_ff1, w_ff2, final_norm_g):
    batch, seq, _ = x.shape
    depth = w_mod.shape[0]
    assert depth == 1
    t = batch * seq
    x2d = x.reshape(t, D_MODEL)
    l = 0

    w_kr = w_in[l][:, OFF_MLA_KR:OFF_MLA_KR + MLA_ROPE_DIM]
    w_main = jnp.concatenate(
        [w_in[l][:, OFF_SWA_K:OFF_SWA_V], w_in[l][:, OFF_MLA_CQ:OFF_MLA_KR],
         w_kr, _rot_cols(w_kr)], axis=1).astype(BF16)
    w_t = jnp.concatenate(
        [w_in[l][:, :SWA_Q_COLS], w_in[l][:, OFF_SWA_V:OFF_MLA_CQ]], axis=1).T.astype(BF16)

    wq = w_uq[l].reshape(MLA_Q_RANK, MLA_HEADS, MLA_QK_DIM)
    wq_nope = wq[..., :MLA_NOPE_DIM]
    wq_rope = wq[..., MLA_NOPE_DIM:]
    hh = MLA_HEADS // 2
    zq = jnp.zeros((MLA_Q_RANK, hh, 128 - MLA_ROPE_DIM), F32)
    wq_main = jnp.concatenate(
        [jnp.concatenate([wq_nope[:, :hh], wq_rope[:, :hh], zq], axis=-1),
         jnp.concatenate([wq_nope[:, hh:], zq, wq_rope[:, hh:]], axis=-1)], axis=1).reshape(
        MLA_Q_RANK, MLA_HEADS * MLA_QK_PAD).astype(BF16)
    wq_rot_all = _rot_cols(wq_rope)
    wq_rot = jnp.concatenate([wq_rot_all[:, :hh], wq_rot_all[:, hh:]], axis=-1).reshape(
        MLA_Q_RANK, hh * 128).astype(BF16)

    wkv = w_ukv[l].reshape(MLA_KV_RANK, MLA_HEADS, MLA_NOPE_DIM + MLA_V_DIM)
    w_uk = wkv[..., :MLA_NOPE_DIM].reshape(MLA_KV_RANK, MLA_HEADS * MLA_NOPE_DIM).astype(BF16)
    w_vt = wkv[..., MLA_NOPE_DIM:].reshape(MLA_KV_RANK, MLA_HEADS * MLA_V_DIM).T.astype(BF16)

    w_out_a = w_out[l][:SWA_Q_COLS].astype(BF16)
    w_out_b = w_out[l][SWA_Q_COLS:].astype(BF16)
    w1 = w_ff1[l].astype(BF16)
    w2 = w_ff2[l].astype(BF16)

    cos_tab, sin_tab = _rope_tables(seq)

    c8 = jnp.pad(c, ((0, 8 - batch), (0, 0)))
    mod = _mod_call(c8, w_mod[l], b_mod[l].reshape(1, -1))[:batch]
    mod3 = mod.reshape(batch, 6, D_MODEL)

    bias_tab = _bias_call(rel_bias, _t5_bucket_table().T)

    qat, ka, vat, qb, kcat, vt = _proj_call(
        x2d, mod3, attn_norm_g[l].reshape(1, -1), w_main, w_t,
        mla_q_norm_g[l].reshape(1, -1), mla_kv_norm_g[l].reshape(1, -1),
        wq_main, wq_rot, w_uk, w_vt, cos_tab, sin_tab, batch=batch, seq=seq)

    oa = _swa_call(swa_sinks[l], qat, ka, vat, bias_tab, batch=batch, seq=seq)
    ob = _mla_call(qb, kcat, vt, batch=batch, seq=seq)

    x1, h2 = _out_call(x2d, oa, ob, mod3, mlp_norm_g[l].reshape(1, -1), w_out_a, w_out_b,
                       seq=seq)
    out = _mlp_call(x1, h2, mod3, final_norm_g.reshape(1, -1), w1, w2, seq=seq)
    return out.reshape(batch, seq, D_MODEL)
```

```python
import functools
import math

import jax
import jax.numpy as jnp
import numpy as np
from jax import lax
from jax.experimental import pallas as pl
from jax.experimental.pallas import tpu as pltpu

F32 = jnp.float32
BF16 = jnp.bfloat16

D_MODEL = 2048
BLOCK = 128
EPS = 1e-6

SWA_HEADS = 16
SWA_KV_HEADS = 2
SWA_HEAD_DIM = 64
SWA_GROUP = SWA_HEADS // SWA_KV_HEADS
WINDOW = 128
REL_BUCKETS = 32
REL_MAX_DIST = 128

MLA_HEADS = 8
MLA_Q_RANK = 384
MLA_KV_RANK = 128
MLA_NOPE_DIM = 128
MLA_ROPE_DIM = 64
MLA_V_DIM = 128
MLA_QK_DIM = MLA_NOPE_DIM + MLA_ROPE_DIM
MLA_QK_PAD = 256
ROPE_THETA = 10000.0
MLA_Q_SCALE = MLA_QK_DIM ** -0.5 * math.log2(math.e)
D_FF = 4 * D_MODEL

SWA_Q_COLS = SWA_HEADS * SWA_HEAD_DIM
SWA_KV_COLS = SWA_KV_HEADS * SWA_HEAD_DIM
OFF_SWA_K = SWA_Q_COLS
OFF_SWA_V = OFF_SWA_K + SWA_KV_COLS
OFF_MLA_CQ = OFF_SWA_V + SWA_KV_COLS
OFF_MLA_CKV = OFF_MLA_CQ + MLA_Q_RANK
OFF_MLA_KR = OFF_MLA_CKV + MLA_KV_RANK
P_CQ = SWA_KV_COLS
P_CKV = P_CQ + MLA_Q_RANK
P_KR = P_CKV + MLA_KV_RANK
P_COLS = P_KR + 128
PT_ROWS = SWA_Q_COLS + SWA_KV_COLS
LOG2E = math.log2(math.e)

VMEM_LIMIT_BYTES = 62 * 1024 * 1024

TM_PROJ = 512
TM_SWA = 1024
TM_OUT = 512
TM_MLP = 512
TF_MLP = 2048
TN_MOD = 1536
BIAS_HEADS_PER_STEP = 4
ROW_CHUNKS = 2


def _params(sem):
    return pltpu.CompilerParams(dimension_semantics=sem, vmem_limit_bytes=VMEM_LIMIT_BYTES)


def _const_spec(shape):
    nd = len(shape)
    return pl.BlockSpec(shape, lambda *_: (0,) * nd, pipeline_mode=pl.Buffered(1))


def _rms(x):
    return x * lax.rsqrt(jnp.mean(x * x, axis=-1, keepdims=True) + EPS)


def _mod_kernel(c_ref, w_ref, b_ref, o_ref):
    c = c_ref[...]
    ca = c * (1.0 / (1.0 + jnp.exp(-c)))
    o_ref[...] = jnp.dot(ca.astype(BF16), w_ref[...].astype(BF16),
                         preferred_element_type=F32) + b_ref[...]


def _mod_call(c8, w_mod, b_mod):
    n = w_mod.shape[1]
    return pl.pallas_call(
        _mod_kernel,
        out_shape=jax.ShapeDtypeStruct((8, n), F32),
        grid=(n // TN_MOD,),
        in_specs=[
            pl.BlockSpec((8, D_MODEL), lambda j: (0, 0)),
            pl.BlockSpec((D_MODEL, TN_MOD), lambda j: (0, j)),
            pl.BlockSpec((1, TN_MOD), lambda j: (0, j)),
        ],
        out_specs=pl.BlockSpec((8, TN_MOD), lambda j: (0, j)),
        compiler_params=_params(("arbitrary",)),
        name="mod",
    )(c8, w_mod, b_mod)


def _bias_kernel(rel_ref, bucket_ref, o_ref):
    bucket = bucket_ref[...]
    k_loc = lax.broadcasted_iota(jnp.int32, bucket.shape, 0)
    q_loc = lax.broadcasted_iota(jnp.int32, bucket.shape, 1)
    dist = q_loc + BLOCK - k_loc
    in_window = (dist >= 0) & (dist < WINDOW)
    for i in range(BIAS_HEADS_PER_STEP):
        h = pl.program_id(0) * BIAS_HEADS_PER_STEP + i
        acc = jnp.zeros(bucket.shape, F32)
        for k in range(REL_BUCKETS):
            acc = jnp.where(bucket == k, rel_ref[k, h], acc)
        acc = acc * LOG2E
        o_ref[0, i] = jnp.where(in_window & (k_loc >= BLOCK), acc, -jnp.inf)
        o_ref[1, i] = jnp.where(in_window, acc, -jnp.inf)


def _bias_call(rel_bias, bucket_t):
    hs = BIAS_HEADS_PER_STEP
    return pl.pallas_call(
        _bias_kernel,
        out_shape=jax.ShapeDtypeStruct((2, SWA_HEADS, 2 * BLOCK, BLOCK), F32),
        grid=(SWA_HEADS // hs,),
        in_specs=[
            pl.BlockSpec(memory_space=pltpu.SMEM),
            pl.BlockSpec((2 * BLOCK, BLOCK), lambda s: (0, 0)),
        ],
        out_specs=pl.BlockSpec((2, hs, 2 * BLOCK, BLOCK), lambda s: (0, s, 0, 0)),
        compiler_params=_params(("arbitrary",)),
        name="t5_bias",
    )(rel_bias, bucket_t)


def _proj_kernel(x_ref, mod_ref, g_ref, win_ref, wt_ref, gq_ref, gkv_ref, wqm_ref, wqr_ref,
                 wuk_ref, wvt_ref, cos_ref, sin_ref,
                 qat_ref, ka_ref, vat_ref, qb_ref, kc_ref, vt_ref, *, q_scale):
    x = x_ref[...]
    mod = mod_ref[0]
    sh1 = mod[0:1]
    sc1 = mod[1:2]
    h = ((_rms(x) * g_ref[...]) * (1.0 + sc1) + sh1).astype(BF16)
    proj = jnp.dot(h, win_ref[...], preferred_element_type=F32)
    proj_t = lax.dot_general(wt_ref[...], h, (((1,), (1,)), ((), ())),
                             preferred_element_type=F32)

    qat_ref[...] = (proj_t[:SWA_Q_COLS] * (SWA_HEAD_DIM ** -0.5 * LOG2E)).astype(BF16)
    vat_ref[...] = proj_t[SWA_Q_COLS:].astype(BF16)
    ka_ref[...] = proj[:, :P_CQ].astype(BF16)

    cq = (_rms(proj[:, P_CQ:P_CKV]) * gq_ref[...]).astype(BF16)
    ckv = (_rms(proj[:, P_CKV:P_KR]) * gkv_ref[...]).astype(BF16)
    cos_lo = cos_ref[...]
    sin_lo = sin_ref[...]
    cos_hi = pltpu.roll(cos_lo, MLA_ROPE_DIM, 1)
    sin_hi = pltpu.roll(sin_lo, MLA_ROPE_DIM, 1)
    kr = proj[:, P_KR:P_COLS]
    kr_sw = pltpu.roll(kr, MLA_ROPE_DIM, 1)
    krope_lo = (kr * cos_lo + kr_sw * sin_lo).astype(BF16)
    krope_hi = (kr_sw * cos_hi + kr * sin_hi).astype(BF16)

    qmain = jnp.dot(cq, wqm_ref[...], preferred_element_type=F32)
    qrot = jnp.dot(cq, wqr_ref[...], preferred_element_type=F32)
    knope = jnp.dot(ckv, wuk_ref[...], preferred_element_type=F32)
    vt = lax.dot_general(wvt_ref[...], ckv, (((1,), (1,)), ((), ())),
                         preferred_element_type=F32)

    half_heads = MLA_HEADS // 2
    for hd in range(MLA_HEADS):
        lo = hd * MLA_QK_PAD
        mid = lo + MLA_NOPE_DIM
        hi = lo + MLA_QK_PAD
        low = hd < half_heads
        cos_t, sin_t = (cos_lo, sin_lo) if low else (cos_hi, sin_hi)
        rot = qrot[:, (hd % half_heads) * 128:(hd % half_heads + 1) * 128]
        qb_ref[:, lo:mid] = (qmain[:, lo:mid] * q_scale).astype(BF16)
        qb_ref[:, mid:hi] = ((qmain[:, mid:hi] * cos_t + rot * sin_t) * q_scale).astype(BF16)
        kc_ref[:, lo:mid] = knope[:, hd * MLA_NOPE_DIM:(hd + 1) * MLA_NOPE_DIM].astype(BF16)
        kc_ref[:, mid:hi] = krope_lo if low else krope_hi
        vt_ref[0, hd, 0] = vt[hd * MLA_V_DIM:(hd + 1) * MLA_V_DIM].astype(BF16)


def _proj_call(x2d, mod3, g_attn, w_main, w_t, gq, gkv, wq_main, wq_rot, w_uk, w_vt,
               cos_tab, sin_tab, *, batch, seq):
    tm = TM_PROJ
    tiles_per_seq = seq // tm
    t = batch * seq
    tok = lambda i: (i, 0)
    tok_t = lambda i: (0, i)
    pos = lambda i: (i % tiles_per_seq, 0)
    out_shape = (
        jax.ShapeDtypeStruct((SWA_Q_COLS, t), BF16),
        jax.ShapeDtypeStruct((t, SWA_KV_COLS), BF16),
        jax.ShapeDtypeStruct((SWA_KV_COLS, t), BF16),
        jax.ShapeDtypeStruct((t, MLA_HEADS * MLA_QK_PAD), BF16),
        jax.ShapeDtypeStruct((t, MLA_HEADS * MLA_QK_PAD), BF16),
        jax.ShapeDtypeStruct((batch, MLA_HEADS, tiles_per_seq, MLA_V_DIM, tm), BF16),
    )
    return pl.pallas_call(
        functools.partial(_proj_kernel, q_scale=MLA_Q_SCALE),
        out_shape=out_shape,
        grid=(t // tm,),
        in_specs=[
            pl.BlockSpec((tm, D_MODEL), tok),
            pl.BlockSpec((1, 6, D_MODEL), lambda i: (i // tiles_per_seq, 0, 0)),
            _const_spec((1, D_MODEL)),
            _const_spec((D_MODEL, P_COLS)),
            _const_spec((PT_ROWS, D_MODEL)),
            _const_spec((1, MLA_Q_RANK)),
            _const_spec((1, MLA_KV_RANK)),
            _const_spec((MLA_Q_RANK, MLA_HEADS * MLA_QK_PAD)),
            _const_spec((MLA_Q_RANK, MLA_HEADS // 2 * 128)),
            _const_spec((MLA_KV_RANK, MLA_HEADS * MLA_NOPE_DIM)),
            _const_spec((MLA_HEADS * MLA_V_DIM, MLA_KV_RANK)),
            pl.BlockSpec((tm, 128), pos),
            pl.BlockSpec((tm, 128), pos),
        ],
        out_specs=(
            pl.BlockSpec((SWA_Q_COLS, tm), tok_t),
            pl.BlockSpec((tm, SWA_KV_COLS), tok),
            pl.BlockSpec((SWA_KV_COLS, tm), tok_t),
            pl.BlockSpec((tm, MLA_HEADS * MLA_QK_PAD), tok),
            pl.BlockSpec((tm, MLA_HEADS * MLA_QK_PAD), tok),
            pl.BlockSpec((1, MLA_HEADS, 1, MLA_V_DIM, tm),
                         lambda i: (i // tiles_per_seq, 0, i % tiles_per_seq, 0, 0)),
        ),
        compiler_params=_params(("parallel",)),
        name="in_proj",
    )(x2d, mod3, g_attn, w_main, w_t, gq, gkv, wq_main, wq_rot, w_uk, w_vt,
      cos_tab, sin_tab)


def _swa_kernel(sinks_ref, qt_ref, kp_ref, kc_ref, vtp_ref, vtc_ref, bias_ref, o_ref,
                s_ref, p_ref, inv_ref, *, blocks_per_step):
    first_step = pl.program_id(1) == 0
    k_all = jnp.concatenate([kp_ref[...], kc_ref[...]], axis=0)
    vt_all = jnp.concatenate([vtp_ref[...], vtc_ref[...]], axis=1)
    zeros_q = jnp.zeros((SWA_HEAD_DIM, SWA_GROUP * BLOCK), BF16)
    units = [(i, g) for i in range(blocks_per_step) for g in range(SWA_KV_HEADS)]

    def stage_scores(u):
        i, g = units[u]
        k_band = k_all[i * BLOCK:(i + 2) * BLOCK]
        qt = qt_ref[:, i * BLOCK:(i + 1) * BLOCK]
        q_g = jnp.concatenate(
            [qt[(g * SWA_GROUP + hh) * SWA_HEAD_DIM:(g * SWA_GROUP + hh + 1) * SWA_HEAD_DIM]
             for hh in range(SWA_GROUP)], axis=1)
        rhs = jnp.concatenate([q_g, zeros_q] if g == 0 else [zeros_q, q_g], axis=0)
        s_ref[u % 2] = jnp.dot(k_band, rhs, preferred_element_type=F32)

    def stage_softmax(u):
        i, g = units[u]
        variant = jnp.where(first_step, 0, 1) if i == 0 else 1
        for hh in range(SWA_GROUP):
            hd = g * SWA_GROUP + hh
            st = s_ref[u % 2, :, hh * BLOCK:(hh + 1) * BLOCK] + bias_ref[variant, hd]
            sink = sinks_ref[hd] * LOG2E
            m = jnp.maximum(jnp.max(st, axis=0, keepdims=True), sink)
            p = jnp.exp2(st - m)
            denom = jnp.sum(p, axis=0, keepdims=True) + jnp.exp2(sink - m)
            inv_ref[u % 2, hh] = 1.0 / denom
            p_ref[u % 2, :, hh * BLOCK:(hh + 1) * BLOCK] = p.astype(BF16)

    def stage_pv(u):
        i, g = units[u]
        vt_band = vt_all[g * SWA_HEAD_DIM:(g + 1) * SWA_HEAD_DIM, i * BLOCK:(i + 2) * BLOCK]
        ot_g = jnp.dot(vt_band, p_ref[u % 2], preferred_element_type=F32)
        for pair in range(SWA_GROUP // 2):
            h0 = 2 * pair
            slab = jnp.concatenate(
                [ot_g[:, (h0 + e) * BLOCK:(h0 + e + 1) * BLOCK] * inv_ref[u % 2, h0 + e]
                 for e in range(2)], axis=0)
            col = (g * SWA_GROUP + h0) * SWA_HEAD_DIM
            o_ref[i * BLOCK:(i + 1) * BLOCK, col:col + 2 * SWA_HEAD_DIM] = (
                slab.T.astype(BF16))

    n = len(units)
    for t in range(n + 2):
        if 0 <= t - 2 < n:
            stage_pv(t - 2)
        if 0 <= t - 1 < n:
            stage_softmax(t - 1)
        if t < n:
            stage_scores(t)


def _swa_call(sinks, qat, ka, vat, bias_tab, *, batch, seq):
    step = TM_SWA
    blocks_per_step = step // BLOCK
    steps_per_seq = seq // step
    nb = seq // BLOCK
    prev_blk = lambda b, s: b * nb + jnp.maximum(s * blocks_per_step - 1, 0)
    return pl.pallas_call(
        functools.partial(_swa_kernel, blocks_per_step=blocks_per_step),
        out_shape=jax.ShapeDtypeStruct((batch * seq, SWA_Q_COLS), BF16),
        grid=(batch, steps_per_seq),
        in_specs=[
            pl.BlockSpec(memory_space=pltpu.SMEM),
            pl.BlockSpec((SWA_Q_COLS, step), lambda b, s: (0, b * steps_per_seq + s)),
            pl.BlockSpec((BLOCK, SWA_KV_COLS), lambda b, s: (prev_blk(b, s), 0)),
            pl.BlockSpec((step, SWA_KV_COLS), lambda b, s: (b * steps_per_seq + s, 0)),
            pl.BlockSpec((SWA_KV_COLS, BLOCK), lambda b, s: (0, prev_blk(b, s))),
            pl.BlockSpec((SWA_KV_COLS, step), lambda b, s: (0, b * steps_per_seq + s)),
            _const_spec((2, SWA_HEADS, 2 * BLOCK, BLOCK)),
        ],
        out_specs=pl.BlockSpec((step, SWA_Q_COLS), lambda b, s: (b * steps_per_seq + s, 0)),
        scratch_shapes=[
            pltpu.VMEM((2, 2 * BLOCK, SWA_GROUP * BLOCK), F32),
            pltpu.VMEM((2, 2 * BLOCK, SWA_GROUP * BLOCK), BF16),
            pltpu.VMEM((2, SWA_GROUP, 1, BLOCK), F32),
        ],
        compiler_params=_params(("parallel", "arbitrary")),
        name="swa",
    )(sinks, qat, ka, ka, vat, vat, bias_tab)


def _mla_kernel(q_ref, k_ref, vt_ref, o_ref, s_ref, cm_ref, p_ref, al_ref, m_ref, l_ref, acc_ref,
                *, tile, n_tiles):
    pairs = [(qi, j) for qi in range(n_tiles) for j in range(qi + 1)]

    def stage_scores(u):
        qi, j = pairs[u]
        k = k_ref[j * tile:(j + 1) * tile, :]
        q = q_ref[qi * tile:(qi + 1) * tile, :]
        st = lax.dot_general(k, q, (((1,), (1,)), ((), ())), preferred_element_type=F32)
        if j == qi:
            kk = lax.broadcasted_iota(jnp.int32, st.shape, 0)
            qq = lax.broadcasted_iota(jnp.int32, st.shape, 1)
            st = jnp.where(kk <= qq, st, -jnp.inf)
        cm_ref[u % 2] = jnp.max(st, axis=0, keepdims=True)
        s_ref[u % 2] = st

    def stage_softmax(u):
        qi, j = pairs[u]
        st = s_ref[u % 2]
        cmax = cm_ref[u % 2]
        if j == 0:
            m_new = cmax
            p = jnp.exp2(st - m_new)
            l_new = jnp.sum(p, axis=0, keepdims=True)
        else:
            m_old = m_ref[qi % 2]
            m_new = jnp.maximum(m_old, cmax)
            alpha = jnp.exp2(m_old - m_new)
            p = jnp.exp2(st - m_new)
            l_new = alpha * l_ref[qi % 2] + jnp.sum(p, axis=0, keepdims=True)
            al_ref[u % 2] = alpha
        m_ref[qi % 2] = m_new
        l_ref[qi % 2] = l_new
        p_ref[u % 2] = p.astype(BF16)

    def stage_pv(u):
        qi, j = pairs[u]
        pv = jnp.dot(vt_ref[0, 0, j], p_ref[u % 2], preferred_element_type=F32)
        if j == 0:
            acc = pv
        else:
            acc = al_ref[u % 2] * acc_ref[...] + pv
        if j == qi:
            o_ref[qi * tile:(qi + 1) * tile, :] = (acc / l_ref[qi % 2]).T.astype(BF16)
        else:
            acc_ref[...] = acc

    n = len(pairs)
    for t in range(n + 2):
        if 0 <= t - 2 < n:
            stage_pv(t - 2)
        if 0 <= t - 1 < n:
            stage_softmax(t - 1)
        if t < n:
            stage_scores(t)


def _mla_call(qb, kcat, vt, *, batch, seq):
    tile = TM_PROJ
    n_tiles = seq // tile
    return pl.pallas_call(
        functools.partial(_mla_kernel, tile=tile, n_tiles=n_tiles),
        out_shape=jax.ShapeDtypeStruct((batch * seq, MLA_HEADS * MLA_V_DIM), BF16),
        grid=(batch, MLA_HEADS),
        in_specs=[
            pl.BlockSpec((seq, MLA_QK_PAD), lambda b, h: (b, h)),
            pl.BlockSpec((seq, MLA_QK_PAD), lambda b, h: (b, h)),
            pl.BlockSpec((1, 1, n_tiles, MLA_V_DIM, tile), lambda b, h: (b, h, 0, 0, 0)),
        ],
        out_specs=pl.BlockSpec((seq, MLA_V_DIM), lambda b, h: (b, h)),
        scratch_shapes=[
            pltpu.VMEM((2, tile, tile), F32),
            pltpu.VMEM((2, 1, tile), F32),
            pltpu.VMEM((2, tile, tile), BF16),
            pltpu.VMEM((2, 1, tile), F32),
            pltpu.VMEM((2, 1, tile), F32),
            pltpu.VMEM((2, 1, tile), F32),
            pltpu.VMEM((MLA_V_DIM, tile), F32),
        ],
        compiler_params=_params(("parallel", "parallel")),
        name="mla",
    )(qb, kcat, vt)


def _out_kernel(x_ref, oa_ref, ob_ref, mod_ref, g_ref, wa_ref, wb_ref, x1_ref, h_ref):
    mod = mod_ref[0]
    g1 = mod[2:3]
    sh2 = mod[3:4]
    sc2 = mod[4:5]
    y = (jnp.dot(oa_ref[...], wa_ref[...], preferred_element_type=F32)
         + jnp.dot(ob_ref[...], wb_ref[...], preferred_element_type=F32))
    x1 = x_ref[...] + g1 * y
    x1_ref[...] = x1
    h_ref[...] = ((_rms(x1) * g_ref[...]) * (1.0 + sc2) + sh2).astype(BF16)


def _out_call(x2d, oa, ob, mod3, g_mlp, w_out_a, w_out_b, *, seq):
    tm = TM_OUT
    t = x2d.shape[0]
    tiles_per_seq = seq // tm
    tok = lambda i: (i, 0)
    return pl.pallas_call(
        _out_kernel,
        out_shape=(jax.ShapeDtypeStruct((t, D_MODEL), F32),
                   jax.ShapeDtypeStruct((t, D_MODEL), BF16)),
        grid=(t // tm,),
        in_specs=[
            pl.BlockSpec((tm, D_MODEL), tok),
            pl.BlockSpec((tm, SWA_Q_COLS), tok),
            pl.BlockSpec((tm, MLA_HEADS * MLA_V_DIM), tok),
            pl.BlockSpec((1, 6, D_MODEL), lambda i: (i // tiles_per_seq, 0, 0)),
            _const_spec((1, D_MODEL)),
            _const_spec((SWA_Q_COLS, D_MODEL)),
            _const_spec((MLA_HEADS * MLA_V_DIM, D_MODEL)),
        ],
        out_specs=(pl.BlockSpec((tm, D_MODEL), tok), pl.BlockSpec((tm, D_MODEL), tok)),
        compiler_params=_params(("parallel",)),
        name="out_proj",
    )(x2d, oa, ob, mod3, g_mlp, w_out_a, w_out_b)


def _mlp_kernel(x_ref, h_ref, mod_ref, gf_ref, w1_ref, w2_ref, o_ref):
    j = pl.program_id(1)

    @pl.when(j == 0)
    def _():
        o_ref[...] = jnp.zeros(o_ref.shape, F32)

    def ff_tile(sl):
        u = jnp.maximum(jnp.dot(h_ref[sl, :], w1_ref[...], preferred_element_type=F32), 0.0)
        return jnp.dot((u * u).astype(BF16), w2_ref[...], preferred_element_type=F32)

    last = pl.num_programs(1) - 1

    @pl.when(j < last)
    def _():
        o_ref[...] += ff_tile(slice(None))

    @pl.when(j == last)
    def _():
        g2 = mod_ref[0][5:6]
        rows = o_ref.shape[0] // ROW_CHUNKS
        for r in range(ROW_CHUNKS):
            sl = slice(r * rows, (r + 1) * rows)
            x2 = x_ref[sl, :] + g2 * (o_ref[sl, :] + ff_tile(sl))
            o_ref[sl, :] = _rms(x2) * gf_ref[...]


def _mlp_call(x1, h2, mod3, g_final, w1, w2, *, seq):
    tm = TM_MLP
    tf = TF_MLP
    t = x1.shape[0]
    tiles_per_seq = seq // tm
    return pl.pallas_call(
        _mlp_kernel,
        out_shape=jax.ShapeDtypeStruct((t, D_MODEL), F32),
        grid=(t // tm, D_FF // tf),
        in_specs=[
            pl.BlockSpec((tm, D_MODEL), lambda i, j: (i, 0)),
            pl.BlockSpec((tm, D_MODEL), lambda i, j: (i, 0)),
            pl.BlockSpec((1, 6, D_MODEL), lambda i, j: (i // tiles_per_seq, 0, 0)),
            _const_spec((1, D_MODEL)),
            pl.BlockSpec((D_MODEL, tf), lambda i, j: (0, j)),
            pl.BlockSpec((tf, D_MODEL), lambda i, j: (j, 0)),
        ],
        out_specs=pl.BlockSpec((tm, D_MODEL), lambda i, j: (i, 0)),
        compiler_params=_params(("parallel", "arbitrary")),
        name="mlp",
    )(x1, h2, mod3, g_final, w1, w2)


def _t5_bucket_table():
    q_loc = np.arange(BLOCK)[:, None]
    k_loc = np.arange(2 * BLOCK)[None, :]
    n = np.maximum(q_loc + BLOCK - k_loc, 0)
    max_exact = REL_BUCKETS // 2
    nf = np.maximum(n, 1).astype(np.float64)
    large = max_exact + (np.log(nf / max_exact) / math.log(REL_MAX_DIST / max_exact)
                         * (REL_BUCKETS - max_exact)).astype(np.int32)
    large = np.minimum(large, REL_BUCKETS - 1)
    return np.where(n < max_exact, n, large).astype(np.int32)


def _rope_tables(seq):
    half = MLA_ROPE_DIM // 2
    inv_freq = ROPE_THETA ** (-np.arange(half, dtype=np.float64) / half)
    ang = np.arange(seq, dtype=np.float64)[:, None] * inv_freq[None, :]
    zeros = np.zeros((seq, 128 - MLA_ROPE_DIM))
    cos_tab = np.concatenate([np.cos(ang), np.cos(ang), zeros], axis=1)
    sin_tab = np.concatenate([np.sin(ang), np.sin(ang), zeros], axis=1)
    return cos_tab.astype(np.float32), sin_tab.astype(np.float32)


def _rot_cols(w):
    half = w.shape[-1] // 2
    return jnp.concatenate([-w[..., half:], w[..., :half]], axis=-1)


def kernel(x, c, w_mod, b_mod, attn_norm_g, w_in, swa_sinks, rel_bias, mla_q_norm_g, w_uq,
           mla_kv_norm_g, w_ukv, w_out, mlp_norm_g, w_ff1, w_ff2, final_norm_g):
    batch, seq, _ = x.shape
    depth = w_mod.shape[0]
    assert depth == 1
    t = batch * seq
    x2d = x.reshape(t, D_MODEL)
    l = 0

    w_kr = w_in[l][:, OFF_MLA_KR:OFF_MLA_KR + MLA_ROPE_DIM]
    w_main = jnp.concatenate(
        [w_in[l][:, OFF_SWA_K:OFF_SWA_V], w_in[l][:, OFF_MLA_CQ:OFF_MLA_KR],
         w_kr, _rot_cols(w_kr)], axis=1).astype(BF16)
    w_t = jnp.concatenate(
        [w_in[l][:, :SWA_Q_COLS], w_in[l][:, OFF_SWA_V:OFF_MLA_CQ]], axis=1).T.astype(BF16)

    wq = w_uq[l].reshape(MLA_Q_RANK, MLA_HEADS, MLA_QK_DIM)
    wq_nope = wq[..., :MLA_NOPE_DIM]
    wq_rope = wq[..., MLA_NOPE_DIM:]
    hh = MLA_HEADS // 2
    zq = jnp.zeros((MLA_Q_RANK, hh, 128 - MLA_ROPE_DIM), F32)
    wq_main = jnp.concatenate(
        [jnp.concatenate([wq_nope[:, :hh], wq_rope[:, :hh], zq], axis=-1),
         jnp.concatenate([wq_nope[:, hh:], zq, wq_rope[:, hh:]], axis=-1)], axis=1).reshape(
        MLA_Q_RANK, MLA_HEADS * MLA_QK_PAD).astype(BF16)
    wq_rot_all = _rot_cols(wq_rope)
    wq_rot = jnp.concatenate([wq_rot_all[:, :hh], wq_rot_all[:, hh:]], axis=-1).reshape(
        MLA_Q_RANK, hh * 128).astype(BF16)

    wkv = w_ukv[l].reshape(MLA_KV_RANK, MLA_HEADS, MLA_NOPE_DIM + MLA_V_DIM)
    w_uk = wkv[..., :MLA_NOPE_DIM].reshape(MLA_KV_RANK, MLA_HEADS * MLA_NOPE_DIM).astype(BF16)
    w_vt = wkv[..., MLA_NOPE_DIM:].reshape(MLA_KV_RANK, MLA_HEADS * MLA_V_DIM).T.astype(BF16)

    w_out_a = w_out[l][:SWA_Q_COLS].astype(BF16)
    w_out_b = w_out[l][SWA_Q_COLS:].astype(BF16)
    w1 = w_ff1[l].astype(BF16)
    w2 = w_ff2[l].astype(BF16)

    cos_tab, sin_tab = _rope_tables(seq)

    c8 = jnp.pad(c, ((0, 8 - batch), (0, 0)))
    mod = _mod_call(c8, w_mod[l], b_mod[l].reshape(1, -1))[:batch]
    mod3 = mod.reshape(batch, 6, D_MODEL)

    bias_tab = _bias_call(rel_bias, _t5_bucket_table().T)

    qat, ka, vat, qb, kcat, vt = _proj_call(
        x2d, mod3, attn_norm_g[l].reshape(1, -1), w_main, w_t,
        mla_q_norm_g[l].reshape(1, -1), mla_kv_norm_g[l].reshape(1, -1),
        wq_main, wq_rot, w_uk, w_vt, cos_tab, sin_tab, batch=batch, seq=seq)

    oa = _swa_call(swa_sinks[l], qat, ka, vat, bias_tab, batch=batch, seq=seq)
    ob = _mla_call(qb, kcat, vt, batch=batch, seq=seq)

    x1, h2 = _out_call(x2d, oa, ob, mod3, mlp_norm_g[l].reshape(1, -1), w_out_a, w_out_b,
                       seq=seq)
    out = _mlp_call(x1, h2, mod3, final_norm_g.reshape(1, -1), w1, w2, seq=seq)
    return out.reshape(batch, seq, D_MODEL)
```

```python
import functools
import math

import jax
import jax.numpy as jnp
import numpy as np
from jax import lax
from jax.experimental import pallas as pl
from jax.experimental.pallas import tpu as pltpu

F32 = jnp.float32
BF16 = jnp.bfloat16

D_MODEL = 2048
BLOCK = 128
EPS = 1e-6

SWA_HEADS = 16
SWA_KV_HEADS = 2
SWA_HEAD_DIM = 64
SWA_GROUP = SWA_HEADS // SWA_KV_HEADS
WINDOW = 128
REL_BUCKETS = 32
REL_MAX_DIST = 128

MLA_HEADS = 8
MLA_Q_RANK = 384
MLA_KV_RANK = 128
MLA_NOPE_DIM = 128
MLA_ROPE_DIM = 64
MLA_V_DIM = 128
MLA_QK_DIM = MLA_NOPE_DIM + MLA_ROPE_DIM
MLA_QK_PAD = 256
ROPE_THETA = 10000.0
MLA_Q_SCALE = MLA_QK_DIM ** -0.5 * math.log2(math.e)
D_FF = 4 * D_MODEL

SWA_Q_COLS = SWA_HEADS * SWA_HEAD_DIM
SWA_KV_COLS = SWA_KV_HEADS * SWA_HEAD_DIM
OFF_SWA_K = SWA_Q_COLS
OFF_SWA_V = OFF_SWA_K + SWA_KV_COLS
OFF_MLA_CQ = OFF_SWA_V + SWA_KV_COLS
OFF_MLA_CKV = OFF_MLA_CQ + MLA_Q_RANK
OFF_MLA_KR = OFF_MLA_CKV + MLA_KV_RANK
P_CQ = SWA_KV_COLS
P_CKV = P_CQ + MLA_Q_RANK
P_KR = P_CKV + MLA_KV_RANK
P_COLS = P_KR + 128
PT_ROWS = SWA_Q_COLS + SWA_KV_COLS
LOG2E = math.log2(math.e)

VMEM_LIMIT_BYTES = 62 * 1024 * 1024

TM_PROJ = 512
TM_SWA = 1024
TM_OUT = 512
TM_MLP = 512
TF_MLP = 2048
TN_MOD = 1536
BIAS_HEADS_PER_STEP = 4
MLA_SLOTS = 2
ROW_CHUNKS = 2


def _params(sem):
    return pltpu.CompilerParams(dimension_semantics=sem, vmem_limit_bytes=VMEM_LIMIT_BYTES)


def _const_spec(shape):
    nd = len(shape)
    return pl.BlockSpec(shape, lambda *_: (0,) * nd, pipeline_mode=pl.Buffered(1))


def _rms(x):
    return x * lax.rsqrt(jnp.mean(x * x, axis=-1, keepdims=True) + EPS)


def _mod_kernel(c_ref, w_ref, b_ref, o_ref):
    c = c_ref[...]
    ca = c * (1.0 / (1.0 + jnp.exp(-c)))
    o_ref[...] = jnp.dot(ca.astype(BF16), w_ref[...].astype(BF16),
                         preferred_element_type=F32) + b_ref[...]


def _mod_call(c8, w_mod, b_mod):
    n = w_mod.shape[1]
    return pl.pallas_call(
        _mod_kernel,
        out_shape=jax.ShapeDtypeStruct((8, n), F32),
        grid=(n // TN_MOD,),
        in_specs=[
            pl.BlockSpec((8, D_MODEL), lambda j: (0, 0)),
            pl.BlockSpec((D_MODEL, TN_MOD), lambda j: (0, j)),
            pl.BlockSpec((1, TN_MOD), lambda j: (0, j)),
        ],
        out_specs=pl.BlockSpec((8, TN_MOD), lambda j: (0, j)),
        compiler_params=_params(("arbitrary",)),
        name="mod",
    )(c8, w_mod, b_mod)


def _bias_kernel(rel_ref, bucket_ref, o_ref):
    bucket = bucket_ref[...]
    k_loc = lax.broadcasted_iota(jnp.int32, bucket.shape, 0)
    q_loc = lax.broadcasted_iota(jnp.int32, bucket.shape, 1)
    dist = q_loc + BLOCK - k_loc
    in_window = (dist >= 0) & (dist < WINDOW)
    for i in range(BIAS_HEADS_PER_STEP):
        h = pl.program_id(0) * BIAS_HEADS_PER_STEP + i
        acc = jnp.zeros(bucket.shape, F32)
        for k in range(REL_BUCKETS):
            acc = jnp.where(bucket == k, rel_ref[k, h], acc)
        acc = acc * LOG2E
        o_ref[0, i] = jnp.where(in_window & (k_loc >= BLOCK), acc, -jnp.inf)
        o_ref[1, i] = jnp.where(in_window, acc, -jnp.inf)


def _bias_call(rel_bias, bucket_t):
    hs = BIAS_HEADS_PER_STEP
    return pl.pallas_call(
        _bias_kernel,
        out_shape=jax.ShapeDtypeStruct((2, SWA_HEADS, 2 * BLOCK, BLOCK), F32),
        grid=(SWA_HEADS // hs,),
        in_specs=[
            pl.BlockSpec(memory_space=pltpu.SMEM),
            pl.BlockSpec((2 * BLOCK, BLOCK), lambda s: (0, 0)),
        ],
        out_specs=pl.BlockSpec((2, hs, 2 * BLOCK, BLOCK), lambda s: (0, s, 0, 0)),
        compiler_params=_params(("arbitrary",)),
        name="t5_bias",
    )(rel_bias, bucket_t)


def _proj_kernel(x_ref, mod_ref, g_ref, win_ref, wt_ref, gq_ref, gkv_ref, wqm_ref, wqr_ref,
                 wuk_ref, wvt_ref, cos_ref, sin_ref, wo_ref, w1_ref, w2_ref,
                 qat_ref, ka_ref, vat_ref, qb_ref, kc_ref, vt_ref, wob_ref, w1b_ref, w2b_ref,
                 *, q_scale):
    wob_ref[...] = wo_ref[...].astype(BF16)
    w1b_ref[...] = w1_ref[...].astype(BF16)
    w2b_ref[...] = w2_ref[...].astype(BF16)

    x = x_ref[...]
    mod = mod_ref[0]
    sh1 = mod[0:1]
    sc1 = mod[1:2]
    h = ((_rms(x) * g_ref[...]) * (1.0 + sc1) + sh1).astype(BF16)
    proj = jnp.dot(h, win_ref[...], preferred_element_type=F32)
    proj_t = lax.dot_general(wt_ref[...], h, (((1,), (1,)), ((), ())),
                             preferred_element_type=F32)

    qat_ref[...] = (proj_t[:SWA_Q_COLS] * (SWA_HEAD_DIM ** -0.5 * LOG2E)).astype(BF16)
    vat_ref[...] = proj_t[SWA_Q_COLS:].astype(BF16)
    ka_ref[...] = proj[:, :P_CQ].astype(BF16)

    cq = (_rms(proj[:, P_CQ:P_CKV]) * gq_ref[...]).astype(BF16)
    ckv = (_rms(proj[:, P_CKV:P_KR]) * gkv_ref[...]).astype(BF16)
    cos_lo = cos_ref[...]
    sin_lo = sin_ref[...]
    cos_hi = pltpu.roll(cos_lo, MLA_ROPE_DIM, 1)
    sin_hi = pltpu.roll(sin_lo, MLA_ROPE_DIM, 1)
    kr = proj[:, P_KR:P_COLS]
    kr_sw = pltpu.roll(kr, MLA_ROPE_DIM, 1)
    krope_lo = (kr * cos_lo + kr_sw * sin_lo).astype(BF16)
    krope_hi = (kr_sw * cos_hi + kr * sin_hi).astype(BF16)

    qmain = jnp.dot(cq, wqm_ref[...], preferred_element_type=F32)
    qrot = jnp.dot(cq, wqr_ref[...], preferred_element_type=F32)
    knope = jnp.dot(ckv, wuk_ref[...], preferred_element_type=F32)
    vt = lax.dot_general(wvt_ref[...], ckv, (((1,), (1,)), ((), ())),
                         preferred_element_type=F32)

    half_heads = MLA_HEADS // 2
    for hd in range(MLA_HEADS):
        lo = hd * MLA_QK_PAD
        mid = lo + MLA_NOPE_DIM
        hi = lo + MLA_QK_PAD
        low = hd < half_heads
        cos_t, sin_t = (cos_lo, sin_lo) if low else (cos_hi, sin_hi)
        rot = qrot[:, (hd % half_heads) * 128:(hd % half_heads + 1) * 128]
        qb_ref[:, lo:mid] = (qmain[:, lo:mid] * q_scale).astype(BF16)
        qb_ref[:, mid:hi] = ((qmain[:, mid:hi] * cos_t + rot * sin_t) * q_scale).astype(BF16)
        kc_ref[:, lo:mid] = knope[:, hd * MLA_NOPE_DIM:(hd + 1) * MLA_NOPE_DIM].astype(BF16)
        kc_ref[:, mid:hi] = krope_lo if low else krope_hi
        vt_ref[0, hd, 0] = vt[hd * MLA_V_DIM:(hd + 1) * MLA_V_DIM].astype(BF16)


def _proj_call(x2d, mod3, g_attn, w_main, w_t, gq, gkv, wq_main, wq_rot, w_uk, w_vt,
               cos_tab, sin_tab, w_out, w_ff1, w_ff2, *, batch, seq):
    tm = TM_PROJ
    tiles_per_seq = seq // tm
    t = batch * seq
    steps = t // tm
    slab = lambda w: (w.shape[0] // steps, w.shape[1])
    assert all(w.shape[0] % (steps * 16) == 0 for w in (w_out, w_ff1, w_ff2))
    tok = lambda i: (i, 0)
    tok_t = lambda i: (0, i)
    pos = lambda i: (i % tiles_per_seq, 0)
    out_shape = (
        jax.ShapeDtypeStruct((SWA_Q_COLS, t), BF16),
        jax.ShapeDtypeStruct((t, SWA_KV_COLS), BF16),
        jax.ShapeDtypeStruct((SWA_KV_COLS, t), BF16),
        jax.ShapeDtypeStruct((t, MLA_HEADS * MLA_QK_PAD), BF16),
        jax.ShapeDtypeStruct((t, MLA_HEADS * MLA_QK_PAD), BF16),
        jax.ShapeDtypeStruct((batch, MLA_HEADS, tiles_per_seq, MLA_V_DIM, tm), BF16),
        jax.ShapeDtypeStruct(w_out.shape, BF16),
        jax.ShapeDtypeStruct(w_ff1.shape, BF16),
        jax.ShapeDtypeStruct(w_ff2.shape, BF16),
    )
    return pl.pallas_call(
        functools.partial(_proj_kernel, q_scale=MLA_Q_SCALE),
        out_shape=out_shape,
        grid=(t // tm,),
        in_specs=[
            pl.BlockSpec((tm, D_MODEL), tok),
            pl.BlockSpec((1, 6, D_MODEL), lambda i: (i // tiles_per_seq, 0, 0)),
            _const_spec((1, D_MODEL)),
            _const_spec((D_MODEL, P_COLS)),
            _const_spec((PT_ROWS, D_MODEL)),
            _const_spec((1, MLA_Q_RANK)),
            _const_spec((1, MLA_KV_RANK)),
            _const_spec((MLA_Q_RANK, MLA_HEADS * MLA_QK_PAD)),
            _const_spec((MLA_Q_RANK, MLA_HEADS // 2 * 128)),
            _const_spec((MLA_KV_RANK, MLA_HEADS * MLA_NOPE_DIM)),
            _const_spec((MLA_HEADS * MLA_V_DIM, MLA_KV_RANK)),
            pl.BlockSpec((tm, 128), pos),
            pl.BlockSpec((tm, 128), pos),
            pl.BlockSpec(slab(w_out), tok),
            pl.BlockSpec(slab(w_ff1), tok),
            pl.BlockSpec(slab(w_ff2), tok),
        ],
        out_specs=(
            pl.BlockSpec((SWA_Q_COLS, tm), tok_t),
            pl.BlockSpec((tm, SWA_KV_COLS), tok),
            pl.BlockSpec((SWA_KV_COLS, tm), tok_t),
            pl.BlockSpec((tm, MLA_HEADS * MLA_QK_PAD), tok),
            pl.BlockSpec((tm, MLA_HEADS * MLA_QK_PAD), tok),
            pl.BlockSpec((1, MLA_HEADS, 1, MLA_V_DIM, tm),
                         lambda i: (i // tiles_per_seq, 0, i % tiles_per_seq, 0, 0)),
            pl.BlockSpec(slab(w_out), tok),
            pl.BlockSpec(slab(w_ff1), tok),
            pl.BlockSpec(slab(w_ff2), tok),
        ),
        compiler_params=_params(("parallel",)),
        name="in_proj",
    )(x2d, mod3, g_attn, w_main, w_t, gq, gkv, wq_main, wq_rot, w_uk, w_vt,
      cos_tab, sin_tab, w_out, w_ff1, w_ff2)


def _swa_kernel(sinks_ref, qt_ref, kp_ref, kc_ref, vtp_ref, vtc_ref, bias_ref, o_ref,
                s_ref, p_ref, inv_ref, *, blocks_per_step):
    first_step = pl.program_id(1) == 0
    k_all = jnp.concatenate([kp_ref[...], kc_ref[...]], axis=0)
    vt_all = jnp.concatenate([vtp_ref[...], vtc_ref[...]], axis=1)
    zeros_q = jnp.zeros((SWA_HEAD_DIM, SWA_GROUP * BLOCK), BF16)
    units = [(i, g) for i in range(blocks_per_step) for g in range(SWA_KV_HEADS)]

    def stage_scores(u):
        i, g = units[u]
        k_band = k_all[i * BLOCK:(i + 2) * BLOCK]
        qt = qt_ref[:, i * BLOCK:(i + 1) * BLOCK]
        q_g = jnp.concatenate(
            [qt[(g * SWA_GROUP + hh) * SWA_HEAD_DIM:(g * SWA_GROUP + hh + 1) * SWA_HEAD_DIM]
             for hh in range(SWA_GROUP)], axis=1)
        rhs = jnp.concatenate([q_g, zeros_q] if g == 0 else [zeros_q, q_g], axis=0)
        s_ref[u % 2] = jnp.dot(k_band, rhs, preferred_element_type=F32)

    def stage_softmax(u):
        i, g = units[u]
        variant = jnp.where(first_step, 0, 1) if i == 0 else 1
        for hh in range(SWA_GROUP):
            hd = g * SWA_GROUP + hh
            st = s_ref[u % 2, :, hh * BLOCK:(hh + 1) * BLOCK] + bias_ref[variant, hd]
            sink = sinks_ref[hd] * LOG2E
            m = jnp.maximum(jnp.max(st, axis=0, keepdims=True), sink)
            p = jnp.exp2(st - m)
            denom = jnp.sum(p, axis=0, keepdims=True) + jnp.exp2(sink - m)
            inv_ref[u % 2, hh] = 1.0 / denom
            p_ref[u % 2, :, hh * BLOCK:(hh + 1) * BLOCK] = p.astype(BF16)

    def stage_pv(u):
        i, g = units[u]
        vt_band = vt_all[g * SWA_HEAD_DIM:(g + 1) * SWA_HEAD_DIM, i * BLOCK:(i + 2) * BLOCK]
        ot_g = jnp.dot(vt_band, p_ref[u % 2], preferred_element_type=F32)
        for pair in range(SWA_GROUP // 2):
            h0 = 2 * pair
            slab = jnp.concatenate(
                [ot_g[:, (h0 + e) * BLOCK:(h0 + e + 1) * BLOCK] * inv_ref[u % 2, h0 + e]
                 for e in range(2)], axis=0)
            col = (g * SWA_GROUP + h0) * SWA_HEAD_DIM
            o_ref[i * BLOCK:(i + 1) * BLOCK, col:col + 2 * SWA_HEAD_DIM] = (
                slab.T.astype(BF16))

    n = len(units)
    for t in range(n + 2):
        if 0 <= t - 2 < n:
            stage_pv(t - 2)
        if 0 <= t - 1 < n:
            stage_softmax(t - 1)
        if t < n:
            stage_scores(t)


def _swa_call(sinks, qat, ka, vat, bias_tab, *, batch, seq):
    step = TM_SWA
    blocks_per_step = step // BLOCK
    steps_per_seq = seq // step
    nb = seq // BLOCK
    prev_blk = lambda b, s: b * nb + jnp.maximum(s * blocks_per_step - 1, 0)
    return pl.pallas_call(
        functools.partial(_swa_kernel, blocks_per_step=blocks_per_step),
        out_shape=jax.ShapeDtypeStruct((batch * seq, SWA_Q_COLS), BF16),
        grid=(batch, steps_per_seq),
        in_specs=[
            pl.BlockSpec(memory_space=pltpu.SMEM),
            pl.BlockSpec((SWA_Q_COLS, step), lambda b, s: (0, b * steps_per_seq + s)),
            pl.BlockSpec((BLOCK, SWA_KV_COLS), lambda b, s: (prev_blk(b, s), 0)),
            pl.BlockSpec((step, SWA_KV_COLS), lambda b, s: (b * steps_per_seq + s, 0)),
            pl.BlockSpec((SWA_KV_COLS, BLOCK), lambda b, s: (0, prev_blk(b, s))),
            pl.BlockSpec((SWA_KV_COLS, step), lambda b, s: (0, b * steps_per_seq + s)),
            _const_spec((2, SWA_HEADS, 2 * BLOCK, BLOCK)),
        ],
        out_specs=pl.BlockSpec((step, SWA_Q_COLS), lambda b, s: (b * steps_per_seq + s, 0)),
        scratch_shapes=[
            pltpu.VMEM((2, 2 * BLOCK, SWA_GROUP * BLOCK), F32),
            pltpu.VMEM((2, 2 * BLOCK, SWA_GROUP * BLOCK), BF16),
            pltpu.VMEM((2, SWA_GROUP, 1, BLOCK), F32),
        ],
        compiler_params=_params(("parallel", "arbitrary")),
        name="swa",
    )(sinks, qat, ka, ka, vat, vat, bias_tab)


def _mla_kernel(q_ref, k_ref, vt_ref, o_ref, s_ref, p_ref, al_ref, m_ref, l_ref, acc_ref,
                *, tile, n_tiles):
    pairs = [(qi, j) for qi in range(n_tiles) for j in range(qi + 1)]

    def stage_scores(u):
        qi, j = pairs[u]
        k = k_ref[j * tile:(j + 1) * tile, :]
        q = q_ref[qi * tile:(qi + 1) * tile, :]
        s_ref[u % MLA_SLOTS] = lax.dot_general(k, q, (((1,), (1,)), ((), ())),
                                               preferred_element_type=F32)

    def stage_softmax(u):
        qi, j = pairs[u]
        st = s_ref[u % MLA_SLOTS]
        if j == qi:
            kk = lax.broadcasted_iota(jnp.int32, st.shape, 0)
            qq = lax.broadcasted_iota(jnp.int32, st.shape, 1)
            st = jnp.where(kk <= qq, st, -jnp.inf)
        cmax = jnp.max(st, axis=0, keepdims=True)
        if j == 0:
            m_new = cmax
            p = jnp.exp2(st - m_new)
            l_new = jnp.sum(p, axis=0, keepdims=True)
        else:
            m_old = m_ref[qi % 2]
            m_new = jnp.maximum(m_old, cmax)
            alpha = jnp.exp2(m_old - m_new)
            p = jnp.exp2(st - m_new)
            l_new = alpha * l_ref[qi % 2] + jnp.sum(p, axis=0, keepdims=True)
            al_ref[u % MLA_SLOTS] = alpha
        m_ref[qi % 2] = m_new
        l_ref[qi % 2] = l_new
        p_ref[u % MLA_SLOTS] = p.astype(BF16)

    def stage_pv(u):
        qi, j = pairs[u]
        pv = jnp.dot(vt_ref[0, 0, j], p_ref[u % MLA_SLOTS], preferred_element_type=F32)
        if j == 0:
            acc = pv
        else:
            acc = al_ref[u % MLA_SLOTS] * acc_ref[...] + pv
        if j == qi:
            o_ref[qi * tile:(qi + 1) * tile, :] = (acc / l_ref[qi % 2]).T.astype(BF16)
        else:
            acc_ref[...] = acc

    n = len(pairs)
    for t in range(n + 2):
        if 0 <= t - 2 < n:
            stage_pv(t - 2)
        if 0 <= t - 1 < n:
            stage_softmax(t - 1)
        if t < n:
            stage_scores(t)


def _mla_call(qb, kcat, vt, *, batch, seq):
    tile = TM_PROJ
    n_tiles = seq // tile
    return pl.pallas_call(
        functools.partial(_mla_kernel, tile=tile, n_tiles=n_tiles),
        out_shape=jax.ShapeDtypeStruct((batch * seq, MLA_HEADS * MLA_V_DIM), BF16),
        grid=(batch, MLA_HEADS),
        in_specs=[
            pl.BlockSpec((seq, MLA_QK_PAD), lambda b, h: (b, h)),
            pl.BlockSpec((seq, MLA_QK_PAD), lambda b, h: (b, h)),
            pl.BlockSpec((1, 1, n_tiles, MLA_V_DIM, tile), lambda b, h: (b, h, 0, 0, 0)),
        ],
        out_specs=pl.BlockSpec((seq, MLA_V_DIM), lambda b, h: (b, h)),
        scratch_shapes=[
            pltpu.VMEM((MLA_SLOTS, tile, tile), F32),
            pltpu.VMEM((MLA_SLOTS, tile, tile), BF16),
            pltpu.VMEM((MLA_SLOTS, 1, tile), F32),
            pltpu.VMEM((2, 1, tile), F32),
            pltpu.VMEM((2, 1, tile), F32),
            pltpu.VMEM((MLA_V_DIM, tile), F32),
        ],
        compiler_params=_params(("parallel", "parallel")),
        name="mla",
    )(qb, kcat, vt)


def _out_kernel(x_ref, oa_ref, ob_ref, mod_ref, g_ref, wa_ref, wb_ref, x1_ref, h_ref):
    mod = mod_ref[0]
    g1 = mod[2:3]
    sh2 = mod[3:4]
    sc2 = mod[4:5]
    y = (jnp.dot(oa_ref[...], wa_ref[...], preferred_element_type=F32)
         + jnp.dot(ob_ref[...], wb_ref[...], preferred_element_type=F32))
    x1 = x_ref[...] + g1 * y
    x1_ref[...] = x1
    h_ref[...] = ((_rms(x1) * g_ref[...]) * (1.0 + sc2) + sh2).astype(BF16)


def _out_call(x2d, oa, ob, mod3, g_mlp, w_out, *, seq):
    tm = TM_OUT
    t = x2d.shape[0]
    tiles_per_seq = seq // tm
    tok = lambda i: (i, 0)
    w_half = lambda half: pl.BlockSpec((SWA_Q_COLS, D_MODEL), lambda i: (half, 0),
                                       pipeline_mode=pl.Buffered(1))
    assert w_out.shape[0] == 2 * SWA_Q_COLS
    return pl.pallas_call(
        _out_kernel,
        out_shape=(jax.ShapeDtypeStruct((t, D_MODEL), F32),
                   jax.ShapeDtypeStruct((t, D_MODEL), BF16)),
        grid=(t // tm,),
        in_specs=[
            pl.BlockSpec((tm, D_MODEL), tok),
            pl.BlockSpec((tm, SWA_Q_COLS), tok),
            pl.BlockSpec((tm, MLA_HEADS * MLA_V_DIM), tok),
            pl.BlockSpec((1, 6, D_MODEL), lambda i: (i // tiles_per_seq, 0, 0)),
            _const_spec((1, D_MODEL)),
            w_half(0),
            w_half(1),
        ],
        out_specs=(pl.BlockSpec((tm, D_MODEL), tok), pl.BlockSpec((tm, D_MODEL), tok)),
        compiler_params=_params(("parallel",)),
        name="out_proj",
    )(x2d, oa, ob, mod3, g_mlp, w_out, w_out)


def _mlp_kernel(x_ref, h_ref, mod_ref, gf_ref, w1_ref, w2_ref, o_ref):
    j = pl.program_id(1)

    @pl.when(j == 0)
    def _():
        o_ref[...] = jnp.zeros(o_ref.shape, F32)

    def ff_tile(sl):
        u = jnp.maximum(jnp.dot(h_ref[sl, :], w1_ref[...], preferred_element_type=F32), 0.0)
        return jnp.dot((u * u).astype(BF16), w2_ref[...], preferred_element_type=F32)

    last = pl.num_programs(1) - 1

    @pl.when(j < last)
    def _():
        o_ref[...] += ff_tile(slice(None))

    @pl.when(j == last)
    def _():
        g2 = mod_ref[0][5:6]
        rows = o_ref.shape[0] // ROW_CHUNKS
        for r in range(ROW_CHUNKS):
            sl = slice(r * rows, (r + 1) * rows)
            x2 = x_ref[sl, :] + g2 * (o_ref[sl, :] + ff_tile(sl))
            o_ref[sl, :] = _rms(x2) * gf_ref[...]


def _mlp_call(x1, h2, mod3, g_final, w1, w2, *, seq):
    tm = TM_MLP
    tf = TF_MLP
    t = x1.shape[0]
    tiles_per_seq = seq // tm
    return pl.pallas_call(
        _mlp_kernel,
        out_shape=jax.ShapeDtypeStruct((t, D_MODEL), F32),
        grid=(t // tm, D_FF // tf),
        in_specs=[
            pl.BlockSpec((tm, D_MODEL), lambda i, j: (i, 0)),
            pl.BlockSpec((tm, D_MODEL), lambda i, j: (i, 0)),
            pl.BlockSpec((1, 6, D_MODEL), lambda i, j: (i // tiles_per_seq, 0, 0)),
            _const_spec((1, D_MODEL)),
            pl.BlockSpec((D_MODEL, tf), lambda i, j: (0, j)),
            pl.BlockSpec((tf, D_MODEL), lambda i, j: (j, 0)),
        ],
        out_specs=pl.BlockSpec((tm, D_MODEL), lambda i, j: (i, 0)),
        compiler_params=_params(("parallel", "arbitrary")),
        name="mlp",
    )(x1, h2, mod3, g_final, w1, w2)


def _t5_bucket_table():
    q_loc = np.arange(BLOCK)[:, None]
    k_loc = np.arange(2 * BLOCK)[None, :]
    n = np.maximum(q_loc + BLOCK - k_loc, 0)
    max_exact = REL_BUCKETS // 2
    nf = np.maximum(n, 1).astype(np.float64)
    large = max_exact + (np.log(nf / max_exact) / math.log(REL_MAX_DIST / max_exact)
                         * (REL_BUCKETS - max_exact)).astype(np.int32)
    large = np.minimum(large, REL_BUCKETS - 1)
    return np.where(n < max_exact, n, large).astype(np.int32)


def _rope_tables(seq):
    half = MLA_ROPE_DIM // 2
    inv_freq = ROPE_THETA ** (-np.arange(half, dtype=np.float64) / half)
    ang = np.arange(seq, dtype=np.float64)[:, None] * inv_freq[None, :]
    zeros = np.zeros((seq, 128 - MLA_ROPE_DIM))
    cos_tab = np.concatenate([np.cos(ang), np.cos(ang), zeros], axis=1)
    sin_tab = np.concatenate([np.sin(ang), np.sin(ang), zeros], axis=1)
    return cos_tab.astype(np.float32), sin_tab.astype(np.float32)


def _rot_cols(w):
    half = w.shape[-1] // 2
    return jnp.concatenate([-w[..., half:], w[..., :half]], axis=-1)


def kernel(x, c, w_mod, b_mod, attn_norm_g, w_in, swa_sinks, rel_bias, mla_q_norm_g, w_uq,
           mla_kv_norm_g, w_ukv, w_out, mlp_norm_g, w_ff1, w_ff2, final_norm_g):
    batch, seq, _ = x.shape
    depth = w_mod.shape[0]
    assert depth == 1
    t = batch * seq
    x2d = x.reshape(t, D_MODEL)
    l = 0

    w_kr = w_in[l][:, OFF_MLA_KR:OFF_MLA_KR + MLA_ROPE_DIM]
    w_main = jnp.concatenate(
        [w_in[l][:, OFF_SWA_K:OFF_SWA_V], w_in[l][:, OFF_MLA_CQ:OFF_MLA_KR],
         w_kr, _rot_cols(w_kr)], axis=1).astype(BF16)
    w_t = jnp.concatenate(
        [w_in[l][:, :SWA_Q_COLS], w_in[l][:, OFF_SWA_V:OFF_MLA_CQ]], axis=1).T.astype(BF16)

    wq = w_uq[l].reshape(MLA_Q_RANK, MLA_HEADS, MLA_QK_DIM)
    wq_nope = wq[..., :MLA_NOPE_DIM]
    wq_rope = wq[..., MLA_NOPE_DIM:]
    hh = MLA_HEADS // 2
    zq = jnp.zeros((MLA_Q_RANK, hh, 128 - MLA_ROPE_DIM), F32)
    wq_main = jnp.concatenate(
        [jnp.concatenate([wq_nope[:, :hh], wq_rope[:, :hh], zq], axis=-1),
         jnp.concatenate([wq_nope[:, hh:], zq, wq_rope[:, hh:]], axis=-1)], axis=1).reshape(
        MLA_Q_RANK, MLA_HEADS * MLA_QK_PAD).astype(BF16)
    wq_rot_all = _rot_cols(wq_rope)
    wq_rot = jnp.concatenate([wq_rot_all[:, :hh], wq_rot_all[:, hh:]], axis=-1).reshape(
        MLA_Q_RANK, hh * 128).astype(BF16)

    wkv = w_ukv[l].reshape(MLA_KV_RANK, MLA_HEADS, MLA_NOPE_DIM + MLA_V_DIM)
    w_uk = wkv[..., :MLA_NOPE_DIM].reshape(MLA_KV_RANK, MLA_HEADS * MLA_NOPE_DIM).astype(BF16)
    w_vt = wkv[..., MLA_NOPE_DIM:].reshape(MLA_KV_RANK, MLA_HEADS * MLA_V_DIM).T.astype(BF16)

    cos_tab, sin_tab = _rope_tables(seq)

    c8 = jnp.pad(c, ((0, 8 - batch), (0, 0)))
    mod = _mod_call(c8, w_mod[l], b_mod[l].reshape(1, -1))[:batch]
    mod3 = mod.reshape(batch, 6, D_MODEL)

    bias_tab = _bias_call(rel_bias, _t5_bucket_table().T)

    qat, ka, vat, qb, kcat, vt, w_out_b16, w1, w2 = _proj_call(
        x2d, mod3, attn_norm_g[l].reshape(1, -1), w_main, w_t,
        mla_q_norm_g[l].reshape(1, -1), mla_kv_norm_g[l].reshape(1, -1),
        wq_main, wq_rot, w_uk, w_vt, cos_tab, sin_tab, w_out[l], w_ff1[l], w_ff2[l],
        batch=batch, seq=seq)

    oa = _swa_call(swa_sinks[l], qat, ka, vat, bias_tab, batch=batch, seq=seq)
    ob = _mla_call(qb, kcat, vt, batch=batch, seq=seq)

    x1, h2 = _out_call(x2d, oa, ob, mod3, mlp_norm_g[l].reshape(1, -1), w_out_b16, seq=seq)
    out = _mlp_call(x1, h2, mod3, final_norm_g.reshape(1, -1), w1, w2, seq=seq)
    return out.reshape(batch, seq, D_MODEL)
```

```python
import functools
import math

import jax
import jax.numpy as jnp
import numpy as np
from jax import lax
from jax.experimental import pallas as pl
from jax.experimental.pallas import tpu as pltpu

F32 = jnp.float32
BF16 = jnp.bfloat16

D_MODEL = 2048
BLOCK = 128
EPS = 1e-6

SWA_HEADS = 16
SWA_KV_HEADS = 2
SWA_HEAD_DIM = 64
SWA_GROUP = SWA_HEADS // SWA_KV_HEADS
WINDOW = 128
REL_BUCKETS = 32
REL_MAX_DIST = 128

MLA_HEADS = 8
MLA_Q_RANK = 384
MLA_KV_RANK = 128
MLA_NOPE_DIM = 128
MLA_ROPE_DIM = 64
MLA_V_DIM = 128
MLA_QK_DIM = MLA_NOPE_DIM + MLA_ROPE_DIM
MLA_QK_PAD = 256
ROPE_THETA = 10000.0
MLA_Q_SCALE = MLA_QK_DIM ** -0.5 * math.log2(math.e)
D_FF = 4 * D_MODEL

SWA_Q_COLS = SWA_HEADS * SWA_HEAD_DIM
SWA_KV_COLS = SWA_KV_HEADS * SWA_HEAD_DIM
OFF_SWA_K = SWA_Q_COLS
OFF_SWA_V = OFF_SWA_K + SWA_KV_COLS
OFF_MLA_CQ = OFF_SWA_V + SWA_KV_COLS
OFF_MLA_CKV = OFF_MLA_CQ + MLA_Q_RANK
OFF_MLA_KR = OFF_MLA_CKV + MLA_KV_RANK
P_CQ = SWA_KV_COLS
P_CKV = P_CQ + MLA_Q_RANK
P_KR = P_CKV + MLA_KV_RANK
P_COLS = P_KR + 128
PT_ROWS = SWA_Q_COLS + SWA_KV_COLS
LOG2E = math.log2(math.e)

VMEM_LIMIT_BYTES = 62 * 1024 * 1024

TM_PROJ = 512
TM_SWA = 1024
TM_OUT = 512
TM_MLP = 512
TF_MLP = 2048
TN_MOD = 1536
BIAS_HEADS_PER_STEP = 4
MLA_SLOTS = 2
ROW_CHUNKS = 2


def _params(sem):
    return pltpu.CompilerParams(dimension_semantics=sem, vmem_limit_bytes=VMEM_LIMIT_BYTES)


def _const_spec(shape):
    nd = len(shape)
    return pl.BlockSpec(shape, lambda *_: (0,) * nd, pipeline_mode=pl.Buffered(1))


def _rms(x):
    return x * lax.rsqrt(jnp.mean(x * x, axis=-1, keepdims=True) + EPS)


def _mod_kernel(c_ref, w_ref, b_ref, o_ref):
    c = c_ref[...]
    ca = c * (1.0 / (1.0 + jnp.exp(-c)))
    o_ref[...] = jnp.dot(ca.astype(BF16), w_ref[...].astype(BF16),
                         preferred_element_type=F32) + b_ref[...]


def _mod_call(c8, w_mod, b_mod):
    n = w_mod.shape[1]
    return pl.pallas_call(
        _mod_kernel,
        out_shape=jax.ShapeDtypeStruct((8, n), F32),
        grid=(n // TN_MOD,),
        in_specs=[
            pl.BlockSpec((8, D_MODEL), lambda j: (0, 0)),
            pl.BlockSpec((D_MODEL, TN_MOD), lambda j: (0, j)),
            pl.BlockSpec((1, TN_MOD), lambda j: (0, j)),
        ],
        out_specs=pl.BlockSpec((8, TN_MOD), lambda j: (0, j)),
        compiler_params=_params(("arbitrary",)),
        name="mod",
    )(c8, w_mod, b_mod)


def _bias_kernel(rel_ref, bucket_ref, o_ref):
    bucket = bucket_ref[...]
    k_loc = lax.broadcasted_iota(jnp.int32, bucket.shape, 0)
    q_loc = lax.broadcasted_iota(jnp.int32, bucket.shape, 1)
    dist = q_loc + BLOCK - k_loc
    in_window = (dist >= 0) & (dist < WINDOW)
    for i in range(BIAS_HEADS_PER_STEP):
        h = pl.program_id(0) * BIAS_HEADS_PER_STEP + i
        acc = jnp.zeros(bucket.shape, F32)
        for k in range(REL_BUCKETS):
            acc = jnp.where(bucket == k, rel_ref[k, h], acc)
        acc = acc * LOG2E
        o_ref[0, i] = jnp.where(in_window & (k_loc >= BLOCK), acc, -jnp.inf)
        o_ref[1, i] = jnp.where(in_window, acc, -jnp.inf)


def _bias_call(rel_bias, bucket_t):
    hs = BIAS_HEADS_PER_STEP
    return pl.pallas_call(
        _bias_kernel,
        out_shape=jax.ShapeDtypeStruct((2, SWA_HEADS, 2 * BLOCK, BLOCK), F32),
        grid=(SWA_HEADS // hs,),
        in_specs=[
            pl.BlockSpec(memory_space=pltpu.SMEM),
            pl.BlockSpec((2 * BLOCK, BLOCK), lambda s: (0, 0)),
        ],
        out_specs=pl.BlockSpec((2, hs, 2 * BLOCK, BLOCK), lambda s: (0, s, 0, 0)),
        compiler_params=_params(("arbitrary",)),
        name="t5_bias",
    )(rel_bias, bucket_t)


def _proj_kernel(x_ref, mod_ref, g_ref, win_ref, wt_ref, gq_ref, gkv_ref, wqm_ref, wqr_ref,
                 wuk_ref, wvt_ref, cos_ref, sin_ref, wo_ref, w1_ref, w2_ref,
                 qat_ref, ka_ref, vat_ref, qb_ref, kc_ref, vt_ref, wob_ref, w1b_ref, w2b_ref,
                 *, q_scale):
    wob_ref[...] = wo_ref[...].astype(BF16)
    w1b_ref[...] = w1_ref[...].astype(BF16)
    w2b_ref[...] = w2_ref[...].astype(BF16)

    x = x_ref[...]
    mod = mod_ref[0]
    sh1 = mod[0:1]
    sc1 = mod[1:2]
    h = ((_rms(x) * g_ref[...]) * (1.0 + sc1) + sh1).astype(BF16)
    proj = jnp.dot(h, win_ref[...], preferred_element_type=F32)
    proj_t = lax.dot_general(wt_ref[...], h, (((1,), (1,)), ((), ())),
                             preferred_element_type=F32)

    qat_ref[...] = (proj_t[:SWA_Q_COLS] * (SWA_HEAD_DIM ** -0.5 * LOG2E)).astype(BF16)
    vat_ref[...] = proj_t[SWA_Q_COLS:].astype(BF16)
    ka_ref[...] = proj[:, :P_CQ].astype(BF16)

    cq = (_rms(proj[:, P_CQ:P_CKV]) * gq_ref[...]).astype(BF16)
    ckv = (_rms(proj[:, P_CKV:P_KR]) * gkv_ref[...]).astype(BF16)
    cos_lo = cos_ref[...]
    sin_lo = sin_ref[...]
    cos_hi = pltpu.roll(cos_lo, MLA_ROPE_DIM, 1)
    sin_hi = pltpu.roll(sin_lo, MLA_ROPE_DIM, 1)
    kr = proj[:, P_KR:P_COLS]
    kr_sw = pltpu.roll(kr, MLA_ROPE_DIM, 1)
    krope_lo = (kr * cos_lo + kr_sw * sin_lo).astype(BF16)
    krope_hi = (kr_sw * cos_hi + kr * sin_hi).astype(BF16)

    qmain = jnp.dot(cq, wqm_ref[...], preferred_element_type=F32)
    qrot = jnp.dot(cq, wqr_ref[...], preferred_element_type=F32)
    knope = jnp.dot(ckv, wuk_ref[...], preferred_element_type=F32)
    vt = lax.dot_general(wvt_ref[...], ckv, (((1,), (1,)), ((), ())),
                         preferred_element_type=F32)

    half_heads = MLA_HEADS // 2
    for hd in range(MLA_HEADS):
        lo = hd * MLA_QK_PAD
        mid = lo + MLA_NOPE_DIM
        hi = lo + MLA_QK_PAD
        low = hd < half_heads
        cos_t, sin_t = (cos_lo, sin_lo) if low else (cos_hi, sin_hi)
        rot = qrot[:, (hd % half_heads) * 128:(hd % half_heads + 1) * 128]
        qb_ref[:, lo:mid] = (qmain[:, lo:mid] * q_scale).astype(BF16)
        qb_ref[:, mid:hi] = ((qmain[:, mid:hi] * cos_t + rot * sin_t) * q_scale).astype(BF16)
        kc_ref[:, lo:mid] = knope[:, hd * MLA_NOPE_DIM:(hd + 1) * MLA_NOPE_DIM].astype(BF16)
        kc_ref[:, mid:hi] = krope_lo if low else krope_hi
        vt_ref[0, hd, 0] = vt[hd * MLA_V_DIM:(hd + 1) * MLA_V_DIM].astype(BF16)


def _proj_call(x2d, mod3, g_attn, w_main, w_t, gq, gkv, wq_main, wq_rot, w_uk, w_vt,
               cos_tab, sin_tab, w_out, w_ff1, w_ff2, *, batch, seq):
    tm = TM_PROJ
    tiles_per_seq = seq // tm
    t = batch * seq
    steps = t // tm
    slab = lambda w: (w.shape[0] // steps, w.shape[1])
    assert all(w.shape[0] % (steps * 16) == 0 for w in (w_out, w_ff1, w_ff2))
    tok = lambda i: (i, 0)
    tok_t = lambda i: (0, i)
    pos = lambda i: (i % tiles_per_seq, 0)
    out_shape = (
        jax.ShapeDtypeStruct((SWA_Q_COLS, t), BF16),
        jax.ShapeDtypeStruct((t, SWA_KV_COLS), BF16),
        jax.ShapeDtypeStruct((SWA_KV_COLS, t), BF16),
        jax.ShapeDtypeStruct((t, MLA_HEADS * MLA_QK_PAD), BF16),
        jax.ShapeDtypeStruct((t, MLA_HEADS * MLA_QK_PAD), BF16),
        jax.ShapeDtypeStruct((batch, MLA_HEADS, tiles_per_seq, MLA_V_DIM, tm), BF16),
        jax.ShapeDtypeStruct(w_out.shape, BF16),
        jax.ShapeDtypeStruct(w_ff1.shape, BF16),
        jax.ShapeDtypeStruct(w_ff2.shape, BF16),
    )
    return pl.pallas_call(
        functools.partial(_proj_kernel, q_scale=MLA_Q_SCALE),
        out_shape=out_shape,
        grid=(t // tm,),
        in_specs=[
            pl.BlockSpec((tm, D_MODEL), tok),
            pl.BlockSpec((1, 6, D_MODEL), lambda i: (i // tiles_per_seq, 0, 0)),
            _const_spec((1, D_MODEL)),
            _const_spec((D_MODEL, P_COLS)),
            _const_spec((PT_ROWS, D_MODEL)),
            _const_spec((1, MLA_Q_RANK)),
            _const_spec((1, MLA_KV_RANK)),
            _const_spec((MLA_Q_RANK, MLA_HEADS * MLA_QK_PAD)),
            _const_spec((MLA_Q_RANK, MLA_HEADS // 2 * 128)),
            _const_spec((MLA_KV_RANK, MLA_HEADS * MLA_NOPE_DIM)),
            _const_spec((MLA_HEADS * MLA_V_DIM, MLA_KV_RANK)),
            pl.BlockSpec((tm, 128), pos),
            pl.BlockSpec((tm, 128), pos),
            pl.BlockSpec(slab(w_out), tok),
            pl.BlockSpec(slab(w_ff1), tok),
            pl.BlockSpec(slab(w_ff2), tok),
        ],
        out_specs=(
            pl.BlockSpec((SWA_Q_COLS, tm), tok_t),
            pl.BlockSpec((tm, SWA_KV_COLS), tok),
            pl.BlockSpec((SWA_KV_COLS, tm), tok_t),
            pl.BlockSpec((tm, MLA_HEADS * MLA_QK_PAD), tok),
            pl.BlockSpec((tm, MLA_HEADS * MLA_QK_PAD), tok),
            pl.BlockSpec((1, MLA_HEADS, 1, MLA_V_DIM, tm),
                         lambda i: (i // tiles_per_seq, 0, i % tiles_per_seq, 0, 0)),
            pl.BlockSpec(slab(w_out), tok),
            pl.BlockSpec(slab(w_ff1), tok),
            pl.BlockSpec(slab(w_ff2), tok),
        ),
        compiler_params=_params(("parallel",)),
        name="in_proj",
    )(x2d, mod3, g_attn, w_main, w_t, gq, gkv, wq_main, wq_rot, w_uk, w_vt,
      cos_tab, sin_tab, w_out, w_ff1, w_ff2)


def _swa_kernel(sinks_ref, qt_ref, kp_ref, kc_ref, vtp_ref, vtc_ref, bias_ref, o_ref,
                s_ref, p_ref, inv_ref, *, blocks_per_step):
    first_step = pl.program_id(1) == 0
    k_all = jnp.concatenate([kp_ref[...], kc_ref[...]], axis=0)
    vt_all = jnp.concatenate([vtp_ref[...], vtc_ref[...]], axis=1)
    zeros_q = jnp.zeros((SWA_HEAD_DIM, SWA_GROUP * BLOCK), BF16)
    units = [(i, g) for i in range(blocks_per_step) for g in range(SWA_KV_HEADS)]

    def stage_scores(u):
        i, g = units[u]
        k_band = k_all[i * BLOCK:(i + 2) * BLOCK]
        qt = qt_ref[:, i * BLOCK:(i + 1) * BLOCK]
        q_g = jnp.concatenate(
            [qt[(g * SWA_GROUP + hh) * SWA_HEAD_DIM:(g * SWA_GROUP + hh + 1) * SWA_HEAD_DIM]
             for hh in range(SWA_GROUP)], axis=1)
        rhs = jnp.concatenate([q_g, zeros_q] if g == 0 else [zeros_q, q_g], axis=0)
        s_ref[u % 2] = jnp.dot(k_band, rhs, preferred_element_type=F32)

    def stage_softmax(u):
        i, g = units[u]
        variant = jnp.where(first_step, 0, 1) if i == 0 else 1
        for hh in range(SWA_GROUP):
            hd = g * SWA_GROUP + hh
            st = s_ref[u % 2, :, hh * BLOCK:(hh + 1) * BLOCK] + bias_ref[variant, hd]
            sink = sinks_ref[hd] * LOG2E
            m = jnp.maximum(jnp.max(st, axis=0, keepdims=True), sink)
            p = jnp.exp2(st - m)
            denom = jnp.sum(p, axis=0, keepdims=True) + jnp.exp2(sink - m)
            inv_ref[u % 2, hh] = 1.0 / denom
            p_ref[u % 2, :, hh * BLOCK:(hh + 1) * BLOCK] = p.astype(BF16)

    def stage_pv(u):
        i, g = units[u]
        vt_band = vt_all[g * SWA_HEAD_DIM:(g + 1) * SWA_HEAD_DIM, i * BLOCK:(i + 2) * BLOCK]
        ot_g = jnp.dot(vt_band, p_ref[u % 2], preferred_element_type=F32)
        for pair in range(SWA_GROUP // 2):
            h0 = 2 * pair
            slab = jnp.concatenate(
                [ot_g[:, (h0 + e) * BLOCK:(h0 + e + 1) * BLOCK] * inv_ref[u % 2, h0 + e]
                 for e in range(2)], axis=0)
            col = (g * SWA_GROUP + h0) * SWA_HEAD_DIM
            o_ref[i * BLOCK:(i + 1) * BLOCK, col:col + 2 * SWA_HEAD_DIM] = (
                slab.T.astype(BF16))

    n = len(units)
    for t in range(n + 2):
        if 0 <= t - 2 < n:
            stage_pv(t - 2)
        if 0 <= t - 1 < n:
            stage_softmax(t - 1)
        if t < n:
            stage_scores(t)


def _swa_call(sinks, qat, ka, vat, bias_tab, *, batch, seq):
    step = TM_SWA
    blocks_per_step = step // BLOCK
    steps_per_seq = seq // step
    nb = seq // BLOCK
    prev_blk = lambda b, s: b * nb + jnp.maximum(s * blocks_per_step - 1, 0)
    return pl.pallas_call(
        functools.partial(_swa_kernel, blocks_per_step=blocks_per_step),
        out_shape=jax.ShapeDtypeStruct((batch * seq, SWA_Q_COLS), BF16),
        grid=(batch, steps_per_seq),
        in_specs=[
            pl.BlockSpec(memory_space=pltpu.SMEM),
            pl.BlockSpec((SWA_Q_COLS, step), lambda b, s: (0, b * steps_per_seq + s)),
            pl.BlockSpec((BLOCK, SWA_KV_COLS), lambda b, s: (prev_blk(b, s), 0)),
            pl.BlockSpec((step, SWA_KV_COLS), lambda b, s: (b * steps_per_seq + s, 0)),
            pl.BlockSpec((SWA_KV_COLS, BLOCK), lambda b, s: (0, prev_blk(b, s))),
            pl.BlockSpec((SWA_KV_COLS, step), lambda b, s: (0, b * steps_per_seq + s)),
            _const_spec((2, SWA_HEADS, 2 * BLOCK, BLOCK)),
        ],
        out_specs=pl.BlockSpec((step, SWA_Q_COLS), lambda b, s: (b * steps_per_seq + s, 0)),
        scratch_shapes=[
            pltpu.VMEM((2, 2 * BLOCK, SWA_GROUP * BLOCK), F32),
            pltpu.VMEM((2, 2 * BLOCK, SWA_GROUP * BLOCK), BF16),
            pltpu.VMEM((2, SWA_GROUP, 1, BLOCK), F32),
        ],
        compiler_params=_params(("parallel", "arbitrary")),
        name="swa",
    )(sinks, qat, ka, ka, vat, vat, bias_tab)


def _mla_kernel(q_ref, k_ref, vt_ref, o_ref, s_ref, p_ref, al_ref, m_ref, l_ref, acc_ref,
                *, tile, n_tiles):
    pairs = [(qi, j) for qi in range(n_tiles) for j in range(qi + 1)]

    def stage_scores(u):
        qi, j = pairs[u]
        k = k_ref[j * tile:(j + 1) * tile, :]
        q = q_ref[qi * tile:(qi + 1) * tile, :]
        s_ref[u % MLA_SLOTS] = lax.dot_general(k, q, (((1,), (1,)), ((), ())),
                                               preferred_element_type=F32)

    def stage_softmax(u):
        qi, j = pairs[u]
        st = s_ref[u % MLA_SLOTS]
        if j == qi:
            kk = lax.broadcasted_iota(jnp.int32, st.shape, 0)
            qq = lax.broadcasted_iota(jnp.int32, st.shape, 1)
            st = jnp.where(kk <= qq, st, -jnp.inf)
        cmax = jnp.max(st, axis=0, keepdims=True)
        if j == 0:
            m_new = cmax
            p = jnp.exp2(st - m_new)
            l_new = jnp.sum(p, axis=0, keepdims=True)
        else:
            m_old = m_ref[qi % 2]
            m_new = jnp.maximum(m_old, cmax)
            alpha = jnp.exp2(m_old - m_new)
            p = jnp.exp2(st - m_new)
            l_new = alpha * l_ref[qi % 2] + jnp.sum(p, axis=0, keepdims=True)
            al_ref[u % MLA_SLOTS] = alpha
        m_ref[qi % 2] = m_new
        l_ref[qi % 2] = l_new
        p_ref[u % MLA_SLOTS] = p.astype(BF16)

    def stage_pv(u):
        qi, j = pairs[u]
        pv = jnp.dot(vt_ref[0, 0, j], p_ref[u % MLA_SLOTS], preferred_element_type=F32)
        if j == 0:
            acc = pv
        else:
            acc = al_ref[u % MLA_SLOTS] * acc_ref[...] + pv
        if j == qi:
            o_ref[qi * tile:(qi + 1) * tile, :] = (acc / l_ref[qi % 2]).T.astype(BF16)
        else:
            acc_ref[...] = acc

    n = len(pairs)
    for t in range(n + 2):
        if 0 <= t - 2 < n:
            stage_pv(t - 2)
        if 0 <= t - 1 < n:
            stage_softmax(t - 1)
        if t < n:
            stage_scores(t)


def _mla_call(qb, kcat, vt, *, batch, seq):
    tile = TM_PROJ
    n_tiles = seq // tile
    return pl.pallas_call(
        functools.partial(_mla_kernel, tile=tile, n_tiles=n_tiles),
        out_shape=jax.ShapeDtypeStruct((batch * seq, MLA_HEADS * MLA_V_DIM), BF16),
        grid=(batch, MLA_HEADS),
        in_specs=[
            pl.BlockSpec((seq, MLA_QK_PAD), lambda b, h: (b, h)),
            pl.BlockSpec((seq, MLA_QK_PAD), lambda b, h: (b, h)),
            pl.BlockSpec((1, 1, n_tiles, MLA_V_DIM, tile), lambda b, h: (b, h, 0, 0, 0)),
        ],
        out_specs=pl.BlockSpec((seq, MLA_V_DIM), lambda b, h: (b, h)),
        scratch_shapes=[
            pltpu.VMEM((MLA_SLOTS, tile, tile), F32),
            pltpu.VMEM((MLA_SLOTS, tile, tile), BF16),
            pltpu.VMEM((MLA_SLOTS, 1, tile), F32),
            pltpu.VMEM((2, 1, tile), F32),
            pltpu.VMEM((2, 1, tile), F32),
            pltpu.VMEM((MLA_V_DIM, tile), F32),
        ],
        compiler_params=_params(("parallel", "parallel")),
        name="mla",
    )(qb, kcat, vt)


def _out_kernel(x_ref, oa_ref, ob_ref, mod_ref, wa_ref, wb_ref, x1_ref):
    g1 = mod_ref[0][2:3]
    y = (jnp.dot(oa_ref[...], wa_ref[...], preferred_element_type=F32)
         + jnp.dot(ob_ref[...], wb_ref[...], preferred_element_type=F32))
    x1_ref[...] = x_ref[...] + g1 * y


def _out_call(x2d, oa, ob, mod3, w_out, *, seq):
    tm = TM_OUT
    t = x2d.shape[0]
    tiles_per_seq = seq // tm
    tok = lambda i: (i, 0)
    w_half = lambda half: pl.BlockSpec((SWA_Q_COLS, D_MODEL), lambda i: (half, 0),
                                       pipeline_mode=pl.Buffered(1))
    assert w_out.shape[0] == 2 * SWA_Q_COLS
    return pl.pallas_call(
        _out_kernel,
        out_shape=jax.ShapeDtypeStruct((t, D_MODEL), F32),
        grid=(t // tm,),
        in_specs=[
            pl.BlockSpec((tm, D_MODEL), tok),
            pl.BlockSpec((tm, SWA_Q_COLS), tok),
            pl.BlockSpec((tm, MLA_HEADS * MLA_V_DIM), tok),
            pl.BlockSpec((1, 6, D_MODEL), lambda i: (i // tiles_per_seq, 0, 0)),
            w_half(0),
            w_half(1),
        ],
        out_specs=pl.BlockSpec((tm, D_MODEL), tok),
        compiler_params=_params(("parallel",)),
        name="out_proj",
    )(x2d, oa, ob, mod3, w_out, w_out)


def _mlp_kernel(x_ref, mod_ref, g_ref, gf_ref, w1_ref, w2_ref, o_ref, h_ref):
    j = pl.program_id(1)
    last = pl.num_programs(1) - 1
    mod = mod_ref[0]
    rows = o_ref.shape[0] // ROW_CHUNKS
    chunks = [slice(r * rows, (r + 1) * rows) for r in range(ROW_CHUNKS)]

    def ff_tile(h):
        u = jnp.maximum(jnp.dot(h, w1_ref[...], preferred_element_type=F32), 0.0)
        return jnp.dot((u * u).astype(BF16), w2_ref[...], preferred_element_type=F32)

    @pl.when(j == 0)
    def _():
        sh2 = mod[3:4]
        sc2 = mod[4:5]
        for sl in chunks:
            h = ((_rms(x_ref[sl, :]) * g_ref[...]) * (1.0 + sc2) + sh2).astype(BF16)
            h_ref[sl, :] = h
            o_ref[sl, :] = ff_tile(h)

    @pl.when((j > 0) & (j < last))
    def _():
        o_ref[...] += ff_tile(h_ref[...])

    @pl.when(j == last)
    def _():
        g2 = mod[5:6]
        for sl in chunks:
            x2 = x_ref[sl, :] + g2 * (o_ref[sl, :] + ff_tile(h_ref[sl, :]))
            o_ref[sl, :] = _rms(x2) * gf_ref[...]


def _mlp_call(x1, mod3, g_mlp, g_final, w1, w2, *, seq):
    tm = TM_MLP
    tf = TF_MLP
    assert D_FF // tf >= 2
    t = x1.shape[0]
    tiles_per_seq = seq // tm
    return pl.pallas_call(
        _mlp_kernel,
        out_shape=jax.ShapeDtypeStruct((t, D_MODEL), F32),
        grid=(t // tm, D_FF // tf),
        in_specs=[
            pl.BlockSpec((tm, D_MODEL), lambda i, j: (i, 0)),
            pl.BlockSpec((1, 6, D_MODEL), lambda i, j: (i // tiles_per_seq, 0, 0)),
            _const_spec((1, D_MODEL)),
            _const_spec((1, D_MODEL)),
            pl.BlockSpec((D_MODEL, tf), lambda i, j: (0, j)),
            pl.BlockSpec((tf, D_MODEL), lambda i, j: (j, 0)),
        ],
        out_specs=pl.BlockSpec((tm, D_MODEL), lambda i, j: (i, 0)),
        scratch_shapes=[pltpu.VMEM((tm, D_MODEL), BF16)],
        compiler_params=_params(("parallel", "arbitrary")),
        name="mlp",
    )(x1, mod3, g_mlp, g_final, w1, w2)


def _t5_bucket_table():
    q_loc = np.arange(BLOCK)[:, None]
    k_loc = np.arange(2 * BLOCK)[None, :]
    n = np.maximum(q_loc + BLOCK - k_loc, 0)
    max_exact = REL_BUCKETS // 2
    nf = np.maximum(n, 1).astype(np.float64)
    large = max_exact + (np.log(nf / max_exact) / math.log(REL_MAX_DIST / max_exact)
                         * (REL_BUCKETS - max_exact)).astype(np.int32)
    large = np.minimum(large, REL_BUCKETS - 1)
    return np.where(n < max_exact, n, large).astype(np.int32)


def _rope_tables(seq):
    half = MLA_ROPE_DIM // 2
    inv_freq = ROPE_THETA ** (-np.arange(half, dtype=np.float64) / half)
    ang = np.arange(seq, dtype=np.float64)[:, None] * inv_freq[None, :]
    zeros = np.zeros((seq, 128 - MLA_ROPE_DIM))
    cos_tab = np.concatenate([np.cos(ang), np.cos(ang), zeros], axis=1)
    sin_tab = np.concatenate([np.sin(ang), np.sin(ang), zeros], axis=1)
    return cos_tab.astype(np.float32), sin_tab.astype(np.float32)


def _rot_cols(w):
    half = w.shape[-1] // 2
    return jnp.concatenate([-w[..., half:], w[..., :half]], axis=-1)


def kernel(x, c, w_mod, b_mod, attn_norm_g, w_in, swa_sinks, rel_bias, mla_q_norm_g, w_uq,
           mla_kv_norm_g, w_ukv, w_out, mlp_norm_g, w_ff1, w_ff2, final_norm_g):
    batch, seq, _ = x.shape
    depth = w_mod.shape[0]
    assert depth == 1
    t = batch * seq
    x2d = x.reshape(t, D_MODEL)
    l = 0

    w_kr = w_in[l][:, OFF_MLA_KR:OFF_MLA_KR + MLA_ROPE_DIM]
    w_main = jnp.concatenate(
        [w_in[l][:, OFF_SWA_K:OFF_SWA_V], w_in[l][:, OFF_MLA_CQ:OFF_MLA_KR],
         w_kr, _rot_cols(w_kr)], axis=1).astype(BF16)
    w_t = jnp.concatenate(
        [w_in[l][:, :SWA_Q_COLS], w_in[l][:, OFF_SWA_V:OFF_MLA_CQ]], axis=1).T.astype(BF16)

    wq = w_uq[l].reshape(MLA_Q_RANK, MLA_HEADS, MLA_QK_DIM)
    wq_nope = wq[..., :MLA_NOPE_DIM]
    wq_rope = wq[..., MLA_NOPE_DIM:]
    hh = MLA_HEADS // 2
    zq = jnp.zeros((MLA_Q_RANK, hh, 128 - MLA_ROPE_DIM), F32)
    wq_main = jnp.concatenate(
        [jnp.concatenate([wq_nope[:, :hh], wq_rope[:, :hh], zq], axis=-1),
         jnp.concatenate([wq_nope[:, hh:], zq, wq_rope[:, hh:]], axis=-1)], axis=1).reshape(
        MLA_Q_RANK, MLA_HEADS * MLA_QK_PAD).astype(BF16)
    wq_rot_all = _rot_cols(wq_rope)
    wq_rot = jnp.concatenate([wq_rot_all[:, :hh], wq_rot_all[:, hh:]], axis=-1).reshape(
        MLA_Q_RANK, hh * 128).astype(BF16)

    wkv = w_ukv[l].reshape(MLA_KV_RANK, MLA_HEADS, MLA_NOPE_DIM + MLA_V_DIM)
    w_uk = wkv[..., :MLA_NOPE_DIM].reshape(MLA_KV_RANK, MLA_HEADS * MLA_NOPE_DIM).astype(BF16)
    w_vt = wkv[..., MLA_NOPE_DIM:].reshape(MLA_KV_RANK, MLA_HEADS * MLA_V_DIM).T.astype(BF16)

    cos_tab, sin_tab = _rope_tables(seq)

    c8 = jnp.pad(c, ((0, 8 - batch), (0, 0)))
    mod = _mod_call(c8, w_mod[l], b_mod[l].reshape(1, -1))[:batch]
    mod3 = mod.reshape(batch, 6, D_MODEL)

    bias_tab = _bias_call(rel_bias, _t5_bucket_table().T)

    qat, ka, vat, qb, kcat, vt, w_out_b16, w1, w2 = _proj_call(
        x2d, mod3, attn_norm_g[l].reshape(1, -1), w_main, w_t,
        mla_q_norm_g[l].reshape(1, -1), mla_kv_norm_g[l].reshape(1, -1),
        wq_main, wq_rot, w_uk, w_vt, cos_tab, sin_tab, w_out[l], w_ff1[l], w_ff2[l],
        batch=batch, seq=seq)

    oa = _swa_call(swa_sinks[l], qat, ka, vat, bias_tab, batch=batch, seq=seq)
    ob = _mla_call(qb, kcat, vt, batch=batch, seq=seq)

    x1 = _out_call(x2d, oa, ob, mod3, w_out_b16, seq=seq)
    out = _mlp_call(x1, mod3, mlp_norm_g[l].reshape(1, -1), final_norm_g.reshape(1, -1),
                    w1, w2, seq=seq)
    return out.reshape(batch, seq, D_MODEL)
```

```python
import functools
import math

import jax
import jax.numpy as jnp
import numpy as np
from jax import lax
from jax.experimental import pallas as pl
from jax.experimental.pallas import tpu as pltpu

F32 = jnp.float32
BF16 = jnp.bfloat16

D_MODEL = 2048
BLOCK = 128
EPS = 1e-6

SWA_HEADS = 16
SWA_KV_HEADS = 2
SWA_HEAD_DIM = 64
SWA_GROUP = SWA_HEADS // SWA_KV_HEADS
WINDOW = 128
REL_BUCKETS = 32
REL_MAX_DIST = 128

MLA_HEADS = 8
MLA_Q_RANK = 384
MLA_KV_RANK = 128
MLA_NOPE_DIM = 128
MLA_ROPE_DIM = 64
MLA_V_DIM = 128
MLA_QK_DIM = MLA_NOPE_DIM + MLA_ROPE_DIM
MLA_QK_PAD = 256
ROPE_THETA = 10000.0
MLA_Q_SCALE = MLA_QK_DIM ** -0.5 * math.log2(math.e)
D_FF = 4 * D_MODEL

SWA_Q_COLS = SWA_HEADS * SWA_HEAD_DIM
SWA_KV_COLS = SWA_KV_HEADS * SWA_HEAD_DIM
OFF_SWA_K = SWA_Q_COLS
OFF_SWA_V = OFF_SWA_K + SWA_KV_COLS
OFF_MLA_CQ = OFF_SWA_V + SWA_KV_COLS
OFF_MLA_CKV = OFF_MLA_CQ + MLA_Q_RANK
OFF_MLA_KR = OFF_MLA_CKV + MLA_KV_RANK
P_CQ = SWA_KV_COLS
P_CKV = P_CQ + MLA_Q_RANK
P_KR = P_CKV + MLA_KV_RANK
P_COLS = P_KR + 128
PT_ROWS = SWA_Q_COLS + SWA_KV_COLS
LOG2E = math.log2(math.e)

VMEM_LIMIT_BYTES = 62 * 1024 * 1024

TM_PROJ = 512
TM_SWA = 1024
TM_OUT = 512
TM_MLP = 512
TF_MLP = 2048
TN_MOD = 1536
BIAS_HEADS_PER_STEP = 4
MLA_SLOTS = 2
MLA_COL_SPLIT = 2
ROW_CHUNKS = 2


def _params(sem):
    return pltpu.CompilerParams(dimension_semantics=sem, vmem_limit_bytes=VMEM_LIMIT_BYTES)


def _const_spec(shape):
    nd = len(shape)
    return pl.BlockSpec(shape, lambda *_: (0,) * nd, pipeline_mode=pl.Buffered(1))


def _rms(x):
    return x * lax.rsqrt(jnp.mean(x * x, axis=-1, keepdims=True) + EPS)


def _mod_kernel(c_ref, w_ref, b_ref, o_ref):
    c = c_ref[...]
    ca = c * (1.0 / (1.0 + jnp.exp(-c)))
    o_ref[...] = jnp.dot(ca.astype(BF16), w_ref[...].astype(BF16),
                         preferred_element_type=F32) + b_ref[...]


def _mod_call(c8, w_mod, b_mod):
    n = w_mod.shape[1]
    return pl.pallas_call(
        _mod_kernel,
        out_shape=jax.ShapeDtypeStruct((8, n), F32),
        grid=(n // TN_MOD,),
        in_specs=[
            pl.BlockSpec((8, D_MODEL), lambda j: (0, 0)),
            pl.BlockSpec((D_MODEL, TN_MOD), lambda j: (0, j)),
            pl.BlockSpec((1, TN_MOD), lambda j: (0, j)),
        ],
        out_specs=pl.BlockSpec((8, TN_MOD), lambda j: (0, j)),
        compiler_params=_params(("arbitrary",)),
        name="mod",
    )(c8, w_mod, b_mod)


def _bias_kernel(rel_ref, bucket_ref, o_ref):
    bucket = bucket_ref[...]
    k_loc = lax.broadcasted_iota(jnp.int32, bucket.shape, 0)
    q_loc = lax.broadcasted_iota(jnp.int32, bucket.shape, 1)
    dist = q_loc + BLOCK - k_loc
    in_window = (dist >= 0) & (dist < WINDOW)
    for i in range(BIAS_HEADS_PER_STEP):
        h = pl.program_id(0) * BIAS_HEADS_PER_STEP + i
        acc = jnp.zeros(bucket.shape, F32)
        for k in range(REL_BUCKETS):
            acc = jnp.where(bucket == k, rel_ref[k, h], acc)
        acc = acc * LOG2E
        o_ref[0, i] = jnp.where(in_window & (k_loc >= BLOCK), acc, -jnp.inf)
        o_ref[1, i] = jnp.where(in_window, acc, -jnp.inf)


def _bias_call(rel_bias, bucket_t):
    hs = BIAS_HEADS_PER_STEP
    return pl.pallas_call(
        _bias_kernel,
        out_shape=jax.ShapeDtypeStruct((2, SWA_HEADS, 2 * BLOCK, BLOCK), F32),
        grid=(SWA_HEADS // hs,),
        in_specs=[
            pl.BlockSpec(memory_space=pltpu.SMEM),
            pl.BlockSpec((2 * BLOCK, BLOCK), lambda s: (0, 0)),
        ],
        out_specs=pl.BlockSpec((2, hs, 2 * BLOCK, BLOCK), lambda s: (0, s, 0, 0)),
        compiler_params=_params(("arbitrary",)),
        name="t5_bias",
    )(rel_bias, bucket_t)


def _proj_kernel(x_ref, mod_ref, g_ref, win_ref, wt_ref, gq_ref, gkv_ref, wqm_ref, wqr_ref,
                 wuk_ref, wvt_ref, cos_ref, sin_ref, wo_ref, w1_ref, w2_ref,
                 qat_ref, ka_ref, vat_ref, qb_ref, kc_ref, vt_ref, wob_ref, w1b_ref, w2b_ref,
                 *, q_scale):
    wob_ref[...] = wo_ref[...].astype(BF16)
    w1b_ref[...] = w1_ref[...].astype(BF16)
    w2b_ref[...] = w2_ref[...].astype(BF16)

    x = x_ref[...]
    mod = mod_ref[0]
    sh1 = mod[0:1]
    sc1 = mod[1:2]
    h = ((_rms(x) * g_ref[...]) * (1.0 + sc1) + sh1).astype(BF16)
    proj = jnp.dot(h, win_ref[...], preferred_element_type=F32)
    proj_t = lax.dot_general(wt_ref[...], h, (((1,), (1,)), ((), ())),
                             preferred_element_type=F32)

    qat_ref[...] = (proj_t[:SWA_Q_COLS] * (SWA_HEAD_DIM ** -0.5 * LOG2E)).astype(BF16)
    vat_ref[...] = proj_t[SWA_Q_COLS:].astype(BF16)
    ka_ref[...] = proj[:, :P_CQ].astype(BF16)

    cq = (_rms(proj[:, P_CQ:P_CKV]) * gq_ref[...]).astype(BF16)
    ckv = (_rms(proj[:, P_CKV:P_KR]) * gkv_ref[...]).astype(BF16)
    cos_lo = cos_ref[...]
    sin_lo = sin_ref[...]
    cos_hi = pltpu.roll(cos_lo, MLA_ROPE_DIM, 1)
    sin_hi = pltpu.roll(sin_lo, MLA_ROPE_DIM, 1)
    kr = proj[:, P_KR:P_COLS]
    kr_sw = pltpu.roll(kr, MLA_ROPE_DIM, 1)
    krope_lo = (kr * cos_lo + kr_sw * sin_lo).astype(BF16)
    krope_hi = (kr_sw * cos_hi + kr * sin_hi).astype(BF16)

    qmain = jnp.dot(cq, wqm_ref[...], preferred_element_type=F32)
    qrot = jnp.dot(cq, wqr_ref[...], preferred_element_type=F32)
    knope = jnp.dot(ckv, wuk_ref[...], preferred_element_type=F32)
    vt = lax.dot_general(wvt_ref[...], ckv, (((1,), (1,)), ((), ())),
                         preferred_element_type=F32)

    half_heads = MLA_HEADS // 2
    for hd in range(MLA_HEADS):
        lo = hd * MLA_QK_PAD
        mid = lo + MLA_NOPE_DIM
        hi = lo + MLA_QK_PAD
        low = hd < half_heads
        cos_t, sin_t = (cos_lo, sin_lo) if low else (cos_hi, sin_hi)
        rot = qrot[:, (hd % half_heads) * 128:(hd % half_heads + 1) * 128]
        qb_ref[:, lo:mid] = (qmain[:, lo:mid] * q_scale).astype(BF16)
        qb_ref[:, mid:hi] = ((qmain[:, mid:hi] * cos_t + rot * sin_t) * q_scale).astype(BF16)
        kc_ref[:, lo:mid] = knope[:, hd * MLA_NOPE_DIM:(hd + 1) * MLA_NOPE_DIM].astype(BF16)
        kc_ref[:, mid:hi] = krope_lo if low else krope_hi
        vt_ref[0, hd, 0] = vt[hd * MLA_V_DIM:(hd + 1) * MLA_V_DIM].astype(BF16)


def _proj_call(x2d, mod3, g_attn, w_main, w_t, gq, gkv, wq_main, wq_rot, w_uk, w_vt,
               cos_tab, sin_tab, w_out, w_ff1, w_ff2, *, batch, seq):
    tm = TM_PROJ
    tiles_per_seq = seq // tm
    t = batch * seq
    steps = t // tm
    slab = lambda w: (w.shape[0] // steps, w.shape[1])
    assert all(w.shape[0] % (steps * 16) == 0 for w in (w_out, w_ff1, w_ff2))
    tok = lambda i: (i, 0)
    tok_t = lambda i: (0, i)
    pos = lambda i: (i % tiles_per_seq, 0)
    out_shape = (
        jax.ShapeDtypeStruct((SWA_Q_COLS, t), BF16),
        jax.ShapeDtypeStruct((t, SWA_KV_COLS), BF16),
        jax.ShapeDtypeStruct((SWA_KV_COLS, t), BF16),
        jax.ShapeDtypeStruct((t, MLA_HEADS * MLA_QK_PAD), BF16),
        jax.ShapeDtypeStruct((t, MLA_HEADS * MLA_QK_PAD), BF16),
        jax.ShapeDtypeStruct((batch, MLA_HEADS, tiles_per_seq, MLA_V_DIM, tm), BF16),
        jax.ShapeDtypeStruct(w_out.shape, BF16),
        jax.ShapeDtypeStruct(w_ff1.shape, BF16),
        jax.ShapeDtypeStruct(w_ff2.shape, BF16),
    )
    return pl.pallas_call(
        functools.partial(_proj_kernel, q_scale=MLA_Q_SCALE),
        out_shape=out_shape,
        grid=(t // tm,),
        in_specs=[
            pl.BlockSpec((tm, D_MODEL), tok),
            pl.BlockSpec((1, 6, D_MODEL), lambda i: (i // tiles_per_seq, 0, 0)),
            _const_spec((1, D_MODEL)),
            _const_spec((D_MODEL, P_COLS)),
            _const_spec((PT_ROWS, D_MODEL)),
            _const_spec((1, MLA_Q_RANK)),
            _const_spec((1, MLA_KV_RANK)),
            _const_spec((MLA_Q_RANK, MLA_HEADS * MLA_QK_PAD)),
            _const_spec((MLA_Q_RANK, MLA_HEADS // 2 * 128)),
            _const_spec((MLA_KV_RANK, MLA_HEADS * MLA_NOPE_DIM)),
            _const_spec((MLA_HEADS * MLA_V_DIM, MLA_KV_RANK)),
            pl.BlockSpec((tm, 128), pos),
            pl.BlockSpec((tm, 128), pos),
            pl.BlockSpec(slab(w_out), tok),
            pl.BlockSpec(slab(w_ff1), tok),
            pl.BlockSpec(slab(w_ff2), tok),
        ],
        out_specs=(
            pl.BlockSpec((SWA_Q_COLS, tm), tok_t),
            pl.BlockSpec((tm, SWA_KV_COLS), tok),
            pl.BlockSpec((SWA_KV_COLS, tm), tok_t),
            pl.BlockSpec((tm, MLA_HEADS * MLA_QK_PAD), tok),
            pl.BlockSpec((tm, MLA_HEADS * MLA_QK_PAD), tok),
            pl.BlockSpec((1, MLA_HEADS, 1, MLA_V_DIM, tm),
                         lambda i: (i // tiles_per_seq, 0, i % tiles_per_seq, 0, 0)),
            pl.BlockSpec(slab(w_out), tok),
            pl.BlockSpec(slab(w_ff1), tok),
            pl.BlockSpec(slab(w_ff2), tok),
        ),
        compiler_params=_params(("parallel",)),
        name="in_proj",
    )(x2d, mod3, g_attn, w_main, w_t, gq, gkv, wq_main, wq_rot, w_uk, w_vt,
      cos_tab, sin_tab, w_out, w_ff1, w_ff2)


def _swa_kernel(sinks_ref, qt_ref, kp_ref, kc_ref, vtp_ref, vtc_ref, bias_ref, o_ref,
                s_ref, p_ref, inv_ref, *, blocks_per_step):
    first_step = pl.program_id(1) == 0
    k_all = jnp.concatenate([kp_ref[...], kc_ref[...]], axis=0)
    vt_all = jnp.concatenate([vtp_ref[...], vtc_ref[...]], axis=1)
    zeros_q = jnp.zeros((SWA_HEAD_DIM, SWA_GROUP * BLOCK), BF16)
    units = [(i, g) for i in range(blocks_per_step) for g in range(SWA_KV_HEADS)]

    def stage_scores(u):
        i, g = units[u]
        k_band = k_all[i * BLOCK:(i + 2) * BLOCK]
        qt = qt_ref[:, i * BLOCK:(i + 1) * BLOCK]
        q_g = jnp.concatenate(
            [qt[(g * SWA_GROUP + hh) * SWA_HEAD_DIM:(g * SWA_GROUP + hh + 1) * SWA_HEAD_DIM]
             for hh in range(SWA_GROUP)], axis=1)
        rhs = jnp.concatenate([q_g, zeros_q] if g == 0 else [zeros_q, q_g], axis=0)
        s_ref[u % 2] = jnp.dot(k_band, rhs, preferred_element_type=F32)

    def stage_softmax(u):
        i, g = units[u]
        variant = jnp.where(first_step, 0, 1) if i == 0 else 1
        for hh in range(SWA_GROUP):
            hd = g * SWA_GROUP + hh
            st = s_ref[u % 2, :, hh * BLOCK:(hh + 1) * BLOCK] + bias_ref[variant, hd]
            sink = sinks_ref[hd] * LOG2E
            m = jnp.maximum(jnp.max(st, axis=0, keepdims=True), sink)
            p = jnp.exp2(st - m)
            denom = jnp.sum(p, axis=0, keepdims=True) + jnp.exp2(sink - m)
            inv_ref[u % 2, hh] = 1.0 / denom
            p_ref[u % 2, :, hh * BLOCK:(hh + 1) * BLOCK] = p.astype(BF16)

    def stage_pv(u):
        i, g = units[u]
        vt_band = vt_all[g * SWA_HEAD_DIM:(g + 1) * SWA_HEAD_DIM, i * BLOCK:(i + 2) * BLOCK]
        ot_g = jnp.dot(vt_band, p_ref[u % 2], preferred_element_type=F32)
        for pair in range(SWA_GROUP // 2):
            h0 = 2 * pair
            slab = jnp.concatenate(
                [ot_g[:, (h0 + e) * BLOCK:(h0 + e + 1) * BLOCK] * inv_ref[u % 2, h0 + e]
                 for e in range(2)], axis=0)
            col = (g * SWA_GROUP + h0) * SWA_HEAD_DIM
            o_ref[i * BLOCK:(i + 1) * BLOCK, col:col + 2 * SWA_HEAD_DIM] = (
                slab.T.astype(BF16))

    n = len(units)
    for t in range(n + 2):
        if 0 <= t - 2 < n:
            stage_pv(t - 2)
        if 0 <= t - 1 < n:
            stage_softmax(t - 1)
        if t < n:
            stage_scores(t)


def _swa_call(sinks, qat, ka, vat, bias_tab, *, batch, seq):
    step = TM_SWA
    blocks_per_step = step // BLOCK
    steps_per_seq = seq // step
    nb = seq // BLOCK
    prev_blk = lambda b, s: b * nb + jnp.maximum(s * blocks_per_step - 1, 0)
    return pl.pallas_call(
        functools.partial(_swa_kernel, blocks_per_step=blocks_per_step),
        out_shape=jax.ShapeDtypeStruct((batch * seq, SWA_Q_COLS), BF16),
        grid=(batch, steps_per_seq),
        in_specs=[
            pl.BlockSpec(memory_space=pltpu.SMEM),
            pl.BlockSpec((SWA_Q_COLS, step), lambda b, s: (0, b * steps_per_seq + s)),
            pl.BlockSpec((BLOCK, SWA_KV_COLS), lambda b, s: (prev_blk(b, s), 0)),
            pl.BlockSpec((step, SWA_KV_COLS), lambda b, s: (b * steps_per_seq + s, 0)),
            pl.BlockSpec((SWA_KV_COLS, BLOCK), lambda b, s: (0, prev_blk(b, s))),
            pl.BlockSpec((SWA_KV_COLS, step), lambda b, s: (0, b * steps_per_seq + s)),
            _const_spec((2, SWA_HEADS, 2 * BLOCK, BLOCK)),
        ],
        out_specs=pl.BlockSpec((step, SWA_Q_COLS), lambda b, s: (b * steps_per_seq + s, 0)),
        scratch_shapes=[
            pltpu.VMEM((2, 2 * BLOCK, SWA_GROUP * BLOCK), F32),
            pltpu.VMEM((2, 2 * BLOCK, SWA_GROUP * BLOCK), BF16),
            pltpu.VMEM((2, SWA_GROUP, 1, BLOCK), F32),
        ],
        compiler_params=_params(("parallel", "arbitrary")),
        name="swa",
    )(sinks, qat, ka, ka, vat, vat, bias_tab)


def _mla_kernel(q_ref, k_ref, vt_ref, o_ref, s_ref, p_ref, al_ref, m_ref, l_ref, acc_ref,
                *, tile, n_tiles):
    pairs = [(qi, j) for qi in range(n_tiles) for j in range(qi + 1)]
    width = tile // MLA_COL_SPLIT

    def geometry(u, c):
        qi, j = pairs[u]
        cols = slice(c * width, (c + 1) * width)
        n_keys = (c + 1) * width if j == qi else tile
        return qi, j, cols, n_keys

    def stage_scores(u, c):
        qi, j, cols, n_keys = geometry(u, c)
        k = k_ref[j * tile:j * tile + n_keys, :]
        q = q_ref[qi * tile + c * width:qi * tile + (c + 1) * width, :]
        s_ref[u % MLA_SLOTS, :n_keys, cols] = lax.dot_general(
            k, q, (((1,), (1,)), ((), ())), preferred_element_type=F32)

    def stage_softmax(u, c):
        qi, j, cols, n_keys = geometry(u, c)
        st = s_ref[u % MLA_SLOTS, :n_keys, cols]
        if j == qi:
            kk = lax.broadcasted_iota(jnp.int32, st.shape, 0)
            qq = lax.broadcasted_iota(jnp.int32, st.shape, 1) + c * width
            st = jnp.where(kk <= qq, st, -jnp.inf)
        cmax = jnp.max(st, axis=0, keepdims=True)
        if j == 0:
            m_new = cmax
            p = jnp.exp2(st - m_new)
            l_new = jnp.sum(p, axis=0, keepdims=True)
        else:
            m_old = m_ref[qi % 2, :, cols]
            m_new = jnp.maximum(m_old, cmax)
            alpha = jnp.exp2(m_old - m_new)
            p = jnp.exp2(st - m_new)
            l_new = alpha * l_ref[qi % 2, :, cols] + jnp.sum(p, axis=0, keepdims=True)
            al_ref[u % MLA_SLOTS, :, cols] = alpha
        m_ref[qi % 2, :, cols] = m_new
        l_ref[qi % 2, :, cols] = l_new
        p_ref[u % MLA_SLOTS, :n_keys, cols] = p.astype(BF16)

    def stage_pv(u, c):
        qi, j, cols, n_keys = geometry(u, c)
        pv = jnp.dot(vt_ref[0, 0, j, :, :n_keys], p_ref[u % MLA_SLOTS, :n_keys, cols],
                     preferred_element_type=F32)
        if j == 0:
            acc = pv
        else:
            acc = al_ref[u % MLA_SLOTS, :, cols] * acc_ref[:, cols] + pv
        if j == qi:
            rows = slice(qi * tile + c * width, qi * tile + (c + 1) * width)
            o_ref[rows, :] = (acc / l_ref[qi % 2, :, cols]).T.astype(BF16)
        else:
            acc_ref[:, cols] = acc

    n = len(pairs)
    skew = MLA_SLOTS - 1
    for t in range(n + 2 * skew):
        for c in range(MLA_COL_SPLIT):
            if 0 <= t - 2 * skew < n:
                stage_pv(t - 2 * skew, c)
            if 0 <= t - skew < n:
                stage_softmax(t - skew, c)
            if t < n:
                stage_scores(t, c)


def _mla_call(qb, kcat, vt, *, batch, seq):
    tile = TM_PROJ
    n_tiles = seq // tile
    return pl.pallas_call(
        functools.partial(_mla_kernel, tile=tile, n_tiles=n_tiles),
        out_shape=jax.ShapeDtypeStruct((batch * seq, MLA_HEADS * MLA_V_DIM), BF16),
        grid=(batch, MLA_HEADS),
        in_specs=[
            pl.BlockSpec((seq, MLA_QK_PAD), lambda b, h: (b, h)),
            pl.BlockSpec((seq, MLA_QK_PAD), lambda b, h: (b, h)),
            pl.BlockSpec((1, 1, n_tiles, MLA_V_DIM, tile), lambda b, h: (b, h, 0, 0, 0)),
        ],
        out_specs=pl.BlockSpec((seq, MLA_V_DIM), lambda b, h: (b, h)),
        scratch_shapes=[
            pltpu.VMEM((MLA_SLOTS, tile, tile), F32),
            pltpu.VMEM((MLA_SLOTS, tile, tile), BF16),
            pltpu.VMEM((MLA_SLOTS, 1, tile), F32),
            pltpu.VMEM((2, 1, tile), F32),
            pltpu.VMEM((2, 1, tile), F32),
            pltpu.VMEM((MLA_V_DIM, tile), F32),
        ],
        compiler_params=_params(("parallel", "parallel")),
        name="mla",
    )(qb, kcat, vt)


def _out_kernel(x_ref, oa_ref, ob_ref, mod_ref, wa_ref, wb_ref, x1_ref):
    g1 = mod_ref[0][2:3]
    y = (jnp.dot(oa_ref[...], wa_ref[...], preferred_element_type=F32)
         + jnp.dot(ob_ref[...], wb_ref[...], preferred_element_type=F32))
    x1_ref[...] = x_ref[...] + g1 * y


def _out_call(x2d, oa, ob, mod3, w_out, *, seq):
    tm = TM_OUT
    t = x2d.shape[0]
    tiles_per_seq = seq // tm
    tok = lambda i: (i, 0)
    w_half = lambda half: pl.BlockSpec((SWA_Q_COLS, D_MODEL), lambda i: (half, 0),
                                       pipeline_mode=pl.Buffered(1))
    assert w_out.shape[0] == 2 * SWA_Q_COLS
    return pl.pallas_call(
        _out_kernel,
        out_shape=jax.ShapeDtypeStruct((t, D_MODEL), F32),
        grid=(t // tm,),
        in_specs=[
            pl.BlockSpec((tm, D_MODEL), tok),
            pl.BlockSpec((tm, SWA_Q_COLS), tok),
            pl.BlockSpec((tm, MLA_HEADS * MLA_V_DIM), tok),
            pl.BlockSpec((1, 6, D_MODEL), lambda i: (i // tiles_per_seq, 0, 0)),
            w_half(0),
            w_half(1),
        ],
        out_specs=pl.BlockSpec((tm, D_MODEL), tok),
        compiler_params=_params(("parallel",)),
        name="out_proj",
    )(x2d, oa, ob, mod3, w_out, w_out)


def _mlp_kernel(x_ref, mod_ref, g_ref, gf_ref, w1_ref, w2_ref, o_ref, h_ref):
    j = pl.program_id(1)
    last = pl.num_programs(1) - 1
    mod = mod_ref[0]
    rows = o_ref.shape[0] // ROW_CHUNKS
    chunks = [slice(r * rows, (r + 1) * rows) for r in range(ROW_CHUNKS)]

    def ff_tile(h):
        u = jnp.maximum(jnp.dot(h, w1_ref[...], preferred_element_type=F32), 0.0)
        return jnp.dot((u * u).astype(BF16), w2_ref[...], preferred_element_type=F32)

    @pl.when(j == 0)
    def _():
        sh2 = mod[3:4]
        sc2 = mod[4:5]
        for sl in chunks:
            h = ((_rms(x_ref[sl, :]) * g_ref[...]) * (1.0 + sc2) + sh2).astype(BF16)
            h_ref[sl, :] = h
            o_ref[sl, :] = ff_tile(h)

    @pl.when((j > 0) & (j < last))
    def _():
        o_ref[...] += ff_tile(h_ref[...])

    @pl.when(j == last)
    def _():
        g2 = mod[5:6]
        for sl in chunks:
            x2 = x_ref[sl, :] + g2 * (o_ref[sl, :] + ff_tile(h_ref[sl, :]))
            o_ref[sl, :] = _rms(x2) * gf_ref[...]


def _mlp_call(x1, mod3, g_mlp, g_final, w1, w2, *, seq):
    tm = TM_MLP
    tf = TF_MLP
    assert D_FF // tf >= 2
    t = x1.shape[0]
    tiles_per_seq = seq // tm
    return pl.pallas_call(
        _mlp_kernel,
        out_shape=jax.ShapeDtypeStruct((t, D_MODEL), F32),
        grid=(t // tm, D_FF // tf),
        in_specs=[
            pl.BlockSpec((tm, D_MODEL), lambda i, j: (i, 0)),
            pl.BlockSpec((1, 6, D_MODEL), lambda i, j: (i // tiles_per_seq, 0, 0)),
            _const_spec((1, D_MODEL)),
            _const_spec((1, D_MODEL)),
            pl.BlockSpec((D_MODEL, tf), lambda i, j: (0, j)),
            pl.BlockSpec((tf, D_MODEL), lambda i, j: (j, 0)),
        ],
        out_specs=pl.BlockSpec((tm, D_MODEL), lambda i, j: (i, 0)),
        scratch_shapes=[pltpu.VMEM((tm, D_MODEL), BF16)],
        compiler_params=_params(("parallel", "arbitrary")),
        name="mlp",
    )(x1, mod3, g_mlp, g_final, w1, w2)


def _t5_bucket_table():
    q_loc = np.arange(BLOCK)[:, None]
    k_loc = np.arange(2 * BLOCK)[None, :]
    n = np.maximum(q_loc + BLOCK - k_loc, 0)
    max_exact = REL_BUCKETS // 2
    nf = np.maximum(n, 1).astype(np.float64)
    large = max_exact + (np.log(nf / max_exact) / math.log(REL_MAX_DIST / max_exact)
                         * (REL_BUCKETS - max_exact)).astype(np.int32)
    large = np.minimum(large, REL_BUCKETS - 1)
    return np.where(n < max_exact, n, large).astype(np.int32)


def _rope_tables(seq):
    half = MLA_ROPE_DIM // 2
    inv_freq = ROPE_THETA ** (-np.arange(half, dtype=np.float64) / half)
    ang = np.arange(seq, dtype=np.float64)[:, None] * inv_freq[None, :]
    zeros = np.zeros((seq, 128 - MLA_ROPE_DIM))
    cos_tab = np.concatenate([np.cos(ang), np.cos(ang), zeros], axis=1)
    sin_tab = np.concatenate([np.sin(ang), np.sin(ang), zeros], axis=1)
    return cos_tab.astype(np.float32), sin_tab.astype(np.float32)


def _rot_cols(w):
    half = w.shape[-1] // 2
    return jnp.concatenate([-w[..., half:], w[..., :half]], axis=-1)


def kernel(x, c, w_mod, b_mod, attn_norm_g, w_in, swa_sinks, rel_bias, mla_q_norm_g, w_uq,
           mla_kv_norm_g, w_ukv, w_out, mlp_norm_g, w_ff1, w_ff2, final_norm_g):
    batch, seq, _ = x.shape
    depth = w_mod.shape[0]
    assert depth == 1
    t = batch * seq
    x2d = x.reshape(t, D_MODEL)
    l = 0

    w_kr = w_in[l][:, OFF_MLA_KR:OFF_MLA_KR + MLA_ROPE_DIM]
    w_main = jnp.concatenate(
        [w_in[l][:, OFF_SWA_K:OFF_SWA_V], w_in[l][:, OFF_MLA_CQ:OFF_MLA_KR],
         w_kr, _rot_cols(w_kr)], axis=1).astype(BF16)
    w_t = jnp.concatenate(
        [w_in[l][:, :SWA_Q_COLS], w_in[l][:, OFF_SWA_V:OFF_MLA_CQ]], axis=1).T.astype(BF16)

    wq = w_uq[l].reshape(MLA_Q_RANK, MLA_HEADS, MLA_QK_DIM)
    wq_nope = wq[..., :MLA_NOPE_DIM]
    wq_rope = wq[..., MLA_NOPE_DIM:]
    hh = MLA_HEADS // 2
    zq = jnp.zeros((MLA_Q_RANK, hh, 128 - MLA_ROPE_DIM), F32)
    wq_main = jnp.concatenate(
        [jnp.concatenate([wq_nope[:, :hh], wq_rope[:, :hh], zq], axis=-1),
         jnp.concatenate([wq_nope[:, hh:], zq, wq_rope[:, hh:]], axis=-1)], axis=1).reshape(
        MLA_Q_RANK, MLA_HEADS * MLA_QK_PAD).astype(BF16)
    wq_rot_all = _rot_cols(wq_rope)
    wq_rot = jnp.concatenate([wq_rot_all[:, :hh], wq_rot_all[:, hh:]], axis=-1).reshape(
        MLA_Q_RANK, hh * 128).astype(BF16)

    wkv = w_ukv[l].reshape(MLA_KV_RANK, MLA_HEADS, MLA_NOPE_DIM + MLA_V_DIM)
    w_uk = wkv[..., :MLA_NOPE_DIM].reshape(MLA_KV_RANK, MLA_HEADS * MLA_NOPE_DIM).astype(BF16)
    w_vt = wkv[..., MLA_NOPE_DIM:].reshape(MLA_KV_RANK, MLA_HEADS * MLA_V_DIM).T.astype(BF16)

    cos_tab, sin_tab = _rope_tables(seq)

    c8 = jnp.pad(c, ((0, 8 - batch), (0, 0)))
    mod = _mod_call(c8, w_mod[l], b_mod[l].reshape(1, -1))[:batch]
    mod3 = mod.reshape(batch, 6, D_MODEL)

    bias_tab = _bias_call(rel_bias, _t5_bucket_table().T)

    qat, ka, vat, qb, kcat, vt, w_out_b16, w1, w2 = _proj_call(
        x2d, mod3, attn_norm_g[l].reshape(1, -1), w_main, w_t,
        mla_q_norm_g[l].reshape(1, -1), mla_kv_norm_g[l].reshape(1, -1),
        wq_main, wq_rot, w_uk, w_vt, cos_tab, sin_tab, w_out[l], w_ff1[l], w_ff2[l],
        batch=batch, seq=seq)

    oa = _swa_call(swa_sinks[l], qat, ka, vat, bias_tab, batch=batch, seq=seq)
    ob = _mla_call(qb, kcat, vt, batch=batch, seq=seq)

    x1 = _out_call(x2d, oa, ob, mod3, w_out_b16, seq=seq)
    out = _mlp_call(x1, mod3, mlp_norm_g[l].reshape(1, -1), final_norm_g.reshape(1, -1),
                    w1, w2, seq=seq)
    return out.reshape(batch, seq, D_MODEL)
```

```python
import functools
import math

import jax
import jax.numpy as jnp
import numpy as np
from jax import lax
from jax.experimental import pallas as pl
from jax.experimental.pallas import tpu as pltpu

F32 = jnp.float32
BF16 = jnp.bfloat16

D_MODEL = 2048
BLOCK = 128
EPS = 1e-6

SWA_HEADS = 16
SWA_KV_HEADS = 2
SWA_HEAD_DIM = 64
SWA_GROUP = SWA_HEADS // SWA_KV_HEADS
WINDOW = 128
REL_BUCKETS = 32
REL_MAX_DIST = 128

MLA_HEADS = 8
MLA_Q_RANK = 384
MLA_KV_RANK = 128
MLA_NOPE_DIM = 128
MLA_ROPE_DIM = 64
MLA_V_DIM = 128
MLA_QK_DIM = MLA_NOPE_DIM + MLA_ROPE_DIM
MLA_QK_PAD = 256
ROPE_THETA = 10000.0
MLA_Q_SCALE = MLA_QK_DIM ** -0.5 * math.log2(math.e)
D_FF = 4 * D_MODEL

SWA_Q_COLS = SWA_HEADS * SWA_HEAD_DIM
SWA_KV_COLS = SWA_KV_HEADS * SWA_HEAD_DIM
OFF_SWA_K = SWA_Q_COLS
OFF_SWA_V = OFF_SWA_K + SWA_KV_COLS
OFF_MLA_CQ = OFF_SWA_V + SWA_KV_COLS
OFF_MLA_CKV = OFF_MLA_CQ + MLA_Q_RANK
OFF_MLA_KR = OFF_MLA_CKV + MLA_KV_RANK
P_CQ = SWA_KV_COLS
P_CKV = P_CQ + MLA_Q_RANK
P_KR = P_CKV + MLA_KV_RANK
P_COLS = P_KR + 128
PT_ROWS = SWA_Q_COLS + SWA_KV_COLS
LOG2E = math.log2(math.e)

VMEM_LIMIT_BYTES = 62 * 1024 * 1024

TM_PROJ = 512
TM_SWA = 1024
TM_OUT = 512
TM_MLP = 512
TF_MLP = 2048
TN_MOD = 1024
MOD_PRE = 2
MOD_POST = 4
BIAS_HEADS_PER_STEP = 4
SWA_UNIT_HEADS = 8
MLA_SLOTS = 2
MLA_COL_SPLIT = 2
ROW_CHUNKS = 2


def _params(sem):
    return pltpu.CompilerParams(dimension_semantics=sem, vmem_limit_bytes=VMEM_LIMIT_BYTES)


def _const_spec(shape):
    nd = len(shape)
    return pl.BlockSpec(shape, lambda *_: (0,) * nd, pipeline_mode=pl.Buffered(1))


def _rms(x):
    return x * lax.rsqrt(jnp.mean(x * x, axis=-1, keepdims=True) + EPS)


def _mod_tile(c_ref, w_ref, b_ref):
    c = c_ref[...]
    ca = c * (1.0 / (1.0 + jnp.exp(-c)))
    return jnp.dot(ca.astype(BF16), w_ref[...].astype(BF16),
                   preferred_element_type=F32) + b_ref[...]


def _mod_kernel(c_ref, w_ref, b_ref, o_ref):
    o_ref[...] = _mod_tile(c_ref, w_ref, b_ref)


def _mod_call(c8, w_mod, b_mod):
    n = MOD_PRE * D_MODEL
    return pl.pallas_call(
        _mod_kernel,
        out_shape=jax.ShapeDtypeStruct((8, n), F32),
        grid=(n // TN_MOD,),
        in_specs=[
            pl.BlockSpec((8, D_MODEL), lambda j: (0, 0)),
            pl.BlockSpec((D_MODEL, TN_MOD), lambda j: (0, j)),
            pl.BlockSpec((1, TN_MOD), lambda j: (0, j)),
        ],
        out_specs=pl.BlockSpec((8, TN_MOD), lambda j: (0, j)),
        compiler_params=_params(("arbitrary",)),
        name="mod",
    )(c8, w_mod, b_mod)


def _bias_kernel(rel_ref, bucket_ref, o_ref):
    bucket = bucket_ref[...]
    k_loc = lax.broadcasted_iota(jnp.int32, bucket.shape, 0)
    q_loc = lax.broadcasted_iota(jnp.int32, bucket.shape, 1)
    dist = q_loc + BLOCK - k_loc
    in_window = (dist >= 0) & (dist < WINDOW)
    for i in range(BIAS_HEADS_PER_STEP):
        h = pl.program_id(0) * BIAS_HEADS_PER_STEP + i
        acc = jnp.zeros(bucket.shape, F32)
        for k in range(REL_BUCKETS):
            acc = jnp.where(bucket == k, rel_ref[k, h], acc)
        acc = acc * LOG2E
        o_ref[0, i] = jnp.where(in_window & (k_loc >= BLOCK), acc, -jnp.inf)
        o_ref[1, i] = jnp.where(in_window, acc, -jnp.inf)


def _bias_call(rel_bias, bucket_t):
    hs = BIAS_HEADS_PER_STEP
    return pl.pallas_call(
        _bias_kernel,
        out_shape=jax.ShapeDtypeStruct((2, SWA_HEADS, 2 * BLOCK, BLOCK), F32),
        grid=(SWA_HEADS // hs,),
        in_specs=[
            pl.BlockSpec(memory_space=pltpu.SMEM),
            pl.BlockSpec((2 * BLOCK, BLOCK), lambda s: (0, 0)),
        ],
        out_specs=pl.BlockSpec((2, hs, 2 * BLOCK, BLOCK), lambda s: (0, s, 0, 0)),
        compiler_params=_params(("arbitrary",)),
        name="t5_bias",
    )(rel_bias, bucket_t)


def _proj_kernel(x_ref, mod_ref, g_ref, win_ref, wt_ref, gq_ref, gkv_ref, wqm_ref, wqr_ref,
                 wuk_ref, wvt_ref, cos_ref, sin_ref, wo_ref, w1_ref, w2_ref,
                 qat_ref, ka_ref, vat_ref, qb_ref, kc_ref, vt_ref, wob_ref, w1b_ref, w2b_ref,
                 *, q_scale):
    wob_ref[...] = wo_ref[...].astype(BF16)
    w1b_ref[...] = w1_ref[...].astype(BF16)
    w2b_ref[...] = w2_ref[...].astype(BF16)

    x = x_ref[...]
    mod = mod_ref[0]
    sh1 = mod[0:1]
    sc1 = mod[1:2]
    h = ((_rms(x) * g_ref[...]) * (1.0 + sc1) + sh1).astype(BF16)
    proj = jnp.dot(h, win_ref[...], preferred_element_type=F32)
    proj_t = lax.dot_general(wt_ref[...], h, (((1,), (1,)), ((), ())),
                             preferred_element_type=F32)

    qat_ref[...] = (proj_t[:SWA_Q_COLS] * (SWA_HEAD_DIM ** -0.5 * LOG2E)).astype(BF16)
    vat_ref[...] = proj_t[SWA_Q_COLS:].astype(BF16)
    ka_ref[...] = proj[:, :P_CQ].astype(BF16)

    cq = (_rms(proj[:, P_CQ:P_CKV]) * gq_ref[...]).astype(BF16)
    ckv = (_rms(proj[:, P_CKV:P_KR]) * gkv_ref[...]).astype(BF16)
    cos_lo = cos_ref[...]
    sin_lo = sin_ref[...]
    cos_hi = pltpu.roll(cos_lo, MLA_ROPE_DIM, 1)
    sin_hi = pltpu.roll(sin_lo, MLA_ROPE_DIM, 1)
    kr = proj[:, P_KR:P_COLS]
    kr_sw = pltpu.roll(kr, MLA_ROPE_DIM, 1)
    krope_lo = (kr * cos_lo + kr_sw * sin_lo).astype(BF16)
    krope_hi = (kr_sw * cos_hi + kr * sin_hi).astype(BF16)

    qmain = jnp.dot(cq, wqm_ref[...], preferred_element_type=F32)
    qrot = jnp.dot(cq, wqr_ref[...], preferred_element_type=F32)
    knope = jnp.dot(ckv, wuk_ref[...], preferred_element_type=F32)
    vt = lax.dot_general(wvt_ref[...], ckv, (((1,), (1,)), ((), ())),
                         preferred_element_type=F32)

    half_heads = MLA_HEADS // 2
    for hd in range(MLA_HEADS):
        lo = hd * MLA_QK_PAD
        mid = lo + MLA_NOPE_DIM
        hi = lo + MLA_QK_PAD
        low = hd < half_heads
        cos_t, sin_t = (cos_lo, sin_lo) if low else (cos_hi, sin_hi)
        rot = qrot[:, (hd % half_heads) * 128:(hd % half_heads + 1) * 128]
        qb_ref[:, lo:mid] = (qmain[:, lo:mid] * q_scale).astype(BF16)
        qb_ref[:, mid:hi] = ((qmain[:, mid:hi] * cos_t + rot * sin_t) * q_scale).astype(BF16)
        kc_ref[:, lo:mid] = knope[:, hd * MLA_NOPE_DIM:(hd + 1) * MLA_NOPE_DIM].astype(BF16)
        kc_ref[:, mid:hi] = krope_lo if low else krope_hi
        vt_ref[0, hd, 0] = vt[hd * MLA_V_DIM:(hd + 1) * MLA_V_DIM].astype(BF16)


def _proj_call(x2d, mod3, g_attn, w_main, w_t, gq, gkv, wq_main, wq_rot, w_uk, w_vt,
               cos_tab, sin_tab, w_out, w_ff1, w_ff2, *, batch, seq):
    tm = TM_PROJ
    tiles_per_seq = seq // tm
    t = batch * seq
    steps = t // tm
    slab = lambda w: (w.shape[0] // steps, w.shape[1])
    assert all(w.shape[0] % (steps * 16) == 0 for w in (w_out, w_ff1, w_ff2))
    tok = lambda i: (i, 0)
    tok_t = lambda i: (0, i)
    pos = lambda i: (i % tiles_per_seq, 0)
    out_shape = (
        jax.ShapeDtypeStruct((SWA_Q_COLS, t), BF16),
        jax.ShapeDtypeStruct((t, SWA_KV_COLS), BF16),
        jax.ShapeDtypeStruct((SWA_KV_COLS, t), BF16),
        jax.ShapeDtypeStruct((t, MLA_HEADS * MLA_QK_PAD), BF16),
        jax.ShapeDtypeStruct((t, MLA_HEADS * MLA_QK_PAD), BF16),
        jax.ShapeDtypeStruct((batch, MLA_HEADS, tiles_per_seq, MLA_V_DIM, tm), BF16),
        jax.ShapeDtypeStruct(w_out.shape, BF16),
        jax.ShapeDtypeStruct(w_ff1.shape, BF16),
        jax.ShapeDtypeStruct(w_ff2.shape, BF16),
    )
    return pl.pallas_call(
        functools.partial(_proj_kernel, q_scale=MLA_Q_SCALE),
        out_shape=out_shape,
        grid=(t // tm,),
        in_specs=[
            pl.BlockSpec((tm, D_MODEL), tok),
            pl.BlockSpec((1, MOD_PRE, D_MODEL), lambda i: (i // tiles_per_seq, 0, 0)),
            _const_spec((1, D_MODEL)),
            _const_spec((D_MODEL, P_COLS)),
            _const_spec((PT_ROWS, D_MODEL)),
            _const_spec((1, MLA_Q_RANK)),
            _const_spec((1, MLA_KV_RANK)),
            _const_spec((MLA_Q_RANK, MLA_HEADS * MLA_QK_PAD)),
            _const_spec((MLA_Q_RANK, MLA_HEADS // 2 * 128)),
            _const_spec((MLA_KV_RANK, MLA_HEADS * MLA_NOPE_DIM)),
            _const_spec((MLA_HEADS * MLA_V_DIM, MLA_KV_RANK)),
            pl.BlockSpec((tm, 128), pos),
            pl.BlockSpec((tm, 128), pos),
            pl.BlockSpec(slab(w_out), tok),
            pl.BlockSpec(slab(w_ff1), tok),
            pl.BlockSpec(slab(w_ff2), tok),
        ],
        out_specs=(
            pl.BlockSpec((SWA_Q_COLS, tm), tok_t),
            pl.BlockSpec((tm, SWA_KV_COLS), tok),
            pl.BlockSpec((SWA_KV_COLS, tm), tok_t),
            pl.BlockSpec((tm, MLA_HEADS * MLA_QK_PAD), tok),
            pl.BlockSpec((tm, MLA_HEADS * MLA_QK_PAD), tok),
            pl.BlockSpec((1, MLA_HEADS, 1, MLA_V_DIM, tm),
                         lambda i: (i // tiles_per_seq, 0, i % tiles_per_seq, 0, 0)),
            pl.BlockSpec(slab(w_out), tok),
            pl.BlockSpec(slab(w_ff1), tok),
            pl.BlockSpec(slab(w_ff2), tok),
        ),
        compiler_params=_params(("parallel",)),
        name="in_proj",
    )(x2d, mod3, g_attn, w_main, w_t, gq, gkv, wq_main, wq_rot, w_uk, w_vt,
      cos_tab, sin_tab, w_out, w_ff1, w_ff2)


def _swa_kernel(sinks_ref, qt_ref, kp_ref, kc_ref, vtp_ref, vtc_ref, bias_ref,
                c_ref, wmod_ref, bmod_ref, o_ref, mod_ref,
                s_ref, p_ref, inv_ref, *, blocks_per_step):
    mod_ref[...] = _mod_tile(c_ref, wmod_ref, bmod_ref)

    first_step = pl.program_id(1) == 0
    k_all = jnp.concatenate([kp_ref[...], kc_ref[...]], axis=0)
    vt_all = jnp.concatenate([vtp_ref[...], vtc_ref[...]], axis=1)
    uh = SWA_UNIT_HEADS
    zeros_q = jnp.zeros((SWA_HEAD_DIM, uh * BLOCK), BF16)
    units = [(i, g, part) for i in range(blocks_per_step) for g in range(SWA_KV_HEADS)
             for part in range(SWA_GROUP // uh)]

    def stage_scores(u):
        i, g, part = units[u]
        h_first = g * SWA_GROUP + part * uh
        k_band = k_all[i * BLOCK:(i + 2) * BLOCK]
        qt = qt_ref[:, i * BLOCK:(i + 1) * BLOCK]
        q_g = jnp.concatenate(
            [qt[(h_first + hh) * SWA_HEAD_DIM:(h_first + hh + 1) * SWA_HEAD_DIM]
             for hh in range(uh)], axis=1)
        rhs = jnp.concatenate([q_g, zeros_q] if g == 0 else [zeros_q, q_g], axis=0)
        s_ref[u % 2] = jnp.dot(k_band, rhs, preferred_element_type=F32)

    def stage_softmax(u):
        i, g, part = units[u]
        variant = jnp.where(first_step, 0, 1) if i == 0 else 1
        for hh in range(uh):
            hd = g * SWA_GROUP + part * uh + hh
            st = s_ref[u % 2, :, hh * BLOCK:(hh + 1) * BLOCK] + bias_ref[variant, hd]
            sink = sinks_ref[hd] * LOG2E
            m = jnp.maximum(jnp.max(st, axis=0, keepdims=True), sink)
            p = jnp.exp2(st - m)
            denom = jnp.sum(p, axis=0, keepdims=True) + jnp.exp2(sink - m)
            inv_ref[u % 2, hh] = 1.0 / denom
            p_ref[u % 2, :, hh * BLOCK:(hh + 1) * BLOCK] = p.astype(BF16)

    def stage_pv(u):
        i, g, part = units[u]
        vt_band = vt_all[g * SWA_HEAD_DIM:(g + 1) * SWA_HEAD_DIM, i * BLOCK:(i + 2) * BLOCK]
        ot_g = jnp.dot(vt_band, p_ref[u % 2], preferred_element_type=F32)
        for pair in range(uh // 2):
            h0 = 2 * pair
            slab = jnp.concatenate(
                [ot_g[:, (h0 + e) * BLOCK:(h0 + e + 1) * BLOCK] * inv_ref[u % 2, h0 + e]
                 for e in range(2)], axis=0)
            col = (g * SWA_GROUP + part * uh + h0) * SWA_HEAD_DIM
            o_ref[i * BLOCK:(i + 1) * BLOCK, col:col + 2 * SWA_HEAD_DIM] = (
                slab.T.astype(BF16))

    n = len(units)
    for t in range(n + 2):
        if 0 <= t - 2 < n:
            stage_pv(t - 2)
        if 0 <= t - 1 < n:
            stage_softmax(t - 1)
        if t < n:
            stage_scores(t)


def _swa_call(sinks, qat, ka, vat, bias_tab, c8, w_mod, b_mod, *, batch, seq):
    step = TM_SWA
    blocks_per_step = step // BLOCK
    steps_per_seq = seq // step
    nb = seq // BLOCK
    prev_blk = lambda b, s: b * nb + jnp.maximum(s * blocks_per_step - 1, 0)
    n_steps = batch * steps_per_seq
    mod_cols = MOD_POST * D_MODEL
    tn = mod_cols // n_steps
    assert tn % 128 == 0 and (MOD_PRE * D_MODEL) % tn == 0
    mod_tile = lambda b, s: (0, b * steps_per_seq + s)
    mod_src_tile = lambda b, s: (0, MOD_PRE * D_MODEL // tn + b * steps_per_seq + s)
    return pl.pallas_call(
        functools.partial(_swa_kernel, blocks_per_step=blocks_per_step),
        out_shape=(jax.ShapeDtypeStruct((batch * seq, SWA_Q_COLS), BF16),
                   jax.ShapeDtypeStruct((8, mod_cols), F32)),
        grid=(batch, steps_per_seq),
        in_specs=[
            pl.BlockSpec(memory_space=pltpu.SMEM),
            pl.BlockSpec((SWA_Q_COLS, step), lambda b, s: (0, b * steps_per_seq + s)),
            pl.BlockSpec((BLOCK, SWA_KV_COLS), lambda b, s: (prev_blk(b, s), 0)),
            pl.BlockSpec((step, SWA_KV_COLS), lambda b, s: (b * steps_per_seq + s, 0)),
            pl.BlockSpec((SWA_KV_COLS, BLOCK), lambda b, s: (0, prev_blk(b, s))),
            pl.BlockSpec((SWA_KV_COLS, step), lambda b, s: (0, b * steps_per_seq + s)),
            _const_spec((2, SWA_HEADS, 2 * BLOCK, BLOCK)),
            _const_spec((8, D_MODEL)),
            pl.BlockSpec((D_MODEL, tn), mod_src_tile),
            pl.BlockSpec((1, tn), mod_src_tile),
        ],
        out_specs=(
            pl.BlockSpec((step, SWA_Q_COLS), lambda b, s: (b * steps_per_seq + s, 0)),
            pl.BlockSpec((8, tn), mod_tile),
        ),
        scratch_shapes=[
            pltpu.VMEM((2, 2 * BLOCK, SWA_UNIT_HEADS * BLOCK), F32),
            pltpu.VMEM((2, 2 * BLOCK, SWA_UNIT_HEADS * BLOCK), BF16),
            pltpu.VMEM((2, SWA_UNIT_HEADS, 1, BLOCK), F32),
        ],
        compiler_params=_params(("parallel", "arbitrary")),
        name="swa",
    )(sinks, qat, ka, ka, vat, vat, bias_tab, c8, w_mod, b_mod)


def _mla_kernel(q_ref, k_ref, vt_ref, o_ref, s_ref, p_ref, al_ref, m_ref, l_ref, acc_ref,
                *, tile, n_tiles):
    pairs = [(qi, j) for qi in range(n_tiles) for j in range(qi + 1)]
    width = tile // MLA_COL_SPLIT

    def geometry(u, c):
        qi, j = pairs[u]
        cols = slice(c * width, (c + 1) * width)
        n_keys = (c + 1) * width if j == qi else tile
        return qi, j, cols, n_keys

    def stage_scores(u, c):
        qi, j, cols, n_keys = geometry(u, c)
        k = k_ref[j * tile:j * tile + n_keys, :]
        q = q_ref[qi * tile + c * width:qi * tile + (c + 1) * width, :]
        s_ref[u % MLA_SLOTS, :n_keys, cols] = lax.dot_general(
            k, q, (((1,), (1,)), ((), ())), preferred_element_type=F32)

    def stage_softmax(u, c):
        qi, j, cols, n_keys = geometry(u, c)
        st = s_ref[u % MLA_SLOTS, :n_keys, cols]
        if j == qi:
            kk = lax.broadcasted_iota(jnp.int32, st.shape, 0)
            qq = lax.broadcasted_iota(jnp.int32, st.shape, 1) + c * width
            st = jnp.where(kk <= qq, st, -jnp.inf)
        cmax = jnp.max(st, axis=0, keepdims=True)
        if j == 0:
            m_new = cmax
            p = jnp.exp2(st - m_new)
            l_new = jnp.sum(p, axis=0, keepdims=True)
        else:
            m_old = m_ref[qi % 2, :, cols]
            m_new = jnp.maximum(m_old, cmax)
            alpha = jnp.exp2(m_old - m_new)
            p = jnp.exp2(st - m_new)
            l_new = alpha * l_ref[qi % 2, :, cols] + jnp.sum(p, axis=0, keepdims=True)
            al_ref[u % MLA_SLOTS, :, cols] = alpha
        m_ref[qi % 2, :, cols] = m_new
        l_ref[qi % 2, :, cols] = l_new
        p_ref[u % MLA_SLOTS, :n_keys, cols] = p.astype(BF16)

    def stage_pv(u, c):
        qi, j, cols, n_keys = geometry(u, c)
        pv = jnp.dot(vt_ref[0, 0, j, :, :n_keys], p_ref[u % MLA_SLOTS, :n_keys, cols],
                     preferred_element_type=F32)
        if j == 0:
            acc = pv
        else:
            acc = al_ref[u % MLA_SLOTS, :, cols] * acc_ref[:, cols] + pv
        if j == qi:
            rows = slice(qi * tile + c * width, qi * tile + (c + 1) * width)
            o_ref[rows, :] = (acc / l_ref[qi % 2, :, cols]).T.astype(BF16)
        else:
            acc_ref[:, cols] = acc

    n = len(pairs)
    skew = MLA_SLOTS - 1
    for t in range(n + 2 * skew):
        for c in range(MLA_COL_SPLIT):
            if 0 <= t - 2 * skew < n:
                stage_pv(t - 2 * skew, c)
            if 0 <= t - skew < n:
                stage_softmax(t - skew, c)
            if t < n:
                stage_scores(t, c)


def _mla_call(qb, kcat, vt, *, batch, seq):
    tile = TM_PROJ
    n_tiles = seq // tile
    return pl.pallas_call(
        functools.partial(_mla_kernel, tile=tile, n_tiles=n_tiles),
        out_shape=jax.ShapeDtypeStruct((batch * seq, MLA_HEADS * MLA_V_DIM), BF16),
        grid=(batch, MLA_HEADS),
        in_specs=[
            pl.BlockSpec((seq, MLA_QK_PAD), lambda b, h: (b, h)),
            pl.BlockSpec((seq, MLA_QK_PAD), lambda b, h: (b, h)),
            pl.BlockSpec((1, 1, n_tiles, MLA_V_DIM, tile), lambda b, h: (b, h, 0, 0, 0)),
        ],
        out_specs=pl.BlockSpec((seq, MLA_V_DIM), lambda b, h: (b, h)),
        scratch_shapes=[
            pltpu.VMEM((MLA_SLOTS, tile, tile), F32),
            pltpu.VMEM((MLA_SLOTS, tile, tile), BF16),
            pltpu.VMEM((MLA_SLOTS, 1, tile), F32),
            pltpu.VMEM((2, 1, tile), F32),
            pltpu.VMEM((2, 1, tile), F32),
            pltpu.VMEM((MLA_V_DIM, tile), F32),
        ],
        compiler_params=_params(("parallel", "parallel")),
        name="mla",
    )(qb, kcat, vt)


def _out_kernel(x_ref, oa_ref, ob_ref, mod_ref, wa_ref, wb_ref, x1_ref):
    g1 = mod_ref[0][0:1]
    y = (jnp.dot(oa_ref[...], wa_ref[...], preferred_element_type=F32)
         + jnp.dot(ob_ref[...], wb_ref[...], preferred_element_type=F32))
    x1_ref[...] = x_ref[...] + g1 * y


def _out_call(x2d, oa, ob, mod3, w_out, *, seq):
    tm = TM_OUT
    t = x2d.shape[0]
    tiles_per_seq = seq // tm
    tok = lambda i: (i, 0)
    w_half = lambda half: pl.BlockSpec((SWA_Q_COLS, D_MODEL), lambda i: (half, 0),
                                       pipeline_mode=pl.Buffered(1))
    assert w_out.shape[0] == 2 * SWA_Q_COLS
    return pl.pallas_call(
        _out_kernel,
        out_shape=jax.ShapeDtypeStruct((t, D_MODEL), F32),
        grid=(t // tm,),
        in_specs=[
            pl.BlockSpec((tm, D_MODEL), tok),
            pl.BlockSpec((tm, SWA_Q_COLS), tok),
            pl.BlockSpec((tm, MLA_HEADS * MLA_V_DIM), tok),
            pl.BlockSpec((1, MOD_POST, D_MODEL), lambda i: (i // tiles_per_seq, 0, 0)),
            w_half(0),
            w_half(1),
        ],
        out_specs=pl.BlockSpec((tm, D_MODEL), tok),
        compiler_params=_params(("parallel",)),
        name="out_proj",
    )(x2d, oa, ob, mod3, w_out, w_out)


def _mlp_kernel(x_ref, mod_ref, g_ref, gf_ref, w1_ref, w2_ref, o_ref, h_ref):
    j = pl.program_id(1)
    last = pl.num_programs(1) - 1
    mod = mod_ref[0]
    rows = o_ref.shape[0] // ROW_CHUNKS
    chunks = [slice(r * rows, (r + 1) * rows) for r in range(ROW_CHUNKS)]

    def ff_tile(h):
        u = jnp.maximum(jnp.dot(h, w1_ref[...], preferred_element_type=F32), 0.0)
        return jnp.dot((u * u).astype(BF16), w2_ref[...], preferred_element_type=F32)

    @pl.when(j == 0)
    def _():
        sh2 = mod[1:2]
        sc2 = mod[2:3]
        for sl in chunks:
            h = ((_rms(x_ref[sl, :]) * g_ref[...]) * (1.0 + sc2) + sh2).astype(BF16)
            h_ref[sl, :] = h
            o_ref[sl, :] = ff_tile(h)

    @pl.when((j > 0) & (j < last))
    def _():
        o_ref[...] += ff_tile(h_ref[...])

    @pl.when(j == last)
    def _():
        g2 = mod[3:4]
        for sl in chunks:
            x2 = x_ref[sl, :] + g2 * (o_ref[sl, :] + ff_tile(h_ref[sl, :]))
            o_ref[sl, :] = _rms(x2) * gf_ref[...]


def _mlp_call(x1, mod3, g_mlp, g_final, w1, w2, *, seq):
    tm = TM_MLP
    tf = TF_MLP
    assert D_FF // tf >= 2
    t = x1.shape[0]
    tiles_per_seq = seq // tm
    return pl.pallas_call(
        _mlp_kernel,
        out_shape=jax.ShapeDtypeStruct((t, D_MODEL), F32),
        grid=(t // tm, D_FF // tf),
        in_specs=[
            pl.BlockSpec((tm, D_MODEL), lambda i, j: (i, 0)),
            pl.BlockSpec((1, MOD_POST, D_MODEL), lambda i, j: (i // tiles_per_seq, 0, 0)),
            _const_spec((1, D_MODEL)),
            _const_spec((1, D_MODEL)),
            pl.BlockSpec((D_MODEL, tf), lambda i, j: (0, j)),
            pl.BlockSpec((tf, D_MODEL), lambda i, j: (j, 0)),
        ],
        out_specs=pl.BlockSpec((tm, D_MODEL), lambda i, j: (i, 0)),
        scratch_shapes=[pltpu.VMEM((tm, D_MODEL), BF16)],
        compiler_params=_params(("parallel", "arbitrary")),
        name="mlp",
    )(x1, mod3, g_mlp, g_final, w1, w2)


def _t5_bucket_table():
    q_loc = np.arange(BLOCK)[:, None]
    k_loc = np.arange(2 * BLOCK)[None, :]
    n = np.maximum(q_loc + BLOCK - k_loc, 0)
    max_exact = REL_BUCKETS // 2
    nf = np.maximum(n, 1).astype(np.float64)
    large = max_exact + (np.log(nf / max_exact) / math.log(REL_MAX_DIST / max_exact)
                         * (REL_BUCKETS - max_exact)).astype(np.int32)
    large = np.minimum(large, REL_BUCKETS - 1)
    return np.where(n < max_exact, n, large).astype(np.int32)


def _rope_tables(seq):
    half = MLA_ROPE_DIM // 2
    inv_freq = ROPE_THETA ** (-np.arange(half, dtype=np.float64) / half)
    ang = np.arange(seq, dtype=np.float64)[:, None] * inv_freq[None, :]
    zeros = np.zeros((seq, 128 - MLA_ROPE_DIM))
    cos_tab = np.concatenate([np.cos(ang), np.cos(ang), zeros], axis=1)
    sin_tab = np.concatenate([np.sin(ang), np.sin(ang), zeros], axis=1)
    return cos_tab.astype(np.float32), sin_tab.astype(np.float32)


def _rot_cols(w):
    half = w.shape[-1] // 2
    return jnp.concatenate([-w[..., half:], w[..., :half]], axis=-1)


def kernel(x, c, w_mod, b_mod, attn_norm_g, w_in, swa_sinks, rel_bias, mla_q_norm_g, w_uq,
           mla_kv_norm_g, w_ukv, w_out, mlp_norm_g, w_ff1, w_ff2, final_norm_g):
    batch, seq, _ = x.shape
    depth = w_mod.shape[0]
    assert depth == 1
    t = batch * seq
    x2d = x.reshape(t, D_MODEL)
    l = 0

    w_kr = w_in[l][:, OFF_MLA_KR:OFF_MLA_KR + MLA_ROPE_DIM]
    w_main = jnp.concatenate(
        [w_in[l][:, OFF_SWA_K:OFF_SWA_V], w_in[l][:, OFF_MLA_CQ:OFF_MLA_KR],
         w_kr, _rot_cols(w_kr)], axis=1).astype(BF16)
    w_t = jnp.concatenate(
        [w_in[l][:, :SWA_Q_COLS], w_in[l][:, OFF_SWA_V:OFF_MLA_CQ]], axis=1).T.astype(BF16)

    wq = w_uq[l].reshape(MLA_Q_RANK, MLA_HEADS, MLA_QK_DIM)
    wq_nope = wq[..., :MLA_NOPE_DIM]
    wq_rope = wq[..., MLA_NOPE_DIM:]
    hh = MLA_HEADS // 2
    zq = jnp.zeros((MLA_Q_RANK, hh, 128 - MLA_ROPE_DIM), F32)
    wq_main = jnp.concatenate(
        [jnp.concatenate([wq_nope[:, :hh], wq_rope[:, :hh], zq], axis=-1),
         jnp.concatenate([wq_nope[:, hh:], zq, wq_rope[:, hh:]], axis=-1)], axis=1).reshape(
        MLA_Q_RANK, MLA_HEADS * MLA_QK_PAD).astype(BF16)
    wq_rot_all = _rot_cols(wq_rope)
    wq_rot = jnp.concatenate([wq_rot_all[:, :hh], wq_rot_all[:, hh:]], axis=-1).reshape(
        MLA_Q_RANK, hh * 128).astype(BF16)

    wkv = w_ukv[l].reshape(MLA_KV_RANK, MLA_HEADS, MLA_NOPE_DIM + MLA_V_DIM)
    w_uk = wkv[..., :MLA_NOPE_DIM].reshape(MLA_KV_RANK, MLA_HEADS * MLA_NOPE_DIM).astype(BF16)
    w_vt = wkv[..., MLA_NOPE_DIM:].reshape(MLA_KV_RANK, MLA_HEADS * MLA_V_DIM).T.astype(BF16)

    cos_tab, sin_tab = _rope_tables(seq)

    assert w_mod.shape[2] == (MOD_PRE + MOD_POST) * D_MODEL
    c8 = jnp.pad(c, ((0, 8 - batch), (0, 0)))
    b_mod2d = b_mod[l].reshape(1, -1)
    mod_pre = _mod_call(c8, w_mod[l], b_mod2d)[:batch].reshape(batch, MOD_PRE, D_MODEL)

    bias_tab = _bias_call(rel_bias, _t5_bucket_table().T)

    qat, ka, vat, qb, kcat, vt, w_out_b16, w1, w2 = _proj_call(
        x2d, mod_pre, attn_norm_g[l].reshape(1, -1), w_main, w_t,
        mla_q_norm_g[l].reshape(1, -1), mla_kv_norm_g[l].reshape(1, -1),
        wq_main, wq_rot, w_uk, w_vt, cos_tab, sin_tab, w_out[l], w_ff1[l], w_ff2[l],
        batch=batch, seq=seq)

    oa, mod_post = _swa_call(swa_sinks[l], qat, ka, vat, bias_tab, c8, w_mod[l], b_mod2d,
                             batch=batch, seq=seq)
    mod_post = mod_post[:batch].reshape(batch, MOD_POST, D_MODEL)
    ob = _mla_call(qb, kcat, vt, batch=batch, seq=seq)

    x1 = _out_call(x2d, oa, ob, mod_post, w_out_b16, seq=seq)
    out = _mlp_call(x1, mod_post, mlp_norm_g[l].reshape(1, -1), final_norm_g.reshape(1, -1),
                    w1, w2, seq=seq)
    return out.reshape(batch, seq, D_MODEL)
```

```python
import functools
import math

import jax
import jax.numpy as jnp
import numpy as np
from jax import lax
from jax.experimental import pallas as pl
from jax.experimental.pallas import tpu as pltpu

F32 = jnp.float32
BF16 = jnp.bfloat16

D_MODEL = 2048
BLOCK = 128
EPS = 1e-6

SWA_HEADS = 16
SWA_KV_HEADS = 2
SWA_HEAD_DIM = 64
SWA_GROUP = SWA_HEADS // SWA_KV_HEADS
WINDOW = 128
REL_BUCKETS = 32
REL_MAX_DIST = 128

MLA_HEADS = 8
MLA_Q_RANK = 384
MLA_KV_RANK = 128
MLA_NOPE_DIM = 128
MLA_ROPE_DIM = 64
MLA_V_DIM = 128
MLA_QK_DIM = MLA_NOPE_DIM + MLA_ROPE_DIM
MLA_QK_PAD = 256
ROPE_THETA = 10000.0
MLA_Q_SCALE = MLA_QK_DIM ** -0.5 * math.log2(math.e)
D_FF = 4 * D_MODEL

SWA_Q_COLS = SWA_HEADS * SWA_HEAD_DIM
SWA_KV_COLS = SWA_KV_HEADS * SWA_HEAD_DIM
OFF_SWA_K = SWA_Q_COLS
OFF_SWA_V = OFF_SWA_K + SWA_KV_COLS
OFF_MLA_CQ = OFF_SWA_V + SWA_KV_COLS
OFF_MLA_CKV = OFF_MLA_CQ + MLA_Q_RANK
OFF_MLA_KR = OFF_MLA_CKV + MLA_KV_RANK
P_CQ = SWA_KV_COLS
P_CKV = P_CQ + MLA_Q_RANK
P_KR = P_CKV + MLA_KV_RANK
P_COLS = P_KR + 128
PT_ROWS = SWA_Q_COLS + SWA_KV_COLS
LOG2E = math.log2(math.e)

VMEM_LIMIT_BYTES = 62 * 1024 * 1024

TM_PROJ = 512
TM_SWA = 512
TM_OUT = 512
TM_MLP = 512
TF_MLP = 2048
TN_MOD = 1024
MOD_PRE = 2
MOD_POST = 4
BIAS_HEADS_PER_STEP = 4
SWA_UNIT_HEADS = 8
MLA_SLOTS = 2
MLA_COL_SPLIT = 2
ROW_CHUNKS = 2


def _params(sem):
    return pltpu.CompilerParams(dimension_semantics=sem, vmem_limit_bytes=VMEM_LIMIT_BYTES)


def _const_spec(shape):
    nd = len(shape)
    return pl.BlockSpec(shape, lambda *_: (0,) * nd, pipeline_mode=pl.Buffered(1))


def _rms(x):
    return x * lax.rsqrt(jnp.mean(x * x, axis=-1, keepdims=True) + EPS)


def _mod_tile(c_ref, w_ref, b_ref):
    c = c_ref[...]
    ca = c * (1.0 / (1.0 + jnp.exp(-c)))
    return jnp.dot(ca.astype(BF16), w_ref[...].astype(BF16),
                   preferred_element_type=F32) + b_ref[...]


def _mod_kernel(c_ref, w_ref, b_ref, o_ref):
    o_ref[...] = _mod_tile(c_ref, w_ref, b_ref)


def _mod_call(c8, w_mod, b_mod):
    n = MOD_PRE * D_MODEL
    return pl.pallas_call(
        _mod_kernel,
        out_shape=jax.ShapeDtypeStruct((8, n), F32),
        grid=(n // TN_MOD,),
        in_specs=[
            pl.BlockSpec((8, D_MODEL), lambda j: (0, 0)),
            pl.BlockSpec((D_MODEL, TN_MOD), lambda j: (0, j)),
            pl.BlockSpec((1, TN_MOD), lambda j: (0, j)),
        ],
        out_specs=pl.BlockSpec((8, TN_MOD), lambda j: (0, j)),
        compiler_params=_params(("arbitrary",)),
        name="mod",
    )(c8, w_mod, b_mod)


def _bias_kernel(rel_ref, bucket_ref, o_ref):
    bucket = bucket_ref[...]
    k_loc = lax.broadcasted_iota(jnp.int32, bucket.shape, 0)
    q_loc = lax.broadcasted_iota(jnp.int32, bucket.shape, 1)
    dist = q_loc + BLOCK - k_loc
    in_window = (dist >= 0) & (dist < WINDOW)
    for i in range(BIAS_HEADS_PER_STEP):
        h = pl.program_id(0) * BIAS_HEADS_PER_STEP + i
        acc = jnp.zeros(bucket.shape, F32)
        for k in range(REL_BUCKETS):
            acc = jnp.where(bucket == k, rel_ref[k, h], acc)
        acc = acc * LOG2E
        o_ref[0, i] = jnp.where(in_window & (k_loc >= BLOCK), acc, -jnp.inf)
        o_ref[1, i] = jnp.where(in_window, acc, -jnp.inf)


def _bias_call(rel_bias, bucket_t):
    hs = BIAS_HEADS_PER_STEP
    return pl.pallas_call(
        _bias_kernel,
        out_shape=jax.ShapeDtypeStruct((2, SWA_HEADS, 2 * BLOCK, BLOCK), F32),
        grid=(SWA_HEADS // hs,),
        in_specs=[
            pl.BlockSpec(memory_space=pltpu.SMEM),
            pl.BlockSpec((2 * BLOCK, BLOCK), lambda s: (0, 0)),
        ],
        out_specs=pl.BlockSpec((2, hs, 2 * BLOCK, BLOCK), lambda s: (0, s, 0, 0)),
        compiler_params=_params(("arbitrary",)),
        name="t5_bias",
    )(rel_bias, bucket_t)


def _proj_kernel(x_ref, mod_ref, g_ref, win_ref, wt_ref, gq_ref, gkv_ref, wqm_ref, wqr_ref,
                 wuk_ref, wvt_ref, cos_ref, sin_ref, wo_ref, w1_ref, w2_ref,
                 qat_ref, ka_ref, vat_ref, qb_ref, kc_ref, vt_ref, wob_ref, w1b_ref, w2b_ref,
                 *, q_scale):
    wob_ref[...] = wo_ref[...].astype(BF16)
    w1b_ref[...] = w1_ref[...].astype(BF16)
    w2b_ref[...] = w2_ref[...].astype(BF16)

    x = x_ref[...]
    mod = mod_ref[0]
    sh1 = mod[0:1]
    sc1 = mod[1:2]
    h = ((_rms(x) * g_ref[...]) * (1.0 + sc1) + sh1).astype(BF16)
    proj = jnp.dot(h, win_ref[...], preferred_element_type=F32)
    proj_t = lax.dot_general(wt_ref[...], h, (((1,), (1,)), ((), ())),
                             preferred_element_type=F32)

    qat_ref[...] = (proj_t[:SWA_Q_COLS] * (SWA_HEAD_DIM ** -0.5 * LOG2E)).astype(BF16)
    vat_ref[...] = proj_t[SWA_Q_COLS:].astype(BF16)
    ka_ref[...] = proj[:, :P_CQ].astype(BF16)

    cq = (_rms(proj[:, P_CQ:P_CKV]) * gq_ref[...]).astype(BF16)
    ckv = (_rms(proj[:, P_CKV:P_KR]) * gkv_ref[...]).astype(BF16)
    cos_lo = cos_ref[...]
    sin_lo = sin_ref[...]
    cos_hi = pltpu.roll(cos_lo, MLA_ROPE_DIM, 1)
    sin_hi = pltpu.roll(sin_lo, MLA_ROPE_DIM, 1)
    kr = proj[:, P_KR:P_COLS]
    kr_sw = pltpu.roll(kr, MLA_ROPE_DIM, 1)
    krope_lo = (kr * cos_lo + kr_sw * sin_lo).astype(BF16)
    krope_hi = (kr_sw * cos_hi + kr * sin_hi).astype(BF16)

    qmain = jnp.dot(cq, wqm_ref[...], preferred_element_type=F32)
    qrot = jnp.dot(cq, wqr_ref[...], preferred_element_type=F32)
    knope = jnp.dot(ckv, wuk_ref[...], preferred_element_type=F32)
    vt = lax.dot_general(wvt_ref[...], ckv, (((1,), (1,)), ((), ())),
                         preferred_element_type=F32)

    half_heads = MLA_HEADS // 2
    for hd in range(MLA_HEADS):
        lo = hd * MLA_QK_PAD
        mid = lo + MLA_NOPE_DIM
        hi = lo + MLA_QK_PAD
        low = hd < half_heads
        cos_t, sin_t = (cos_lo, sin_lo) if low else (cos_hi, sin_hi)
        rot = qrot[:, (hd % half_heads) * 128:(hd % half_heads + 1) * 128]
        qb_ref[:, lo:mid] = (qmain[:, lo:mid] * q_scale).astype(BF16)
        qb_ref[:, mid:hi] = ((qmain[:, mid:hi] * cos_t + rot * sin_t) * q_scale).astype(BF16)
        kc_ref[:, lo:mid] = knope[:, hd * MLA_NOPE_DIM:(hd + 1) * MLA_NOPE_DIM].astype(BF16)
        kc_ref[:, mid:hi] = krope_lo if low else krope_hi
        vt_ref[0, hd, 0] = vt[hd * MLA_V_DIM:(hd + 1) * MLA_V_DIM].astype(BF16)


def _proj_call(x2d, mod3, g_attn, w_main, w_t, gq, gkv, wq_main, wq_rot, w_uk, w_vt,
               cos_tab, sin_tab, w_out, w_ff1, w_ff2, *, batch, seq):
    tm = TM_PROJ
    tiles_per_seq = seq // tm
    t = batch * seq
    steps = t // tm
    slab = lambda w: (w.shape[0] // steps, w.shape[1])
    assert all(w.shape[0] % (steps * 16) == 0 for w in (w_out, w_ff1, w_ff2))
    tok = lambda i: (i, 0)
    tok_t = lambda i: (0, i)
    pos = lambda i: (i % tiles_per_seq, 0)
    out_shape = (
        jax.ShapeDtypeStruct((SWA_Q_COLS, t), BF16),
        jax.ShapeDtypeStruct((t, SWA_KV_COLS), BF16),
        jax.ShapeDtypeStruct((SWA_KV_COLS, t), BF16),
        jax.ShapeDtypeStruct((t, MLA_HEADS * MLA_QK_PAD), BF16),
        jax.ShapeDtypeStruct((t, MLA_HEADS * MLA_QK_PAD), BF16),
        jax.ShapeDtypeStruct((batch, MLA_HEADS, tiles_per_seq, MLA_V_DIM, tm), BF16),
        jax.ShapeDtypeStruct(w_out.shape, BF16),
        jax.ShapeDtypeStruct(w_ff1.shape, BF16),
        jax.ShapeDtypeStruct(w_ff2.shape, BF16),
    )
    return pl.pallas_call(
        functools.partial(_proj_kernel, q_scale=MLA_Q_SCALE),
        out_shape=out_shape,
        grid=(t // tm,),
        in_specs=[
            pl.BlockSpec((tm, D_MODEL), tok),
            pl.BlockSpec((1, MOD_PRE, D_MODEL), lambda i: (i // tiles_per_seq, 0, 0)),
            _const_spec((1, D_MODEL)),
            _const_spec((D_MODEL, P_COLS)),
            _const_spec((PT_ROWS, D_MODEL)),
            _const_spec((1, MLA_Q_RANK)),
            _const_spec((1, MLA_KV_RANK)),
            _const_spec((MLA_Q_RANK, MLA_HEADS * MLA_QK_PAD)),
            _const_spec((MLA_Q_RANK, MLA_HEADS // 2 * 128)),
            _const_spec((MLA_KV_RANK, MLA_HEADS * MLA_NOPE_DIM)),
            _const_spec((MLA_HEADS * MLA_V_DIM, MLA_KV_RANK)),
            pl.BlockSpec((tm, 128), pos),
            pl.BlockSpec((tm, 128), pos),
            pl.BlockSpec(slab(w_out), tok),
            pl.BlockSpec(slab(w_ff1), tok),
            pl.BlockSpec(slab(w_ff2), tok),
        ],
        out_specs=(
            pl.BlockSpec((SWA_Q_COLS, tm), tok_t),
            pl.BlockSpec((tm, SWA_KV_COLS), tok),
            pl.BlockSpec((SWA_KV_COLS, tm), tok_t),
            pl.BlockSpec((tm, MLA_HEADS * MLA_QK_PAD), tok),
            pl.BlockSpec((tm, MLA_HEADS * MLA_QK_PAD), tok),
            pl.BlockSpec((1, MLA_HEADS, 1, MLA_V_DIM, tm),
                         lambda i: (i // tiles_per_seq, 0, i % tiles_per_seq, 0, 0)),
            pl.BlockSpec(slab(w_out), tok),
            pl.BlockSpec(slab(w_ff1), tok),
            pl.BlockSpec(slab(w_ff2), tok),
        ),
        compiler_params=_params(("parallel",)),
        name="in_proj",
    )(x2d, mod3, g_attn, w_main, w_t, gq, gkv, wq_main, wq_rot, w_uk, w_vt,
      cos_tab, sin_tab, w_out, w_ff1, w_ff2)


def _swa_kernel(sinks_ref, qt_ref, kp_ref, kc_ref, vtp_ref, vtc_ref, bias_ref,
                c_ref, wmod_ref, bmod_ref, o_ref, mod_ref,
                s_ref, p_ref, inv_ref, *, blocks_per_step):
    mod_ref[...] = _mod_tile(c_ref, wmod_ref, bmod_ref)

    first_step = pl.program_id(1) == 0
    k_all = jnp.concatenate([kp_ref[...], kc_ref[...]], axis=0)
    vt_all = jnp.concatenate([vtp_ref[...], vtc_ref[...]], axis=1)
    uh = SWA_UNIT_HEADS
    zeros_q = jnp.zeros((SWA_HEAD_DIM, uh * BLOCK), BF16)
    units = [(i, g, part) for i in range(blocks_per_step) for g in range(SWA_KV_HEADS)
             for part in range(SWA_GROUP // uh)]

    def stage_scores(u):
        i, g, part = units[u]
        h_first = g * SWA_GROUP + part * uh
        k_band = k_all[i * BLOCK:(i + 2) * BLOCK]
        qt = qt_ref[:, i * BLOCK:(i + 1) * BLOCK]
        q_g = jnp.concatenate(
            [qt[(h_first + hh) * SWA_HEAD_DIM:(h_first + hh + 1) * SWA_HEAD_DIM]
             for hh in range(uh)], axis=1)
        rhs = jnp.concatenate([q_g, zeros_q] if g == 0 else [zeros_q, q_g], axis=0)
        s_ref[u % 2] = jnp.dot(k_band, rhs, preferred_element_type=F32)

    def stage_softmax(u):
        i, g, part = units[u]
        variant = jnp.where(first_step, 0, 1) if i == 0 else 1
        for hh in range(uh):
            hd = g * SWA_GROUP + part * uh + hh
            st = s_ref[u % 2, :, hh * BLOCK:(hh + 1) * BLOCK] + bias_ref[variant, hd]
            sink = sinks_ref[hd] * LOG2E
            m = jnp.maximum(jnp.max(st, axis=0, keepdims=True), sink)
            p = jnp.exp2(st - m)
            denom = jnp.sum(p, axis=0, keepdims=True) + jnp.exp2(sink - m)
            inv_ref[u % 2, hh] = 1.0 / denom
            p_ref[u % 2, :, hh * BLOCK:(hh + 1) * BLOCK] = p.astype(BF16)

    def stage_pv(u):
        i, g, part = units[u]
        vt_band = vt_all[g * SWA_HEAD_DIM:(g + 1) * SWA_HEAD_DIM, i * BLOCK:(i + 2) * BLOCK]
        ot_g = jnp.dot(vt_band, p_ref[u % 2], preferred_element_type=F32)
        for pair in range(uh // 2):
            h0 = 2 * pair
            slab = jnp.concatenate(
                [ot_g[:, (h0 + e) * BLOCK:(h0 + e + 1) * BLOCK] * inv_ref[u % 2, h0 + e]
                 for e in range(2)], axis=0)
            col = (g * SWA_GROUP + part * uh + h0) * SWA_HEAD_DIM
            o_ref[i * BLOCK:(i + 1) * BLOCK, col:col + 2 * SWA_HEAD_DIM] = (
                slab.T.astype(BF16))

    n = len(units)
    for t in range(n + 2):
        if 0 <= t - 2 < n:
            stage_pv(t - 2)
        if 0 <= t - 1 < n:
            stage_softmax(t - 1)
        if t < n:
            stage_scores(t)


def _swa_call(sinks, qat, ka, vat, bias_tab, c8, w_mod, b_mod, *, batch, seq):
    step = TM_SWA
    blocks_per_step = step // BLOCK
    steps_per_seq = seq // step
    nb = seq // BLOCK
    prev_blk = lambda b, s: b * nb + jnp.maximum(s * blocks_per_step - 1, 0)
    n_steps = batch * steps_per_seq
    mod_cols = MOD_POST * D_MODEL
    tn = mod_cols // n_steps
    assert tn % 128 == 0 and (MOD_PRE * D_MODEL) % tn == 0
    mod_tile = lambda b, s: (0, b * steps_per_seq + s)
    mod_src_tile = lambda b, s: (0, MOD_PRE * D_MODEL // tn + b * steps_per_seq + s)
    return pl.pallas_call(
        functools.partial(_swa_kernel, blocks_per_step=blocks_per_step),
        out_shape=(jax.ShapeDtypeStruct((batch * seq, SWA_Q_COLS), BF16),
                   jax.ShapeDtypeStruct((8, mod_cols), F32)),
        grid=(batch, steps_per_seq),
        in_specs=[
            pl.BlockSpec(memory_space=pltpu.SMEM),
            pl.BlockSpec((SWA_Q_COLS, step), lambda b, s: (0, b * steps_per_seq + s)),
            pl.BlockSpec((BLOCK, SWA_KV_COLS), lambda b, s: (prev_blk(b, s), 0)),
            pl.BlockSpec((step, SWA_KV_COLS), lambda b, s: (b * steps_per_seq + s, 0)),
            pl.BlockSpec((SWA_KV_COLS, BLOCK), lambda b, s: (0, prev_blk(b, s))),
            pl.BlockSpec((SWA_KV_COLS, step), lambda b, s: (0, b * steps_per_seq + s)),
            _const_spec((2, SWA_HEADS, 2 * BLOCK, BLOCK)),
            _const_spec((8, D_MODEL)),
            pl.BlockSpec((D_MODEL, tn), mod_src_tile),
            pl.BlockSpec((1, tn), mod_src_tile),
        ],
        out_specs=(
            pl.BlockSpec((step, SWA_Q_COLS), lambda b, s: (b * steps_per_seq + s, 0)),
            pl.BlockSpec((8, tn), mod_tile),
        ),
        scratch_shapes=[
            pltpu.VMEM((2, 2 * BLOCK, SWA_UNIT_HEADS * BLOCK), F32),
            pltpu.VMEM((2, 2 * BLOCK, SWA_UNIT_HEADS * BLOCK), BF16),
            pltpu.VMEM((2, SWA_UNIT_HEADS, 1, BLOCK), F32),
        ],
        compiler_params=_params(("parallel", "arbitrary")),
        name="swa",
    )(sinks, qat, ka, ka, vat, vat, bias_tab, c8, w_mod, b_mod)


def _mla_kernel(q_ref, k_ref, vt_ref, o_ref, s_ref, p_ref, al_ref, m_ref, l_ref, acc_ref,
                *, tile, n_tiles):
    pairs = [(qi, j) for qi in range(n_tiles) for j in range(qi + 1)]
    width = tile // MLA_COL_SPLIT

    def geometry(u, c):
        qi, j = pairs[u]
        cols = slice(c * width, (c + 1) * width)
        n_keys = (c + 1) * width if j == qi else tile
        return qi, j, cols, n_keys

    def stage_scores(u, c):
        qi, j, cols, n_keys = geometry(u, c)
        k = k_ref[j * tile:j * tile + n_keys, :]
        q = q_ref[qi * tile + c * width:qi * tile + (c + 1) * width, :]
        s_ref[u % MLA_SLOTS, :n_keys, cols] = lax.dot_general(
            k, q, (((1,), (1,)), ((), ())), preferred_element_type=F32)

    def stage_softmax(u, c):
        qi, j, cols, n_keys = geometry(u, c)
        st = s_ref[u % MLA_SLOTS, :n_keys, cols]
        if j == qi:
            kk = lax.broadcasted_iota(jnp.int32, st.shape, 0)
            qq = lax.broadcasted_iota(jnp.int32, st.shape, 1) + c * width
            st = jnp.where(kk <= qq, st, -jnp.inf)
        cmax = jnp.max(st, axis=0, keepdims=True)
        if j == 0:
            m_new = cmax
            p = jnp.exp2(st - m_new)
            l_new = jnp.sum(p, axis=0, keepdims=True)
        else:
            m_old = m_ref[qi % 2, :, cols]
            m_new = jnp.maximum(m_old, cmax)
            alpha = jnp.exp2(m_old - m_new)
            p = jnp.exp2(st - m_new)
            l_new = alpha * l_ref[qi % 2, :, cols] + jnp.sum(p, axis=0, keepdims=True)
            al_ref[u % MLA_SLOTS, :, cols] = alpha
        m_ref[qi % 2, :, cols] = m_new
        l_ref[qi % 2, :, cols] = l_new
        p_ref[u % MLA_SLOTS, :n_keys, cols] = p.astype(BF16)

    def stage_pv(u, c):
        qi, j, cols, n_keys = geometry(u, c)
        pv = jnp.dot(vt_ref[0, 0, j, :, :n_keys], p_ref[u % MLA_SLOTS, :n_keys, cols],
                     preferred_element_type=F32)
        if j == 0:
            acc = pv
        else:
            acc = al_ref[u % MLA_SLOTS, :, cols] * acc_ref[:, cols] + pv
        if j == qi:
            rows = slice(qi * tile + c * width, qi * tile + (c + 1) * width)
            o_ref[rows, :] = (acc / l_ref[qi % 2, :, cols]).T.astype(BF16)
        else:
            acc_ref[:, cols] = acc

    n = len(pairs)
    skew = MLA_SLOTS - 1
    for t in range(n + 2 * skew):
        for c in range(MLA_COL_SPLIT):
            if 0 <= t - 2 * skew < n:
                stage_pv(t - 2 * skew, c)
            if 0 <= t - skew < n:
                stage_softmax(t - skew, c)
            if t < n:
                stage_scores(t, c)


def _mla_call(qb, kcat, vt, *, batch, seq):
    tile = TM_PROJ
    n_tiles = seq // tile
    return pl.pallas_call(
        functools.partial(_mla_kernel, tile=tile, n_tiles=n_tiles),
        out_shape=jax.ShapeDtypeStruct((batch * seq, MLA_HEADS * MLA_V_DIM), BF16),
        grid=(batch, MLA_HEADS),
        in_specs=[
            pl.BlockSpec((seq, MLA_QK_PAD), lambda b, h: (b, h)),
            pl.BlockSpec((seq, MLA_QK_PAD), lambda b, h: (b, h)),
            pl.BlockSpec((1, 1, n_tiles, MLA_V_DIM, tile), lambda b, h: (b, h, 0, 0, 0)),
        ],
        out_specs=pl.BlockSpec((seq, MLA_V_DIM), lambda b, h: (b, h)),
        scratch_shapes=[
            pltpu.VMEM((MLA_SLOTS, tile, tile), F32),
            pltpu.VMEM((MLA_SLOTS, tile, tile), BF16),
            pltpu.VMEM((MLA_SLOTS, 1, tile), F32),
            pltpu.VMEM((2, 1, tile), F32),
            pltpu.VMEM((2, 1, tile), F32),
            pltpu.VMEM((MLA_V_DIM, tile), F32),
        ],
        compiler_params=_params(("parallel", "parallel")),
        name="mla",
    )(qb, kcat, vt)


def _out_kernel(x_ref, oa_ref, ob_ref, mod_ref, wa_ref, wb_ref, x1_ref):
    g1 = mod_ref[0][0:1]
    y = (jnp.dot(oa_ref[...], wa_ref[...], preferred_element_type=F32)
         + jnp.dot(ob_ref[...], wb_ref[...], preferred_element_type=F32))
    x1_ref[...] = x_ref[...] + g1 * y


def _out_call(x2d, oa, ob, mod3, w_out, *, seq):
    tm = TM_OUT
    t = x2d.shape[0]
    tiles_per_seq = seq // tm
    tok = lambda i: (i, 0)
    w_half = lambda half: pl.BlockSpec((SWA_Q_COLS, D_MODEL), lambda i: (half, 0),
                                       pipeline_mode=pl.Buffered(1))
    assert w_out.shape[0] == 2 * SWA_Q_COLS
    return pl.pallas_call(
        _out_kernel,
        out_shape=jax.ShapeDtypeStruct((t, D_MODEL), F32),
        grid=(t // tm,),
        in_specs=[
            pl.BlockSpec((tm, D_MODEL), tok),
            pl.BlockSpec((tm, SWA_Q_COLS), tok),
            pl.BlockSpec((tm, MLA_HEADS * MLA_V_DIM), tok),
            pl.BlockSpec((1, MOD_POST, D_MODEL), lambda i: (i // tiles_per_seq, 0, 0)),
            w_half(0),
            w_half(1),
        ],
        out_specs=pl.BlockSpec((tm, D_MODEL), tok),
        compiler_params=_params(("parallel",)),
        name="out_proj",
    )(x2d, oa, ob, mod3, w_out, w_out)


def _mlp_kernel(x_ref, mod_ref, g_ref, gf_ref, w1_ref, w2_ref, o_ref, h_ref):
    j = pl.program_id(1)
    last = pl.num_programs(1) - 1
    mod = mod_ref[0]
    rows = o_ref.shape[0] // ROW_CHUNKS
    chunks = [slice(r * rows, (r + 1) * rows) for r in range(ROW_CHUNKS)]

    def ff_tile(h):
        u = jnp.maximum(jnp.dot(h, w1_ref[...], preferred_element_type=F32), 0.0)
        return jnp.dot((u * u).astype(BF16), w2_ref[...], preferred_element_type=F32)

    @pl.when(j == 0)
    def _():
        sh2 = mod[1:2]
        sc2 = mod[2:3]
        for sl in chunks:
            h = ((_rms(x_ref[sl, :]) * g_ref[...]) * (1.0 + sc2) + sh2).astype(BF16)
            h_ref[sl, :] = h
            o_ref[sl, :] = ff_tile(h)

    @pl.when((j > 0) & (j < last))
    def _():
        o_ref[...] += ff_tile(h_ref[...])

    @pl.when(j == last)
    def _():
        g2 = mod[3:4]
        for sl in chunks:
            x2 = x_ref[sl, :] + g2 * (o_ref[sl, :] + ff_tile(h_ref[sl, :]))
            o_ref[sl, :] = _rms(x2) * gf_ref[...]


def _mlp_call(x1, mod3, g_mlp, g_final, w1, w2, *, seq):
    tm = TM_MLP
    tf = TF_MLP
    assert D_FF // tf >= 2
    t = x1.shape[0]
    tiles_per_seq = seq // tm
    return pl.pallas_call(
        _mlp_kernel,
        out_shape=jax.ShapeDtypeStruct((t, D_MODEL), F32),
        grid=(t // tm, D_FF // tf),
        in_specs=[
            pl.BlockSpec((tm, D_MODEL), lambda i, j: (i, 0)),
            pl.BlockSpec((1, MOD_POST, D_MODEL), lambda i, j: (i // tiles_per_seq, 0, 0)),
            _const_spec((1, D_MODEL)),
            _const_spec((1, D_MODEL)),
            pl.BlockSpec((D_MODEL, tf), lambda i, j: (0, j)),
            pl.BlockSpec((tf, D_MODEL), lambda i, j: (j, 0)),
        ],
        out_specs=pl.BlockSpec((tm, D_MODEL), lambda i, j: (i, 0)),
        scratch_shapes=[pltpu.VMEM((tm, D_MODEL), BF16)],
        compiler_params=_params(("parallel", "arbitrary")),
        name="mlp",
    )(x1, mod3, g_mlp, g_final, w1, w2)


def _t5_bucket_table():
    q_loc = np.arange(BLOCK)[:, None]
    k_loc = np.arange(2 * BLOCK)[None, :]
    n = np.maximum(q_loc + BLOCK - k_loc, 0)
    max_exact = REL_BUCKETS // 2
    nf = np.maximum(n, 1).astype(np.float64)
    large = max_exact + (np.log(nf / max_exact) / math.log(REL_MAX_DIST / max_exact)
                         * (REL_BUCKETS - max_exact)).astype(np.int32)
    large = np.minimum(large, REL_BUCKETS - 1)
    return np.where(n < max_exact, n, large).astype(np.int32)


def _rope_tables(seq):
    half = MLA_ROPE_DIM // 2
    inv_freq = ROPE_THETA ** (-np.arange(half, dtype=np.float64) / half)
    ang = np.arange(seq, dtype=np.float64)[:, None] * inv_freq[None, :]
    zeros = np.zeros((seq, 128 - MLA_ROPE_DIM))
    cos_tab = np.concatenate([np.cos(ang), np.cos(ang), zeros], axis=1)
    sin_tab = np.concatenate([np.sin(ang), np.sin(ang), zeros], axis=1)
    return cos_tab.astype(np.float32), sin_tab.astype(np.float32)


def _rot_cols(w):
    half = w.shape[-1] // 2
    return jnp.concatenate([-w[..., half:], w[..., :half]], axis=-1)


def kernel(x, c, w_mod, b_mod, attn_norm_g, w_in, swa_sinks, rel_bias, mla_q_norm_g, w_uq,
           mla_kv_norm_g, w_ukv, w_out, mlp_norm_g, w_ff1, w_ff2, final_norm_g):
    batch, seq, _ = x.shape
    depth = w_mod.shape[0]
    assert depth == 1
    t = batch * seq
    x2d = x.reshape(t, D_MODEL)
    l = 0

    w_kr = w_in[l][:, OFF_MLA_KR:OFF_MLA_KR + MLA_ROPE_DIM]
    w_main = jnp.concatenate(
        [w_in[l][:, OFF_SWA_K:OFF_SWA_V], w_in[l][:, OFF_MLA_CQ:OFF_MLA_KR],
         w_kr, _rot_cols(w_kr)], axis=1).astype(BF16)
    w_t = jnp.concatenate(
        [w_in[l][:, :SWA_Q_COLS], w_in[l][:, OFF_SWA_V:OFF_MLA_CQ]], axis=1).T.astype(BF16)

    wq = w_uq[l].reshape(MLA_Q_RANK, MLA_HEADS, MLA_QK_DIM)
    wq_nope = wq[..., :MLA_NOPE_DIM]
    wq_rope = wq[..., MLA_NOPE_DIM:]
    hh = MLA_HEADS // 2
    zq = jnp.zeros((MLA_Q_RANK, hh, 128 - MLA_ROPE_DIM), F32)
    wq_main = jnp.concatenate(
        [jnp.concatenate([wq_nope[:, :hh], wq_rope[:, :hh], zq], axis=-1),
         jnp.concatenate([wq_nope[:, hh:], zq, wq_rope[:, hh:]], axis=-1)], axis=1).reshape(
        MLA_Q_RANK, MLA_HEADS * MLA_QK_PAD).astype(BF16)
    wq_rot_all = _rot_cols(wq_rope)
    wq_rot = jnp.concatenate([wq_rot_all[:, :hh], wq_rot_all[:, hh:]], axis=-1).reshape(
        MLA_Q_RANK, hh * 128).astype(BF16)

    wkv = w_ukv[l].reshape(MLA_KV_RANK, MLA_HEADS, MLA_NOPE_DIM + MLA_V_DIM)
    w_uk = wkv[..., :MLA_NOPE_DIM].reshape(MLA_KV_RANK, MLA_HEADS * MLA_NOPE_DIM).astype(BF16)
    w_vt = wkv[..., MLA_NOPE_DIM:].reshape(MLA_KV_RANK, MLA_HEADS * MLA_V_DIM).T.astype(BF16)

    cos_tab, sin_tab = _rope_tables(seq)

    assert w_mod.shape[2] == (MOD_PRE + MOD_POST) * D_MODEL
    c8 = jnp.pad(c, ((0, 8 - batch), (0, 0)))
    b_mod2d = b_mod[l].reshape(1, -1)
    mod_pre = _mod_call(c8, w_mod[l], b_mod2d)[:batch].reshape(batch, MOD_PRE, D_MODEL)

    bias_tab = _bias_call(rel_bias, _t5_bucket_table().T)

    qat, ka, vat, qb, kcat, vt, w_out_b16, w1, w2 = _proj_call(
        x2d, mod_pre, attn_norm_g[l].reshape(1, -1), w_main, w_t,
        mla_q_norm_g[l].reshape(1, -1), mla_kv_norm_g[l].reshape(1, -1),
        wq_main, wq_rot, w_uk, w_vt, cos_tab, sin_tab, w_out[l], w_ff1[l], w_ff2[l],
        batch=batch, seq=seq)

    oa, mod_post = _swa_call(swa_sinks[l], qat, ka, vat, bias_tab, c8, w_mod[l], b_mod2d,
                             batch=batch, seq=seq)
    mod_post = mod_post[:batch].reshape(batch, MOD_POST, D_MODEL)
    ob = _mla_call(qb, kcat, vt, batch=batch, seq=seq)

    x1 = _out_call(x2d, oa, ob, mod_post, w_out_b16, seq=seq)
    out = _mlp_call(x1, mod_post, mlp_norm_g[l].reshape(1, -1), final_norm_g.reshape(1, -1),
                    w1, w2, seq=seq)
    return out.reshape(batch, seq, D_MODEL)
```

```python
import functools
import math

import jax
import jax.numpy as jnp
import numpy as np
from jax import lax
from jax.experimental import pallas as pl
from jax.experimental.pallas import tpu as pltpu

F32 = jnp.float32
BF16 = jnp.bfloat16

D_MODEL = 2048
BLOCK = 128
EPS = 1e-6

SWA_HEADS = 16
SWA_KV_HEADS = 2
SWA_HEAD_DIM = 64
SWA_GROUP = SWA_HEADS // SWA_KV_HEADS
WINDOW = 128
REL_BUCKETS = 32
REL_MAX_DIST = 128

MLA_HEADS = 8
MLA_Q_RANK = 384
MLA_KV_RANK = 128
MLA_NOPE_DIM = 128
MLA_ROPE_DIM = 64
MLA_V_DIM = 128
MLA_VT_ROWS = MLA_V_DIM + 16
MLA_QK_DIM = MLA_NOPE_DIM + MLA_ROPE_DIM
MLA_QK_PAD = 256
ROPE_THETA = 10000.0
MLA_Q_SCALE = MLA_QK_DIM ** -0.5 * math.log2(math.e)
D_FF = 4 * D_MODEL

SWA_Q_COLS = SWA_HEADS * SWA_HEAD_DIM
SWA_KV_COLS = SWA_KV_HEADS * SWA_HEAD_DIM
OFF_SWA_K = SWA_Q_COLS
OFF_SWA_V = OFF_SWA_K + SWA_KV_COLS
OFF_MLA_CQ = OFF_SWA_V + SWA_KV_COLS
OFF_MLA_CKV = OFF_MLA_CQ + MLA_Q_RANK
OFF_MLA_KR = OFF_MLA_CKV + MLA_KV_RANK
P_CQ = SWA_KV_COLS
P_CKV = P_CQ + MLA_Q_RANK
P_KR = P_CKV + MLA_KV_RANK
P_COLS = P_KR + 128
PT_ROWS = SWA_Q_COLS + SWA_KV_COLS
LOG2E = math.log2(math.e)

VMEM_LIMIT_BYTES = 62 * 1024 * 1024

TM_PROJ = 512
MLA_TILE = 512
VT_TILE = min(MLA_TILE, TM_PROJ)
TM_SWA = 512
TM_OUT = 512
TM_MLP = 512
TF_MLP = 2048
TN_MOD = 1024
MOD_PRE = 2
MOD_POST = 4
BIAS_HEADS_PER_STEP = 4
SWA_UNIT_HEADS = 8
MLA_SLOTS = 2
MLA_COL_SPLIT = 2
ROW_CHUNKS = 2


def _params(sem):
    return pltpu.CompilerParams(dimension_semantics=sem, vmem_limit_bytes=VMEM_LIMIT_BYTES)


def _const_spec(shape):
    nd = len(shape)
    return pl.BlockSpec(shape, lambda *_: (0,) * nd, pipeline_mode=pl.Buffered(1))


def _rms(x):
    return x * lax.rsqrt(jnp.mean(x * x, axis=-1, keepdims=True) + EPS)


def _mod_tile(c_ref, w_ref, b_ref):
    c = c_ref[...]
    ca = c * (1.0 / (1.0 + jnp.exp(-c)))
    return jnp.dot(ca.astype(BF16), w_ref[...].astype(BF16),
                   preferred_element_type=F32) + b_ref[...]


def _mod_kernel(c_ref, w_ref, b_ref, o_ref):
    o_ref[...] = _mod_tile(c_ref, w_ref, b_ref)


def _mod_call(c8, w_mod, b_mod):
    n = MOD_PRE * D_MODEL
    return pl.pallas_call(
        _mod_kernel,
        out_shape=jax.ShapeDtypeStruct((8, n), F32),
        grid=(n // TN_MOD,),
        in_specs=[
            pl.BlockSpec((8, D_MODEL), lambda j: (0, 0)),
            pl.BlockSpec((D_MODEL, TN_MOD), lambda j: (0, j)),
            pl.BlockSpec((1, TN_MOD), lambda j: (0, j)),
        ],
        out_specs=pl.BlockSpec((8, TN_MOD), lambda j: (0, j)),
        compiler_params=_params(("arbitrary",)),
        name="mod",
    )(c8, w_mod, b_mod)


def _bias_kernel(rel_ref, bucket_ref, o_ref):
    bucket = bucket_ref[...]
    k_loc = lax.broadcasted_iota(jnp.int32, bucket.shape, 0)
    q_loc = lax.broadcasted_iota(jnp.int32, bucket.shape, 1)
    dist = q_loc + BLOCK - k_loc
    in_window = (dist >= 0) & (dist < WINDOW)
    for i in range(BIAS_HEADS_PER_STEP):
        h = pl.program_id(0) * BIAS_HEADS_PER_STEP + i
        acc = jnp.zeros(bucket.shape, F32)
        for k in range(REL_BUCKETS):
            acc = jnp.where(bucket == k, rel_ref[k, h], acc)
        acc = acc * LOG2E
        o_ref[0, i] = jnp.where(in_window & (k_loc >= BLOCK), acc, -jnp.inf)
        o_ref[1, i] = jnp.where(in_window, acc, -jnp.inf)


def _bias_call(rel_bias, bucket_t):
    hs = BIAS_HEADS_PER_STEP
    return pl.pallas_call(
        _bias_kernel,
        out_shape=jax.ShapeDtypeStruct((2, SWA_HEADS, 2 * BLOCK, BLOCK), F32),
        grid=(SWA_HEADS // hs,),
        in_specs=[
            pl.BlockSpec(memory_space=pltpu.SMEM),
            pl.BlockSpec((2 * BLOCK, BLOCK), lambda s: (0, 0)),
        ],
        out_specs=pl.BlockSpec((2, hs, 2 * BLOCK, BLOCK), lambda s: (0, s, 0, 0)),
        compiler_params=_params(("arbitrary",)),
        name="t5_bias",
    )(rel_bias, bucket_t)


def _proj_kernel(x_ref, mod_ref, g_ref, win_ref, wt_ref, gq_ref, gkv_ref, wqm_ref, wqr_ref,
                 wuk_ref, wvt_ref, cos_ref, sin_ref, wo_ref, w1_ref, w2_ref,
                 qat_ref, ka_ref, vat_ref, qb_ref, kc_ref, vt_ref, wob_ref, w1b_ref, w2b_ref,
                 *, q_scale):
    wob_ref[...] = wo_ref[...].astype(BF16)
    w1b_ref[...] = w1_ref[...].astype(BF16)
    w2b_ref[...] = w2_ref[...].astype(BF16)

    x = x_ref[...]
    mod = mod_ref[0]
    sh1 = mod[0:1]
    sc1 = mod[1:2]
    h = ((_rms(x) * g_ref[...]) * (1.0 + sc1) + sh1).astype(BF16)
    proj = jnp.dot(h, win_ref[...], preferred_element_type=F32)
    proj_t = lax.dot_general(wt_ref[...], h, (((1,), (1,)), ((), ())),
                             preferred_element_type=F32)

    qat_ref[...] = (proj_t[:SWA_Q_COLS] * (SWA_HEAD_DIM ** -0.5 * LOG2E)).astype(BF16)
    vat_ref[...] = proj_t[SWA_Q_COLS:].astype(BF16)
    ka_ref[...] = proj[:, :P_CQ].astype(BF16)

    cq = (_rms(proj[:, P_CQ:P_CKV]) * gq_ref[...]).astype(BF16)
    ckv = (_rms(proj[:, P_CKV:P_KR]) * gkv_ref[...]).astype(BF16)
    cos_lo = cos_ref[...]
    sin_lo = sin_ref[...]
    cos_hi = pltpu.roll(cos_lo, MLA_ROPE_DIM, 1)
    sin_hi = pltpu.roll(sin_lo, MLA_ROPE_DIM, 1)
    kr = proj[:, P_KR:P_COLS]
    kr_sw = pltpu.roll(kr, MLA_ROPE_DIM, 1)
    krope_lo = (kr * cos_lo + kr_sw * sin_lo).astype(BF16)
    krope_hi = (kr_sw * cos_hi + kr * sin_hi).astype(BF16)

    qmain = jnp.dot(cq, wqm_ref[...], preferred_element_type=F32)
    qrot = jnp.dot(cq, wqr_ref[...], preferred_element_type=F32)
    knope = jnp.dot(ckv, wuk_ref[...], preferred_element_type=F32)
    vt = lax.dot_general(wvt_ref[...], ckv, (((1,), (1,)), ((), ())),
                         preferred_element_type=F32)

    half_heads = MLA_HEADS // 2
    pad_rows = MLA_VT_ROWS - MLA_V_DIM
    ones_rows = jnp.where(lax.broadcasted_iota(jnp.int32, (pad_rows, VT_TILE), 0) == 0,
                          1.0, 0.0).astype(BF16)
    for hd in range(MLA_HEADS):
        lo = hd * MLA_QK_PAD
        mid = lo + MLA_NOPE_DIM
        hi = lo + MLA_QK_PAD
        low = hd < half_heads
        cos_t, sin_t = (cos_lo, sin_lo) if low else (cos_hi, sin_hi)
        rot = qrot[:, (hd % half_heads) * 128:(hd % half_heads + 1) * 128]
        qb_ref[:, lo:mid] = (qmain[:, lo:mid] * q_scale).astype(BF16)
        qb_ref[:, mid:hi] = ((qmain[:, mid:hi] * cos_t + rot * sin_t) * q_scale).astype(BF16)
        kc_ref[:, lo:mid] = knope[:, hd * MLA_NOPE_DIM:(hd + 1) * MLA_NOPE_DIM].astype(BF16)
        kc_ref[:, mid:hi] = krope_lo if low else krope_hi
        for kt in range(vt_ref.shape[2]):
            vt_ref[0, hd, kt, :MLA_V_DIM] = vt[hd * MLA_V_DIM:(hd + 1) * MLA_V_DIM,
                                               kt * VT_TILE:(kt + 1) * VT_TILE].astype(BF16)
            vt_ref[0, hd, kt, MLA_V_DIM:] = ones_rows


def _proj_call(x2d, mod3, g_attn, w_main, w_t, gq, gkv, wq_main, wq_rot, w_uk, w_vt,
               cos_tab, sin_tab, w_out, w_ff1, w_ff2, *, batch, seq):
    tm = TM_PROJ
    tiles_per_seq = seq // tm
    t = batch * seq
    steps = t // tm
    slab = lambda w: (w.shape[0] // steps, w.shape[1])
    assert all(w.shape[0] % (steps * 16) == 0 for w in (w_out, w_ff1, w_ff2))
    tok = lambda i: (i, 0)
    tok_t = lambda i: (0, i)
    pos = lambda i: (i % tiles_per_seq, 0)
    out_shape = (
        jax.ShapeDtypeStruct((SWA_Q_COLS, t), BF16),
        jax.ShapeDtypeStruct((t, SWA_KV_COLS), BF16),
        jax.ShapeDtypeStruct((SWA_KV_COLS, t), BF16),
        jax.ShapeDtypeStruct((t, MLA_HEADS * MLA_QK_PAD), BF16),
        jax.ShapeDtypeStruct((t, MLA_HEADS * MLA_QK_PAD), BF16),
        jax.ShapeDtypeStruct((batch, MLA_HEADS, seq // VT_TILE, MLA_VT_ROWS, VT_TILE), BF16),
        jax.ShapeDtypeStruct(w_out.shape, BF16),
        jax.ShapeDtypeStruct(w_ff1.shape, BF16),
        jax.ShapeDtypeStruct(w_ff2.shape, BF16),
    )
    return pl.pallas_call(
        functools.partial(_proj_kernel, q_scale=MLA_Q_SCALE),
        out_shape=out_shape,
        grid=(t // tm,),
        in_specs=[
            pl.BlockSpec((tm, D_MODEL), tok),
            pl.BlockSpec((1, MOD_PRE, D_MODEL), lambda i: (i // tiles_per_seq, 0, 0)),
            _const_spec((1, D_MODEL)),
            _const_spec((D_MODEL, P_COLS)),
            _const_spec((PT_ROWS, D_MODEL)),
            _const_spec((1, MLA_Q_RANK)),
            _const_spec((1, MLA_KV_RANK)),
            _const_spec((MLA_Q_RANK, MLA_HEADS * MLA_QK_PAD)),
            _const_spec((MLA_Q_RANK, MLA_HEADS // 2 * 128)),
            _const_spec((MLA_KV_RANK, MLA_HEADS * MLA_NOPE_DIM)),
            _const_spec((MLA_HEADS * MLA_V_DIM, MLA_KV_RANK)),
            pl.BlockSpec((tm, 128), pos),
            pl.BlockSpec((tm, 128), pos),
            pl.BlockSpec(slab(w_out), tok),
            pl.BlockSpec(slab(w_ff1), tok),
            pl.BlockSpec(slab(w_ff2), tok),
        ],
        out_specs=(
            pl.BlockSpec((SWA_Q_COLS, tm), tok_t),
            pl.BlockSpec((tm, SWA_KV_COLS), tok),
            pl.BlockSpec((SWA_KV_COLS, tm), tok_t),
            pl.BlockSpec((tm, MLA_HEADS * MLA_QK_PAD), tok),
            pl.BlockSpec((tm, MLA_HEADS * MLA_QK_PAD), tok),
            pl.BlockSpec((1, MLA_HEADS, tm // VT_TILE, MLA_VT_ROWS, VT_TILE),
                         lambda i: (i // tiles_per_seq, 0, i % tiles_per_seq, 0, 0)),
            pl.BlockSpec(slab(w_out), tok),
            pl.BlockSpec(slab(w_ff1), tok),
            pl.BlockSpec(slab(w_ff2), tok),
        ),
        compiler_params=_params(("parallel",)),
        name="in_proj",
    )(x2d, mod3, g_attn, w_main, w_t, gq, gkv, wq_main, wq_rot, w_uk, w_vt,
      cos_tab, sin_tab, w_out, w_ff1, w_ff2)


def _swa_kernel(sinks_ref, qt_ref, kp_ref, kc_ref, vtp_ref, vtc_ref, bias_ref,
                c_ref, wmod_ref, bmod_ref, o_ref, mod_ref,
                s_ref, p_ref, inv_ref, *, blocks_per_step):
    mod_ref[...] = _mod_tile(c_ref, wmod_ref, bmod_ref)

    first_step = pl.program_id(1) == 0
    k_all = jnp.concatenate([kp_ref[...], kc_ref[...]], axis=0)
    vt_all = jnp.concatenate([vtp_ref[...], vtc_ref[...]], axis=1)
    uh = SWA_UNIT_HEADS
    zeros_q = jnp.zeros((SWA_HEAD_DIM, uh * BLOCK), BF16)
    units = [(i, g, part) for i in range(blocks_per_step) for g in range(SWA_KV_HEADS)
             for part in range(SWA_GROUP // uh)]

    def stage_scores(u):
        i, g, part = units[u]
        h_first = g * SWA_GROUP + part * uh
        k_band = k_all[i * BLOCK:(i + 2) * BLOCK]
        qt = qt_ref[:, i * BLOCK:(i + 1) * BLOCK]
        q_g = jnp.concatenate(
            [qt[(h_first + hh) * SWA_HEAD_DIM:(h_first + hh + 1) * SWA_HEAD_DIM]
             for hh in range(uh)], axis=1)
        rhs = jnp.concatenate([q_g, zeros_q] if g == 0 else [zeros_q, q_g], axis=0)
        s_ref[u % 2] = jnp.dot(k_band, rhs, preferred_element_type=F32)

    def stage_softmax(u):
        i, g, part = units[u]
        variant = jnp.where(first_step, 0, 1) if i == 0 else 1
        for hh in range(uh):
            hd = g * SWA_GROUP + part * uh + hh
            st = s_ref[u % 2, :, hh * BLOCK:(hh + 1) * BLOCK] + bias_ref[variant, hd]
            sink = sinks_ref[hd] * LOG2E
            m = jnp.maximum(jnp.max(st, axis=0, keepdims=True), sink)
            p = jnp.exp2(st - m)
            denom = jnp.sum(p, axis=0, keepdims=True) + jnp.exp2(sink - m)
            inv_ref[u % 2, hh] = 1.0 / denom
            p_ref[u % 2, :, hh * BLOCK:(hh + 1) * BLOCK] = p.astype(BF16)

    def stage_pv(u):
        i, g, part = units[u]
        vt_band = vt_all[g * SWA_HEAD_DIM:(g + 1) * SWA_HEAD_DIM, i * BLOCK:(i + 2) * BLOCK]
        ot_g = jnp.dot(vt_band, p_ref[u % 2], preferred_element_type=F32)
        for pair in range(uh // 2):
            h0 = 2 * pair
            slab = jnp.concatenate(
                [ot_g[:, (h0 + e) * BLOCK:(h0 + e + 1) * BLOCK] * inv_ref[u % 2, h0 + e]
                 for e in range(2)], axis=0)
            col = (g * SWA_GROUP + part * uh + h0) * SWA_HEAD_DIM
            o_ref[i * BLOCK:(i + 1) * BLOCK, col:col + 2 * SWA_HEAD_DIM] = (
                slab.T.astype(BF16))

    n = len(units)
    for t in range(n + 2):
        if 0 <= t - 2 < n:
            stage_pv(t - 2)
        if 0 <= t - 1 < n:
            stage_softmax(t - 1)
        if t < n:
            stage_scores(t)


def _swa_call(sinks, qat, ka, vat, bias_tab, c8, w_mod, b_mod, *, batch, seq):
    step = TM_SWA
    blocks_per_step = step // BLOCK
    steps_per_seq = seq // step
    nb = seq // BLOCK
    prev_blk = lambda b, s: b * nb + jnp.maximum(s * blocks_per_step - 1, 0)
    n_steps = batch * steps_per_seq
    mod_cols = MOD_POST * D_MODEL
    tn = mod_cols // n_steps
    assert tn % 128 == 0 and (MOD_PRE * D_MODEL) % tn == 0
    mod_tile = lambda b, s: (0, b * steps_per_seq + s)
    mod_src_tile = lambda b, s: (0, MOD_PRE * D_MODEL // tn + b * steps_per_seq + s)
    return pl.pallas_call(
        functools.partial(_swa_kernel, blocks_per_step=blocks_per_step),
        out_shape=(jax.ShapeDtypeStruct((batch * seq, SWA_Q_COLS), BF16),
                   jax.ShapeDtypeStruct((8, mod_cols), F32)),
        grid=(batch, steps_per_seq),
        in_specs=[
            pl.BlockSpec(memory_space=pltpu.SMEM),
            pl.BlockSpec((SWA_Q_COLS, step), lambda b, s: (0, b * steps_per_seq + s)),
            pl.BlockSpec((BLOCK, SWA_KV_COLS), lambda b, s: (prev_blk(b, s), 0)),
            pl.BlockSpec((step, SWA_KV_COLS), lambda b, s: (b * steps_per_seq + s, 0)),
            pl.BlockSpec((SWA_KV_COLS, BLOCK), lambda b, s: (0, prev_blk(b, s))),
            pl.BlockSpec((SWA_KV_COLS, step), lambda b, s: (0, b * steps_per_seq + s)),
            _const_spec((2, SWA_HEADS, 2 * BLOCK, BLOCK)),
            _const_spec((8, D_MODEL)),
            pl.BlockSpec((D_MODEL, tn), mod_src_tile),
            pl.BlockSpec((1, tn), mod_src_tile),
        ],
        out_specs=(
            pl.BlockSpec((step, SWA_Q_COLS), lambda b, s: (b * steps_per_seq + s, 0)),
            pl.BlockSpec((8, tn), mod_tile),
        ),
        scratch_shapes=[
            pltpu.VMEM((2, 2 * BLOCK, SWA_UNIT_HEADS * BLOCK), F32),
            pltpu.VMEM((2, 2 * BLOCK, SWA_UNIT_HEADS * BLOCK), BF16),
            pltpu.VMEM((2, SWA_UNIT_HEADS, 1, BLOCK), F32),
        ],
        compiler_params=_params(("parallel", "arbitrary")),
        name="swa",
    )(sinks, qat, ka, ka, vat, vat, bias_tab, c8, w_mod, b_mod)


def _mla_kernel(q_ref, k_ref, vt_ref, o_ref, s_ref, p_ref, al_ref, m_ref, acc_ref,
                *, tile, n_tiles):
    pairs = [(qi, j) for qi in range(n_tiles) for j in range(qi + 1)]
    width = tile // MLA_COL_SPLIT

    def geometry(u, c):
        qi, j = pairs[u]
        cols = slice(c * width, (c + 1) * width)
        n_keys = (c + 1) * width if j == qi else tile
        return qi, j, cols, n_keys

    def stage_scores(u, c):
        qi, j, cols, n_keys = geometry(u, c)
        k = k_ref[j * tile:j * tile + n_keys, :]
        q = q_ref[qi * tile + c * width:qi * tile + (c + 1) * width, :]
        s_ref[u % MLA_SLOTS, :n_keys, cols] = lax.dot_general(
            k, q, (((1,), (1,)), ((), ())), preferred_element_type=F32)

    def stage_softmax(u, c):
        qi, j, cols, n_keys = geometry(u, c)
        st = s_ref[u % MLA_SLOTS, :n_keys, cols]
        if j == qi:
            kk = lax.broadcasted_iota(jnp.int32, st.shape, 0)
            qq = lax.broadcasted_iota(jnp.int32, st.shape, 1) + c * width
            st = jnp.where(kk <= qq, st, -jnp.inf)
        cmax = jnp.max(st, axis=0, keepdims=True)
        if j == 0:
            m_new = cmax
        else:
            m_old = m_ref[qi % 2, :, cols]
            m_new = jnp.maximum(m_old, cmax)
            al_ref[u % MLA_SLOTS, :, cols] = jnp.exp2(m_old - m_new)
        m_ref[qi % 2, :, cols] = m_new
        p_ref[u % MLA_SLOTS, :n_keys, cols] = jnp.exp2(st - m_new).astype(BF16)

    def stage_pv(u, c):
        qi, j, cols, n_keys = geometry(u, c)
        vt_tile = vt_ref.shape[-1]
        pv = None
        for lo in range(0, n_keys, vt_tile):
            nk = min(vt_tile, n_keys - lo)
            part = jnp.dot(vt_ref[0, 0, (j * tile + lo) // vt_tile, :, :nk],
                           p_ref[u % MLA_SLOTS, lo:lo + nk, cols], preferred_element_type=F32)
            pv = part if pv is None else pv + part
        if j == 0:
            acc = pv
        else:
            acc = al_ref[u % MLA_SLOTS, :, cols] * acc_ref[:, cols] + pv
        if j == qi:
            rows = slice(qi * tile + c * width, qi * tile + (c + 1) * width)
            denom = acc[MLA_V_DIM:MLA_V_DIM + 1]
            o_ref[rows, :] = (acc[:MLA_V_DIM] / denom).T.astype(BF16)
        else:
            acc_ref[:, cols] = acc

    n = len(pairs)
    skew = MLA_SLOTS - 1
    for t in range(n + 2 * skew):
        for c in range(MLA_COL_SPLIT):
            if 0 <= t - 2 * skew < n:
                stage_pv(t - 2 * skew, c)
            if 0 <= t - skew < n:
                stage_softmax(t - skew, c)
            if t < n:
                stage_scores(t, c)


def _mla_call(qb, kcat, vt, *, batch, seq):
    tile = MLA_TILE
    n_tiles = seq // tile
    return pl.pallas_call(
        functools.partial(_mla_kernel, tile=tile, n_tiles=n_tiles),
        out_shape=jax.ShapeDtypeStruct((batch * seq, MLA_HEADS * MLA_V_DIM), BF16),
        grid=(batch, MLA_HEADS),
        in_specs=[
            pl.BlockSpec((seq, MLA_QK_PAD), lambda b, h: (b, h)),
            pl.BlockSpec((seq, MLA_QK_PAD), lambda b, h: (b, h)),
            pl.BlockSpec((1, 1, seq // VT_TILE, MLA_VT_ROWS, VT_TILE), lambda b, h: (b, h, 0, 0, 0)),
        ],
        out_specs=pl.BlockSpec((seq, MLA_V_DIM), lambda b, h: (b, h)),
        scratch_shapes=[
            pltpu.VMEM((MLA_SLOTS, tile, tile), F32),
            pltpu.VMEM((MLA_SLOTS, tile, tile), BF16),
            pltpu.VMEM((MLA_SLOTS, 1, tile), F32),
            pltpu.VMEM((2, 1, tile), F32),
            pltpu.VMEM((MLA_VT_ROWS, tile), F32),
        ],
        compiler_params=_params(("parallel", "parallel")),
        name="mla",
    )(qb, kcat, vt)


def _out_kernel(x_ref, oa_ref, ob_ref, mod_ref, wa_ref, wb_ref, x1_ref):
    g1 = mod_ref[0][0:1]
    y = (jnp.dot(oa_ref[...], wa_ref[...], preferred_element_type=F32)
         + jnp.dot(ob_ref[...], wb_ref[...], preferred_element_type=F32))
    x1_ref[...] = x_ref[...] + g1 * y


def _out_call(x2d, oa, ob, mod3, w_out, *, seq):
    tm = TM_OUT
    t = x2d.shape[0]
    tiles_per_seq = seq // tm
    tok = lambda i: (i, 0)
    w_half = lambda half: pl.BlockSpec((SWA_Q_COLS, D_MODEL), lambda i: (half, 0),
                                       pipeline_mode=pl.Buffered(1))
    assert w_out.shape[0] == 2 * SWA_Q_COLS
    return pl.pallas_call(
        _out_kernel,
        out_shape=jax.ShapeDtypeStruct((t, D_MODEL), F32),
        grid=(t // tm,),
        in_specs=[
            pl.BlockSpec((tm, D_MODEL), tok),
            pl.BlockSpec((tm, SWA_Q_COLS), tok),
            pl.BlockSpec((tm, MLA_HEADS * MLA_V_DIM), tok),
            pl.BlockSpec((1, MOD_POST, D_MODEL), lambda i: (i // tiles_per_seq, 0, 0)),
            w_half(0),
            w_half(1),
        ],
        out_specs=pl.BlockSpec((tm, D_MODEL), tok),
        compiler_params=_params(("parallel",)),
        name="out_proj",
    )(x2d, oa, ob, mod3, w_out, w_out)


def _mlp_kernel(x_ref, mod_ref, g_ref, gf_ref, w1_ref, w2_ref, o_ref, h_ref):
    j = pl.program_id(1)
    last = pl.num_programs(1) - 1
    mod = mod_ref[0]
    rows = o_ref.shape[0] // ROW_CHUNKS
    chunks = [slice(r * rows, (r + 1) * rows) for r in range(ROW_CHUNKS)]

    def ff_tile(h):
        u = jnp.maximum(jnp.dot(h, w1_ref[...], preferred_element_type=F32), 0.0)
        return jnp.dot((u * u).astype(BF16), w2_ref[...], preferred_element_type=F32)

    @pl.when(j == 0)
    def _():
        sh2 = mod[1:2]
        sc2 = mod[2:3]
        for sl in chunks:
            h = ((_rms(x_ref[sl, :]) * g_ref[...]) * (1.0 + sc2) + sh2).astype(BF16)
            h_ref[sl, :] = h
            o_ref[sl, :] = ff_tile(h)

    @pl.when((j > 0) & (j < last))
    def _():
        o_ref[...] += ff_tile(h_ref[...])

    @pl.when(j == last)
    def _():
        g2 = mod[3:4]
        for sl in chunks:
            x2 = x_ref[sl, :] + g2 * (o_ref[sl, :] + ff_tile(h_ref[sl, :]))
            o_ref[sl, :] = _rms(x2) * gf_ref[...]


def _mlp_call(x1, mod3, g_mlp, g_final, w1, w2, *, seq):
    tm = TM_MLP
    tf = TF_MLP
    assert D_FF // tf >= 2
    t = x1.shape[0]
    tiles_per_seq = seq // tm
    return pl.pallas_call(
        _mlp_kernel,
        out_shape=jax.ShapeDtypeStruct((t, D_MODEL), F32),
        grid=(t // tm, D_FF // tf),
        in_specs=[
            pl.BlockSpec((tm, D_MODEL), lambda i, j: (i, 0)),
            pl.BlockSpec((1, MOD_POST, D_MODEL), lambda i, j: (i // tiles_per_seq, 0, 0)),
            _const_spec((1, D_MODEL)),
            _const_spec((1, D_MODEL)),
            pl.BlockSpec((D_MODEL, tf), lambda i, j: (0, j)),
            pl.BlockSpec((tf, D_MODEL), lambda i, j: (j, 0)),
        ],
        out_specs=pl.BlockSpec((tm, D_MODEL), lambda i, j: (i, 0)),
        scratch_shapes=[pltpu.VMEM((tm, D_MODEL), BF16)],
        compiler_params=_params(("parallel", "arbitrary")),
        name="mlp",
    )(x1, mod3, g_mlp, g_final, w1, w2)


def _t5_bucket_table():
    q_loc = np.arange(BLOCK)[:, None]
    k_loc = np.arange(2 * BLOCK)[None, :]
    n = np.maximum(q_loc + BLOCK - k_loc, 0)
    max_exact = REL_BUCKETS // 2
    nf = np.maximum(n, 1).astype(np.float64)
    large = max_exact + (np.log(nf / max_exact) / math.log(REL_MAX_DIST / max_exact)
                         * (REL_BUCKETS - max_exact)).astype(np.int32)
    large = np.minimum(large, REL_BUCKETS - 1)
    return np.where(n < max_exact, n, large).astype(np.int32)


def _rope_tables(seq):
    half = MLA_ROPE_DIM // 2
    inv_freq = ROPE_THETA ** (-np.arange(half, dtype=np.float64) / half)
    ang = np.arange(seq, dtype=np.float64)[:, None] * inv_freq[None, :]
    zeros = np.zeros((seq, 128 - MLA_ROPE_DIM))
    cos_tab = np.concatenate([np.cos(ang), np.cos(ang), zeros], axis=1)
    sin_tab = np.concatenate([np.sin(ang), np.sin(ang), zeros], axis=1)
    return cos_tab.astype(np.float32), sin_tab.astype(np.float32)


def _rot_cols(w):
    half = w.shape[-1] // 2
    return jnp.concatenate([-w[..., half:], w[..., :half]], axis=-1)


def kernel(x, c, w_mod, b_mod, attn_norm_g, w_in, swa_sinks, rel_bias, mla_q_norm_g, w_uq,
           mla_kv_norm_g, w_ukv, w_out, mlp_norm_g, w_ff1, w_ff2, final_norm_g):
    batch, seq, _ = x.shape
    depth = w_mod.shape[0]
    assert depth == 1
    t = batch * seq
    x2d = x.reshape(t, D_MODEL)
    l = 0

    w_kr = w_in[l][:, OFF_MLA_KR:OFF_MLA_KR + MLA_ROPE_DIM]
    w_main = jnp.concatenate(
        [w_in[l][:, OFF_SWA_K:OFF_SWA_V], w_in[l][:, OFF_MLA_CQ:OFF_MLA_KR],
         w_kr, _rot_cols(w_kr)], axis=1).astype(BF16)
    w_t = jnp.concatenate(
        [w_in[l][:, :SWA_Q_COLS], w_in[l][:, OFF_SWA_V:OFF_MLA_CQ]], axis=1).T.astype(BF16)

    wq = w_uq[l].reshape(MLA_Q_RANK, MLA_HEADS, MLA_QK_DIM)
    wq_nope = wq[..., :MLA_NOPE_DIM]
    wq_rope = wq[..., MLA_NOPE_DIM:]
    hh = MLA_HEADS // 2
    zq = jnp.zeros((MLA_Q_RANK, hh, 128 - MLA_ROPE_DIM), F32)
    wq_main = jnp.concatenate(
        [jnp.concatenate([wq_nope[:, :hh], wq_rope[:, :hh], zq], axis=-1),
         jnp.concatenate([wq_nope[:, hh:], zq, wq_rope[:, hh:]], axis=-1)], axis=1).reshape(
        MLA_Q_RANK, MLA_HEADS * MLA_QK_PAD).astype(BF16)
    wq_rot_all = _rot_cols(wq_rope)
    wq_rot = jnp.concatenate([wq_rot_all[:, :hh], wq_rot_all[:, hh:]], axis=-1).reshape(
        MLA_Q_RANK, hh * 128).astype(BF16)

    wkv = w_ukv[l].reshape(MLA_KV_RANK, MLA_HEADS, MLA_NOPE_DIM + MLA_V_DIM)
    w_uk = wkv[..., :MLA_NOPE_DIM].reshape(MLA_KV_RANK, MLA_HEADS * MLA_NOPE_DIM).astype(BF16)
    w_vt = wkv[..., MLA_NOPE_DIM:].reshape(MLA_KV_RANK, MLA_HEADS * MLA_V_DIM).T.astype(BF16)

    cos_tab, sin_tab = _rope_tables(seq)

    assert w_mod.shape[2] == (MOD_PRE + MOD_POST) * D_MODEL
    c8 = jnp.pad(c, ((0, 8 - batch), (0, 0)))
    b_mod2d = b_mod[l].reshape(1, -1)
    mod_pre = _mod_call(c8, w_mod[l], b_mod2d)[:batch].reshape(batch, MOD_PRE, D_MODEL)

    bias_tab = _bias_call(rel_bias, _t5_bucket_table().T)

    qat, ka, vat, qb, kcat, vt, w_out_b16, w1, w2 = _proj_call(
        x2d, mod_pre, attn_norm_g[l].reshape(1, -1), w_main, w_t,
        mla_q_norm_g[l].reshape(1, -1), mla_kv_norm_g[l].reshape(1, -1),
        wq_main, wq_rot, w_uk, w_vt, cos_tab, sin_tab, w_out[l], w_ff1[l], w_ff2[l],
        batch=batch, seq=seq)

    oa, mod_post = _swa_call(swa_sinks[l], qat, ka, vat, bias_tab, c8, w_mod[l], b_mod2d,
                             batch=batch, seq=seq)
    mod_post = mod_post[:batch].reshape(batch, MOD_POST, D_MODEL)
    ob = _mla_call(qb, kcat, vt, batch=batch, seq=seq)

    x1 = _out_call(x2d, oa, ob, mod_post, w_out_b16, seq=seq)
    out = _mlp_call(x1, mod_post, mlp_norm_g[l].reshape(1, -1), final_norm_g.reshape(1, -1),
                    w1, w2, seq=seq)
    return out.reshape(batch, seq, D_MODEL)
```

```python
import functools
import math

import jax
import jax.numpy as jnp
import numpy as np
from jax import lax
from jax.experimental import pallas as pl
from jax.experimental.pallas import tpu as pltpu

F32 = jnp.float32
BF16 = jnp.bfloat16

D_MODEL = 2048
BLOCK = 128
EPS = 1e-6

SWA_HEADS = 16
SWA_KV_HEADS = 2
SWA_HEAD_DIM = 64
SWA_GROUP = SWA_HEADS // SWA_KV_HEADS
WINDOW = 128
REL_BUCKETS = 32
REL_MAX_DIST = 128

MLA_HEADS = 8
MLA_Q_RANK = 384
MLA_KV_RANK = 128
MLA_NOPE_DIM = 128
MLA_ROPE_DIM = 64
MLA_V_DIM = 128
MLA_VT_ROWS = MLA_V_DIM + 16
MLA_QK_DIM = MLA_NOPE_DIM + MLA_ROPE_DIM
MLA_QK_PAD = 256
ROPE_THETA = 10000.0
MLA_Q_SCALE = MLA_QK_DIM ** -0.5 * math.log2(math.e)
D_FF = 4 * D_MODEL

SWA_Q_COLS = SWA_HEADS * SWA_HEAD_DIM
SWA_KV_COLS = SWA_KV_HEADS * SWA_HEAD_DIM
OFF_SWA_K = SWA_Q_COLS
OFF_SWA_V = OFF_SWA_K + SWA_KV_COLS
OFF_MLA_CQ = OFF_SWA_V + SWA_KV_COLS
OFF_MLA_CKV = OFF_MLA_CQ + MLA_Q_RANK
OFF_MLA_KR = OFF_MLA_CKV + MLA_KV_RANK
P_CQ = SWA_KV_COLS
P_CKV = P_CQ + MLA_Q_RANK
P_KR = P_CKV + MLA_KV_RANK
P_COLS = P_KR + 128
PT_ROWS = SWA_Q_COLS + SWA_KV_COLS
LOG2E = math.log2(math.e)

VMEM_LIMIT_BYTES = 62 * 1024 * 1024

TM_PROJ = 512
MLA_TILE = 512
VT_TILE = min(MLA_TILE, TM_PROJ)
TM_SWA = 512
TM_OUT = 512
TM_MLP = 512
TF_MLP = 2048
TN_MOD = 1024
MOD_PRE = 2
MOD_POST = 4
BIAS_HEADS_PER_STEP = 4
SWA_UNIT_HEADS = 8
MLA_SLOTS = 2
MLA_COL_SPLIT = 2
ROW_CHUNKS = 2


def _params(sem):
    return pltpu.CompilerParams(dimension_semantics=sem, vmem_limit_bytes=VMEM_LIMIT_BYTES)


def _const_spec(shape):
    nd = len(shape)
    return pl.BlockSpec(shape, lambda *_: (0,) * nd, pipeline_mode=pl.Buffered(1))


def _rms(x):
    return x * lax.rsqrt(jnp.mean(x * x, axis=-1, keepdims=True) + EPS)


def _mod_tile(c_ref, w_ref, b_ref):
    c = c_ref[...]
    ca = c * (1.0 / (1.0 + jnp.exp(-c)))
    return jnp.dot(ca.astype(BF16), w_ref[...].astype(BF16),
                   preferred_element_type=F32) + b_ref[...]


def _mod_kernel(c_ref, w_ref, b_ref, o_ref):
    o_ref[...] = _mod_tile(c_ref, w_ref, b_ref)


def _mod_call(c8, w_mod, b_mod):
    n = MOD_PRE * D_MODEL
    return pl.pallas_call(
        _mod_kernel,
        out_shape=jax.ShapeDtypeStruct((8, n), F32),
        grid=(n // TN_MOD,),
        in_specs=[
            pl.BlockSpec((8, D_MODEL), lambda j: (0, 0)),
            pl.BlockSpec((D_MODEL, TN_MOD), lambda j: (0, j)),
            pl.BlockSpec((1, TN_MOD), lambda j: (0, j)),
        ],
        out_specs=pl.BlockSpec((8, TN_MOD), lambda j: (0, j)),
        compiler_params=_params(("arbitrary",)),
        name="mod",
    )(c8, w_mod, b_mod)


def _bias_kernel(rel_ref, bucket_ref, o_ref):
    bucket = bucket_ref[...]
    k_loc = lax.broadcasted_iota(jnp.int32, bucket.shape, 0)
    q_loc = lax.broadcasted_iota(jnp.int32, bucket.shape, 1)
    dist = q_loc + BLOCK - k_loc
    in_window = (dist >= 0) & (dist < WINDOW)
    for i in range(BIAS_HEADS_PER_STEP):
        h = pl.program_id(0) * BIAS_HEADS_PER_STEP + i
        acc = jnp.zeros(bucket.shape, F32)
        for k in range(REL_BUCKETS):
            acc = jnp.where(bucket == k, rel_ref[k, h], acc)
        acc = acc * LOG2E
        o_ref[0, i] = jnp.where(in_window & (k_loc >= BLOCK), acc, -jnp.inf)
        o_ref[1, i] = jnp.where(in_window, acc, -jnp.inf)


def _bias_call(rel_bias, bucket_t):
    hs = BIAS_HEADS_PER_STEP
    return pl.pallas_call(
        _bias_kernel,
        out_shape=jax.ShapeDtypeStruct((2, SWA_HEADS, 2 * BLOCK, BLOCK), F32),
        grid=(SWA_HEADS // hs,),
        in_specs=[
            pl.BlockSpec(memory_space=pltpu.SMEM),
            pl.BlockSpec((2 * BLOCK, BLOCK), lambda s: (0, 0)),
        ],
        out_specs=pl.BlockSpec((2, hs, 2 * BLOCK, BLOCK), lambda s: (0, s, 0, 0)),
        compiler_params=_params(("arbitrary",)),
        name="t5_bias",
    )(rel_bias, bucket_t)


def _proj_kernel(x_ref, mod_ref, g_ref, win_ref, wt_ref, gq_ref, gkv_ref, wqm_ref, wqr_ref,
                 wuk_ref, wvt_ref, cos_ref, sin_ref, wo_ref, w1_ref, w2_ref,
                 qat_ref, ka_ref, vat_ref, qb_ref, kc_ref, vt_ref, wob_ref, w1b_ref, w2b_ref,
                 *, q_scale):
    wob_ref[...] = wo_ref[...].astype(BF16)
    w1b_ref[...] = w1_ref[...].astype(BF16)
    w2b_ref[...] = w2_ref[...].astype(BF16)

    x = x_ref[...]
    mod = mod_ref[0]
    sh1 = mod[0:1]
    sc1 = mod[1:2]
    h = ((_rms(x) * g_ref[...]) * (1.0 + sc1) + sh1).astype(BF16)
    proj = jnp.dot(h, win_ref[...], preferred_element_type=F32)
    proj_t = lax.dot_general(wt_ref[...], h, (((1,), (1,)), ((), ())),
                             preferred_element_type=F32)

    qat_ref[...] = (proj_t[:SWA_Q_COLS] * (SWA_HEAD_DIM ** -0.5 * LOG2E)).astype(BF16)
    vat_ref[...] = proj_t[SWA_Q_COLS:].astype(BF16)
    ka_ref[...] = proj[:, :P_CQ].astype(BF16)

    cq = (_rms(proj[:, P_CQ:P_CKV]) * gq_ref[...]).astype(BF16)
    ckv = (_rms(proj[:, P_CKV:P_KR]) * gkv_ref[...]).astype(BF16)
    cos_lo = cos_ref[...]
    sin_lo = sin_ref[...]
    cos_hi = pltpu.roll(cos_lo, MLA_ROPE_DIM, 1)
    sin_hi = pltpu.roll(sin_lo, MLA_ROPE_DIM, 1)
    kr = proj[:, P_KR:P_COLS]
    kr_sw = pltpu.roll(kr, MLA_ROPE_DIM, 1)
    krope_lo = (kr * cos_lo + kr_sw * sin_lo).astype(BF16)
    krope_hi = (kr_sw * cos_hi + kr * sin_hi).astype(BF16)

    qmain = jnp.dot(cq, wqm_ref[...], preferred_element_type=F32)
    qrot = jnp.dot(cq, wqr_ref[...], preferred_element_type=F32)
    knope = jnp.dot(ckv, wuk_ref[...], preferred_element_type=F32)
    vt = lax.dot_general(wvt_ref[...], ckv, (((1,), (1,)), ((), ())),
                         preferred_element_type=F32)

    half_heads = MLA_HEADS // 2
    pad_rows = MLA_VT_ROWS - MLA_V_DIM
    ones_rows = jnp.where(lax.broadcasted_iota(jnp.int32, (pad_rows, VT_TILE), 0) == 0,
                          1.0, 0.0).astype(BF16)
    for hd in range(MLA_HEADS):
        lo = hd * MLA_QK_PAD
        mid = lo + MLA_NOPE_DIM
        hi = lo + MLA_QK_PAD
        low = hd < half_heads
        cos_t, sin_t = (cos_lo, sin_lo) if low else (cos_hi, sin_hi)
        rot = qrot[:, (hd % half_heads) * 128:(hd % half_heads + 1) * 128]
        qb_ref[:, lo:mid] = (qmain[:, lo:mid] * q_scale).astype(BF16)
        qb_ref[:, mid:hi] = ((qmain[:, mid:hi] * cos_t + rot * sin_t) * q_scale).astype(BF16)
        kc_ref[:, lo:mid] = knope[:, hd * MLA_NOPE_DIM:(hd + 1) * MLA_NOPE_DIM].astype(BF16)
        kc_ref[:, mid:hi] = krope_lo if low else krope_hi
        for kt in range(vt_ref.shape[2]):
            vt_ref[0, hd, kt, :MLA_V_DIM] = vt[hd * MLA_V_DIM:(hd + 1) * MLA_V_DIM,
                                               kt * VT_TILE:(kt + 1) * VT_TILE].astype(BF16)
            vt_ref[0, hd, kt, MLA_V_DIM:] = ones_rows


def _proj_call(x2d, mod3, g_attn, w_main, w_t, gq, gkv, wq_main, wq_rot, w_uk, w_vt,
               cos_tab, sin_tab, w_out, w_ff1, w_ff2, *, batch, seq):
    tm = TM_PROJ
    tiles_per_seq = seq // tm
    t = batch * seq
    steps = t // tm
    slab = lambda w: (w.shape[0] // steps, w.shape[1])
    assert all(w.shape[0] % (steps * 16) == 0 for w in (w_out, w_ff1, w_ff2))
    tok = lambda i: (i, 0)
    tok_t = lambda i: (0, i)
    pos = lambda i: (i % tiles_per_seq, 0)
    out_shape = (
        jax.ShapeDtypeStruct((SWA_Q_COLS, t), BF16),
        jax.ShapeDtypeStruct((t, SWA_KV_COLS), BF16),
        jax.ShapeDtypeStruct((SWA_KV_COLS, t), BF16),
        jax.ShapeDtypeStruct((t, MLA_HEADS * MLA_QK_PAD), BF16),
        jax.ShapeDtypeStruct((t, MLA_HEADS * MLA_QK_PAD), BF16),
        jax.ShapeDtypeStruct((batch, MLA_HEADS, seq // VT_TILE, MLA_VT_ROWS, VT_TILE), BF16),
        jax.ShapeDtypeStruct(w_out.shape, BF16),
        jax.ShapeDtypeStruct(w_ff1.shape, BF16),
        jax.ShapeDtypeStruct(w_ff2.shape, BF16),
    )
    return pl.pallas_call(
        functools.partial(_proj_kernel, q_scale=MLA_Q_SCALE),
        out_shape=out_shape,
        grid=(t // tm,),
        in_specs=[
            pl.BlockSpec((tm, D_MODEL), tok),
            pl.BlockSpec((1, MOD_PRE, D_MODEL), lambda i: (i // tiles_per_seq, 0, 0)),
            _const_spec((1, D_MODEL)),
            _const_spec((D_MODEL, P_COLS)),
            _const_spec((PT_ROWS, D_MODEL)),
            _const_spec((1, MLA_Q_RANK)),
            _const_spec((1, MLA_KV_RANK)),
            _const_spec((MLA_Q_RANK, MLA_HEADS * MLA_QK_PAD)),
            _const_spec((MLA_Q_RANK, MLA_HEADS // 2 * 128)),
            _const_spec((MLA_KV_RANK, MLA_HEADS * MLA_NOPE_DIM)),
            _const_spec((MLA_HEADS * MLA_V_DIM, MLA_KV_RANK)),
            pl.BlockSpec((tm, 128), pos),
            pl.BlockSpec((tm, 128), pos),
            pl.BlockSpec(slab(w_out), tok),
            pl.BlockSpec(slab(w_ff1), tok),
            pl.BlockSpec(slab(w_ff2), tok),
        ],
        out_specs=(
            pl.BlockSpec((SWA_Q_COLS, tm), tok_t),
            pl.BlockSpec((tm, SWA_KV_COLS), tok),
            pl.BlockSpec((SWA_KV_COLS, tm), tok_t),
            pl.BlockSpec((tm, MLA_HEADS * MLA_QK_PAD), tok),
            pl.BlockSpec((tm, MLA_HEADS * MLA_QK_PAD), tok),
            pl.BlockSpec((1, MLA_HEADS, tm // VT_TILE, MLA_VT_ROWS, VT_TILE),
                         lambda i: (i // tiles_per_seq, 0, i % tiles_per_seq, 0, 0)),
            pl.BlockSpec(slab(w_out), tok),
            pl.BlockSpec(slab(w_ff1), tok),
            pl.BlockSpec(slab(w_ff2), tok),
        ),
        compiler_params=_params(("parallel",)),
        name="in_proj",
    )(x2d, mod3, g_attn, w_main, w_t, gq, gkv, wq_main, wq_rot, w_uk, w_vt,
      cos_tab, sin_tab, w_out, w_ff1, w_ff2)


def _swa_kernel(sinks_ref, qt_ref, kp_ref, kc_ref, vtp_ref, vtc_ref, bias_ref,
                c_ref, wmod_ref, bmod_ref, o_ref, mod_ref,
                s_ref, p_ref, sink_ref, *, blocks_per_step):
    mod_ref[...] = _mod_tile(c_ref, wmod_ref, bmod_ref)

    first_step = pl.program_id(1) == 0
    k_all = jnp.concatenate([kp_ref[...], kc_ref[...]], axis=0)
    vt_all = jnp.concatenate([vtp_ref[...], vtc_ref[...]], axis=1)
    uh = SWA_UNIT_HEADS
    zeros_q = jnp.zeros((SWA_HEAD_DIM, uh * BLOCK), BF16)
    ones_rows = jnp.where(lax.broadcasted_iota(jnp.int32, (16, 2 * BLOCK), 0) == 0,
                          1.0, 0.0).astype(BF16)
    units = [(i, g, part) for i in range(blocks_per_step) for g in range(SWA_KV_HEADS)
             for part in range(SWA_GROUP // uh)]

    def stage_scores(u):
        i, g, part = units[u]
        h_first = g * SWA_GROUP + part * uh
        k_band = k_all[i * BLOCK:(i + 2) * BLOCK]
        qt = qt_ref[:, i * BLOCK:(i + 1) * BLOCK]
        q_g = jnp.concatenate(
            [qt[(h_first + hh) * SWA_HEAD_DIM:(h_first + hh + 1) * SWA_HEAD_DIM]
             for hh in range(uh)], axis=1)
        rhs = jnp.concatenate([q_g, zeros_q] if g == 0 else [zeros_q, q_g], axis=0)
        s_ref[u % 2] = jnp.dot(k_band, rhs, preferred_element_type=F32)

    def stage_softmax(u):
        i, g, part = units[u]
        variant = jnp.where(first_step, 0, 1) if i == 0 else 1
        for hh in range(uh):
            hd = g * SWA_GROUP + part * uh + hh
            st = s_ref[u % 2, :, hh * BLOCK:(hh + 1) * BLOCK] + bias_ref[variant, hd]
            sink = sinks_ref[hd] * LOG2E
            m = jnp.maximum(jnp.max(st, axis=0, keepdims=True), sink)
            sink_ref[u % 2, hh] = jnp.exp2(sink - m)
            p_ref[u % 2, :, hh * BLOCK:(hh + 1) * BLOCK] = jnp.exp2(st - m).astype(BF16)

    def stage_pv(u):
        i, g, part = units[u]
        vt_band = jnp.concatenate(
            [vt_all[g * SWA_HEAD_DIM:(g + 1) * SWA_HEAD_DIM, i * BLOCK:(i + 2) * BLOCK],
             ones_rows], axis=0)
        ot_g = jnp.dot(vt_band, p_ref[u % 2], preferred_element_type=F32)

        def head_out(hh):
            cols = slice(hh * BLOCK, (hh + 1) * BLOCK)
            denom = ot_g[SWA_HEAD_DIM:SWA_HEAD_DIM + 1, cols] + sink_ref[u % 2, hh]
            return ot_g[:SWA_HEAD_DIM, cols] * (1.0 / denom)

        for pair in range(uh // 2):
            h0 = 2 * pair
            slab = jnp.concatenate([head_out(h0), head_out(h0 + 1)], axis=0)
            col = (g * SWA_GROUP + part * uh + h0) * SWA_HEAD_DIM
            o_ref[i * BLOCK:(i + 1) * BLOCK, col:col + 2 * SWA_HEAD_DIM] = (
                slab.T.astype(BF16))

    n = len(units)
    for t in range(n + 2):
        if 0 <= t - 2 < n:
            stage_pv(t - 2)
        if 0 <= t - 1 < n:
            stage_softmax(t - 1)
        if t < n:
            stage_scores(t)


def _swa_call(sinks, qat, ka, vat, bias_tab, c8, w_mod, b_mod, *, batch, seq):
    step = TM_SWA
    blocks_per_step = step // BLOCK
    steps_per_seq = seq // step
    nb = seq // BLOCK
    prev_blk = lambda b, s: b * nb + jnp.maximum(s * blocks_per_step - 1, 0)
    n_steps = batch * steps_per_seq
    mod_cols = MOD_POST * D_MODEL
    tn = mod_cols // n_steps
    assert tn % 128 == 0 and (MOD_PRE * D_MODEL) % tn == 0
    mod_tile = lambda b, s: (0, b * steps_per_seq + s)
    mod_src_tile = lambda b, s: (0, MOD_PRE * D_MODEL // tn + b * steps_per_seq + s)
    return pl.pallas_call(
        functools.partial(_swa_kernel, blocks_per_step=blocks_per_step),
        out_shape=(jax.ShapeDtypeStruct((batch * seq, SWA_Q_COLS), BF16),
                   jax.ShapeDtypeStruct((8, mod_cols), F32)),
        grid=(batch, steps_per_seq),
        in_specs=[
            pl.BlockSpec(memory_space=pltpu.SMEM),
            pl.BlockSpec((SWA_Q_COLS, step), lambda b, s: (0, b * steps_per_seq + s)),
            pl.BlockSpec((BLOCK, SWA_KV_COLS), lambda b, s: (prev_blk(b, s), 0)),
            pl.BlockSpec((step, SWA_KV_COLS), lambda b, s: (b * steps_per_seq + s, 0)),
            pl.BlockSpec((SWA_KV_COLS, BLOCK), lambda b, s: (0, prev_blk(b, s))),
            pl.BlockSpec((SWA_KV_COLS, step), lambda b, s: (0, b * steps_per_seq + s)),
            _const_spec((2, SWA_HEADS, 2 * BLOCK, BLOCK)),
            _const_spec((8, D_MODEL)),
            pl.BlockSpec((D_MODEL, tn), mod_src_tile),
            pl.BlockSpec((1, tn), mod_src_tile),
        ],
        out_specs=(
            pl.BlockSpec((step, SWA_Q_COLS), lambda b, s: (b * steps_per_seq + s, 0)),
            pl.BlockSpec((8, tn), mod_tile),
        ),
        scratch_shapes=[
            pltpu.VMEM((2, 2 * BLOCK, SWA_UNIT_HEADS * BLOCK), F32),
            pltpu.VMEM((2, 2 * BLOCK, SWA_UNIT_HEADS * BLOCK), BF16),
            pltpu.VMEM((2, SWA_UNIT_HEADS, 1, BLOCK), F32),
        ],
        compiler_params=_params(("parallel", "arbitrary")),
        name="swa",
    )(sinks, qat, ka, ka, vat, vat, bias_tab, c8, w_mod, b_mod)


def _mla_kernel(q_ref, k_ref, vt_ref, o_ref, s_ref, p_ref, al_ref, m_ref, acc_ref,
                *, tile, n_tiles):
    pairs = [(qi, j) for qi in range(n_tiles) for j in range(qi + 1)]
    width = tile // MLA_COL_SPLIT

    def geometry(u, c):
        qi, j = pairs[u]
        cols = slice(c * width, (c + 1) * width)
        n_keys = (c + 1) * width if j == qi else tile
        return qi, j, cols, n_keys

    def stage_scores(u, c):
        qi, j, cols, n_keys = geometry(u, c)
        k = k_ref[j * tile:j * tile + n_keys, :]
        q = q_ref[qi * tile + c * width:qi * tile + (c + 1) * width, :]
        s_ref[u % MLA_SLOTS, :n_keys, cols] = lax.dot_general(
            k, q, (((1,), (1,)), ((), ())), preferred_element_type=F32)

    def stage_softmax(u, c):
        qi, j, cols, n_keys = geometry(u, c)
        st = s_ref[u % MLA_SLOTS, :n_keys, cols]
        if j == qi:
            kk = lax.broadcasted_iota(jnp.int32, st.shape, 0)
            qq = lax.broadcasted_iota(jnp.int32, st.shape, 1) + c * width
            st = jnp.where(kk <= qq, st, -jnp.inf)
        cmax = jnp.max(st, axis=0, keepdims=True)
        if j == 0:
            m_new = cmax
        else:
            m_old = m_ref[qi % 2, :, cols]
            m_new = jnp.maximum(m_old, cmax)
            al_ref[u % MLA_SLOTS, :, cols] = jnp.exp2(m_old - m_new)
        m_ref[qi % 2, :, cols] = m_new
        p_ref[u % MLA_SLOTS, :n_keys, cols] = jnp.exp2(st - m_new).astype(BF16)

    def stage_pv(u, c):
        qi, j, cols, n_keys = geometry(u, c)
        vt_tile = vt_ref.shape[-1]
        pv = None
        for lo in range(0, n_keys, vt_tile):
            nk = min(vt_tile, n_keys - lo)
            part = jnp.dot(vt_ref[0, 0, (j * tile + lo) // vt_tile, :, :nk],
                           p_ref[u % MLA_SLOTS, lo:lo + nk, cols], preferred_element_type=F32)
            pv = part if pv is None else pv + part
        if j == 0:
            acc = pv
        else:
            acc = al_ref[u % MLA_SLOTS, :, cols] * acc_ref[:, cols] + pv
        if j == qi:
            rows = slice(qi * tile + c * width, qi * tile + (c + 1) * width)
            denom = acc[MLA_V_DIM:MLA_V_DIM + 1]
            o_ref[rows, :] = (acc[:MLA_V_DIM] / denom).T.astype(BF16)
        else:
            acc_ref[:, cols] = acc

    n = len(pairs)
    skew = MLA_SLOTS - 1
    for t in range(n + 2 * skew):
        for c in range(MLA_COL_SPLIT):
            if 0 <= t - 2 * skew < n:
                stage_pv(t - 2 * skew, c)
            if 0 <= t - skew < n:
                stage_softmax(t - skew, c)
            if t < n:
                stage_scores(t, c)


def _mla_call(qb, kcat, vt, *, batch, seq):
    tile = MLA_TILE
    n_tiles = seq // tile
    return pl.pallas_call(
        functools.partial(_mla_kernel, tile=tile, n_tiles=n_tiles),
        out_shape=jax.ShapeDtypeStruct((batch * seq, MLA_HEADS * MLA_V_DIM), BF16),
        grid=(batch, MLA_HEADS),
        in_specs=[
            pl.BlockSpec((seq, MLA_QK_PAD), lambda b, h: (b, h)),
            pl.BlockSpec((seq, MLA_QK_PAD), lambda b, h: (b, h)),
            pl.BlockSpec((1, 1, seq // VT_TILE, MLA_VT_ROWS, VT_TILE), lambda b, h: (b, h, 0, 0, 0)),
        ],
        out_specs=pl.BlockSpec((seq, MLA_V_DIM), lambda b, h: (b, h)),
        scratch_shapes=[
            pltpu.VMEM((MLA_SLOTS, tile, tile), F32),
            pltpu.VMEM((MLA_SLOTS, tile, tile), BF16),
            pltpu.VMEM((MLA_SLOTS, 1, tile), F32),
            pltpu.VMEM((2, 1, tile), F32),
            pltpu.VMEM((MLA_VT_ROWS, tile), F32),
        ],
        compiler_params=_params(("parallel", "parallel")),
        name="mla",
    )(qb, kcat, vt)


def _out_kernel(x_ref, oa_ref, ob_ref, mod_ref, wa_ref, wb_ref, x1_ref):
    g1 = mod_ref[0][0:1]
    y = (jnp.dot(oa_ref[...], wa_ref[...], preferred_element_type=F32)
         + jnp.dot(ob_ref[...], wb_ref[...], preferred_element_type=F32))
    x1_ref[...] = x_ref[...] + g1 * y


def _out_call(x2d, oa, ob, mod3, w_out, *, seq):
    tm = TM_OUT
    t = x2d.shape[0]
    tiles_per_seq = seq // tm
    tok = lambda i: (i, 0)
    w_half = lambda half: pl.BlockSpec((SWA_Q_COLS, D_MODEL), lambda i: (half, 0),
                                       pipeline_mode=pl.Buffered(1))
    assert w_out.shape[0] == 2 * SWA_Q_COLS
    return pl.pallas_call(
        _out_kernel,
        out_shape=jax.ShapeDtypeStruct((t, D_MODEL), F32),
        grid=(t // tm,),
        in_specs=[
            pl.BlockSpec((tm, D_MODEL), tok),
            pl.BlockSpec((tm, SWA_Q_COLS), tok),
            pl.BlockSpec((tm, MLA_HEADS * MLA_V_DIM), tok),
            pl.BlockSpec((1, MOD_POST, D_MODEL), lambda i: (i // tiles_per_seq, 0, 0)),
            w_half(0),
            w_half(1),
        ],
        out_specs=pl.BlockSpec((tm, D_MODEL), tok),
        compiler_params=_params(("parallel",)),
        name="out_proj",
    )(x2d, oa, ob, mod3, w_out, w_out)


def _mlp_kernel(x_ref, mod_ref, g_ref, gf_ref, w1_ref, w2_ref, o_ref, h_ref):
    j = pl.program_id(1)
    last = pl.num_programs(1) - 1
    mod = mod_ref[0]
    rows = o_ref.shape[0] // ROW_CHUNKS
    chunks = [slice(r * rows, (r + 1) * rows) for r in range(ROW_CHUNKS)]

    def ff_tile(h):
        u = jnp.maximum(jnp.dot(h, w1_ref[...], preferred_element_type=F32), 0.0)
        return jnp.dot((u * u).astype(BF16), w2_ref[...], preferred_element_type=F32)

    @pl.when(j == 0)
    def _():
        sh2 = mod[1:2]
        sc2 = mod[2:3]
        for sl in chunks:
            h = ((_rms(x_ref[sl, :]) * g_ref[...]) * (1.0 + sc2) + sh2).astype(BF16)
            h_ref[sl, :] = h
            o_ref[sl, :] = ff_tile(h)

    @pl.when((j > 0) & (j < last))
    def _():
        o_ref[...] += ff_tile(h_ref[...])

    @pl.when(j == last)
    def _():
        g2 = mod[3:4]
        for sl in chunks:
            x2 = x_ref[sl, :] + g2 * (o_ref[sl, :] + ff_tile(h_ref[sl, :]))
            o_ref[sl, :] = _rms(x2) * gf_ref[...]


def _mlp_call(x1, mod3, g_mlp, g_final, w1, w2, *, seq):
    tm = TM_MLP
    tf = TF_MLP
    assert D_FF // tf >= 2
    t = x1.shape[0]
    tiles_per_seq = seq // tm
    return pl.pallas_call(
        _mlp_kernel,
        out_shape=jax.ShapeDtypeStruct((t, D_MODEL), F32),
        grid=(t // tm, D_FF // tf),
        in_specs=[
            pl.BlockSpec((tm, D_MODEL), lambda i, j: (i, 0)),
            pl.BlockSpec((1, MOD_POST, D_MODEL), lambda i, j: (i // tiles_per_seq, 0, 0)),
            _const_spec((1, D_MODEL)),
            _const_spec((1, D_MODEL)),
            pl.BlockSpec((D_MODEL, tf), lambda i, j: (0, j)),
            pl.BlockSpec((tf, D_MODEL), lambda i, j: (j, 0)),
        ],
        out_specs=pl.BlockSpec((tm, D_MODEL), lambda i, j: (i, 0)),
        scratch_shapes=[pltpu.VMEM((tm, D_MODEL), BF16)],
        compiler_params=_params(("parallel", "arbitrary")),
        name="mlp",
    )(x1, mod3, g_mlp, g_final, w1, w2)


def _t5_bucket_table():
    q_loc = np.arange(BLOCK)[:, None]
    k_loc = np.arange(2 * BLOCK)[None, :]
    n = np.maximum(q_loc + BLOCK - k_loc, 0)
    max_exact = REL_BUCKETS // 2
    nf = np.maximum(n, 1).astype(np.float64)
    large = max_exact + (np.log(nf / max_exact) / math.log(REL_MAX_DIST / max_exact)
                         * (REL_BUCKETS - max_exact)).astype(np.int32)
    large = np.minimum(large, REL_BUCKETS - 1)
    return np.where(n < max_exact, n, large).astype(np.int32)


def _rope_tables(seq):
    half = MLA_ROPE_DIM // 2
    inv_freq = ROPE_THETA ** (-np.arange(half, dtype=np.float64) / half)
    ang = np.arange(seq, dtype=np.float64)[:, None] * inv_freq[None, :]
    zeros = np.zeros((seq, 128 - MLA_ROPE_DIM))
    cos_tab = np.concatenate([np.cos(ang), np.cos(ang), zeros], axis=1)
    sin_tab = np.concatenate([np.sin(ang), np.sin(ang), zeros], axis=1)
    return cos_tab.astype(np.float32), sin_tab.astype(np.float32)


def _rot_cols(w):
    half = w.shape[-1] // 2
    return jnp.concatenate([-w[..., half:], w[..., :half]], axis=-1)


def kernel(x, c, w_mod, b_mod, attn_norm_g, w_in, swa_sinks, rel_bias, mla_q_norm_g, w_uq,
           mla_kv_norm_g, w_ukv, w_out, mlp_norm_g, w_ff1, w_ff2, final_norm_g):
    batch, seq, _ = x.shape
    depth = w_mod.shape[0]
    assert depth == 1
    t = batch * seq
    x2d = x.reshape(t, D_MODEL)
    l = 0

    w_kr = w_in[l][:, OFF_MLA_KR:OFF_MLA_KR + MLA_ROPE_DIM]
    w_main = jnp.concatenate(
        [w_in[l][:, OFF_SWA_K:OFF_SWA_V], w_in[l][:, OFF_MLA_CQ:OFF_MLA_KR],
         w_kr, _rot_cols(w_kr)], axis=1).astype(BF16)
    w_t = jnp.concatenate(
        [w_in[l][:, :SWA_Q_COLS], w_in[l][:, OFF_SWA_V:OFF_MLA_CQ]], axis=1).T.astype(BF16)

    wq = w_uq[l].reshape(MLA_Q_RANK, MLA_HEADS, MLA_QK_DIM)
    wq_nope = wq[..., :MLA_NOPE_DIM]
    wq_rope = wq[..., MLA_NOPE_DIM:]
    hh = MLA_HEADS // 2
    zq = jnp.zeros((MLA_Q_RANK, hh, 128 - MLA_ROPE_DIM), F32)
    wq_main = jnp.concatenate(
        [jnp.concatenate([wq_nope[:, :hh], wq_rope[:, :hh], zq], axis=-1),
         jnp.concatenate([wq_nope[:, hh:], zq, wq_rope[:, hh:]], axis=-1)], axis=1).reshape(
        MLA_Q_RANK, MLA_HEADS * MLA_QK_PAD).astype(BF16)
    wq_rot_all = _rot_cols(wq_rope)
    wq_rot = jnp.concatenate([wq_rot_all[:, :hh], wq_rot_all[:, hh:]], axis=-1).reshape(
        MLA_Q_RANK, hh * 128).astype(BF16)

    wkv = w_ukv[l].reshape(MLA_KV_RANK, MLA_HEADS, MLA_NOPE_DIM + MLA_V_DIM)
    w_uk = wkv[..., :MLA_NOPE_DIM].reshape(MLA_KV_RANK, MLA_HEADS * MLA_NOPE_DIM).astype(BF16)
    w_vt = wkv[..., MLA_NOPE_DIM:].reshape(MLA_KV_RANK, MLA_HEADS * MLA_V_DIM).T.astype(BF16)

    cos_tab, sin_tab = _rope_tables(seq)

    assert w_mod.shape[2] == (MOD_PRE + MOD_POST) * D_MODEL
    c8 = jnp.pad(c, ((0, 8 - batch), (0, 0)))
    b_mod2d = b_mod[l].reshape(1, -1)
    mod_pre = _mod_call(c8, w_mod[l], b_mod2d)[:batch].reshape(batch, MOD_PRE, D_MODEL)

    bias_tab = _bias_call(rel_bias, _t5_bucket_table().T)

    qat, ka, vat, qb, kcat, vt, w_out_b16, w1, w2 = _proj_call(
        x2d, mod_pre, attn_norm_g[l].reshape(1, -1), w_main, w_t,
        mla_q_norm_g[l].reshape(1, -1), mla_kv_norm_g[l].reshape(1, -1),
        wq_main, wq_rot, w_uk, w_vt, cos_tab, sin_tab, w_out[l], w_ff1[l], w_ff2[l],
        batch=batch, seq=seq)

    oa, mod_post = _swa_call(swa_sinks[l], qat, ka, vat, bias_tab, c8, w_mod[l], b_mod2d,
                             batch=batch, seq=seq)
    mod_post = mod_post[:batch].reshape(batch, MOD_POST, D_MODEL)
    ob = _mla_call(qb, kcat, vt, batch=batch, seq=seq)

    x1 = _out_call(x2d, oa, ob, mod_post, w_out_b16, seq=seq)
    out = _mlp_call(x1, mod_post, mlp_norm_g[l].reshape(1, -1), final_norm_g.reshape(1, -1),
                    w1, w2, seq=seq)
    return out.reshape(batch, seq, D_MODEL)
```

```python
import functools
import math

import jax
import jax.numpy as jnp
import numpy as np
from jax import lax
from jax.experimental import pallas as pl
from jax.experimental.pallas import tpu as pltpu

F32 = jnp.float32
BF16 = jnp.bfloat16

LANES = 128
BF16_ROWS = 16

D_MODEL = 2048
BLOCK = 128
EPS = 1e-6

SWA_HEADS = 16
SWA_KV_HEADS = 2
SWA_HEAD_DIM = 64
SWA_GROUP = SWA_HEADS // SWA_KV_HEADS
WINDOW = 128
REL_BUCKETS = 32
REL_MAX_DIST = 128

MLA_HEADS = 8
MLA_Q_RANK = 384
MLA_KV_RANK = 128
MLA_NOPE_DIM = 128
MLA_ROPE_DIM = 64
MLA_V_DIM = 128
MLA_VT_ROWS = MLA_V_DIM + BF16_ROWS
MLA_QK_DIM = MLA_NOPE_DIM + MLA_ROPE_DIM
MLA_QK_PAD = 256
ROPE_THETA = 10000.0
MLA_Q_SCALE = MLA_QK_DIM ** -0.5 * math.log2(math.e)
D_FF = 4 * D_MODEL

SWA_Q_COLS = SWA_HEADS * SWA_HEAD_DIM
SWA_KV_COLS = SWA_KV_HEADS * SWA_HEAD_DIM
OFF_SWA_K = SWA_Q_COLS
OFF_SWA_V = OFF_SWA_K + SWA_KV_COLS
OFF_MLA_CQ = OFF_SWA_V + SWA_KV_COLS
OFF_MLA_CKV = OFF_MLA_CQ + MLA_Q_RANK
OFF_MLA_KR = OFF_MLA_CKV + MLA_KV_RANK
P_CQ = SWA_KV_COLS
P_CKV = P_CQ + MLA_Q_RANK
P_KR = P_CKV + MLA_KV_RANK
P_COLS = P_KR + LANES
PT_ROWS = SWA_Q_COLS + SWA_KV_COLS
LOG2E = math.log2(math.e)

VMEM_LIMIT_BYTES = 62 * 1024 * 1024

TM_PROJ = 512
MLA_TILE = 512
VT_TILE = min(MLA_TILE, TM_PROJ)
TM_SWA = 512
TM_OUT = 512
TM_MLP = 512
TF_MLP = 2048
MOD_PRE = 2
MOD_POST = 4
BIAS_HEADS_PER_STEP = 4
SWA_UNIT_HEADS = 4
MLA_SLOTS = 2
MLA_COL_SPLIT = 2
ROW_CHUNKS = 2


def _params(sem):
    return pltpu.CompilerParams(dimension_semantics=sem, vmem_limit_bytes=VMEM_LIMIT_BYTES)


def _const_spec(shape):
    nd = len(shape)
    return pl.BlockSpec(shape, lambda *_: (0,) * nd, pipeline_mode=pl.Buffered(1))


def _rms(x):
    return x * lax.rsqrt(jnp.mean(x * x, axis=-1, keepdims=True) + EPS)


def _mod_tile(c_ref, w_ref, b_ref):
    c = c_ref[...]
    ca = c * (1.0 / (1.0 + jnp.exp(-c)))
    return jnp.dot(ca.astype(BF16), w_ref[...].astype(BF16),
                   preferred_element_type=F32) + b_ref[...]


def _prologue_kernel(c_ref, w_ref, b_ref, rel_ref, bucket_ref, mod_ref, o_ref):
    mod_ref[...] = _mod_tile(c_ref, w_ref, b_ref)

    bucket = bucket_ref[...]
    k_loc = lax.broadcasted_iota(jnp.int32, bucket.shape, 0)
    q_loc = lax.broadcasted_iota(jnp.int32, bucket.shape, 1)
    dist = q_loc + BLOCK - k_loc
    in_window = (dist >= 0) & (dist < WINDOW)
    for i in range(BIAS_HEADS_PER_STEP):
        h = pl.program_id(0) * BIAS_HEADS_PER_STEP + i
        acc = jnp.zeros(bucket.shape, F32)
        for k in range(REL_BUCKETS):
            acc = jnp.where(bucket == k, rel_ref[k, h], acc)
        acc = acc * LOG2E
        o_ref[0, i] = jnp.where(in_window & (k_loc >= BLOCK), acc, -jnp.inf)
        o_ref[1, i] = jnp.where(in_window, acc, -jnp.inf)


def _prologue_call(c8, w_mod, b_mod, rel_bias, bucket_t):
    hs = BIAS_HEADS_PER_STEP
    steps = SWA_HEADS // hs
    n = MOD_PRE * D_MODEL
    tn = n // steps
    assert tn % LANES == 0
    return pl.pallas_call(
        _prologue_kernel,
        out_shape=(jax.ShapeDtypeStruct((8, n), F32),
                   jax.ShapeDtypeStruct((2, SWA_HEADS, 2 * BLOCK, BLOCK), F32)),
        grid=(steps,),
        in_specs=[
            pl.BlockSpec((8, D_MODEL), lambda s: (0, 0)),
            pl.BlockSpec((D_MODEL, tn), lambda s: (0, s)),
            pl.BlockSpec((1, tn), lambda s: (0, s)),
            pl.BlockSpec(memory_space=pltpu.SMEM),
            pl.BlockSpec((2 * BLOCK, BLOCK), lambda s: (0, 0)),
        ],
        out_specs=(pl.BlockSpec((8, tn), lambda s: (0, s)),
                   pl.BlockSpec((2, hs, 2 * BLOCK, BLOCK), lambda s: (0, s, 0, 0))),
        compiler_params=_params(("arbitrary",)),
        name="prologue",
    )(c8, w_mod, b_mod, rel_bias, bucket_t)


def _proj_kernel(x_ref, mod_ref, g_ref, win_ref, wt_ref, gq_ref, gkv_ref, wqm_ref, wqr_ref,
                 wuk_ref, wvt_ref, cos_ref, sin_ref, wo_ref, w1_ref, w2_ref,
                 qat_ref, ka_ref, vat_ref, qb_ref, kc_ref, vt_ref, wob_ref, w1b_ref, w2b_ref,
                 *, q_scale):
    wob_ref[...] = wo_ref[...].astype(BF16)
    w1b_ref[...] = w1_ref[...].astype(BF16)
    w2b_ref[...] = w2_ref[...].astype(BF16)

    x = x_ref[...]
    mod = mod_ref[0]
    sh1 = mod[0:1]
    sc1 = mod[1:2]
    h = ((_rms(x) * g_ref[...]) * (1.0 + sc1) + sh1).astype(BF16)
    proj = jnp.dot(h, win_ref[...], preferred_element_type=F32)
    proj_t = lax.dot_general(wt_ref[...], h, (((1,), (1,)), ((), ())),
                             preferred_element_type=F32)

    qat_ref[0] = (proj_t[:SWA_Q_COLS] * (SWA_HEAD_DIM ** -0.5 * LOG2E)).astype(BF16)
    vat_ref[0] = proj_t[SWA_Q_COLS:].astype(BF16)
    ka_ref[...] = proj[:, :P_CQ].astype(BF16)

    cq = (_rms(proj[:, P_CQ:P_CKV]) * gq_ref[...]).astype(BF16)
    ckv = (_rms(proj[:, P_CKV:P_KR]) * gkv_ref[...]).astype(BF16)
    cos_lo = cos_ref[...]
    sin_lo = sin_ref[...]
    cos_hi = pltpu.roll(cos_lo, MLA_ROPE_DIM, 1)
    sin_hi = pltpu.roll(sin_lo, MLA_ROPE_DIM, 1)
    kr = proj[:, P_KR:P_COLS]
    kr_sw = pltpu.roll(kr, MLA_ROPE_DIM, 1)
    krope_lo = (kr * cos_lo + kr_sw * sin_lo).astype(BF16)
    krope_hi = (kr_sw * cos_hi + kr * sin_hi).astype(BF16)

    qmain = jnp.dot(cq, wqm_ref[...], preferred_element_type=F32)
    qrot = jnp.dot(cq, wqr_ref[...], preferred_element_type=F32)
    knope = jnp.dot(ckv, wuk_ref[...], preferred_element_type=F32)
    vt = lax.dot_general(wvt_ref[...], ckv, (((1,), (1,)), ((), ())),
                         preferred_element_type=F32)

    half_heads = MLA_HEADS // 2
    pad_rows = MLA_VT_ROWS - MLA_V_DIM
    ones_rows = jnp.where(lax.broadcasted_iota(jnp.int32, (pad_rows, VT_TILE), 0) == 0,
                          1.0, 0.0).astype(BF16)
    for hd in range(MLA_HEADS):
        lo = hd * MLA_QK_PAD
        mid = lo + MLA_NOPE_DIM
        hi = lo + MLA_QK_PAD
        low = hd < half_heads
        cos_t, sin_t = (cos_lo, sin_lo) if low else (cos_hi, sin_hi)
        rot = qrot[:, (hd % half_heads) * LANES:(hd % half_heads + 1) * LANES]
        qb_ref[:, lo:mid] = (qmain[:, lo:mid] * q_scale).astype(BF16)
        qb_ref[:, mid:hi] = ((qmain[:, mid:hi] * cos_t + rot * sin_t) * q_scale).astype(BF16)
        kc_ref[:, lo:mid] = knope[:, hd * MLA_NOPE_DIM:(hd + 1) * MLA_NOPE_DIM].astype(BF16)
        kc_ref[:, mid:hi] = krope_lo if low else krope_hi
        for kt in range(vt_ref.shape[2]):
            vt_ref[0, hd, kt, :MLA_V_DIM] = vt[hd * MLA_V_DIM:(hd + 1) * MLA_V_DIM,
                                               kt * VT_TILE:(kt + 1) * VT_TILE].astype(BF16)
            vt_ref[0, hd, kt, MLA_V_DIM:] = ones_rows


def _proj_call(x2d, mod3, g_attn, w_main, w_t, gq, gkv, wq_main, wq_rot, w_uk, w_vt,
               cos_tab, sin_tab, w_out, w_ff1, w_ff2, *, batch, seq):
    tm = TM_PROJ
    tiles_per_seq = seq // tm
    t = batch * seq
    steps = t // tm
    slab = lambda w: (w.shape[0] // steps, w.shape[1])
    assert all(w.shape[0] % (steps * BF16_ROWS) == 0 for w in (w_out, w_ff1, w_ff2))
    tok = lambda i: (i, 0)
    tile_t = lambda i: (i, 0, 0)
    pos = lambda i: (i % tiles_per_seq, 0)
    out_shape = (
        jax.ShapeDtypeStruct((t // tm, SWA_Q_COLS, tm), BF16),
        jax.ShapeDtypeStruct((t, SWA_KV_COLS), BF16),
        jax.ShapeDtypeStruct((t // tm, SWA_KV_COLS, tm), BF16),
        jax.ShapeDtypeStruct((t, MLA_HEADS * MLA_QK_PAD), BF16),
        jax.ShapeDtypeStruct((t, MLA_HEADS * MLA_QK_PAD), BF16),
        jax.ShapeDtypeStruct((batch, MLA_HEADS, seq // VT_TILE, MLA_VT_ROWS, VT_TILE), BF16),
        jax.ShapeDtypeStruct(w_out.shape, BF16),
        jax.ShapeDtypeStruct(w_ff1.shape, BF16),
        jax.ShapeDtypeStruct(w_ff2.shape, BF16),
    )
    return pl.pallas_call(
        functools.partial(_proj_kernel, q_scale=MLA_Q_SCALE),
        out_shape=out_shape,
        grid=(t // tm,),
        in_specs=[
            pl.BlockSpec((tm, D_MODEL), tok),
            pl.BlockSpec((1, MOD_PRE, D_MODEL), lambda i: (i // tiles_per_seq, 0, 0)),
            _const_spec((1, D_MODEL)),
            _const_spec((D_MODEL, P_COLS)),
            _const_spec((PT_ROWS, D_MODEL)),
            _const_spec((1, MLA_Q_RANK)),
            _const_spec((1, MLA_KV_RANK)),
            _const_spec((MLA_Q_RANK, MLA_HEADS * MLA_QK_PAD)),
            _const_spec((MLA_Q_RANK, MLA_HEADS // 2 * LANES)),
            _const_spec((MLA_KV_RANK, MLA_HEADS * MLA_NOPE_DIM)),
            _const_spec((MLA_HEADS * MLA_V_DIM, MLA_KV_RANK)),
            pl.BlockSpec((tm, LANES), pos),
            pl.BlockSpec((tm, LANES), pos),
            pl.BlockSpec(slab(w_out), tok),
            pl.BlockSpec(slab(w_ff1), tok),
            pl.BlockSpec(slab(w_ff2), tok),
        ],
        out_specs=(
            pl.BlockSpec((1, SWA_Q_COLS, tm), tile_t),
            pl.BlockSpec((tm, SWA_KV_COLS), tok),
            pl.BlockSpec((1, SWA_KV_COLS, tm), tile_t),
            pl.BlockSpec((tm, MLA_HEADS * MLA_QK_PAD), tok),
            pl.BlockSpec((tm, MLA_HEADS * MLA_QK_PAD), tok),
            pl.BlockSpec((1, MLA_HEADS, tm // VT_TILE, MLA_VT_ROWS, VT_TILE),
                         lambda i: (i // tiles_per_seq, 0, i % tiles_per_seq, 0, 0)),
            pl.BlockSpec(slab(w_out), tok),
            pl.BlockSpec(slab(w_ff1), tok),
            pl.BlockSpec(slab(w_ff2), tok),
        ),
        compiler_params=_params(("parallel",)),
        name="in_proj",
    )(x2d, mod3, g_attn, w_main, w_t, gq, gkv, wq_main, wq_rot, w_uk, w_vt,
      cos_tab, sin_tab, w_out, w_ff1, w_ff2)


def _swa_kernel(sinks_ref, qt_ref, kp_ref, kc_ref, vtp_ref, vtc_ref, bias_ref,
                c_ref, wmod_ref, bmod_ref, o_ref, mod_ref,
                s_ref, p_ref, sink_ref, *, blocks_per_step):
    mod_ref[...] = _mod_tile(c_ref, wmod_ref, bmod_ref)

    first_step = pl.program_id(1) == 0
    k_all = jnp.concatenate([kp_ref[...], kc_ref[...]], axis=0)
    vt_all = jnp.concatenate([vtp_ref[0], vtc_ref[0]], axis=1)
    uh = SWA_UNIT_HEADS
    zeros_q = jnp.zeros((SWA_HEAD_DIM, uh * BLOCK), BF16)
    ones_rows = jnp.where(lax.broadcasted_iota(jnp.int32, (BF16_ROWS, 2 * BLOCK), 0) == 0,
                          1.0, 0.0).astype(BF16)
    units = [(i, g, part) for i in range(blocks_per_step) for g in range(SWA_KV_HEADS)
             for part in range(SWA_GROUP // uh)]

    def stage_scores(u):
        i, g, part = units[u]
        h_first = g * SWA_GROUP + part * uh
        k_band = k_all[i * BLOCK:(i + 2) * BLOCK]
        qt = qt_ref[0, :, i * BLOCK:(i + 1) * BLOCK]
        q_g = jnp.concatenate(
            [qt[(h_first + hh) * SWA_HEAD_DIM:(h_first + hh + 1) * SWA_HEAD_DIM]
             for hh in range(uh)], axis=1)
        rhs = jnp.concatenate([q_g, zeros_q] if g == 0 else [zeros_q, q_g], axis=0)
        s_ref[u % 2] = jnp.dot(k_band, rhs, preferred_element_type=F32)

    def stage_softmax(u):
        i, g, part = units[u]
        variant = jnp.where(first_step, 0, 1) if i == 0 else 1
        for hh in range(uh):
            hd = g * SWA_GROUP + part * uh + hh
            st = s_ref[u % 2, :, hh * BLOCK:(hh + 1) * BLOCK] + bias_ref[variant, hd]
            sink = sinks_ref[hd] * LOG2E
            m = jnp.maximum(jnp.max(st, axis=0, keepdims=True), sink)
            sink_ref[u % 2, hh] = jnp.exp2(sink - m)
            p_ref[u % 2, :, hh * BLOCK:(hh + 1) * BLOCK] = jnp.exp2(st - m).astype(BF16)

    def stage_pv(u):
        i, g, part = units[u]
        vt_band = jnp.concatenate(
            [vt_all[g * SWA_HEAD_DIM:(g + 1) * SWA_HEAD_DIM, i * BLOCK:(i + 2) * BLOCK],
             ones_rows], axis=0)
        ot_g = jnp.dot(vt_band, p_ref[u % 2], preferred_element_type=F32)

        def head_out(hh):
            cols = slice(hh * BLOCK, (hh + 1) * BLOCK)
            denom = ot_g[SWA_HEAD_DIM:SWA_HEAD_DIM + 1, cols] + sink_ref[u % 2, hh]
            return ot_g[:SWA_HEAD_DIM, cols] * (1.0 / denom)

        for pair in range(uh // 2):
            h0 = 2 * pair
            slab = jnp.concatenate([head_out(h0), head_out(h0 + 1)], axis=0)
            col = (g * SWA_GROUP + part * uh + h0) * SWA_HEAD_DIM
            o_ref[i * BLOCK:(i + 1) * BLOCK, col:col + 2 * SWA_HEAD_DIM] = (
                slab.T.astype(BF16))

    n = len(units)
    for t in range(n + 2):
        if 0 <= t - 2 < n:
            stage_pv(t - 2)
        if 0 <= t - 1 < n:
            stage_softmax(t - 1)
        if t < n:
            stage_scores(t)


def _swa_call(sinks, qat, ka, vat, bias_tab, c8, w_mod, b_mod, *, batch, seq):
    step = TM_SWA
    blocks_per_step = step // BLOCK
    steps_per_seq = seq // step
    nb = seq // BLOCK
    prev_blk = lambda b, s: b * nb + jnp.maximum(s * blocks_per_step - 1, 0)
    assert step == TM_PROJ
    tile = lambda b, s: (b * steps_per_seq + s, 0, 0)
    prev_tile_last_blk = lambda b, s: (b * steps_per_seq + jnp.maximum(s - 1, 0), 0,
                                       blocks_per_step - 1)
    n_steps = batch * steps_per_seq
    mod_cols = MOD_POST * D_MODEL
    tn = mod_cols // n_steps
    assert tn % LANES == 0 and (MOD_PRE * D_MODEL) % tn == 0
    mod_tile = lambda b, s: (0, b * steps_per_seq + s)
    mod_src_tile = lambda b, s: (0, MOD_PRE * D_MODEL // tn + b * steps_per_seq + s)
    return pl.pallas_call(
        functools.partial(_swa_kernel, blocks_per_step=blocks_per_step),
        out_shape=(jax.ShapeDtypeStruct((batch * seq, SWA_Q_COLS), BF16),
                   jax.ShapeDtypeStruct((8, mod_cols), F32)),
        grid=(batch, steps_per_seq),
        in_specs=[
            pl.BlockSpec(memory_space=pltpu.SMEM),
            pl.BlockSpec((1, SWA_Q_COLS, step), tile),
            pl.BlockSpec((BLOCK, SWA_KV_COLS), lambda b, s: (prev_blk(b, s), 0)),
            pl.BlockSpec((step, SWA_KV_COLS), lambda b, s: (b * steps_per_seq + s, 0)),
            pl.BlockSpec((1, SWA_KV_COLS, BLOCK), prev_tile_last_blk),
            pl.BlockSpec((1, SWA_KV_COLS, step), tile),
            _const_spec((2, SWA_HEADS, 2 * BLOCK, BLOCK)),
            _const_spec((8, D_MODEL)),
            pl.BlockSpec((D_MODEL, tn), mod_src_tile),
            pl.BlockSpec((1, tn), mod_src_tile),
        ],
        out_specs=(
            pl.BlockSpec((step, SWA_Q_COLS), lambda b, s: (b * steps_per_seq + s, 0)),
            pl.BlockSpec((8, tn), mod_tile),
        ),
        scratch_shapes=[
            pltpu.VMEM((2, 2 * BLOCK, SWA_UNIT_HEADS * BLOCK), F32),
            pltpu.VMEM((2, 2 * BLOCK, SWA_UNIT_HEADS * BLOCK), BF16),
            pltpu.VMEM((2, SWA_UNIT_HEADS, 1, BLOCK), F32),
        ],
        compiler_params=_params(("parallel", "arbitrary")),
        name="swa",
    )(sinks, qat, ka, ka, vat, vat, bias_tab, c8, w_mod, b_mod)


def _mla_kernel(q_ref, k_ref, vt_ref, o_ref, s_ref, p_ref, al_ref, m_ref, acc_ref,
                *, tile, n_tiles):
    pairs = [(qi, j) for qi in range(n_tiles) for j in range(qi + 1)]
    width = tile // MLA_COL_SPLIT

    def geometry(u, c):
        qi, j = pairs[u]
        cols = slice(c * width, (c + 1) * width)
        n_keys = (c + 1) * width if j == qi else tile
        return qi, j, cols, n_keys

    def stage_scores(u, c):
        qi, j, cols, n_keys = geometry(u, c)
        k = k_ref[j * tile:j * tile + n_keys, :]
        q = q_ref[qi * tile + c * width:qi * tile + (c + 1) * width, :]
        s_ref[u % MLA_SLOTS, :n_keys, cols] = lax.dot_general(
            k, q, (((1,), (1,)), ((), ())), preferred_element_type=F32)

    def stage_softmax(u, c):
        qi, j, cols, n_keys = geometry(u, c)
        st = s_ref[u % MLA_SLOTS, :n_keys, cols]
        if j == qi:
            kk = lax.broadcasted_iota(jnp.int32, st.shape, 0)
            qq = lax.broadcasted_iota(jnp.int32, st.shape, 1) + c * width
            st = jnp.where(kk <= qq, st, -jnp.inf)
        cmax = jnp.max(st, axis=0, keepdims=True)
        if j == 0:
            m_new = cmax
        else:
            m_old = m_ref[qi % 2, :, cols]
            m_new = jnp.maximum(m_old, cmax)
            al_ref[u % MLA_SLOTS, :, cols] = jnp.exp2(m_old - m_new)
        m_ref[qi % 2, :, cols] = m_new
        p_ref[u % MLA_SLOTS, :n_keys, cols] = jnp.exp2(st - m_new).astype(BF16)

    def stage_pv(u, c):
        qi, j, cols, n_keys = geometry(u, c)
        vt_tile = vt_ref.shape[-1]
        pv = None
        for lo in range(0, n_keys, vt_tile):
            nk = min(vt_tile, n_keys - lo)
            part = jnp.dot(vt_ref[0, 0, (j * tile + lo) // vt_tile, :, :nk],
                           p_ref[u % MLA_SLOTS, lo:lo + nk, cols], preferred_element_type=F32)
            pv = part if pv is None else pv + part
        if j == 0:
            acc = pv
        else:
            acc = al_ref[u % MLA_SLOTS, :, cols] * acc_ref[:, cols] + pv
        if j == qi:
            rows = slice(qi * tile + c * width, qi * tile + (c + 1) * width)
            denom = acc[MLA_V_DIM:MLA_V_DIM + 1]
            o_ref[rows, :] = (acc[:MLA_V_DIM] / denom).T.astype(BF16)
        else:
            acc_ref[:, cols] = acc

    n = len(pairs)
    skew = MLA_SLOTS - 1
    for t in range(n + 2 * skew):
        for c in range(MLA_COL_SPLIT):
            if 0 <= t - 2 * skew < n:
                stage_pv(t - 2 * skew, c)
            if 0 <= t - skew < n:
                stage_softmax(t - skew, c)
            if t < n:
                stage_scores(t, c)


def _mla_call(qb, kcat, vt, *, batch, seq):
    tile = MLA_TILE
    n_tiles = seq // tile
    return pl.pallas_call(
        functools.partial(_mla_kernel, tile=tile, n_tiles=n_tiles),
        out_shape=jax.ShapeDtypeStruct((batch * seq, MLA_HEADS * MLA_V_DIM), BF16),
        grid=(batch, MLA_HEADS),
        in_specs=[
            pl.BlockSpec((seq, MLA_QK_PAD), lambda b, h: (b, h)),
            pl.BlockSpec((seq, MLA_QK_PAD), lambda b, h: (b, h)),
            pl.BlockSpec((1, 1, seq // VT_TILE, MLA_VT_ROWS, VT_TILE), lambda b, h: (b, h, 0, 0, 0)),
        ],
        out_specs=pl.BlockSpec((seq, MLA_V_DIM), lambda b, h: (b, h)),
        scratch_shapes=[
            pltpu.VMEM((MLA_SLOTS, tile, tile), F32),
            pltpu.VMEM((MLA_SLOTS, tile, tile), BF16),
            pltpu.VMEM((MLA_SLOTS, 1, tile), F32),
            pltpu.VMEM((2, 1, tile), F32),
            pltpu.VMEM((MLA_VT_ROWS, tile), F32),
        ],
        compiler_params=_params(("parallel", "parallel")),
        name="mla",
    )(qb, kcat, vt)


def _out_kernel(x_ref, oa_ref, ob_ref, mod_ref, wa_ref, wb_ref, x1_ref):
    g1 = mod_ref[0][0:1]
    y = (jnp.dot(oa_ref[...], wa_ref[...], preferred_element_type=F32)
         + jnp.dot(ob_ref[...], wb_ref[...], preferred_element_type=F32))
    x1_ref[...] = x_ref[...] + g1 * y


def _out_call(x2d, oa, ob, mod3, w_out, *, seq):
    tm = TM_OUT
    t = x2d.shape[0]
    tiles_per_seq = seq // tm
    tok = lambda i: (i, 0)
    w_half = lambda half: pl.BlockSpec((SWA_Q_COLS, D_MODEL), lambda i: (half, 0),
                                       pipeline_mode=pl.Buffered(1))
    assert w_out.shape[0] == 2 * SWA_Q_COLS
    return pl.pallas_call(
        _out_kernel,
        out_shape=jax.ShapeDtypeStruct((t, D_MODEL), F32),
        grid=(t // tm,),
        in_specs=[
            pl.BlockSpec((tm, D_MODEL), tok),
            pl.BlockSpec((tm, SWA_Q_COLS), tok),
            pl.BlockSpec((tm, MLA_HEADS * MLA_V_DIM), tok),
            pl.BlockSpec((1, MOD_POST, D_MODEL), lambda i: (i // tiles_per_seq, 0, 0)),
            w_half(0),
            w_half(1),
        ],
        out_specs=pl.BlockSpec((tm, D_MODEL), tok),
        compiler_params=_params(("parallel",)),
        name="out_proj",
    )(x2d, oa, ob, mod3, w_out, w_out)


def _mlp_kernel(x_ref, mod_ref, g_ref, gf_ref, w1_ref, w2_ref, o_ref, h_ref):
    j = pl.program_id(1)
    last = pl.num_programs(1) - 1
    mod = mod_ref[0]
    rows = o_ref.shape[0] // ROW_CHUNKS
    chunks = [slice(r * rows, (r + 1) * rows) for r in range(ROW_CHUNKS)]

    def ff_tile(h):
        u = jnp.maximum(jnp.dot(h, w1_ref[...], preferred_element_type=F32), 0.0)
        return jnp.dot((u * u).astype(BF16), w2_ref[...], preferred_element_type=F32)

    @pl.when(j == 0)
    def _():
        sh2 = mod[1:2]
        sc2 = mod[2:3]
        for sl in chunks:
            h = ((_rms(x_ref[sl, :]) * g_ref[...]) * (1.0 + sc2) + sh2).astype(BF16)
            h_ref[sl, :] = h
            o_ref[sl, :] = ff_tile(h)

    @pl.when((j > 0) & (j < last))
    def _():
        o_ref[...] += ff_tile(h_ref[...])

    @pl.when(j == last)
    def _():
        g2 = mod[3:4]
        for sl in chunks:
            x2 = x_ref[sl, :] + g2 * (o_ref[sl, :] + ff_tile(h_ref[sl, :]))
            o_ref[sl, :] = _rms(x2) * gf_ref[...]


def _mlp_call(x1, mod3, g_mlp, g_final, w1, w2, *, seq):
    tm = TM_MLP
    tf = TF_MLP
    assert D_FF // tf >= 2
    t = x1.shape[0]
    tiles_per_seq = seq // tm
    return pl.pallas_call(
        _mlp_kernel,
        out_shape=jax.ShapeDtypeStruct((t, D_MODEL), F32),
        grid=(t // tm, D_FF // tf),
        in_specs=[
            pl.BlockSpec((tm, D_MODEL), lambda i, j: (i, 0)),
            pl.BlockSpec((1, MOD_POST, D_MODEL), lambda i, j: (i // tiles_per_seq, 0, 0)),
            _const_spec((1, D_MODEL)),
            _const_spec((1, D_MODEL)),
            pl.BlockSpec((D_MODEL, tf), lambda i, j: (0, j)),
            pl.BlockSpec((tf, D_MODEL), lambda i, j: (j, 0)),
        ],
        out_specs=pl.BlockSpec((tm, D_MODEL), lambda i, j: (i, 0)),
        scratch_shapes=[pltpu.VMEM((tm, D_MODEL), BF16)],
        compiler_params=_params(("parallel", "arbitrary")),
        name="mlp",
    )(x1, mod3, g_mlp, g_final, w1, w2)


def _t5_bucket_table():
    q_loc = np.arange(BLOCK)[:, None]
    k_loc = np.arange(2 * BLOCK)[None, :]
    n = np.maximum(q_loc + BLOCK - k_loc, 0)
    max_exact = REL_BUCKETS // 2
    nf = np.maximum(n, 1).astype(np.float64)
    large = max_exact + (np.log(nf / max_exact) / math.log(REL_MAX_DIST / max_exact)
                         * (REL_BUCKETS - max_exact)).astype(np.int32)
    large = np.minimum(large, REL_BUCKETS - 1)
    return np.where(n < max_exact, n, large).astype(np.int32)


def _rope_tables(seq):
    half = MLA_ROPE_DIM // 2
    inv_freq = ROPE_THETA ** (-np.arange(half, dtype=np.float64) / half)
    ang = np.arange(seq, dtype=np.float64)[:, None] * inv_freq[None, :]
    zeros = np.zeros((seq, LANES - MLA_ROPE_DIM))
    cos_tab = np.concatenate([np.cos(ang), np.cos(ang), zeros], axis=1)
    sin_tab = np.concatenate([np.sin(ang), np.sin(ang), zeros], axis=1)
    return cos_tab.astype(np.float32), sin_tab.astype(np.float32)


def _rot_cols(w):
    half = w.shape[-1] // 2
    return jnp.concatenate([-w[..., half:], w[..., :half]], axis=-1)


def kernel(x, c, w_mod, b_mod, attn_norm_g, w_in, swa_sinks, rel_bias, mla_q_norm_g, w_uq,
           mla_kv_norm_g, w_ukv, w_out, mlp_norm_g, w_ff1, w_ff2, final_norm_g):
    batch, seq, _ = x.shape
    depth = w_mod.shape[0]
    assert depth == 1
    t = batch * seq
    x2d = x.reshape(t, D_MODEL)
    l = 0

    w_kr = w_in[l][:, OFF_MLA_KR:OFF_MLA_KR + MLA_ROPE_DIM]
    w_main = jnp.concatenate(
        [w_in[l][:, OFF_SWA_K:OFF_SWA_V], w_in[l][:, OFF_MLA_CQ:OFF_MLA_KR],
         w_kr, _rot_cols(w_kr)], axis=1).astype(BF16)
    w_t = jnp.concatenate(
        [w_in[l][:, :SWA_Q_COLS], w_in[l][:, OFF_SWA_V:OFF_MLA_CQ]], axis=1).T.astype(BF16)

    wq = w_uq[l].reshape(MLA_Q_RANK, MLA_HEADS, MLA_QK_DIM)
    wq_nope = wq[..., :MLA_NOPE_DIM]
    wq_rope = wq[..., MLA_NOPE_DIM:]
    hh = MLA_HEADS // 2
    zq = jnp.zeros((MLA_Q_RANK, hh, LANES - MLA_ROPE_DIM), F32)
    wq_main = jnp.concatenate(
        [jnp.concatenate([wq_nope[:, :hh], wq_rope[:, :hh], zq], axis=-1),
         jnp.concatenate([wq_nope[:, hh:], zq, wq_rope[:, hh:]], axis=-1)], axis=1).reshape(
        MLA_Q_RANK, MLA_HEADS * MLA_QK_PAD).astype(BF16)
    wq_rot_all = _rot_cols(wq_rope)
    wq_rot = jnp.concatenate([wq_rot_all[:, :hh], wq_rot_all[:, hh:]], axis=-1).reshape(
        MLA_Q_RANK, hh * LANES).astype(BF16)

    wkv = w_ukv[l].reshape(MLA_KV_RANK, MLA_HEADS, MLA_NOPE_DIM + MLA_V_DIM)
    w_uk = wkv[..., :MLA_NOPE_DIM].reshape(MLA_KV_RANK, MLA_HEADS * MLA_NOPE_DIM).astype(BF16)
    w_vt = wkv[..., MLA_NOPE_DIM:].reshape(MLA_KV_RANK, MLA_HEADS * MLA_V_DIM).T.astype(BF16)

    cos_tab, sin_tab = _rope_tables(seq)

    assert w_mod.shape[2] == (MOD_PRE + MOD_POST) * D_MODEL
    c8 = jnp.pad(c, ((0, 8 - batch), (0, 0)))
    b_mod2d = b_mod[l].reshape(1, -1)
    mod_pre, bias_tab = _prologue_call(c8, w_mod[l], b_mod2d, rel_bias, _t5_bucket_table().T)
    mod_pre = mod_pre[:batch].reshape(batch, MOD_PRE, D_MODEL)

    qat, ka, vat, qb, kcat, vt, w_out_b16, w1, w2 = _proj_call(
        x2d, mod_pre, attn_norm_g[l].reshape(1, -1), w_main, w_t,
        mla_q_norm_g[l].reshape(1, -1), mla_kv_norm_g[l].reshape(1, -1),
        wq_main, wq_rot, w_uk, w_vt, cos_tab, sin_tab, w_out[l], w_ff1[l], w_ff2[l],
        batch=batch, seq=seq)

    oa, mod_post = _swa_call(swa_sinks[l], qat, ka, vat, bias_tab, c8, w_mod[l], b_mod2d,
                             batch=batch, seq=seq)
    mod_post = mod_post[:batch].reshape(batch, MOD_POST, D_MODEL)
    ob = _mla_call(qb, kcat, vt, batch=batch, seq=seq)

    x1 = _out_call(x2d, oa, ob, mod_post, w_out_b16, seq=seq)
    out = _mlp_call(x1, mod_post, mlp_norm_g[l].reshape(1, -1), final_norm_g.reshape(1, -1),
                    w1, w2, seq=seq)
    return out.reshape(batch, seq, D_MODEL)
```

```python
import functools
import math

import jax
import jax.numpy as jnp
import numpy as np
from jax import lax
from jax.experimental import pallas as pl
from jax.experimental.pallas import tpu as pltpu

F32 = jnp.float32
BF16 = jnp.bfloat16

LANES = 128
BF16_ROWS = 16

D_MODEL = 2048
BLOCK = 128
EPS = 1e-6

SWA_HEADS = 16
SWA_KV_HEADS = 2
SWA_HEAD_DIM = 64
SWA_GROUP = SWA_HEADS // SWA_KV_HEADS
WINDOW = 128
REL_BUCKETS = 32
REL_MAX_DIST = 128

MLA_HEADS = 8
MLA_Q_RANK = 384
MLA_KV_RANK = 128
MLA_NOPE_DIM = 128
MLA_ROPE_DIM = 64
MLA_V_DIM = 128
MLA_VT_ROWS = MLA_V_DIM + BF16_ROWS
MLA_QK_DIM = MLA_NOPE_DIM + MLA_ROPE_DIM
MLA_QK_PAD = 256
ROPE_THETA = 10000.0
MLA_Q_SCALE = MLA_QK_DIM ** -0.5 * math.log2(math.e)
D_FF = 4 * D_MODEL

SWA_Q_COLS = SWA_HEADS * SWA_HEAD_DIM
SWA_KV_COLS = SWA_KV_HEADS * SWA_HEAD_DIM
OFF_SWA_K = SWA_Q_COLS
OFF_SWA_V = OFF_SWA_K + SWA_KV_COLS
OFF_MLA_CQ = OFF_SWA_V + SWA_KV_COLS
OFF_MLA_CKV = OFF_MLA_CQ + MLA_Q_RANK
OFF_MLA_KR = OFF_MLA_CKV + MLA_KV_RANK
P_CQ = SWA_KV_COLS
P_CKV = P_CQ + MLA_Q_RANK
P_KR = P_CKV + MLA_KV_RANK
P_COLS = P_KR + LANES
PT_ROWS = SWA_Q_COLS + SWA_KV_COLS
LOG2E = math.log2(math.e)

VMEM_LIMIT_BYTES = 62 * 1024 * 1024

TM_PROJ = 512
MLA_TILE = 512
VT_TILE = min(MLA_TILE, TM_PROJ)
TM_SWA = 512
TM_OUT = 1024
TM_MLP = 512
TF_MLP = 2048
MOD_PRE = 2
MOD_POST = 4
BIAS_HEADS_PER_STEP = 4
SWA_UNIT_HEADS = 4
MLA_SLOTS = 2
MLA_COL_SPLIT = 2
ROW_CHUNKS = 2


def _params(sem):
    return pltpu.CompilerParams(dimension_semantics=sem, vmem_limit_bytes=VMEM_LIMIT_BYTES)


def _const_spec(shape):
    nd = len(shape)
    return pl.BlockSpec(shape, lambda *_: (0,) * nd, pipeline_mode=pl.Buffered(1))


def _rms(x):
    return x * lax.rsqrt(jnp.mean(x * x, axis=-1, keepdims=True) + EPS)


def _mod_tile(c_ref, w_ref, b_ref):
    c = c_ref[...]
    ca = c * (1.0 / (1.0 + jnp.exp(-c)))
    return jnp.dot(ca.astype(BF16), w_ref[...].astype(BF16),
                   preferred_element_type=F32) + b_ref[...]


def _prologue_kernel(c_ref, w_ref, b_ref, rel_ref, bucket_ref, mod_ref, o_ref):
    mod_ref[...] = _mod_tile(c_ref, w_ref, b_ref)

    bucket = bucket_ref[...]
    k_loc = lax.broadcasted_iota(jnp.int32, bucket.shape, 0)
    q_loc = lax.broadcasted_iota(jnp.int32, bucket.shape, 1)
    dist = q_loc + BLOCK - k_loc
    in_window = (dist >= 0) & (dist < WINDOW)
    for i in range(BIAS_HEADS_PER_STEP):
        h = pl.program_id(0) * BIAS_HEADS_PER_STEP + i
        acc = jnp.zeros(bucket.shape, F32)
        for k in range(REL_BUCKETS):
            acc = jnp.where(bucket == k, rel_ref[k, h], acc)
        acc = acc * LOG2E
        o_ref[0, i] = jnp.where(in_window & (k_loc >= BLOCK), acc, -jnp.inf)
        o_ref[1, i] = jnp.where(in_window, acc, -jnp.inf)


def _prologue_call(c, w_mod, b_mod, rel_bias, bucket_t):
    hs = BIAS_HEADS_PER_STEP
    steps = SWA_HEADS // hs
    n = MOD_PRE * D_MODEL
    tn = n // steps
    assert tn % LANES == 0
    return pl.pallas_call(
        _prologue_kernel,
        out_shape=(jax.ShapeDtypeStruct((c.shape[0], n), F32),
                   jax.ShapeDtypeStruct((2, SWA_HEADS, 2 * BLOCK, BLOCK), F32)),
        grid=(steps,),
        in_specs=[
            pl.BlockSpec(c.shape, lambda s: (0, 0)),
            pl.BlockSpec((D_MODEL, tn), lambda s: (0, s)),
            pl.BlockSpec((1, tn), lambda s: (0, s)),
            pl.BlockSpec(memory_space=pltpu.SMEM),
            pl.BlockSpec((2 * BLOCK, BLOCK), lambda s: (0, 0)),
        ],
        out_specs=(pl.BlockSpec((c.shape[0], tn), lambda s: (0, s)),
                   pl.BlockSpec((2, hs, 2 * BLOCK, BLOCK), lambda s: (0, s, 0, 0))),
        compiler_params=_params(("arbitrary",)),
        name="prologue",
    )(c, w_mod, b_mod, rel_bias, bucket_t)


def _proj_kernel(x_ref, mod_ref, g_ref, win_ref, wt_ref, gq_ref, gkv_ref, wqm_ref, wqr_ref,
                 wuk_ref, wvt_ref, cos_ref, sin_ref, wo_ref, w1_ref, w2_ref,
                 qat_ref, ka_ref, vat_ref, qb_ref, kc_ref, vt_ref, wob_ref, w1b_ref, w2b_ref,
                 *, q_scale):
    wob_ref[...] = wo_ref[...].astype(BF16)
    w1b_ref[...] = w1_ref[...].astype(BF16)
    w2b_ref[...] = w2_ref[...].astype(BF16)

    x = x_ref[...]
    mod = mod_ref[0]
    sh1 = mod[0:1]
    sc1 = mod[1:2]
    h = ((_rms(x) * g_ref[...]) * (1.0 + sc1) + sh1).astype(BF16)
    proj = jnp.dot(h, win_ref[...], preferred_element_type=F32)
    proj_t = lax.dot_general(wt_ref[...], h, (((1,), (1,)), ((), ())),
                             preferred_element_type=F32)

    qat_ref[0] = (proj_t[:SWA_Q_COLS] * (SWA_HEAD_DIM ** -0.5 * LOG2E)).astype(BF16)
    vat_ref[0] = proj_t[SWA_Q_COLS:].astype(BF16)
    ka_ref[...] = proj[:, :P_CQ].astype(BF16)

    cq = (_rms(proj[:, P_CQ:P_CKV]) * gq_ref[...]).astype(BF16)
    ckv = (_rms(proj[:, P_CKV:P_KR]) * gkv_ref[...]).astype(BF16)
    cos_lo = cos_ref[...]
    sin_lo = sin_ref[...]
    cos_hi = pltpu.roll(cos_lo, MLA_ROPE_DIM, 1)
    sin_hi = pltpu.roll(sin_lo, MLA_ROPE_DIM, 1)
    kr = proj[:, P_KR:P_COLS]
    kr_sw = pltpu.roll(kr, MLA_ROPE_DIM, 1)
    krope_lo = (kr * cos_lo + kr_sw * sin_lo).astype(BF16)
    krope_hi = (kr_sw * cos_hi + kr * sin_hi).astype(BF16)

    qmain = jnp.dot(cq, wqm_ref[...], preferred_element_type=F32)
    qrot = jnp.dot(cq, wqr_ref[...], preferred_element_type=F32)
    knope = jnp.dot(ckv, wuk_ref[...], preferred_element_type=F32)
    vt = lax.dot_general(wvt_ref[...], ckv, (((1,), (1,)), ((), ())),
                         preferred_element_type=F32)

    half_heads = MLA_HEADS // 2
    pad_rows = MLA_VT_ROWS - MLA_V_DIM
    ones_rows = jnp.where(lax.broadcasted_iota(jnp.int32, (pad_rows, VT_TILE), 0) == 0,
                          1.0, 0.0).astype(BF16)
    for hd in range(MLA_HEADS):
        lo = hd * MLA_QK_PAD
        mid = lo + MLA_NOPE_DIM
        hi = lo + MLA_QK_PAD
        low = hd < half_heads
        cos_t, sin_t = (cos_lo, sin_lo) if low else (cos_hi, sin_hi)
        rot = qrot[:, (hd % half_heads) * LANES:(hd % half_heads + 1) * LANES]
        qb_ref[:, lo:mid] = (qmain[:, lo:mid] * q_scale).astype(BF16)
        qb_ref[:, mid:hi] = ((qmain[:, mid:hi] * cos_t + rot * sin_t) * q_scale).astype(BF16)
        kc_ref[:, lo:mid] = knope[:, hd * MLA_NOPE_DIM:(hd + 1) * MLA_NOPE_DIM].astype(BF16)
        kc_ref[:, mid:hi] = krope_lo if low else krope_hi
        for kt in range(vt_ref.shape[2]):
            vt_ref[0, hd, kt, :MLA_V_DIM] = vt[hd * MLA_V_DIM:(hd + 1) * MLA_V_DIM,
                                               kt * VT_TILE:(kt + 1) * VT_TILE].astype(BF16)
            vt_ref[0, hd, kt, MLA_V_DIM:] = ones_rows


def _proj_call(x2d, mod3, g_attn, w_main, w_t, gq, gkv, wq_main, wq_rot, w_uk, w_vt,
               cos_tab, sin_tab, w_out, w_ff1, w_ff2, *, batch, seq):
    tm = TM_PROJ
    tiles_per_seq = seq // tm
    t = batch * seq
    steps = t // tm
    slab = lambda w: (w.shape[0] // steps, w.shape[1])
    assert all(w.shape[0] % (steps * BF16_ROWS) == 0 for w in (w_out, w_ff1, w_ff2))
    tok = lambda i: (i, 0)
    tile_t = lambda i: (i, 0, 0)
    pos = lambda i: (i % tiles_per_seq, 0)
    out_shape = (
        jax.ShapeDtypeStruct((t // tm, SWA_Q_COLS, tm), BF16),
        jax.ShapeDtypeStruct((t, SWA_KV_COLS), BF16),
        jax.ShapeDtypeStruct((t // tm, SWA_KV_COLS, tm), BF16),
        jax.ShapeDtypeStruct((t, MLA_HEADS * MLA_QK_PAD), BF16),
        jax.ShapeDtypeStruct((t, MLA_HEADS * MLA_QK_PAD), BF16),
        jax.ShapeDtypeStruct((batch, MLA_HEADS, seq // VT_TILE, MLA_VT_ROWS, VT_TILE), BF16),
        jax.ShapeDtypeStruct(w_out.shape, BF16),
        jax.ShapeDtypeStruct(w_ff1.shape, BF16),
        jax.ShapeDtypeStruct(w_ff2.shape, BF16),
    )
    return pl.pallas_call(
        functools.partial(_proj_kernel, q_scale=MLA_Q_SCALE),
        out_shape=out_shape,
        grid=(t // tm,),
        in_specs=[
            pl.BlockSpec((tm, D_MODEL), tok),
            pl.BlockSpec((1, MOD_PRE, D_MODEL), lambda i: (i // tiles_per_seq, 0, 0)),
            _const_spec((1, D_MODEL)),
            _const_spec((D_MODEL, P_COLS)),
            _const_spec((PT_ROWS, D_MODEL)),
            _const_spec((1, MLA_Q_RANK)),
            _const_spec((1, MLA_KV_RANK)),
            _const_spec((MLA_Q_RANK, MLA_HEADS * MLA_QK_PAD)),
            _const_spec((MLA_Q_RANK, MLA_HEADS // 2 * LANES)),
            _const_spec((MLA_KV_RANK, MLA_HEADS * MLA_NOPE_DIM)),
            _const_spec((MLA_HEADS * MLA_V_DIM, MLA_KV_RANK)),
            pl.BlockSpec((tm, LANES), pos),
            pl.BlockSpec((tm, LANES), pos),
            pl.BlockSpec(slab(w_out), tok),
            pl.BlockSpec(slab(w_ff1), tok),
            pl.BlockSpec(slab(w_ff2), tok),
        ],
        out_specs=(
            pl.BlockSpec((1, SWA_Q_COLS, tm), tile_t),
            pl.BlockSpec((tm, SWA_KV_COLS), tok),
            pl.BlockSpec((1, SWA_KV_COLS, tm), tile_t),
            pl.BlockSpec((tm, MLA_HEADS * MLA_QK_PAD), tok),
            pl.BlockSpec((tm, MLA_HEADS * MLA_QK_PAD), tok),
            pl.BlockSpec((1, MLA_HEADS, tm // VT_TILE, MLA_VT_ROWS, VT_TILE),
                         lambda i: (i // tiles_per_seq, 0, i % tiles_per_seq, 0, 0)),
            pl.BlockSpec(slab(w_out), tok),
            pl.BlockSpec(slab(w_ff1), tok),
            pl.BlockSpec(slab(w_ff2), tok),
        ),
        compiler_params=_params(("parallel",)),
        name="in_proj",
    )(x2d, mod3, g_attn, w_main, w_t, gq, gkv, wq_main, wq_rot, w_uk, w_vt,
      cos_tab, sin_tab, w_out, w_ff1, w_ff2)


def _swa_kernel(sinks_ref, qt_ref, kp_ref, kc_ref, vtp_ref, vtc_ref, bias_ref,
                c_ref, wmod_ref, bmod_ref, o_ref, mod_ref,
                s_ref, p_ref, sink_ref, *, blocks_per_step):
    mod_ref[...] = _mod_tile(c_ref, wmod_ref, bmod_ref)

    first_step = pl.program_id(1) == 0
    k_all = jnp.concatenate([kp_ref[...], kc_ref[...]], axis=0)
    vt_all = jnp.concatenate([vtp_ref[0], vtc_ref[0]], axis=1)
    uh = SWA_UNIT_HEADS
    zeros_q = jnp.zeros((SWA_HEAD_DIM, uh * BLOCK), BF16)
    ones_rows = jnp.where(lax.broadcasted_iota(jnp.int32, (BF16_ROWS, 2 * BLOCK), 0) == 0,
                          1.0, 0.0).astype(BF16)
    units = [(i, g, part) for i in range(blocks_per_step) for g in range(SWA_KV_HEADS)
             for part in range(SWA_GROUP // uh)]

    def stage_scores(u):
        i, g, part = units[u]
        h_first = g * SWA_GROUP + part * uh
        k_band = k_all[i * BLOCK:(i + 2) * BLOCK]
        qt = qt_ref[0, :, i * BLOCK:(i + 1) * BLOCK]
        q_g = jnp.concatenate(
            [qt[(h_first + hh) * SWA_HEAD_DIM:(h_first + hh + 1) * SWA_HEAD_DIM]
             for hh in range(uh)], axis=1)
        rhs = jnp.concatenate([q_g, zeros_q] if g == 0 else [zeros_q, q_g], axis=0)
        s_ref[u % 2] = jnp.dot(k_band, rhs, preferred_element_type=F32)

    def stage_softmax(u):
        i, g, part = units[u]
        variant = jnp.where(first_step, 0, 1) if i == 0 else 1
        for hh in range(uh):
            hd = g * SWA_GROUP + part * uh + hh
            st = s_ref[u % 2, :, hh * BLOCK:(hh + 1) * BLOCK] + bias_ref[variant, hd]
            sink = sinks_ref[hd] * LOG2E
            m = jnp.maximum(jnp.max(st, axis=0, keepdims=True), sink)
            sink_ref[u % 2, hh] = jnp.exp2(sink - m)
            p_ref[u % 2, :, hh * BLOCK:(hh + 1) * BLOCK] = jnp.exp2(st - m).astype(BF16)

    def stage_pv(u):
        i, g, part = units[u]
        vt_band = jnp.concatenate(
            [vt_all[g * SWA_HEAD_DIM:(g + 1) * SWA_HEAD_DIM, i * BLOCK:(i + 2) * BLOCK],
             ones_rows], axis=0)
        ot_g = jnp.dot(vt_band, p_ref[u % 2], preferred_element_type=F32)

        def head_out(hh):
            cols = slice(hh * BLOCK, (hh + 1) * BLOCK)
            denom = ot_g[SWA_HEAD_DIM:SWA_HEAD_DIM + 1, cols] + sink_ref[u % 2, hh]
            return ot_g[:SWA_HEAD_DIM, cols] * (1.0 / denom)

        for pair in range(uh // 2):
            h0 = 2 * pair
            slab = jnp.concatenate([head_out(h0), head_out(h0 + 1)], axis=0)
            col = (g * SWA_GROUP + part * uh + h0) * SWA_HEAD_DIM
            o_ref[i * BLOCK:(i + 1) * BLOCK, col:col + 2 * SWA_HEAD_DIM] = (
                slab.T.astype(BF16))

    n = len(units)
    for t in range(n + 2):
        if 0 <= t - 2 < n:
            stage_pv(t - 2)
        if 0 <= t - 1 < n:
            stage_softmax(t - 1)
        if t < n:
            stage_scores(t)


def _swa_call(sinks, qat, ka, vat, bias_tab, c, w_mod, b_mod, *, batch, seq):
    step = TM_SWA
    blocks_per_step = step // BLOCK
    steps_per_seq = seq // step
    nb = seq // BLOCK
    prev_blk = lambda b, s: b * nb + jnp.maximum(s * blocks_per_step - 1, 0)
    assert step == TM_PROJ
    tile = lambda b, s: (b * steps_per_seq + s, 0, 0)
    prev_tile_last_blk = lambda b, s: (b * steps_per_seq + jnp.maximum(s - 1, 0), 0,
                                       blocks_per_step - 1)
    n_steps = batch * steps_per_seq
    mod_cols = MOD_POST * D_MODEL
    tn = mod_cols // n_steps
    assert tn % LANES == 0 and (MOD_PRE * D_MODEL) % tn == 0
    mod_tile = lambda b, s: (0, b * steps_per_seq + s)
    mod_src_tile = lambda b, s: (0, MOD_PRE * D_MODEL // tn + b * steps_per_seq + s)
    return pl.pallas_call(
        functools.partial(_swa_kernel, blocks_per_step=blocks_per_step),
        out_shape=(jax.ShapeDtypeStruct((batch * seq, SWA_Q_COLS), BF16),
                   jax.ShapeDtypeStruct((c.shape[0], mod_cols), F32)),
        grid=(batch, steps_per_seq),
        in_specs=[
            pl.BlockSpec(memory_space=pltpu.SMEM),
            pl.BlockSpec((1, SWA_Q_COLS, step), tile),
            pl.BlockSpec((BLOCK, SWA_KV_COLS), lambda b, s: (prev_blk(b, s), 0)),
            pl.BlockSpec((step, SWA_KV_COLS), lambda b, s: (b * steps_per_seq + s, 0)),
            pl.BlockSpec((1, SWA_KV_COLS, BLOCK), prev_tile_last_blk),
            pl.BlockSpec((1, SWA_KV_COLS, step), tile),
            _const_spec((2, SWA_HEADS, 2 * BLOCK, BLOCK)),
            _const_spec(c.shape),
            pl.BlockSpec((D_MODEL, tn), mod_src_tile),
            pl.BlockSpec((1, tn), mod_src_tile),
        ],
        out_specs=(
            pl.BlockSpec((step, SWA_Q_COLS), lambda b, s: (b * steps_per_seq + s, 0)),
            pl.BlockSpec((c.shape[0], tn), mod_tile),
        ),
        scratch_shapes=[
            pltpu.VMEM((2, 2 * BLOCK, SWA_UNIT_HEADS * BLOCK), F32),
            pltpu.VMEM((2, 2 * BLOCK, SWA_UNIT_HEADS * BLOCK), BF16),
            pltpu.VMEM((2, SWA_UNIT_HEADS, 1, BLOCK), F32),
        ],
        compiler_params=_params(("parallel", "arbitrary")),
        name="swa",
    )(sinks, qat, ka, ka, vat, vat, bias_tab, c, w_mod, b_mod)


def _mla_kernel(q_ref, k_ref, vt_ref, o_ref, s_ref, p_ref, al_ref, m_ref, acc_ref,
                *, tile, n_tiles):
    pairs = [(qi, j) for qi in range(n_tiles) for j in range(qi + 1)]
    width = tile // MLA_COL_SPLIT

    def geometry(u, c):
        qi, j = pairs[u]
        cols = slice(c * width, (c + 1) * width)
        n_keys = (c + 1) * width if j == qi else tile
        return qi, j, cols, n_keys

    def stage_scores(u, c):
        qi, j, cols, n_keys = geometry(u, c)
        k = k_ref[j * tile:j * tile + n_keys, :]
        q = q_ref[qi * tile + c * width:qi * tile + (c + 1) * width, :]
        s_ref[u % MLA_SLOTS, :n_keys, cols] = lax.dot_general(
            k, q, (((1,), (1,)), ((), ())), preferred_element_type=F32)

    def stage_softmax(u, c):
        qi, j, cols, n_keys = geometry(u, c)
        st = s_ref[u % MLA_SLOTS, :n_keys, cols]
        if j == qi:
            kk = lax.broadcasted_iota(jnp.int32, st.shape, 0)
            qq = lax.broadcasted_iota(jnp.int32, st.shape, 1) + c * width
            st = jnp.where(kk <= qq, st, -jnp.inf)
        cmax = jnp.max(st, axis=0, keepdims=True)
        if j == 0:
            m_new = cmax
        else:
            m_old = m_ref[qi % 2, :, cols]
            m_new = jnp.maximum(m_old, cmax)
            al_ref[u % MLA_SLOTS, :, cols] = jnp.exp2(m_old - m_new)
        m_ref[qi % 2, :, cols] = m_new
        p_ref[u % MLA_SLOTS, :n_keys, cols] = jnp.exp2(st - m_new).astype(BF16)

    def stage_pv(u, c):
        qi, j, cols, n_keys = geometry(u, c)
        vt_tile = vt_ref.shape[-1]
        pv = None
        for lo in range(0, n_keys, vt_tile):
            nk = min(vt_tile, n_keys - lo)
            part = jnp.dot(vt_ref[0, 0, (j * tile + lo) // vt_tile, :, :nk],
                           p_ref[u % MLA_SLOTS, lo:lo + nk, cols], preferred_element_type=F32)
            pv = part if pv is None else pv + part
        if j == 0:
            acc = pv
        else:
            acc = al_ref[u % MLA_SLOTS, :, cols] * acc_ref[:, cols] + pv
        if j == qi:
            rows = slice(qi * tile + c * width, qi * tile + (c + 1) * width)
            denom = acc[MLA_V_DIM:MLA_V_DIM + 1]
            o_ref[rows, :] = (acc[:MLA_V_DIM] / denom).T.astype(BF16)
        else:
            acc_ref[:, cols] = acc

    n = len(pairs)
    skew = MLA_SLOTS - 1
    for t in range(n + 2 * skew):
        for c in range(MLA_COL_SPLIT):
            if 0 <= t - 2 * skew < n:
                stage_pv(t - 2 * skew, c)
            if 0 <= t - skew < n:
                stage_softmax(t - skew, c)
            if t < n:
                stage_scores(t, c)


def _mla_call(qb, kcat, vt, *, batch, seq):
    tile = MLA_TILE
    n_tiles = seq // tile
    return pl.pallas_call(
        functools.partial(_mla_kernel, tile=tile, n_tiles=n_tiles),
        out_shape=jax.ShapeDtypeStruct((batch * seq, MLA_HEADS * MLA_V_DIM), BF16),
        grid=(batch, MLA_HEADS),
        in_specs=[
            pl.BlockSpec((seq, MLA_QK_PAD), lambda b, h: (b, h)),
            pl.BlockSpec((seq, MLA_QK_PAD), lambda b, h: (b, h)),
            pl.BlockSpec((1, 1, seq // VT_TILE, MLA_VT_ROWS, VT_TILE), lambda b, h: (b, h, 0, 0, 0)),
        ],
        out_specs=pl.BlockSpec((seq, MLA_V_DIM), lambda b, h: (b, h)),
        scratch_shapes=[
            pltpu.VMEM((MLA_SLOTS, tile, tile), F32),
            pltpu.VMEM((MLA_SLOTS, tile, tile), BF16),
            pltpu.VMEM((MLA_SLOTS, 1, tile), F32),
            pltpu.VMEM((2, 1, tile), F32),
            pltpu.VMEM((MLA_VT_ROWS, tile), F32),
        ],
        compiler_params=_params(("parallel", "parallel")),
        name="mla",
    )(qb, kcat, vt)


def _out_kernel(x_ref, oa_ref, ob_ref, mod_ref, wa_ref, wb_ref, x1_ref):
    g1 = mod_ref[0][0:1]
    y = (jnp.dot(oa_ref[...], wa_ref[...], preferred_element_type=F32)
         + jnp.dot(ob_ref[...], wb_ref[...], preferred_element_type=F32))
    x1_ref[...] = x_ref[...] + g1 * y


def _out_call(x2d, oa, ob, mod3, w_out, *, seq):
    tm = TM_OUT
    t = x2d.shape[0]
    tiles_per_seq = seq // tm
    tok = lambda i: (i, 0)
    w_half = lambda half: pl.BlockSpec((SWA_Q_COLS, D_MODEL), lambda i: (half, 0),
                                       pipeline_mode=pl.Buffered(1))
    assert w_out.shape[0] == 2 * SWA_Q_COLS
    return pl.pallas_call(
        _out_kernel,
        out_shape=jax.ShapeDtypeStruct((t, D_MODEL), F32),
        grid=(t // tm,),
        in_specs=[
            pl.BlockSpec((tm, D_MODEL), tok),
            pl.BlockSpec((tm, SWA_Q_COLS), tok),
            pl.BlockSpec((tm, MLA_HEADS * MLA_V_DIM), tok),
            pl.BlockSpec((1, MOD_POST, D_MODEL), lambda i: (i // tiles_per_seq, 0, 0)),
            w_half(0),
            w_half(1),
        ],
        out_specs=pl.BlockSpec((tm, D_MODEL), tok),
        compiler_params=_params(("parallel",)),
        name="out_proj",
    )(x2d, oa, ob, mod3, w_out, w_out)


def _mlp_kernel(x_ref, mod_ref, g_ref, gf_ref, w1_ref, w2_ref, o_ref, h_ref):
    j = pl.program_id(1)
    last = pl.num_programs(1) - 1
    mod = mod_ref[0]
    rows = o_ref.shape[0] // ROW_CHUNKS
    chunks = [slice(r * rows, (r + 1) * rows) for r in range(ROW_CHUNKS)]

    def ff_tile(h):
        u = jnp.maximum(jnp.dot(h, w1_ref[...], preferred_element_type=F32), 0.0)
        return jnp.dot((u * u).astype(BF16), w2_ref[...], preferred_element_type=F32)

    @pl.when(j == 0)
    def _():
        sh2 = mod[1:2]
        sc2 = mod[2:3]
        for sl in chunks:
            h = ((_rms(x_ref[sl, :]) * g_ref[...]) * (1.0 + sc2) + sh2).astype(BF16)
            h_ref[sl, :] = h
            o_ref[sl, :] = ff_tile(h)

    @pl.when((j > 0) & (j < last))
    def _():
        o_ref[...] += ff_tile(h_ref[...])

    @pl.when(j == last)
    def _():
        g2 = mod[3:4]
        for sl in chunks:
            x2 = x_ref[sl, :] + g2 * (o_ref[sl, :] + ff_tile(h_ref[sl, :]))
            o_ref[sl, :] = _rms(x2) * gf_ref[...]


def _mlp_call(x1, mod3, g_mlp, g_final, w1, w2, *, seq):
    tm = TM_MLP
    tf = TF_MLP
    assert D_FF // tf >= 2
    t = x1.shape[0]
    tiles_per_seq = seq // tm
    return pl.pallas_call(
        _mlp_kernel,
        out_shape=jax.ShapeDtypeStruct((t, D_MODEL), F32),
        grid=(t // tm, D_FF // tf),
        in_specs=[
            pl.BlockSpec((tm, D_MODEL), lambda i, j: (i, 0)),
            pl.BlockSpec((1, MOD_POST, D_MODEL), lambda i, j: (i // tiles_per_seq, 0, 0)),
            _const_spec((1, D_MODEL)),
            _const_spec((1, D_MODEL)),
            pl.BlockSpec((D_MODEL, tf), lambda i, j: (0, j)),
            pl.BlockSpec((tf, D_MODEL), lambda i, j: (j, 0)),
        ],
        out_specs=pl.BlockSpec((tm, D_MODEL), lambda i, j: (i, 0)),
        scratch_shapes=[pltpu.VMEM((tm, D_MODEL), BF16)],
        compiler_params=_params(("parallel", "arbitrary")),
        name="mlp",
    )(x1, mod3, g_mlp, g_final, w1, w2)


def _t5_bucket_table():
    q_loc = np.arange(BLOCK)[:, None]
    k_loc = np.arange(2 * BLOCK)[None, :]
    n = np.maximum(q_loc + BLOCK - k_loc, 0)
    max_exact = REL_BUCKETS // 2
    nf = np.maximum(n, 1).astype(np.float64)
    large = max_exact + (np.log(nf / max_exact) / math.log(REL_MAX_DIST / max_exact)
                         * (REL_BUCKETS - max_exact)).astype(np.int32)
    large = np.minimum(large, REL_BUCKETS - 1)
    return np.where(n < max_exact, n, large).astype(np.int32)


def _rope_tables(seq):
    half = MLA_ROPE_DIM // 2
    inv_freq = ROPE_THETA ** (-np.arange(half, dtype=np.float64) / half)
    ang = np.arange(seq, dtype=np.float64)[:, None] * inv_freq[None, :]
    zeros = np.zeros((seq, LANES - MLA_ROPE_DIM))
    cos_tab = np.concatenate([np.cos(ang), np.cos(ang), zeros], axis=1)
    sin_tab = np.concatenate([np.sin(ang), np.sin(ang), zeros], axis=1)
    return cos_tab.astype(np.float32), sin_tab.astype(np.float32)


def _rot_cols(w):
    half = w.shape[-1] // 2
    return jnp.concatenate([-w[..., half:], w[..., :half]], axis=-1)


def kernel(x, c, w_mod, b_mod, attn_norm_g, w_in, swa_sinks, rel_bias, mla_q_norm_g, w_uq,
           mla_kv_norm_g, w_ukv, w_out, mlp_norm_g, w_ff1, w_ff2, final_norm_g):
    batch, seq, _ = x.shape
    depth = w_mod.shape[0]
    assert depth == 1
    t = batch * seq
    x2d = x.reshape(t, D_MODEL)
    l = 0

    w_kr = w_in[l][:, OFF_MLA_KR:OFF_MLA_KR + MLA_ROPE_DIM]
    w_main = jnp.concatenate(
        [w_in[l][:, OFF_SWA_K:OFF_SWA_V], w_in[l][:, OFF_MLA_CQ:OFF_MLA_KR],
         w_kr, _rot_cols(w_kr)], axis=1).astype(BF16)
    w_t = jnp.concatenate(
        [w_in[l][:, :SWA_Q_COLS], w_in[l][:, OFF_SWA_V:OFF_MLA_CQ]], axis=1).T.astype(BF16)

    wq = w_uq[l].reshape(MLA_Q_RANK, MLA_HEADS, MLA_QK_DIM)
    wq_nope = wq[..., :MLA_NOPE_DIM]
    wq_rope = wq[..., MLA_NOPE_DIM:]
    hh = MLA_HEADS // 2
    zq = jnp.zeros((MLA_Q_RANK, hh, LANES - MLA_ROPE_DIM), F32)
    wq_main = jnp.concatenate(
        [jnp.concatenate([wq_nope[:, :hh], wq_rope[:, :hh], zq], axis=-1),
         jnp.concatenate([wq_nope[:, hh:], zq, wq_rope[:, hh:]], axis=-1)], axis=1).reshape(
        MLA_Q_RANK, MLA_HEADS * MLA_QK_PAD).astype(BF16)
    wq_rot_all = _rot_cols(wq_rope)
    wq_rot = jnp.concatenate([wq_rot_all[:, :hh], wq_rot_all[:, hh:]], axis=-1).reshape(
        MLA_Q_RANK, hh * LANES).astype(BF16)

    wkv = w_ukv[l].reshape(MLA_KV_RANK, MLA_HEADS, MLA_NOPE_DIM + MLA_V_DIM)
    w_uk = wkv[..., :MLA_NOPE_DIM].reshape(MLA_KV_RANK, MLA_HEADS * MLA_NOPE_DIM).astype(BF16)
    w_vt = wkv[..., MLA_NOPE_DIM:].reshape(MLA_KV_RANK, MLA_HEADS * MLA_V_DIM).T.astype(BF16)

    cos_tab, sin_tab = _rope_tables(seq)

    assert w_mod.shape[2] == (MOD_PRE + MOD_POST) * D_MODEL
    b_mod2d = b_mod[l].reshape(1, -1)
    mod_pre, bias_tab = _prologue_call(c, w_mod[l], b_mod2d, rel_bias, _t5_bucket_table().T)
    mod_pre = mod_pre.reshape(batch, MOD_PRE, D_MODEL)

    qat, ka, vat, qb, kcat, vt, w_out_b16, w1, w2 = _proj_call(
        x2d, mod_pre, attn_norm_g[l].reshape(1, -1), w_main, w_t,
        mla_q_norm_g[l].reshape(1, -1), mla_kv_norm_g[l].reshape(1, -1),
        wq_main, wq_rot, w_uk, w_vt, cos_tab, sin_tab, w_out[l], w_ff1[l], w_ff2[l],
        batch=batch, seq=seq)

    oa, mod_post = _swa_call(swa_sinks[l], qat, ka, vat, bias_tab, c, w_mod[l], b_mod2d,
                             batch=batch, seq=seq)
    mod_post = mod_post.reshape(batch, MOD_POST, D_MODEL)
    ob = _mla_call(qb, kcat, vt, batch=batch, seq=seq)

    x1 = _out_call(x2d, oa, ob, mod_post, w_out_b16, seq=seq)
    out = _mlp_call(x1, mod_post, mlp_norm_g[l].reshape(1, -1), final_norm_g.reshape(1, -1),
                    w1, w2, seq=seq)
    return out.reshape(batch, seq, D_MODEL)
```

```python
import functools
import math

import jax
import jax.numpy as jnp
import numpy as np
from jax import lax
from jax.experimental import pallas as pl
from jax.experimental.pallas import tpu as pltpu

F32 = jnp.float32
BF16 = jnp.bfloat16

LANES = 128
BF16_ROWS = 16

D_MODEL = 2048
BLOCK = 128
EPS = 1e-6

SWA_HEADS = 16
SWA_KV_HEADS = 2
SWA_HEAD_DIM = 64
SWA_GROUP = SWA_HEADS // SWA_KV_HEADS
WINDOW = 128
REL_BUCKETS = 32
REL_MAX_DIST = 128

MLA_HEADS = 8
MLA_Q_RANK = 384
MLA_KV_RANK = 128
MLA_NOPE_DIM = 128
MLA_ROPE_DIM = 64
MLA_V_DIM = 128
MLA_VT_ROWS = MLA_V_DIM + BF16_ROWS
MLA_QK_DIM = MLA_NOPE_DIM + MLA_ROPE_DIM
MLA_QK_PAD = 256
ROPE_THETA = 10000.0
MLA_Q_SCALE = MLA_QK_DIM ** -0.5 * math.log2(math.e)
D_FF = 4 * D_MODEL

SWA_Q_COLS = SWA_HEADS * SWA_HEAD_DIM
SWA_KV_COLS = SWA_KV_HEADS * SWA_HEAD_DIM
OFF_SWA_K = SWA_Q_COLS
OFF_SWA_V = OFF_SWA_K + SWA_KV_COLS
OFF_MLA_CQ = OFF_SWA_V + SWA_KV_COLS
OFF_MLA_CKV = OFF_MLA_CQ + MLA_Q_RANK
OFF_MLA_KR = OFF_MLA_CKV + MLA_KV_RANK
P_CQ = SWA_KV_COLS
P_CKV = P_CQ + MLA_Q_RANK
P_KR = P_CKV + MLA_KV_RANK
P_COLS = P_KR + LANES
PT_ROWS = SWA_Q_COLS + SWA_KV_COLS
LOG2E = math.log2(math.e)

VMEM_LIMIT_BYTES = 62 * 1024 * 1024

TM_PROJ = 512
MLA_TILE = 512
VT_TILE = min(MLA_TILE, TM_PROJ)
TM_SWA = 512
TM_OUT = 1024
TM_MLP = 512
TF_MLP = 2048
MOD_PRE = 2
MOD_POST = 4
BIAS_HEADS_PER_STEP = 4
SWA_UNIT_HEADS = 4
MLA_SLOTS = 2
MLA_COL_SPLIT = 2
ROW_CHUNKS = 2
PROJ_ROW_CHUNKS = 2


def _params(sem):
    return pltpu.CompilerParams(dimension_semantics=sem, vmem_limit_bytes=VMEM_LIMIT_BYTES)


def _const_spec(shape):
    nd = len(shape)
    return pl.BlockSpec(shape, lambda *_: (0,) * nd, pipeline_mode=pl.Buffered(1))


def _rms(x):
    return x * lax.rsqrt(jnp.mean(x * x, axis=-1, keepdims=True) + EPS)


def _mod_tile(c_ref, w_ref, b_ref):
    c = c_ref[...]
    ca = c * (1.0 / (1.0 + jnp.exp(-c)))
    return jnp.dot(ca.astype(BF16), w_ref[...].astype(BF16),
                   preferred_element_type=F32) + b_ref[...]


def _prologue_kernel(c_ref, w_ref, b_ref, rel_ref, bucket_ref, mod_ref, o_ref):
    mod_ref[...] = _mod_tile(c_ref, w_ref, b_ref)

    bucket = bucket_ref[...]
    k_loc = lax.broadcasted_iota(jnp.int32, bucket.shape, 0)
    q_loc = lax.broadcasted_iota(jnp.int32, bucket.shape, 1)
    dist = q_loc + BLOCK - k_loc
    in_window = (dist >= 0) & (dist < WINDOW)
    for i in range(BIAS_HEADS_PER_STEP):
        h = pl.program_id(0) * BIAS_HEADS_PER_STEP + i
        acc = jnp.zeros(bucket.shape, F32)
        for k in range(REL_BUCKETS):
            acc = jnp.where(bucket == k, rel_ref[k, h], acc)
        acc = acc * LOG2E
        o_ref[0, i] = jnp.where(in_window & (k_loc >= BLOCK), acc, -jnp.inf)
        o_ref[1, i] = jnp.where(in_window, acc, -jnp.inf)


def _prologue_call(c, w_mod, b_mod, rel_bias, bucket_t):
    hs = BIAS_HEADS_PER_STEP
    steps = SWA_HEADS // hs
    n = MOD_PRE * D_MODEL
    tn = n // steps
    assert tn % LANES == 0
    return pl.pallas_call(
        _prologue_kernel,
        out_shape=(jax.ShapeDtypeStruct((c.shape[0], n), F32),
                   jax.ShapeDtypeStruct((2, SWA_HEADS, 2 * BLOCK, BLOCK), F32)),
        grid=(steps,),
        in_specs=[
            pl.BlockSpec(c.shape, lambda s: (0, 0)),
            pl.BlockSpec((D_MODEL, tn), lambda s: (0, s)),
            pl.BlockSpec((1, tn), lambda s: (0, s)),
            pl.BlockSpec(memory_space=pltpu.SMEM),
            pl.BlockSpec((2 * BLOCK, BLOCK), lambda s: (0, 0)),
        ],
        out_specs=(pl.BlockSpec((c.shape[0], tn), lambda s: (0, s)),
                   pl.BlockSpec((2, hs, 2 * BLOCK, BLOCK), lambda s: (0, s, 0, 0))),
        compiler_params=_params(("arbitrary",)),
        name="prologue",
    )(c, w_mod, b_mod, rel_bias, bucket_t)


def _proj_kernel(x_ref, mod_ref, g_ref, win_ref, wt_ref, gq_ref, gkv_ref, wqm_ref, wqr_ref,
                 wuk_ref, wvt_ref, cos_ref, sin_ref, wo_ref, w1_ref, w2_ref,
                 qat_ref, ka_ref, vat_ref, qb_ref, kc_ref, vt_ref, wob_ref, w1b_ref, w2b_ref,
                 *, q_scale):
    wob_ref[...] = wo_ref[...].astype(BF16)
    w1b_ref[...] = w1_ref[...].astype(BF16)
    w2b_ref[...] = w2_ref[...].astype(BF16)

    mod = mod_ref[0]
    sh1 = mod[0:1]
    sc1 = mod[1:2]
    half_heads = MLA_HEADS // 2
    pad_rows = MLA_VT_ROWS - MLA_V_DIM
    ones_rows = jnp.where(lax.broadcasted_iota(jnp.int32, (pad_rows, VT_TILE), 0) == 0,
                          1.0, 0.0).astype(BF16)
    for hd in range(MLA_HEADS):
        for kt in range(vt_ref.shape[2]):
            vt_ref[0, hd, kt, MLA_V_DIM:] = ones_rows

    rows = x_ref.shape[0] // PROJ_ROW_CHUNKS
    assert VT_TILE % rows == 0
    for r in range(PROJ_ROW_CHUNKS):
        rs = slice(r * rows, (r + 1) * rows)
        h = ((_rms(x_ref[rs, :]) * g_ref[...]) * (1.0 + sc1) + sh1).astype(BF16)
        proj = jnp.dot(h, win_ref[...], preferred_element_type=F32)
        proj_t = lax.dot_general(wt_ref[...], h, (((1,), (1,)), ((), ())),
                                 preferred_element_type=F32)

        qat_ref[0, :, rs] = (proj_t[:SWA_Q_COLS] * (SWA_HEAD_DIM ** -0.5 * LOG2E)).astype(BF16)
        vat_ref[0, :, rs] = proj_t[SWA_Q_COLS:].astype(BF16)
        ka_ref[rs, :] = proj[:, :P_CQ].astype(BF16)

        cq = (_rms(proj[:, P_CQ:P_CKV]) * gq_ref[...]).astype(BF16)
        ckv = (_rms(proj[:, P_CKV:P_KR]) * gkv_ref[...]).astype(BF16)
        cos_lo = cos_ref[rs, :]
        sin_lo = sin_ref[rs, :]
        cos_hi = pltpu.roll(cos_lo, MLA_ROPE_DIM, 1)
        sin_hi = pltpu.roll(sin_lo, MLA_ROPE_DIM, 1)
        kr = proj[:, P_KR:P_COLS]
        kr_sw = pltpu.roll(kr, MLA_ROPE_DIM, 1)
        krope_lo = (kr * cos_lo + kr_sw * sin_lo).astype(BF16)
        krope_hi = (kr_sw * cos_hi + kr * sin_hi).astype(BF16)

        qmain = jnp.dot(cq, wqm_ref[...], preferred_element_type=F32)
        qrot = jnp.dot(cq, wqr_ref[...], preferred_element_type=F32)
        knope = jnp.dot(ckv, wuk_ref[...], preferred_element_type=F32)
        vt = lax.dot_general(wvt_ref[...], ckv, (((1,), (1,)), ((), ())),
                             preferred_element_type=F32)

        vt_tile = (r * rows) // VT_TILE
        vt_cols = slice((r * rows) % VT_TILE, (r * rows) % VT_TILE + rows)
        for hd in range(MLA_HEADS):
            lo = hd * MLA_QK_PAD
            mid = lo + MLA_NOPE_DIM
            hi = lo + MLA_QK_PAD
            low = hd < half_heads
            cos_t, sin_t = (cos_lo, sin_lo) if low else (cos_hi, sin_hi)
            rot = qrot[:, (hd % half_heads) * LANES:(hd % half_heads + 1) * LANES]
            qb_ref[rs, lo:mid] = (qmain[:, lo:mid] * q_scale).astype(BF16)
            qb_ref[rs, mid:hi] = ((qmain[:, mid:hi] * cos_t + rot * sin_t) * q_scale).astype(BF16)
            kc_ref[rs, lo:mid] = knope[:, hd * MLA_NOPE_DIM:(hd + 1) * MLA_NOPE_DIM].astype(BF16)
            kc_ref[rs, mid:hi] = krope_lo if low else krope_hi
            vt_ref[0, hd, vt_tile, :MLA_V_DIM, vt_cols] = (
                vt[hd * MLA_V_DIM:(hd + 1) * MLA_V_DIM].astype(BF16))


def _proj_call(x2d, mod3, g_attn, w_main, w_t, gq, gkv, wq_main, wq_rot, w_uk, w_vt,
               cos_tab, sin_tab, w_out, w_ff1, w_ff2, *, batch, seq):
    tm = TM_PROJ
    tiles_per_seq = seq // tm
    t = batch * seq
    steps = t // tm
    slab = lambda w: (w.shape[0] // steps, w.shape[1])
    assert all(w.shape[0] % (steps * BF16_ROWS) == 0 for w in (w_out, w_ff1, w_ff2))
    tok = lambda i: (i, 0)
    tile_t = lambda i: (i, 0, 0)
    pos = lambda i: (i % tiles_per_seq, 0)
    out_shape = (
        jax.ShapeDtypeStruct((t // tm, SWA_Q_COLS, tm), BF16),
        jax.ShapeDtypeStruct((t, SWA_KV_COLS), BF16),
        jax.ShapeDtypeStruct((t // tm, SWA_KV_COLS, tm), BF16),
        jax.ShapeDtypeStruct((t, MLA_HEADS * MLA_QK_PAD), BF16),
        jax.ShapeDtypeStruct((t, MLA_HEADS * MLA_QK_PAD), BF16),
        jax.ShapeDtypeStruct((batch, MLA_HEADS, seq // VT_TILE, MLA_VT_ROWS, VT_TILE), BF16),
        jax.ShapeDtypeStruct(w_out.shape, BF16),
        jax.ShapeDtypeStruct(w_ff1.shape, BF16),
        jax.ShapeDtypeStruct(w_ff2.shape, BF16),
    )
    return pl.pallas_call(
        functools.partial(_proj_kernel, q_scale=MLA_Q_SCALE),
        out_shape=out_shape,
        grid=(t // tm,),
        in_specs=[
            pl.BlockSpec((tm, D_MODEL), tok),
            pl.BlockSpec((1, MOD_PRE, D_MODEL), lambda i: (i // tiles_per_seq, 0, 0)),
            _const_spec((1, D_MODEL)),
            _const_spec((D_MODEL, P_COLS)),
            _const_spec((PT_ROWS, D_MODEL)),
            _const_spec((1, MLA_Q_RANK)),
            _const_spec((1, MLA_KV_RANK)),
            _const_spec((MLA_Q_RANK, MLA_HEADS * MLA_QK_PAD)),
            _const_spec((MLA_Q_RANK, MLA_HEADS // 2 * LANES)),
            _const_spec((MLA_KV_RANK, MLA_HEADS * MLA_NOPE_DIM)),
            _const_spec((MLA_HEADS * MLA_V_DIM, MLA_KV_RANK)),
            pl.BlockSpec((tm, LANES), pos),
            pl.BlockSpec((tm, LANES), pos),
            pl.BlockSpec(slab(w_out), tok),
            pl.BlockSpec(slab(w_ff1), tok),
            pl.BlockSpec(slab(w_ff2), tok),
        ],
        out_specs=(
            pl.BlockSpec((1, SWA_Q_COLS, tm), tile_t),
            pl.BlockSpec((tm, SWA_KV_COLS), tok),
            pl.BlockSpec((1, SWA_KV_COLS, tm), tile_t),
            pl.BlockSpec((tm, MLA_HEADS * MLA_QK_PAD), tok),
            pl.BlockSpec((tm, MLA_HEADS * MLA_QK_PAD), tok),
            pl.BlockSpec((1, MLA_HEADS, tm // VT_TILE, MLA_VT_ROWS, VT_TILE),
                         lambda i: (i // tiles_per_seq, 0, i % tiles_per_seq, 0, 0)),
            pl.BlockSpec(slab(w_out), tok),
            pl.BlockSpec(slab(w_ff1), tok),
            pl.BlockSpec(slab(w_ff2), tok),
        ),
        compiler_params=_params(("parallel",)),
        name="in_proj",
    )(x2d, mod3, g_attn, w_main, w_t, gq, gkv, wq_main, wq_rot, w_uk, w_vt,
      cos_tab, sin_tab, w_out, w_ff1, w_ff2)


def _swa_kernel(sinks_ref, qt_ref, kp_ref, kc_ref, vtp_ref, vtc_ref, bias_ref,
                c_ref, wmod_ref, bmod_ref, o_ref, mod_ref,
                s_ref, p_ref, sink_ref, *, blocks_per_step):
    mod_ref[...] = _mod_tile(c_ref, wmod_ref, bmod_ref)

    first_step = pl.program_id(1) == 0
    k_all = jnp.concatenate([kp_ref[...], kc_ref[...]], axis=0)
    vt_all = jnp.concatenate([vtp_ref[0], vtc_ref[0]], axis=1)
    uh = SWA_UNIT_HEADS
    zeros_q = jnp.zeros((SWA_HEAD_DIM, uh * BLOCK), BF16)
    ones_rows = jnp.where(lax.broadcasted_iota(jnp.int32, (BF16_ROWS, 2 * BLOCK), 0) == 0,
                          1.0, 0.0).astype(BF16)
    units = [(i, g, part) for i in range(blocks_per_step) for g in range(SWA_KV_HEADS)
             for part in range(SWA_GROUP // uh)]

    def stage_scores(u):
        i, g, part = units[u]
        h_first = g * SWA_GROUP + part * uh
        k_band = k_all[i * BLOCK:(i + 2) * BLOCK]
        qt = qt_ref[0, :, i * BLOCK:(i + 1) * BLOCK]
        q_g = jnp.concatenate(
            [qt[(h_first + hh) * SWA_HEAD_DIM:(h_first + hh + 1) * SWA_HEAD_DIM]
             for hh in range(uh)], axis=1)
        rhs = jnp.concatenate([q_g, zeros_q] if g == 0 else [zeros_q, q_g], axis=0)
        s_ref[u % 2] = jnp.dot(k_band, rhs, preferred_element_type=F32)

    def stage_softmax(u):
        i, g, part = units[u]
        variant = jnp.where(first_step, 0, 1) if i == 0 else 1
        for hh in range(uh):
            hd = g * SWA_GROUP + part * uh + hh
            st = s_ref[u % 2, :, hh * BLOCK:(hh + 1) * BLOCK] + bias_ref[variant, hd]
            sink = sinks_ref[hd] * LOG2E
            m = jnp.maximum(jnp.max(st, axis=0, keepdims=True), sink)
            sink_ref[u % 2, hh] = jnp.exp2(sink - m)
            p_ref[u % 2, :, hh * BLOCK:(hh + 1) * BLOCK] = jnp.exp2(st - m).astype(BF16)

    def stage_pv(u):
        i, g, part = units[u]
        vt_band = jnp.concatenate(
            [vt_all[g * SWA_HEAD_DIM:(g + 1) * SWA_HEAD_DIM, i * BLOCK:(i + 2) * BLOCK],
             ones_rows], axis=0)
        ot_g = jnp.dot(vt_band, p_ref[u % 2], preferred_element_type=F32)

        def head_out(hh):
            cols = slice(hh * BLOCK, (hh + 1) * BLOCK)
            denom = ot_g[SWA_HEAD_DIM:SWA_HEAD_DIM + 1, cols] + sink_ref[u % 2, hh]
            return ot_g[:SWA_HEAD_DIM, cols] * (1.0 / denom)

        for pair in range(uh // 2):
            h0 = 2 * pair
            slab = jnp.concatenate([head_out(h0), head_out(h0 + 1)], axis=0)
            col = (g * SWA_GROUP + part * uh + h0) * SWA_HEAD_DIM
            o_ref[i * BLOCK:(i + 1) * BLOCK, col:col + 2 * SWA_HEAD_DIM] = (
                slab.T.astype(BF16))

    n = len(units)
    for t in range(n + 2):
        if 0 <= t - 2 < n:
            stage_pv(t - 2)
        if 0 <= t - 1 < n:
            stage_softmax(t - 1)
        if t < n:
            stage_scores(t)


def _swa_call(sinks, qat, ka, vat, bias_tab, c, w_mod, b_mod, *, batch, seq):
    step = TM_SWA
    blocks_per_step = step // BLOCK
    steps_per_seq = seq // step
    nb = seq // BLOCK
    prev_blk = lambda b, s: b * nb + jnp.maximum(s * blocks_per_step - 1, 0)
    assert step == TM_PROJ
    tile = lambda b, s: (b * steps_per_seq + s, 0, 0)
    prev_tile_last_blk = lambda b, s: (b * steps_per_seq + jnp.maximum(s - 1, 0), 0,
                                       blocks_per_step - 1)
    n_steps = batch * steps_per_seq
    mod_cols = MOD_POST * D_MODEL
    tn = mod_cols // n_steps
    assert tn % LANES == 0 and (MOD_PRE * D_MODEL) % tn == 0
    mod_tile = lambda b, s: (0, b * steps_per_seq + s)
    mod_src_tile = lambda b, s: (0, MOD_PRE * D_MODEL // tn + b * steps_per_seq + s)
    return pl.pallas_call(
        functools.partial(_swa_kernel, blocks_per_step=blocks_per_step),
        out_shape=(jax.ShapeDtypeStruct((batch * seq, SWA_Q_COLS), BF16),
                   jax.ShapeDtypeStruct((c.shape[0], mod_cols), F32)),
        grid=(batch, steps_per_seq),
        in_specs=[
            pl.BlockSpec(memory_space=pltpu.SMEM),
            pl.BlockSpec((1, SWA_Q_COLS, step), tile),
            pl.BlockSpec((BLOCK, SWA_KV_COLS), lambda b, s: (prev_blk(b, s), 0)),
            pl.BlockSpec((step, SWA_KV_COLS), lambda b, s: (b * steps_per_seq + s, 0)),
            pl.BlockSpec((1, SWA_KV_COLS, BLOCK), prev_tile_last_blk),
            pl.BlockSpec((1, SWA_KV_COLS, step), tile),
            _const_spec((2, SWA_HEADS, 2 * BLOCK, BLOCK)),
            _const_spec(c.shape),
            pl.BlockSpec((D_MODEL, tn), mod_src_tile),
            pl.BlockSpec((1, tn), mod_src_tile),
        ],
        out_specs=(
            pl.BlockSpec((step, SWA_Q_COLS), lambda b, s: (b * steps_per_seq + s, 0)),
            pl.BlockSpec((c.shape[0], tn), mod_tile),
        ),
        scratch_shapes=[
            pltpu.VMEM((2, 2 * BLOCK, SWA_UNIT_HEADS * BLOCK), F32),
            pltpu.VMEM((2, 2 * BLOCK, SWA_UNIT_HEADS * BLOCK), BF16),
            pltpu.VMEM((2, SWA_UNIT_HEADS, 1, BLOCK), F32),
        ],
        compiler_params=_params(("parallel", "arbitrary")),
        name="swa",
    )(sinks, qat, ka, ka, vat, vat, bias_tab, c, w_mod, b_mod)


def _mla_kernel(q_ref, k_ref, vt_ref, o_ref, s_ref, p_ref, al_ref, m_ref, acc_ref,
                *, tile, n_tiles):
    pairs = [(qi, j) for qi in range(n_tiles) for j in range(qi + 1)]
    width = tile // MLA_COL_SPLIT

    def geometry(u, c):
        qi, j = pairs[u]
        cols = slice(c * width, (c + 1) * width)
        n_keys = (c + 1) * width if j == qi else tile
        return qi, j, cols, n_keys

    def stage_scores(u, c):
        qi, j, cols, n_keys = geometry(u, c)
        k = k_ref[j * tile:j * tile + n_keys, :]
        q = q_ref[qi * tile + c * width:qi * tile + (c + 1) * width, :]
        s_ref[u % MLA_SLOTS, :n_keys, cols] = lax.dot_general(
            k, q, (((1,), (1,)), ((), ())), preferred_element_type=F32)

    def stage_softmax(u, c):
        qi, j, cols, n_keys = geometry(u, c)
        st = s_ref[u % MLA_SLOTS, :n_keys, cols]
        if j == qi:
            kk = lax.broadcasted_iota(jnp.int32, st.shape, 0)
            qq = lax.broadcasted_iota(jnp.int32, st.shape, 1) + c * width
            st = jnp.where(kk <= qq, st, -jnp.inf)
        cmax = jnp.max(st, axis=0, keepdims=True)
        if j == 0:
            m_new = cmax
        else:
            m_old = m_ref[qi % 2, :, cols]
            m_new = jnp.maximum(m_old, cmax)
            al_ref[u % MLA_SLOTS, :, cols] = jnp.exp2(m_old - m_new)
        m_ref[qi % 2, :, cols] = m_new
        p_ref[u % MLA_SLOTS, :n_keys, cols] = jnp.exp2(st - m_new).astype(BF16)

    def stage_pv(u, c):
        qi, j, cols, n_keys = geometry(u, c)
        vt_tile = vt_ref.shape[-1]
        pv = None
        for lo in range(0, n_keys, vt_tile):
            nk = min(vt_tile, n_keys - lo)
            part = jnp.dot(vt_ref[0, 0, (j * tile + lo) // vt_tile, :, :nk],
                           p_ref[u % MLA_SLOTS, lo:lo + nk, cols], preferred_element_type=F32)
            pv = part if pv is None else pv + part
        if j == 0:
            acc = pv
        else:
            acc = al_ref[u % MLA_SLOTS, :, cols] * acc_ref[:, cols] + pv
        if j == qi:
            rows = slice(qi * tile + c * width, qi * tile + (c + 1) * width)
            denom = acc[MLA_V_DIM:MLA_V_DIM + 1]
            o_ref[rows, :] = (acc[:MLA_V_DIM] / denom).T.astype(BF16)
        else:
            acc_ref[:, cols] = acc

    n = len(pairs)
    skew = MLA_SLOTS - 1
    for t in range(n + 2 * skew):
        for c in range(MLA_COL_SPLIT):
            if 0 <= t - 2 * skew < n:
                stage_pv(t - 2 * skew, c)
            if 0 <= t - skew < n:
                stage_softmax(t - skew, c)
            if t < n:
                stage_scores(t, c)


def _mla_call(qb, kcat, vt, *, batch, seq):
    tile = MLA_TILE
    n_tiles = seq // tile
    return pl.pallas_call(
        functools.partial(_mla_kernel, tile=tile, n_tiles=n_tiles),
        out_shape=jax.ShapeDtypeStruct((batch * seq, MLA_HEADS * MLA_V_DIM), BF16),
        grid=(batch, MLA_HEADS),
        in_specs=[
            pl.BlockSpec((seq, MLA_QK_PAD), lambda b, h: (b, h)),
            pl.BlockSpec((seq, MLA_QK_PAD), lambda b, h: (b, h)),
            pl.BlockSpec((1, 1, seq // VT_TILE, MLA_VT_ROWS, VT_TILE), lambda b, h: (b, h, 0, 0, 0)),
        ],
        out_specs=pl.BlockSpec((seq, MLA_V_DIM), lambda b, h: (b, h)),
        scratch_shapes=[
            pltpu.VMEM((MLA_SLOTS, tile, tile), F32),
            pltpu.VMEM((MLA_SLOTS, tile, tile), BF16),
            pltpu.VMEM((MLA_SLOTS, 1, tile), F32),
            pltpu.VMEM((2, 1, tile), F32),
            pltpu.VMEM((MLA_VT_ROWS, tile), F32),
        ],
        compiler_params=_params(("parallel", "parallel")),
        name="mla",
    )(qb, kcat, vt)


def _out_kernel(x_ref, oa_ref, ob_ref, mod_ref, wa_ref, wb_ref, x1_ref):
    g1 = mod_ref[0][0:1]
    y = (jnp.dot(oa_ref[...], wa_ref[...], preferred_element_type=F32)
         + jnp.dot(ob_ref[...], wb_ref[...], preferred_element_type=F32))
    x1_ref[...] = x_ref[...] + g1 * y


def _out_call(x2d, oa, ob, mod3, w_out, *, seq):
    tm = TM_OUT
    t = x2d.shape[0]
    tiles_per_seq = seq // tm
    tok = lambda i: (i, 0)
    w_half = lambda half: pl.BlockSpec((SWA_Q_COLS, D_MODEL), lambda i: (half, 0),
                                       pipeline_mode=pl.Buffered(1))
    assert w_out.shape[0] == 2 * SWA_Q_COLS
    return pl.pallas_call(
        _out_kernel,
        out_shape=jax.ShapeDtypeStruct((t, D_MODEL), F32),
        grid=(t // tm,),
        in_specs=[
            pl.BlockSpec((tm, D_MODEL), tok),
            pl.BlockSpec((tm, SWA_Q_COLS), tok),
            pl.BlockSpec((tm, MLA_HEADS * MLA_V_DIM), tok),
            pl.BlockSpec((1, MOD_POST, D_MODEL), lambda i: (i // tiles_per_seq, 0, 0)),
            w_half(0),
            w_half(1),
        ],
        out_specs=pl.BlockSpec((tm, D_MODEL), tok),
        compiler_params=_params(("parallel",)),
        name="out_proj",
    )(x2d, oa, ob, mod3, w_out, w_out)


def _mlp_kernel(x_ref, mod_ref, g_ref, gf_ref, w1_ref, w2_ref, o_ref, h_ref):
    j = pl.program_id(1)
    last = pl.num_programs(1) - 1
    mod = mod_ref[0]
    rows = o_ref.shape[0] // ROW_CHUNKS
    chunks = [slice(r * rows, (r + 1) * rows) for r in range(ROW_CHUNKS)]

    def ff_tile(h):
        u = jnp.maximum(jnp.dot(h, w1_ref[...], preferred_element_type=F32), 0.0)
        return jnp.dot((u * u).astype(BF16), w2_ref[...], preferred_element_type=F32)

    @pl.when(j == 0)
    def _():
        sh2 = mod[1:2]
        sc2 = mod[2:3]
        for sl in chunks:
            h = ((_rms(x_ref[sl, :]) * g_ref[...]) * (1.0 + sc2) + sh2).astype(BF16)
            h_ref[sl, :] = h
            o_ref[sl, :] = ff_tile(h)

    @pl.when((j > 0) & (j < last))
    def _():
        o_ref[...] += ff_tile(h_ref[...])

    @pl.when(j == last)
    def _():
        g2 = mod[3:4]
        for sl in chunks:
            x2 = x_ref[sl, :] + g2 * (o_ref[sl, :] + ff_tile(h_ref[sl, :]))
            o_ref[sl, :] = _rms(x2) * gf_ref[...]


def _mlp_call(x1, mod3, g_mlp, g_final, w1, w2, *, seq):
    tm = TM_MLP
    tf = TF_MLP
    assert D_FF // tf >= 2
    t = x1.shape[0]
    tiles_per_seq = seq // tm
    return pl.pallas_call(
        _mlp_kernel,
        out_shape=jax.ShapeDtypeStruct((t, D_MODEL), F32),
        grid=(t // tm, D_FF // tf),
        in_specs=[
            pl.BlockSpec((tm, D_MODEL), lambda i, j: (i, 0)),
            pl.BlockSpec((1, MOD_POST, D_MODEL), lambda i, j: (i // tiles_per_seq, 0, 0)),
            _const_spec((1, D_MODEL)),
            _const_spec((1, D_MODEL)),
            pl.BlockSpec((D_MODEL, tf), lambda i, j: (0, j)),
            pl.BlockSpec((tf, D_MODEL), lambda i, j: (j, 0)),
        ],
        out_specs=pl.BlockSpec((tm, D_MODEL), lambda i, j: (i, 0)),
        scratch_shapes=[pltpu.VMEM((tm, D_MODEL), BF16)],
        compiler_params=_params(("parallel", "arbitrary")),
        name="mlp",
    )(x1, mod3, g_mlp, g_final, w1, w2)


def _t5_bucket_table():
    q_loc = np.arange(BLOCK)[:, None]
    k_loc = np.arange(2 * BLOCK)[None, :]
    n = np.maximum(q_loc + BLOCK - k_loc, 0)
    max_exact = REL_BUCKETS // 2
    nf = np.maximum(n, 1).astype(np.float64)
    large = max_exact + (np.log(nf / max_exact) / math.log(REL_MAX_DIST / max_exact)
                         * (REL_BUCKETS - max_exact)).astype(np.int32)
    large = np.minimum(large, REL_BUCKETS - 1)
    return np.where(n < max_exact, n, large).astype(np.int32)


def _rope_tables(seq):
    half = MLA_ROPE_DIM // 2
    inv_freq = ROPE_THETA ** (-np.arange(half, dtype=np.float64) / half)
    ang = np.arange(seq, dtype=np.float64)[:, None] * inv_freq[None, :]
    zeros = np.zeros((seq, LANES - MLA_ROPE_DIM))
    cos_tab = np.concatenate([np.cos(ang), np.cos(ang), zeros], axis=1)
    sin_tab = np.concatenate([np.sin(ang), np.sin(ang), zeros], axis=1)
    return cos_tab.astype(np.float32), sin_tab.astype(np.float32)


def _rot_cols(w):
    half = w.shape[-1] // 2
    return jnp.concatenate([-w[..., half:], w[..., :half]], axis=-1)


def kernel(x, c, w_mod, b_mod, attn_norm_g, w_in, swa_sinks, rel_bias, mla_q_norm_g, w_uq,
           mla_kv_norm_g, w_ukv, w_out, mlp_norm_g, w_ff1, w_ff2, final_norm_g):
    batch, seq, _ = x.shape
    depth = w_mod.shape[0]
    assert depth == 1
    t = batch * seq
    x2d = x.reshape(t, D_MODEL)
    l = 0

    w_kr = w_in[l][:, OFF_MLA_KR:OFF_MLA_KR + MLA_ROPE_DIM]
    w_main = jnp.concatenate(
        [w_in[l][:, OFF_SWA_K:OFF_SWA_V], w_in[l][:, OFF_MLA_CQ:OFF_MLA_KR],
         w_kr, _rot_cols(w_kr)], axis=1).astype(BF16)
    w_t = jnp.concatenate(
        [w_in[l][:, :SWA_Q_COLS], w_in[l][:, OFF_SWA_V:OFF_MLA_CQ]], axis=1).T.astype(BF16)

    wq = w_uq[l].reshape(MLA_Q_RANK, MLA_HEADS, MLA_QK_DIM)
    wq_nope = wq[..., :MLA_NOPE_DIM]
    wq_rope = wq[..., MLA_NOPE_DIM:]
    hh = MLA_HEADS // 2
    zq = jnp.zeros((MLA_Q_RANK, hh, LANES - MLA_ROPE_DIM), F32)
    wq_main = jnp.concatenate(
        [jnp.concatenate([wq_nope[:, :hh], wq_rope[:, :hh], zq], axis=-1),
         jnp.concatenate([wq_nope[:, hh:], zq, wq_rope[:, hh:]], axis=-1)], axis=1).reshape(
        MLA_Q_RANK, MLA_HEADS * MLA_QK_PAD).astype(BF16)
    wq_rot_all = _rot_cols(wq_rope)
    wq_rot = jnp.concatenate([wq_rot_all[:, :hh], wq_rot_all[:, hh:]], axis=-1).reshape(
        MLA_Q_RANK, hh * LANES).astype(BF16)

    wkv = w_ukv[l].reshape(MLA_KV_RANK, MLA_HEADS, MLA_NOPE_DIM + MLA_V_DIM)
    w_uk = wkv[..., :MLA_NOPE_DIM].reshape(MLA_KV_RANK, MLA_HEADS * MLA_NOPE_DIM).astype(BF16)
    w_vt = wkv[..., MLA_NOPE_DIM:].reshape(MLA_KV_RANK, MLA_HEADS * MLA_V_DIM).T.astype(BF16)

    cos_tab, sin_tab = _rope_tables(seq)

    assert w_mod.shape[2] == (MOD_PRE + MOD_POST) * D_MODEL
    b_mod2d = b_mod[l].reshape(1, -1)
    mod_pre, bias_tab = _prologue_call(c, w_mod[l], b_mod2d, rel_bias, _t5_bucket_table().T)
    mod_pre = mod_pre.reshape(batch, MOD_PRE, D_MODEL)

    qat, ka, vat, qb, kcat, vt, w_out_b16, w1, w2 = _proj_call(
        x2d, mod_pre, attn_norm_g[l].reshape(1, -1), w_main, w_t,
        mla_q_norm_g[l].reshape(1, -1), mla_kv_norm_g[l].reshape(1, -1),
        wq_main, wq_rot, w_uk, w_vt, cos_tab, sin_tab, w_out[l], w_ff1[l], w_ff2[l],
        batch=batch, seq=seq)

    oa, mod_post = _swa_call(swa_sinks[l], qat, ka, vat, bias_tab, c, w_mod[l], b_mod2d,
                             batch=batch, seq=seq)
    mod_post = mod_post.reshape(batch, MOD_POST, D_MODEL)
    ob = _mla_call(qb, kcat, vt, batch=batch, seq=seq)

    x1 = _out_call(x2d, oa, ob, mod_post, w_out_b16, seq=seq)
    out = _mlp_call(x1, mod_post, mlp_norm_g[l].reshape(1, -1), final_norm_g.reshape(1, -1),
                    w1, w2, seq=seq)
    return out.reshape(batch, seq, D_MODEL)
```

```python
import functools
import math

import jax
import jax.numpy as jnp
import numpy as np
from jax import lax
from jax.experimental import pallas as pl
from jax.experimental.pallas import tpu as pltpu

F32 = jnp.float32
BF16 = jnp.bfloat16

LANES = 128
BF16_ROWS = 16

D_MODEL = 2048
BLOCK = 128
EPS = 1e-6

SWA_HEADS = 16
SWA_KV_HEADS = 2
SWA_HEAD_DIM = 64
SWA_GROUP = SWA_HEADS // SWA_KV_HEADS
WINDOW = 128
REL_BUCKETS = 32
REL_MAX_DIST = 128

MLA_HEADS = 8
MLA_Q_RANK = 384
MLA_KV_RANK = 128
MLA_NOPE_DIM = 128
MLA_ROPE_DIM = 64
MLA_V_DIM = 128
MLA_VT_ROWS = MLA_V_DIM + BF16_ROWS
MLA_QK_DIM = MLA_NOPE_DIM + MLA_ROPE_DIM
MLA_QK_PAD = 256
ROPE_THETA = 10000.0
MLA_Q_SCALE = MLA_QK_DIM ** -0.5 * math.log2(math.e)
D_FF = 4 * D_MODEL

SWA_Q_COLS = SWA_HEADS * SWA_HEAD_DIM
SWA_KV_COLS = SWA_KV_HEADS * SWA_HEAD_DIM
OFF_SWA_K = SWA_Q_COLS
OFF_SWA_V = OFF_SWA_K + SWA_KV_COLS
OFF_MLA_CQ = OFF_SWA_V + SWA_KV_COLS
OFF_MLA_CKV = OFF_MLA_CQ + MLA_Q_RANK
OFF_MLA_KR = OFF_MLA_CKV + MLA_KV_RANK
P_CQ = SWA_KV_COLS
P_CKV = P_CQ + MLA_Q_RANK
P_KR = P_CKV + MLA_KV_RANK
P_COLS = P_KR + LANES
PT_ROWS = SWA_Q_COLS + SWA_KV_COLS
LOG2E = math.log2(math.e)

VMEM_LIMIT_BYTES = 62 * 1024 * 1024

TM_PROJ = 512
MLA_TILE = 512
VT_TILE = min(MLA_TILE, TM_PROJ)
TM_SWA = 512
TM_OUT = 1024
TM_MLP = 512
TF_MLP = 2048
MOD_PRE = 2
MOD_POST = 4
BIAS_HEADS_PER_STEP = 4
SWA_UNIT_HEADS = 4
MLA_SLOTS = 2
MLA_COL_SPLIT = 2
ROW_CHUNKS = 2
PROJ_ROW_CHUNKS = 2


def _params(sem):
    return pltpu.CompilerParams(dimension_semantics=sem, vmem_limit_bytes=VMEM_LIMIT_BYTES)


def _const_spec(shape):
    nd = len(shape)
    return pl.BlockSpec(shape, lambda *_: (0,) * nd, pipeline_mode=pl.Buffered(1))


def _rms(x):
    return x * lax.rsqrt(jnp.mean(x * x, axis=-1, keepdims=True) + EPS)


def _mod_tile(c_ref, w_ref, b_ref):
    c = c_ref[...]
    ca = c * (1.0 / (1.0 + jnp.exp(-c)))
    return jnp.dot(ca.astype(BF16), w_ref[...].astype(BF16),
                   preferred_element_type=F32) + b_ref[...]


def _prologue_kernel(c_ref, w_ref, b_ref, rel_ref, bucket_ref, mod_ref, o_ref):
    mod_ref[...] = _mod_tile(c_ref, w_ref, b_ref)

    bucket = bucket_ref[...]
    k_loc = lax.broadcasted_iota(jnp.int32, bucket.shape, 0)
    q_loc = lax.broadcasted_iota(jnp.int32, bucket.shape, 1)
    dist = q_loc + BLOCK - k_loc
    in_window = (dist >= 0) & (dist < WINDOW)
    for i in range(BIAS_HEADS_PER_STEP):
        h = pl.program_id(0) * BIAS_HEADS_PER_STEP + i
        acc = jnp.zeros(bucket.shape, F32)
        for k in range(REL_BUCKETS):
            acc = jnp.where(bucket == k, rel_ref[k, h], acc)
        acc = acc * LOG2E
        o_ref[0, i] = jnp.where(in_window & (k_loc >= BLOCK), acc, -jnp.inf)
        o_ref[1, i] = jnp.where(in_window, acc, -jnp.inf)


def _prologue_call(c, w_mod, b_mod, rel_bias, bucket_t):
    hs = BIAS_HEADS_PER_STEP
    steps = SWA_HEADS // hs
    n = MOD_PRE * D_MODEL
    tn = n // steps
    assert tn % LANES == 0
    return pl.pallas_call(
        _prologue_kernel,
        out_shape=(jax.ShapeDtypeStruct((c.shape[0], n), F32),
                   jax.ShapeDtypeStruct((2, SWA_HEADS, 2 * BLOCK, BLOCK), F32)),
        grid=(steps,),
        in_specs=[
            pl.BlockSpec(c.shape, lambda s: (0, 0)),
            pl.BlockSpec((D_MODEL, tn), lambda s: (0, s)),
            pl.BlockSpec((1, tn), lambda s: (0, s)),
            pl.BlockSpec(memory_space=pltpu.SMEM),
            pl.BlockSpec((2 * BLOCK, BLOCK), lambda s: (0, 0)),
        ],
        out_specs=(pl.BlockSpec((c.shape[0], tn), lambda s: (0, s)),
                   pl.BlockSpec((2, hs, 2 * BLOCK, BLOCK), lambda s: (0, s, 0, 0))),
        compiler_params=_params(("arbitrary",)),
        name="prologue",
    )(c, w_mod, b_mod, rel_bias, bucket_t)


def _proj_kernel(x_ref, mod_ref, g_ref, win_ref, wt_ref, gq_ref, gkv_ref, wqm_ref, wqr_ref,
                 wuk_ref, wvt_ref, cos_ref, sin_ref, wo_ref, w1_ref, w2_ref,
                 qat_ref, ka_ref, vat_ref, qb_ref, kc_ref, vt_ref, wob_ref, w1b_ref, w2b_ref,
                 h_ref, cq_ref, ckv_ref, kr_ref, *, q_scale):
    mod = mod_ref[0]
    sh1 = mod[0:1]
    sc1 = mod[1:2]
    half_heads = MLA_HEADS // 2
    pad_rows = MLA_VT_ROWS - MLA_V_DIM
    ones_rows = jnp.where(lax.broadcasted_iota(jnp.int32, (pad_rows, VT_TILE), 0) == 0,
                          1.0, 0.0).astype(BF16)
    for hd in range(MLA_HEADS):
        for kt in range(vt_ref.shape[2]):
            vt_ref[0, hd, kt, MLA_V_DIM:] = ones_rows

    rows = x_ref.shape[0] // PROJ_ROW_CHUNKS
    assert VT_TILE % rows == 0
    chunk = lambda r: slice(r * rows, (r + 1) * rows)

    def stage_norm(r):
        rs = chunk(r)
        h_ref[rs, :] = ((_rms(x_ref[rs, :]) * g_ref[...]) * (1.0 + sc1) + sh1).astype(BF16)

    def stage_proj(r):
        rs = chunk(r)
        h = h_ref[rs, :]
        proj = jnp.dot(h, win_ref[...], preferred_element_type=F32)
        proj_t = lax.dot_general(wt_ref[...], h, (((1,), (1,)), ((), ())),
                                 preferred_element_type=F32)

        qat_ref[0, :, rs] = (proj_t[:SWA_Q_COLS] * (SWA_HEAD_DIM ** -0.5 * LOG2E)).astype(BF16)
        vat_ref[0, :, rs] = proj_t[SWA_Q_COLS:].astype(BF16)
        ka_ref[rs, :] = proj[:, :P_CQ].astype(BF16)

        cq_ref[rs, :] = (_rms(proj[:, P_CQ:P_CKV]) * gq_ref[...]).astype(BF16)
        ckv_ref[rs, :] = (_rms(proj[:, P_CKV:P_KR]) * gkv_ref[...]).astype(BF16)
        cos_lo = cos_ref[rs, :]
        sin_lo = sin_ref[rs, :]
        cos_hi = pltpu.roll(cos_lo, MLA_ROPE_DIM, 1)
        sin_hi = pltpu.roll(sin_lo, MLA_ROPE_DIM, 1)
        kr = proj[:, P_KR:P_COLS]
        kr_sw = pltpu.roll(kr, MLA_ROPE_DIM, 1)
        kr_ref[0, rs, :] = (kr * cos_lo + kr_sw * sin_lo).astype(BF16)
        kr_ref[1, rs, :] = (kr_sw * cos_hi + kr * sin_hi).astype(BF16)

    def stage_up(r):
        rs = chunk(r)
        cq = cq_ref[rs, :]
        ckv = ckv_ref[rs, :]
        cos_lo = cos_ref[rs, :]
        sin_lo = sin_ref[rs, :]
        cos_hi = pltpu.roll(cos_lo, MLA_ROPE_DIM, 1)
        sin_hi = pltpu.roll(sin_lo, MLA_ROPE_DIM, 1)
        krope_lo = kr_ref[0, rs, :]
        krope_hi = kr_ref[1, rs, :]
        qmain = jnp.dot(cq, wqm_ref[...], preferred_element_type=F32)
        qrot = jnp.dot(cq, wqr_ref[...], preferred_element_type=F32)
        knope = jnp.dot(ckv, wuk_ref[...], preferred_element_type=F32)
        vt = lax.dot_general(wvt_ref[...], ckv, (((1,), (1,)), ((), ())),
                             preferred_element_type=F32)

        vt_tile = (r * rows) // VT_TILE
        vt_cols = slice((r * rows) % VT_TILE, (r * rows) % VT_TILE + rows)
        for hd in range(MLA_HEADS):
            lo = hd * MLA_QK_PAD
            mid = lo + MLA_NOPE_DIM
            hi = lo + MLA_QK_PAD
            low = hd < half_heads
            cos_t, sin_t = (cos_lo, sin_lo) if low else (cos_hi, sin_hi)
            rot = qrot[:, (hd % half_heads) * LANES:(hd % half_heads + 1) * LANES]
            qb_ref[rs, lo:mid] = (qmain[:, lo:mid] * q_scale).astype(BF16)
            qb_ref[rs, mid:hi] = ((qmain[:, mid:hi] * cos_t + rot * sin_t) * q_scale).astype(BF16)
            kc_ref[rs, lo:mid] = knope[:, hd * MLA_NOPE_DIM:(hd + 1) * MLA_NOPE_DIM].astype(BF16)
            kc_ref[rs, mid:hi] = krope_lo if low else krope_hi
            vt_ref[0, hd, vt_tile, :MLA_V_DIM, vt_cols] = (
                vt[hd * MLA_V_DIM:(hd + 1) * MLA_V_DIM].astype(BF16))

    for t in range(PROJ_ROW_CHUNKS + 2):
        if 0 <= t - 2 < PROJ_ROW_CHUNKS:
            stage_up(t - 2)
        if 0 <= t - 1 < PROJ_ROW_CHUNKS:
            stage_proj(t - 1)
        if t < PROJ_ROW_CHUNKS:
            stage_norm(t)
        if t == 1:
            wob_ref[...] = wo_ref[...].astype(BF16)
            w1b_ref[...] = w1_ref[...].astype(BF16)
            w2b_ref[...] = w2_ref[...].astype(BF16)


def _proj_call(x2d, mod3, g_attn, w_main, w_t, gq, gkv, wq_main, wq_rot, w_uk, w_vt,
               cos_tab, sin_tab, w_out, w_ff1, w_ff2, *, batch, seq):
    tm = TM_PROJ
    tiles_per_seq = seq // tm
    t = batch * seq
    steps = t // tm
    slab = lambda w: (w.shape[0] // steps, w.shape[1])
    assert all(w.shape[0] % (steps * BF16_ROWS) == 0 for w in (w_out, w_ff1, w_ff2))
    tok = lambda i: (i, 0)
    tile_t = lambda i: (i, 0, 0)
    pos = lambda i: (i % tiles_per_seq, 0)
    out_shape = (
        jax.ShapeDtypeStruct((t // tm, SWA_Q_COLS, tm), BF16),
        jax.ShapeDtypeStruct((t, SWA_KV_COLS), BF16),
        jax.ShapeDtypeStruct((t // tm, SWA_KV_COLS, tm), BF16),
        jax.ShapeDtypeStruct((t, MLA_HEADS * MLA_QK_PAD), BF16),
        jax.ShapeDtypeStruct((t, MLA_HEADS * MLA_QK_PAD), BF16),
        jax.ShapeDtypeStruct((batch, MLA_HEADS, seq // VT_TILE, MLA_VT_ROWS, VT_TILE), BF16),
        jax.ShapeDtypeStruct(w_out.shape, BF16),
        jax.ShapeDtypeStruct(w_ff1.shape, BF16),
        jax.ShapeDtypeStruct(w_ff2.shape, BF16),
    )
    return pl.pallas_call(
        functools.partial(_proj_kernel, q_scale=MLA_Q_SCALE),
        out_shape=out_shape,
        grid=(t // tm,),
        in_specs=[
            pl.BlockSpec((tm, D_MODEL), tok),
            pl.BlockSpec((1, MOD_PRE, D_MODEL), lambda i: (i // tiles_per_seq, 0, 0)),
            _const_spec((1, D_MODEL)),
            _const_spec((D_MODEL, P_COLS)),
            _const_spec((PT_ROWS, D_MODEL)),
            _const_spec((1, MLA_Q_RANK)),
            _const_spec((1, MLA_KV_RANK)),
            _const_spec((MLA_Q_RANK, MLA_HEADS * MLA_QK_PAD)),
            _const_spec((MLA_Q_RANK, MLA_HEADS // 2 * LANES)),
            _const_spec((MLA_KV_RANK, MLA_HEADS * MLA_NOPE_DIM)),
            _const_spec((MLA_HEADS * MLA_V_DIM, MLA_KV_RANK)),
            pl.BlockSpec((tm, LANES), pos),
            pl.BlockSpec((tm, LANES), pos),
            pl.BlockSpec(slab(w_out), tok),
            pl.BlockSpec(slab(w_ff1), tok),
            pl.BlockSpec(slab(w_ff2), tok),
        ],
        out_specs=(
            pl.BlockSpec((1, SWA_Q_COLS, tm), tile_t),
            pl.BlockSpec((tm, SWA_KV_COLS), tok),
            pl.BlockSpec((1, SWA_KV_COLS, tm), tile_t),
            pl.BlockSpec((tm, MLA_HEADS * MLA_QK_PAD), tok),
            pl.BlockSpec((tm, MLA_HEADS * MLA_QK_PAD), tok),
            pl.BlockSpec((1, MLA_HEADS, tm // VT_TILE, MLA_VT_ROWS, VT_TILE),
                         lambda i: (i // tiles_per_seq, 0, i % tiles_per_seq, 0, 0)),
            pl.BlockSpec(slab(w_out), tok),
            pl.BlockSpec(slab(w_ff1), tok),
            pl.BlockSpec(slab(w_ff2), tok),
        ),
        scratch_shapes=[
            pltpu.VMEM((tm, D_MODEL), BF16),
            pltpu.VMEM((tm, MLA_Q_RANK), BF16),
            pltpu.VMEM((tm, MLA_KV_RANK), BF16),
            pltpu.VMEM((2, tm, LANES), BF16),
        ],
        compiler_params=_params(("parallel",)),
        name="in_proj",
    )(x2d, mod3, g_attn, w_main, w_t, gq, gkv, wq_main, wq_rot, w_uk, w_vt,
      cos_tab, sin_tab, w_out, w_ff1, w_ff2)


def _swa_kernel(sinks_ref, qt_ref, kp_ref, kc_ref, vtp_ref, vtc_ref, bias_ref,
                c_ref, wmod_ref, bmod_ref, o_ref, mod_ref,
                s_ref, p_ref, sink_ref, *, blocks_per_step):
    mod_ref[...] = _mod_tile(c_ref, wmod_ref, bmod_ref)

    first_step = pl.program_id(1) == 0
    k_all = jnp.concatenate([kp_ref[...], kc_ref[...]], axis=0)
    vt_all = jnp.concatenate([vtp_ref[0], vtc_ref[0]], axis=1)
    uh = SWA_UNIT_HEADS
    zeros_q = jnp.zeros((SWA_HEAD_DIM, uh * BLOCK), BF16)
    ones_rows = jnp.where(lax.broadcasted_iota(jnp.int32, (BF16_ROWS, 2 * BLOCK), 0) == 0,
                          1.0, 0.0).astype(BF16)
    units = [(i, g, part) for i in range(blocks_per_step) for g in range(SWA_KV_HEADS)
             for part in range(SWA_GROUP // uh)]

    def stage_scores(u):
        i, g, part = units[u]
        h_first = g * SWA_GROUP + part * uh
        k_band = k_all[i * BLOCK:(i + 2) * BLOCK]
        qt = qt_ref[0, :, i * BLOCK:(i + 1) * BLOCK]
        q_g = jnp.concatenate(
            [qt[(h_first + hh) * SWA_HEAD_DIM:(h_first + hh + 1) * SWA_HEAD_DIM]
             for hh in range(uh)], axis=1)
        rhs = jnp.concatenate([q_g, zeros_q] if g == 0 else [zeros_q, q_g], axis=0)
        s_ref[u % 2] = jnp.dot(k_band, rhs, preferred_element_type=F32)

    def stage_softmax(u):
        i, g, part = units[u]
        variant = jnp.where(first_step, 0, 1) if i == 0 else 1
        for hh in range(uh):
            hd = g * SWA_GROUP + part * uh + hh
            st = s_ref[u % 2, :, hh * BLOCK:(hh + 1) * BLOCK] + bias_ref[variant, hd]
            sink = sinks_ref[hd] * LOG2E
            m = jnp.maximum(jnp.max(st, axis=0, keepdims=True), sink)
            sink_ref[u % 2, hh] = jnp.exp2(sink - m)
            p_ref[u % 2, :, hh * BLOCK:(hh + 1) * BLOCK] = jnp.exp2(st - m).astype(BF16)

    def stage_pv(u):
        i, g, part = units[u]
        vt_band = jnp.concatenate(
            [vt_all[g * SWA_HEAD_DIM:(g + 1) * SWA_HEAD_DIM, i * BLOCK:(i + 2) * BLOCK],
             ones_rows], axis=0)
        ot_g = jnp.dot(vt_band, p_ref[u % 2], preferred_element_type=F32)

        def head_out(hh):
            cols = slice(hh * BLOCK, (hh + 1) * BLOCK)
            denom = ot_g[SWA_HEAD_DIM:SWA_HEAD_DIM + 1, cols] + sink_ref[u % 2, hh]
            return ot_g[:SWA_HEAD_DIM, cols] * (1.0 / denom)

        for pair in range(uh // 2):
            h0 = 2 * pair
            slab = jnp.concatenate([head_out(h0), head_out(h0 + 1)], axis=0)
            col = (g * SWA_GROUP + part * uh + h0) * SWA_HEAD_DIM
            o_ref[i * BLOCK:(i + 1) * BLOCK, col:col + 2 * SWA_HEAD_DIM] = (
                slab.T.astype(BF16))

    n = len(units)
    for t in range(n + 2):
        if 0 <= t - 2 < n:
            stage_pv(t - 2)
        if 0 <= t - 1 < n:
            stage_softmax(t - 1)
        if t < n:
            stage_scores(t)


def _swa_call(sinks, qat, ka, vat, bias_tab, c, w_mod, b_mod, *, batch, seq):
    step = TM_SWA
    blocks_per_step = step // BLOCK
    steps_per_seq = seq // step
    nb = seq // BLOCK
    prev_blk = lambda b, s: b * nb + jnp.maximum(s * blocks_per_step - 1, 0)
    assert step == TM_PROJ
    tile = lambda b, s: (b * steps_per_seq + s, 0, 0)
    prev_tile_last_blk = lambda b, s: (b * steps_per_seq + jnp.maximum(s - 1, 0), 0,
                                       blocks_per_step - 1)
    n_steps = batch * steps_per_seq
    mod_cols = MOD_POST * D_MODEL
    tn = mod_cols // n_steps
    assert tn % LANES == 0 and (MOD_PRE * D_MODEL) % tn == 0
    mod_tile = lambda b, s: (0, b * steps_per_seq + s)
    mod_src_tile = lambda b, s: (0, MOD_PRE * D_MODEL // tn + b * steps_per_seq + s)
    return pl.pallas_call(
        functools.partial(_swa_kernel, blocks_per_step=blocks_per_step),
        out_shape=(jax.ShapeDtypeStruct((batch * seq, SWA_Q_COLS), BF16),
                   jax.ShapeDtypeStruct((c.shape[0], mod_cols), F32)),
        grid=(batch, steps_per_seq),
        in_specs=[
            pl.BlockSpec(memory_space=pltpu.SMEM),
            pl.BlockSpec((1, SWA_Q_COLS, step), tile),
            pl.BlockSpec((BLOCK, SWA_KV_COLS), lambda b, s: (prev_blk(b, s), 0)),
            pl.BlockSpec((step, SWA_KV_COLS), lambda b, s: (b * steps_per_seq + s, 0)),
            pl.BlockSpec((1, SWA_KV_COLS, BLOCK), prev_tile_last_blk),
            pl.BlockSpec((1, SWA_KV_COLS, step), tile),
            _const_spec((2, SWA_HEADS, 2 * BLOCK, BLOCK)),
            _const_spec(c.shape),
            pl.BlockSpec((D_MODEL, tn), mod_src_tile),
            pl.BlockSpec((1, tn), mod_src_tile),
        ],
        out_specs=(
            pl.BlockSpec((step, SWA_Q_COLS), lambda b, s: (b * steps_per_seq + s, 0)),
            pl.BlockSpec((c.shape[0], tn), mod_tile),
        ),
        scratch_shapes=[
            pltpu.VMEM((2, 2 * BLOCK, SWA_UNIT_HEADS * BLOCK), F32),
            pltpu.VMEM((2, 2 * BLOCK, SWA_UNIT_HEADS * BLOCK), BF16),
            pltpu.VMEM((2, SWA_UNIT_HEADS, 1, BLOCK), F32),
        ],
        compiler_params=_params(("parallel", "arbitrary")),
        name="swa",
    )(sinks, qat, ka, ka, vat, vat, bias_tab, c, w_mod, b_mod)


def _mla_kernel(q_ref, k_ref, vt_ref, o_ref, s_ref, p_ref, al_ref, m_ref, acc_ref,
                *, tile, n_tiles):
    pairs = [(qi, j) for qi in range(n_tiles) for j in range(qi + 1)]
    width = tile // MLA_COL_SPLIT

    def geometry(u, c):
        qi, j = pairs[u]
        cols = slice(c * width, (c + 1) * width)
        n_keys = (c + 1) * width if j == qi else tile
        return qi, j, cols, n_keys

    def stage_scores(u, c):
        qi, j, cols, n_keys = geometry(u, c)
        k = k_ref[j * tile:j * tile + n_keys, :]
        q = q_ref[qi * tile + c * width:qi * tile + (c + 1) * width, :]
        s_ref[u % MLA_SLOTS, :n_keys, cols] = lax.dot_general(
            k, q, (((1,), (1,)), ((), ())), preferred_element_type=F32)

    def stage_softmax(u, c):
        qi, j, cols, n_keys = geometry(u, c)
        st = s_ref[u % MLA_SLOTS, :n_keys, cols]
        if j == qi:
            kk = lax.broadcasted_iota(jnp.int32, st.shape, 0)
            qq = lax.broadcasted_iota(jnp.int32, st.shape, 1) + c * width
            st = jnp.where(kk <= qq, st, -jnp.inf)
        cmax = jnp.max(st, axis=0, keepdims=True)
        if j == 0:
            m_new = cmax
        else:
            m_old = m_ref[qi % 2, :, cols]
            m_new = jnp.maximum(m_old, cmax)
            al_ref[u % MLA_SLOTS, :, cols] = jnp.exp2(m_old - m_new)
        m_ref[qi % 2, :, cols] = m_new
        p_ref[u % MLA_SLOTS, :n_keys, cols] = jnp.exp2(st - m_new).astype(BF16)

    def stage_pv(u, c):
        qi, j, cols, n_keys = geometry(u, c)
        vt_tile = vt_ref.shape[-1]
        pv = None
        for lo in range(0, n_keys, vt_tile):
            nk = min(vt_tile, n_keys - lo)
            part = jnp.dot(vt_ref[0, 0, (j * tile + lo) // vt_tile, :, :nk],
                           p_ref[u % MLA_SLOTS, lo:lo + nk, cols], preferred_element_type=F32)
            pv = part if pv is None else pv + part
        if j == 0:
            acc = pv
        else:
            acc = al_ref[u % MLA_SLOTS, :, cols] * acc_ref[:, cols] + pv
        if j == qi:
            rows = slice(qi * tile + c * width, qi * tile + (c + 1) * width)
            denom = acc[MLA_V_DIM:MLA_V_DIM + 1]
            o_ref[rows, :] = (acc[:MLA_V_DIM] / denom).T.astype(BF16)
        else:
            acc_ref[:, cols] = acc

    n = len(pairs)
    skew = MLA_SLOTS - 1
    for t in range(n + 2 * skew):
        for c in range(MLA_COL_SPLIT):
            if 0 <= t - 2 * skew < n:
                stage_pv(t - 2 * skew, c)
            if 0 <= t - skew < n:
                stage_softmax(t - skew, c)
            if t < n:
                stage_scores(t, c)


def _mla_call(qb, kcat, vt, *, batch, seq):
    tile = MLA_TILE
    n_tiles = seq // tile
    return pl.pallas_call(
        functools.partial(_mla_kernel, tile=tile, n_tiles=n_tiles),
        out_shape=jax.ShapeDtypeStruct((batch * seq, MLA_HEADS * MLA_V_DIM), BF16),
        grid=(batch, MLA_HEADS),
        in_specs=[
            pl.BlockSpec((seq, MLA_QK_PAD), lambda b, h: (b, h)),
            pl.BlockSpec((seq, MLA_QK_PAD), lambda b, h: (b, h)),
            pl.BlockSpec((1, 1, seq // VT_TILE, MLA_VT_ROWS, VT_TILE), lambda b, h: (b, h, 0, 0, 0)),
        ],
        out_specs=pl.BlockSpec((seq, MLA_V_DIM), lambda b, h: (b, h)),
        scratch_shapes=[
            pltpu.VMEM((MLA_SLOTS, tile, tile), F32),
            pltpu.VMEM((MLA_SLOTS, tile, tile), BF16),
            pltpu.VMEM((MLA_SLOTS, 1, tile), F32),
            pltpu.VMEM((2, 1, tile), F32),
            pltpu.VMEM((MLA_VT_ROWS, tile), F32),
        ],
        compiler_params=_params(("parallel", "parallel")),
        name="mla",
    )(qb, kcat, vt)


def _out_kernel(x_ref, oa_ref, ob_ref, mod_ref, wa_ref, wb_ref, x1_ref):
    g1 = mod_ref[0][0:1]
    y = (jnp.dot(oa_ref[...], wa_ref[...], preferred_element_type=F32)
         + jnp.dot(ob_ref[...], wb_ref[...], preferred_element_type=F32))
    x1_ref[...] = x_ref[...] + g1 * y


def _out_call(x2d, oa, ob, mod3, w_out, *, seq):
    tm = TM_OUT
    t = x2d.shape[0]
    tiles_per_seq = seq // tm
    tok = lambda i: (i, 0)
    w_half = lambda half: pl.BlockSpec((SWA_Q_COLS, D_MODEL), lambda i: (half, 0),
                                       pipeline_mode=pl.Buffered(1))
    assert w_out.shape[0] == 2 * SWA_Q_COLS
    return pl.pallas_call(
        _out_kernel,
        out_shape=jax.ShapeDtypeStruct((t, D_MODEL), F32),
        grid=(t // tm,),
        in_specs=[
            pl.BlockSpec((tm, D_MODEL), tok),
            pl.BlockSpec((tm, SWA_Q_COLS), tok),
            pl.BlockSpec((tm, MLA_HEADS * MLA_V_DIM), tok),
            pl.BlockSpec((1, MOD_POST, D_MODEL), lambda i: (i // tiles_per_seq, 0, 0)),
            w_half(0),
            w_half(1),
        ],
        out_specs=pl.BlockSpec((tm, D_MODEL), tok),
        compiler_params=_params(("parallel",)),
        name="out_proj",
    )(x2d, oa, ob, mod3, w_out, w_out)


def _mlp_kernel(x_ref, mod_ref, g_ref, gf_ref, w1_ref, w2_ref, o_ref, h_ref):
    j = pl.program_id(1)
    last = pl.num_programs(1) - 1
    mod = mod_ref[0]
    rows = o_ref.shape[0] // ROW_CHUNKS
    chunks = [slice(r * rows, (r + 1) * rows) for r in range(ROW_CHUNKS)]

    def ff_tile(h):
        u = jnp.maximum(jnp.dot(h, w1_ref[...], preferred_element_type=F32), 0.0)
        return jnp.dot((u * u).astype(BF16), w2_ref[...], preferred_element_type=F32)

    @pl.when(j == 0)
    def _():
        sh2 = mod[1:2]
        sc2 = mod[2:3]
        for sl in chunks:
            h = ((_rms(x_ref[sl, :]) * g_ref[...]) * (1.0 + sc2) + sh2).astype(BF16)
            h_ref[sl, :] = h
            o_ref[sl, :] = ff_tile(h)

    @pl.when((j > 0) & (j < last))
    def _():
        o_ref[...] += ff_tile(h_ref[...])

    @pl.when(j == last)
    def _():
        g2 = mod[3:4]
        for sl in chunks:
            x2 = x_ref[sl, :] + g2 * (o_ref[sl, :] + ff_tile(h_ref[sl, :]))
            o_ref[sl, :] = _rms(x2) * gf_ref[...]


def _mlp_call(x1, mod3, g_mlp, g_final, w1, w2, *, seq):
    tm = TM_MLP
    tf = TF_MLP
    assert D_FF // tf >= 2
    t = x1.shape[0]
    tiles_per_seq = seq // tm
    return pl.pallas_call(
        _mlp_kernel,
        out_shape=jax.ShapeDtypeStruct((t, D_MODEL), F32),
        grid=(t // tm, D_FF // tf),
        in_specs=[
            pl.BlockSpec((tm, D_MODEL), lambda i, j: (i, 0)),
            pl.BlockSpec((1, MOD_POST, D_MODEL), lambda i, j: (i // tiles_per_seq, 0, 0)),
            _const_spec((1, D_MODEL)),
            _const_spec((1, D_MODEL)),
            pl.BlockSpec((D_MODEL, tf), lambda i, j: (0, j)),
            pl.BlockSpec((tf, D_MODEL), lambda i, j: (j, 0)),
        ],
        out_specs=pl.BlockSpec((tm, D_MODEL), lambda i, j: (i, 0)),
        scratch_shapes=[pltpu.VMEM((tm, D_MODEL), BF16)],
        compiler_params=_params(("parallel", "arbitrary")),
        name="mlp",
    )(x1, mod3, g_mlp, g_final, w1, w2)


def _t5_bucket_table():
    q_loc = np.arange(BLOCK)[:, None]
    k_loc = np.arange(2 * BLOCK)[None, :]
    n = np.maximum(q_loc + BLOCK - k_loc, 0)
    max_exact = REL_BUCKETS // 2
    nf = np.maximum(n, 1).astype(np.float64)
    large = max_exact + (np.log(nf / max_exact) / math.log(REL_MAX_DIST / max_exact)
                         * (REL_BUCKETS - max_exact)).astype(np.int32)
    large = np.minimum(large, REL_BUCKETS - 1)
    return np.where(n < max_exact, n, large).astype(np.int32)


def _rope_tables(seq):
    half = MLA_ROPE_DIM // 2
    inv_freq = ROPE_THETA ** (-np.arange(half, dtype=np.float64) / half)
    ang = np.arange(seq, dtype=np.float64)[:, None] * inv_freq[None, :]
    zeros = np.zeros((seq, LANES - MLA_ROPE_DIM))
    cos_tab = np.concatenate([np.cos(ang), np.cos(ang), zeros], axis=1)
    sin_tab = np.concatenate([np.sin(ang), np.sin(ang), zeros], axis=1)
    return cos_tab.astype(np.float32), sin_tab.astype(np.float32)


def _rot_cols(w):
    half = w.shape[-1] // 2
    return jnp.concatenate([-w[..., half:], w[..., :half]], axis=-1)


def kernel(x, c, w_mod, b_mod, attn_norm_g, w_in, swa_sinks, rel_bias, mla_q_norm_g, w_uq,
           mla_kv_norm_g, w_ukv, w_out, mlp_norm_g, w_ff1, w_ff2, final_norm_g):
    batch, seq, _ = x.shape
    depth = w_mod.shape[0]
    assert depth == 1
    t = batch * seq
    x2d = x.reshape(t, D_MODEL)
    l = 0

    w_kr = w_in[l][:, OFF_MLA_KR:OFF_MLA_KR + MLA_ROPE_DIM]
    w_main = jnp.concatenate(
        [w_in[l][:, OFF_SWA_K:OFF_SWA_V], w_in[l][:, OFF_MLA_CQ:OFF_MLA_KR],
         w_kr, _rot_cols(w_kr)], axis=1).astype(BF16)
    w_t = jnp.concatenate(
        [w_in[l][:, :SWA_Q_COLS], w_in[l][:, OFF_SWA_V:OFF_MLA_CQ]], axis=1).T.astype(BF16)

    wq = w_uq[l].reshape(MLA_Q_RANK, MLA_HEADS, MLA_QK_DIM)
    wq_nope = wq[..., :MLA_NOPE_DIM]
    wq_rope = wq[..., MLA_NOPE_DIM:]
    hh = MLA_HEADS // 2
    zq = jnp.zeros((MLA_Q_RANK, hh, LANES - MLA_ROPE_DIM), F32)
    wq_main = jnp.concatenate(
        [jnp.concatenate([wq_nope[:, :hh], wq_rope[:, :hh], zq], axis=-1),
         jnp.concatenate([wq_nope[:, hh:], zq, wq_rope[:, hh:]], axis=-1)], axis=1).reshape(
        MLA_Q_RANK, MLA_HEADS * MLA_QK_PAD).astype(BF16)
    wq_rot_all = _rot_cols(wq_rope)
    wq_rot = jnp.concatenate([wq_rot_all[:, :hh], wq_rot_all[:, hh:]], axis=-1).reshape(
        MLA_Q_RANK, hh * LANES).astype(BF16)

    wkv = w_ukv[l].reshape(MLA_KV_RANK, MLA_HEADS, MLA_NOPE_DIM + MLA_V_DIM)
    w_uk = wkv[..., :MLA_NOPE_DIM].reshape(MLA_KV_RANK, MLA_HEADS * MLA_NOPE_DIM).astype(BF16)
    w_vt = wkv[..., MLA_NOPE_DIM:].reshape(MLA_KV_RANK, MLA_HEADS * MLA_V_DIM).T.astype(BF16)

    cos_tab, sin_tab = _rope_tables(seq)

    assert w_mod.shape[2] == (MOD_PRE + MOD_POST) * D_MODEL
    b_mod2d = b_mod[l].reshape(1, -1)
    mod_pre, bias_tab = _prologue_call(c, w_mod[l], b_mod2d, rel_bias, _t5_bucket_table().T)
    mod_pre = mod_pre.reshape(batch, MOD_PRE, D_MODEL)

    qat, ka, vat, qb, kcat, vt, w_out_b16, w1, w2 = _proj_call(
        x2d, mod_pre, attn_norm_g[l].reshape(1, -1), w_main, w_t,
        mla_q_norm_g[l].reshape(1, -1), mla_kv_norm_g[l].reshape(1, -1),
        wq_main, wq_rot, w_uk, w_vt, cos_tab, sin_tab, w_out[l], w_ff1[l], w_ff2[l],
        batch=batch, seq=seq)

    oa, mod_post = _swa_call(swa_sinks[l], qat, ka, vat, bias_tab, c, w_mod[l], b_mod2d,
                             batch=batch, seq=seq)
    mod_post = mod_post.reshape(batch, MOD_POST, D_MODEL)
    ob = _mla_call(qb, kcat, vt, batch=batch, seq=seq)

    x1 = _out_call(x2d, oa, ob, mod_post, w_out_b16, seq=seq)
    out = _mlp_call(x1, mod_post, mlp_norm_g[l].reshape(1, -1), final_norm_g.reshape(1, -1),
                    w1, w2, seq=seq)
    return out.reshape(batch, seq, D_MODEL)
```

```python
import functools
import math

import jax
import jax.numpy as jnp
import numpy as np
from jax import lax
from jax.experimental import pallas as pl
from jax.experimental.pallas import tpu as pltpu

F32 = jnp.float32
BF16 = jnp.bfloat16

LANES = 128
BF16_ROWS = 16

D_MODEL = 2048
BLOCK = 128
EPS = 1e-6

SWA_HEADS = 16
SWA_KV_HEADS = 2
SWA_HEAD_DIM = 64
SWA_GROUP = SWA_HEADS // SWA_KV_HEADS
WINDOW = 128
REL_BUCKETS = 32
REL_MAX_DIST = 128

MLA_HEADS = 8
MLA_Q_RANK = 384
MLA_KV_RANK = 128
MLA_NOPE_DIM = 128
MLA_ROPE_DIM = 64
MLA_V_DIM = 128
MLA_VT_ROWS = MLA_V_DIM + BF16_ROWS
MLA_QK_DIM = MLA_NOPE_DIM + MLA_ROPE_DIM
MLA_QK_PAD = 256
ROPE_THETA = 10000.0
MLA_Q_SCALE = MLA_QK_DIM ** -0.5 * math.log2(math.e)
D_FF = 4 * D_MODEL

SWA_Q_COLS = SWA_HEADS * SWA_HEAD_DIM
SWA_KV_COLS = SWA_KV_HEADS * SWA_HEAD_DIM
OFF_SWA_K = SWA_Q_COLS
OFF_SWA_V = OFF_SWA_K + SWA_KV_COLS
OFF_MLA_CQ = OFF_SWA_V + SWA_KV_COLS
OFF_MLA_CKV = OFF_MLA_CQ + MLA_Q_RANK
OFF_MLA_KR = OFF_MLA_CKV + MLA_KV_RANK
P_CQ = SWA_KV_COLS
P_CKV = P_CQ + MLA_Q_RANK
P_KR = P_CKV + MLA_KV_RANK
P_COLS = P_KR + LANES
PT_ROWS = SWA_Q_COLS + SWA_KV_COLS
LOG2E = math.log2(math.e)

VMEM_LIMIT_BYTES = 62 * 1024 * 1024

TM_PROJ = 512
MLA_TILE = 512
VT_TILE = min(MLA_TILE, TM_PROJ)
TM_SWA = 512
TM_OUT = 1024
TM_MLP = 512
TF_MLP = 2048
MOD_PRE = 2
MOD_POST = 4
BIAS_HEADS_PER_STEP = 4
SWA_UNIT_HEADS = 4
MLA_SLOTS = 2
MLA_COL_SPLIT = 2
ROW_CHUNKS = 2
PROJ_ROW_CHUNKS = 2


def _params(sem):
    return pltpu.CompilerParams(dimension_semantics=sem, vmem_limit_bytes=VMEM_LIMIT_BYTES)


def _const_spec(shape):
    nd = len(shape)
    return pl.BlockSpec(shape, lambda *_: (0,) * nd, pipeline_mode=pl.Buffered(1))


def _rms(x):
    return x * lax.rsqrt(jnp.mean(x * x, axis=-1, keepdims=True) + EPS)


def _mod_tile(c_ref, w_ref, b_ref):
    c = c_ref[...]
    ca = c * (1.0 / (1.0 + jnp.exp(-c)))
    return jnp.dot(ca.astype(BF16), w_ref[...].astype(BF16),
                   preferred_element_type=F32) + b_ref[...]


def _prologue_kernel(c_ref, w_ref, b_ref, rel_ref, bucket_ref, mod_ref, o_ref):
    mod_ref[...] = _mod_tile(c_ref, w_ref, b_ref)

    bucket = bucket_ref[...]
    k_loc = lax.broadcasted_iota(jnp.int32, bucket.shape, 0)
    q_loc = lax.broadcasted_iota(jnp.int32, bucket.shape, 1)
    dist = q_loc + BLOCK - k_loc
    in_window = (dist >= 0) & (dist < WINDOW)
    for i in range(BIAS_HEADS_PER_STEP):
        h = pl.program_id(0) * BIAS_HEADS_PER_STEP + i
        acc = jnp.zeros(bucket.shape, F32)
        for k in range(REL_BUCKETS):
            acc = jnp.where(bucket == k, rel_ref[k, h], acc)
        acc = acc * LOG2E
        o_ref[0, i] = jnp.where(in_window & (k_loc >= BLOCK), acc, -jnp.inf)
        o_ref[1, i] = jnp.where(in_window, acc, -jnp.inf)


def _prologue_call(c, w_mod, b_mod, rel_bias, bucket_t):
    hs = BIAS_HEADS_PER_STEP
    steps = SWA_HEADS // hs
    n = MOD_PRE * D_MODEL
    tn = n // steps
    assert tn % LANES == 0
    return pl.pallas_call(
        _prologue_kernel,
        out_shape=(jax.ShapeDtypeStruct((c.shape[0], n), F32),
                   jax.ShapeDtypeStruct((2, SWA_HEADS, 2 * BLOCK, BLOCK), F32)),
        grid=(steps,),
        in_specs=[
            pl.BlockSpec(c.shape, lambda s: (0, 0)),
            pl.BlockSpec((D_MODEL, tn), lambda s: (0, s)),
            pl.BlockSpec((1, tn), lambda s: (0, s)),
            pl.BlockSpec(memory_space=pltpu.SMEM),
            pl.BlockSpec((2 * BLOCK, BLOCK), lambda s: (0, 0)),
        ],
        out_specs=(pl.BlockSpec((c.shape[0], tn), lambda s: (0, s)),
                   pl.BlockSpec((2, hs, 2 * BLOCK, BLOCK), lambda s: (0, s, 0, 0))),
        compiler_params=_params(("arbitrary",)),
        name="prologue",
    )(c, w_mod, b_mod, rel_bias, bucket_t)


def _proj_kernel(x_ref, mod_ref, g_ref, win_ref, wt_ref, gq_ref, gkv_ref, wqm_ref, wqr_ref,
                 wuk_ref, wvt_ref, cos_ref, sin_ref, wo_ref, w1_ref, w2_ref,
                 qat_ref, ka_ref, vat_ref, qb_ref, kc_ref, vt_ref, wob_ref, w1b_ref, w2b_ref,
                 *, q_scale):
    wob_ref[...] = wo_ref[...].astype(BF16)
    w1b_ref[...] = w1_ref[...].astype(BF16)
    w2b_ref[...] = w2_ref[...].astype(BF16)

    mod = mod_ref[0]
    sh1 = mod[0:1]
    sc1 = mod[1:2]
    half_heads = MLA_HEADS // 2
    pad_rows = MLA_VT_ROWS - MLA_V_DIM
    ones_rows = jnp.where(lax.broadcasted_iota(jnp.int32, (pad_rows, VT_TILE), 0) == 0,
                          1.0, 0.0).astype(BF16)
    for hd in range(MLA_HEADS):
        for kt in range(vt_ref.shape[2]):
            vt_ref[0, hd, kt, MLA_V_DIM:] = ones_rows

    rows = x_ref.shape[0] // PROJ_ROW_CHUNKS
    assert VT_TILE % rows == 0
    for r in range(PROJ_ROW_CHUNKS):
        rs = slice(r * rows, (r + 1) * rows)
        h = ((_rms(x_ref[rs, :]) * g_ref[...]) * (1.0 + sc1) + sh1).astype(BF16)
        proj = jnp.dot(h, win_ref[...], preferred_element_type=F32)
        proj_t = lax.dot_general(wt_ref[...], h, (((1,), (1,)), ((), ())),
                                 preferred_element_type=F32)

        qat_ref[0, :, rs] = (proj_t[:SWA_Q_COLS] * (SWA_HEAD_DIM ** -0.5 * LOG2E)).astype(BF16)
        vat_ref[0, :, rs] = proj_t[SWA_Q_COLS:].astype(BF16)
        ka_ref[rs, :] = proj[:, :P_CQ].astype(BF16)

        cq = (_rms(proj[:, P_CQ:P_CKV]) * gq_ref[...]).astype(BF16)
        ckv = (_rms(proj[:, P_CKV:P_KR]) * gkv_ref[...]).astype(BF16)
        cos_lo = cos_ref[rs, :]
        sin_lo = sin_ref[rs, :]
        cos_hi = pltpu.roll(cos_lo, MLA_ROPE_DIM, 1)
        sin_hi = pltpu.roll(sin_lo, MLA_ROPE_DIM, 1)
        kr = proj[:, P_KR:P_COLS]
        kr_sw = pltpu.roll(kr, MLA_ROPE_DIM, 1)
        krope_lo = (kr * cos_lo + kr_sw * sin_lo).astype(BF16)
        krope_hi = (kr_sw * cos_hi + kr * sin_hi).astype(BF16)

        qmain = jnp.dot(cq, wqm_ref[...], preferred_element_type=F32)
        qrot = jnp.dot(cq, wqr_ref[...], preferred_element_type=F32)
        knope = jnp.dot(ckv, wuk_ref[...], preferred_element_type=F32)
        vt = lax.dot_general(wvt_ref[...], ckv, (((1,), (1,)), ((), ())),
                             preferred_element_type=F32)

        vt_tile = (r * rows) // VT_TILE
        vt_cols = slice((r * rows) % VT_TILE, (r * rows) % VT_TILE + rows)
        for hd in range(MLA_HEADS):
            lo = hd * MLA_QK_PAD
            mid = lo + MLA_NOPE_DIM
            hi = lo + MLA_QK_PAD
            low = hd < half_heads
            cos_t, sin_t = (cos_lo, sin_lo) if low else (cos_hi, sin_hi)
            rot = qrot[:, (hd % half_heads) * LANES:(hd % half_heads + 1) * LANES]
            qb_ref[rs, lo:mid] = (qmain[:, lo:mid] * q_scale).astype(BF16)
            qb_ref[rs, mid:hi] = ((qmain[:, mid:hi] * cos_t + rot * sin_t) * q_scale).astype(BF16)
            kc_ref[rs, lo:mid] = knope[:, hd * MLA_NOPE_DIM:(hd + 1) * MLA_NOPE_DIM].astype(BF16)
            kc_ref[rs, mid:hi] = krope_lo if low else krope_hi
            vt_ref[0, hd, vt_tile, :MLA_V_DIM, vt_cols] = (
                vt[hd * MLA_V_DIM:(hd + 1) * MLA_V_DIM].astype(BF16))


def _proj_call(x2d, mod3, g_attn, w_main, w_t, gq, gkv, wq_main, wq_rot, w_uk, w_vt,
               cos_tab, sin_tab, w_out, w_ff1, w_ff2, *, batch, seq):
    tm = TM_PROJ
    tiles_per_seq = seq // tm
    t = batch * seq
    steps = t // tm
    slab = lambda w: (w.shape[0] // steps, w.shape[1])
    assert all(w.shape[0] % (steps * BF16_ROWS) == 0 for w in (w_out, w_ff1, w_ff2))
    tok = lambda i: (i, 0)
    tile_t = lambda i: (i, 0, 0)
    pos = lambda i: (i % tiles_per_seq, 0)
    out_shape = (
        jax.ShapeDtypeStruct((t // tm, SWA_Q_COLS, tm), BF16),
        jax.ShapeDtypeStruct((t, SWA_KV_COLS), BF16),
        jax.ShapeDtypeStruct((t // tm, SWA_KV_COLS, tm), BF16),
        jax.ShapeDtypeStruct((t, MLA_HEADS * MLA_QK_PAD), BF16),
        jax.ShapeDtypeStruct((t, MLA_HEADS * MLA_QK_PAD), BF16),
        jax.ShapeDtypeStruct((batch, MLA_HEADS, seq // VT_TILE, MLA_VT_ROWS, VT_TILE), BF16),
        jax.ShapeDtypeStruct(w_out.shape, BF16),
        jax.ShapeDtypeStruct(w_ff1.shape, BF16),
        jax.ShapeDtypeStruct(w_ff2.shape, BF16),
    )
    return pl.pallas_call(
        functools.partial(_proj_kernel, q_scale=MLA_Q_SCALE),
        out_shape=out_shape,
        grid=(t // tm,),
        in_specs=[
            pl.BlockSpec((tm, D_MODEL), tok),
            pl.BlockSpec((1, MOD_PRE, D_MODEL), lambda i: (i // tiles_per_seq, 0, 0)),
            _const_spec((1, D_MODEL)),
            _const_spec((D_MODEL, P_COLS)),
            _const_spec((PT_ROWS, D_MODEL)),
            _const_spec((1, MLA_Q_RANK)),
            _const_spec((1, MLA_KV_RANK)),
            _const_spec((MLA_Q_RANK, MLA_HEADS * MLA_QK_PAD)),
            _const_spec((MLA_Q_RANK, MLA_HEADS // 2 * LANES)),
            _const_spec((MLA_KV_RANK, MLA_HEADS * MLA_NOPE_DIM)),
            _const_spec((MLA_HEADS * MLA_V_DIM, MLA_KV_RANK)),
            pl.BlockSpec((tm, LANES), pos),
            pl.BlockSpec((tm, LANES), pos),
            pl.BlockSpec(slab(w_out), tok),
            pl.BlockSpec(slab(w_ff1), tok),
            pl.BlockSpec(slab(w_ff2), tok),
        ],
        out_specs=(
            pl.BlockSpec((1, SWA_Q_COLS, tm), tile_t),
            pl.BlockSpec((tm, SWA_KV_COLS), tok),
            pl.BlockSpec((1, SWA_KV_COLS, tm), tile_t),
            pl.BlockSpec((tm, MLA_HEADS * MLA_QK_PAD), tok),
            pl.BlockSpec((tm, MLA_HEADS * MLA_QK_PAD), tok),
            pl.BlockSpec((1, MLA_HEADS, tm // VT_TILE, MLA_VT_ROWS, VT_TILE),
                         lambda i: (i // tiles_per_seq, 0, i % tiles_per_seq, 0, 0)),
            pl.BlockSpec(slab(w_out), tok),
            pl.BlockSpec(slab(w_ff1), tok),
            pl.BlockSpec(slab(w_ff2), tok),
        ),
        compiler_params=_params(("parallel",)),
        name="in_proj",
    )(x2d, mod3, g_attn, w_main, w_t, gq, gkv, wq_main, wq_rot, w_uk, w_vt,
      cos_tab, sin_tab, w_out, w_ff1, w_ff2)


def _swa_kernel(sinks_ref, qt_ref, kp_ref, kc_ref, vtp_ref, vtc_ref, bias_ref,
                c_ref, wmod_ref, bmod_ref, o_ref, mod_ref,
                s_ref, p_ref, sink_ref, *, blocks_per_step):
    mod_ref[...] = _mod_tile(c_ref, wmod_ref, bmod_ref)

    first_step = pl.program_id(1) == 0
    k_all = jnp.concatenate([kp_ref[...], kc_ref[...]], axis=0)
    vt_all = jnp.concatenate([vtp_ref[0], vtc_ref[0]], axis=1)
    uh = SWA_UNIT_HEADS
    zeros_q = jnp.zeros((SWA_HEAD_DIM, uh * BLOCK), BF16)
    ones_rows = jnp.where(lax.broadcasted_iota(jnp.int32, (BF16_ROWS, 2 * BLOCK), 0) == 0,
                          1.0, 0.0).astype(BF16)
    units = [(i, g, part) for i in range(blocks_per_step) for g in range(SWA_KV_HEADS)
             for part in range(SWA_GROUP // uh)]

    def stage_scores(u):
        i, g, part = units[u]
        h_first = g * SWA_GROUP + part * uh
        k_band = k_all[i * BLOCK:(i + 2) * BLOCK]
        qt = qt_ref[0, :, i * BLOCK:(i + 1) * BLOCK]
        q_g = jnp.concatenate(
            [qt[(h_first + hh) * SWA_HEAD_DIM:(h_first + hh + 1) * SWA_HEAD_DIM]
             for hh in range(uh)], axis=1)
        rhs = jnp.concatenate([q_g, zeros_q] if g == 0 else [zeros_q, q_g], axis=0)
        s_ref[u % 2] = jnp.dot(k_band, rhs, preferred_element_type=F32)

    def stage_softmax(u):
        i, g, part = units[u]
        variant = jnp.where(first_step, 0, 1) if i == 0 else 1
        for hh in range(uh):
            hd = g * SWA_GROUP + part * uh + hh
            st = s_ref[u % 2, :, hh * BLOCK:(hh + 1) * BLOCK] + bias_ref[variant, hd]
            sink = sinks_ref[hd] * LOG2E
            m = jnp.maximum(jnp.max(st, axis=0, keepdims=True), sink)
            sink_ref[u % 2, hh] = jnp.exp2(sink - m)
            p_ref[u % 2, :, hh * BLOCK:(hh + 1) * BLOCK] = jnp.exp2(st - m).astype(BF16)

    def stage_pv(u):
        i, g, part = units[u]
        vt_band = jnp.concatenate(
            [vt_all[g * SWA_HEAD_DIM:(g + 1) * SWA_HEAD_DIM, i * BLOCK:(i + 2) * BLOCK],
             ones_rows], axis=0)
        ot_g = jnp.dot(vt_band, p_ref[u % 2], preferred_element_type=F32)

        def head_out(hh):
            cols = slice(hh * BLOCK, (hh + 1) * BLOCK)
            denom = ot_g[SWA_HEAD_DIM:SWA_HEAD_DIM + 1, cols] + sink_ref[u % 2, hh]
            return ot_g[:SWA_HEAD_DIM, cols] * (1.0 / denom)

        for pair in range(uh // 2):
            h0 = 2 * pair
            slab = jnp.concatenate([head_out(h0), head_out(h0 + 1)], axis=0)
            col = (g * SWA_GROUP + part * uh + h0) * SWA_HEAD_DIM
            o_ref[i * BLOCK:(i + 1) * BLOCK, col:col + 2 * SWA_HEAD_DIM] = (
                slab.T.astype(BF16))

    n = len(units)
    for t in range(n + 2):
        if 0 <= t - 2 < n:
            stage_pv(t - 2)
        if 0 <= t - 1 < n:
            stage_softmax(t - 1)
        if t < n:
            stage_scores(t)


def _swa_call(sinks, qat, ka, vat, bias_tab, c, w_mod, b_mod, *, batch, seq):
    step = TM_SWA
    blocks_per_step = step // BLOCK
    steps_per_seq = seq // step
    nb = seq // BLOCK
    prev_blk = lambda b, s: b * nb + jnp.maximum(s * blocks_per_step - 1, 0)
    assert step == TM_PROJ
    tile = lambda b, s: (b * steps_per_seq + s, 0, 0)
    prev_tile_last_blk = lambda b, s: (b * steps_per_seq + jnp.maximum(s - 1, 0), 0,
                                       blocks_per_step - 1)
    n_steps = batch * steps_per_seq
    mod_cols = MOD_POST * D_MODEL
    tn = mod_cols // n_steps
    assert tn % LANES == 0 and (MOD_PRE * D_MODEL) % tn == 0
    mod_tile = lambda b, s: (0, b * steps_per_seq + s)
    mod_src_tile = lambda b, s: (0, MOD_PRE * D_MODEL // tn + b * steps_per_seq + s)
    return pl.pallas_call(
        functools.partial(_swa_kernel, blocks_per_step=blocks_per_step),
        out_shape=(jax.ShapeDtypeStruct((batch * seq, SWA_Q_COLS), BF16),
                   jax.ShapeDtypeStruct((c.shape[0], mod_cols), F32)),
        grid=(batch, steps_per_seq),
        in_specs=[
            pl.BlockSpec(memory_space=pltpu.SMEM),
            pl.BlockSpec((1, SWA_Q_COLS, step), tile),
            pl.BlockSpec((BLOCK, SWA_KV_COLS), lambda b, s: (prev_blk(b, s), 0)),
            pl.BlockSpec((step, SWA_KV_COLS), lambda b, s: (b * steps_per_seq + s, 0)),
            pl.BlockSpec((1, SWA_KV_COLS, BLOCK), prev_tile_last_blk),
            pl.BlockSpec((1, SWA_KV_COLS, step), tile),
            _const_spec((2, SWA_HEADS, 2 * BLOCK, BLOCK)),
            _const_spec(c.shape),
            pl.BlockSpec((D_MODEL, tn), mod_src_tile),
            pl.BlockSpec((1, tn), mod_src_tile),
        ],
        out_specs=(
            pl.BlockSpec((step, SWA_Q_COLS), lambda b, s: (b * steps_per_seq + s, 0)),
            pl.BlockSpec((c.shape[0], tn), mod_tile),
        ),
        scratch_shapes=[
            pltpu.VMEM((2, 2 * BLOCK, SWA_UNIT_HEADS * BLOCK), F32),
            pltpu.VMEM((2, 2 * BLOCK, SWA_UNIT_HEADS * BLOCK), BF16),
            pltpu.VMEM((2, SWA_UNIT_HEADS, 1, BLOCK), F32),
        ],
        compiler_params=_params(("parallel", "arbitrary")),
        name="swa",
    )(sinks, qat, ka, ka, vat, vat, bias_tab, c, w_mod, b_mod)


def _mla_kernel(q_ref, k_ref, vt_ref, o_ref, s_ref, p_ref, al_ref, m_ref, acc_ref,
                *, tile, n_tiles):
    pairs = [(qi, j) for qi in range(n_tiles) for j in range(qi + 1)]
    width = tile // MLA_COL_SPLIT

    def geometry(u, c):
        qi, j = pairs[u]
        cols = slice(c * width, (c + 1) * width)
        n_keys = (c + 1) * width if j == qi else tile
        return qi, j, cols, n_keys

    def stage_scores(u, c):
        qi, j, cols, n_keys = geometry(u, c)
        k = k_ref[j * tile:j * tile + n_keys, :]
        q = q_ref[qi * tile + c * width:qi * tile + (c + 1) * width, :]
        s_ref[u % MLA_SLOTS, :n_keys, cols] = lax.dot_general(
            k, q, (((1,), (1,)), ((), ())), preferred_element_type=F32)

    def stage_softmax(u, c):
        qi, j, cols, n_keys = geometry(u, c)
        st = s_ref[u % MLA_SLOTS, :n_keys, cols]
        if j == qi:
            kk = lax.broadcasted_iota(jnp.int32, st.shape, 0)
            qq = lax.broadcasted_iota(jnp.int32, st.shape, 1) + c * width
            st = jnp.where(kk <= qq, st, -jnp.inf)
        cmax = jnp.max(st, axis=0, keepdims=True)
        if j == 0:
            m_new = cmax
        else:
            m_old = m_ref[qi % 2, :, cols]
            m_new = jnp.maximum(m_old, cmax)
            al_ref[u % MLA_SLOTS, :, cols] = jnp.exp2(m_old - m_new)
        m_ref[qi % 2, :, cols] = m_new
        p_ref[u % MLA_SLOTS, :n_keys, cols] = jnp.exp2(st - m_new).astype(BF16)

    def stage_pv(u, c):
        qi, j, cols, n_keys = geometry(u, c)
        vt_tile = vt_ref.shape[-1]
        pv = None
        for lo in range(0, n_keys, vt_tile):
            nk = min(vt_tile, n_keys - lo)
            part = jnp.dot(vt_ref[0, 0, (j * tile + lo) // vt_tile, :, :nk],
                           p_ref[u % MLA_SLOTS, lo:lo + nk, cols], preferred_element_type=F32)
            pv = part if pv is None else pv + part
        if j == 0:
            acc = pv
        else:
            acc = al_ref[u % MLA_SLOTS, :, cols] * acc_ref[:, cols] + pv
        if j == qi:
            rows = slice(qi * tile + c * width, qi * tile + (c + 1) * width)
            denom = acc[MLA_V_DIM:MLA_V_DIM + 1]
            o_ref[rows, :] = (acc[:MLA_V_DIM] / denom).T.astype(BF16)
        else:
            acc_ref[:, cols] = acc

    n = len(pairs)
    skew = MLA_SLOTS - 1
    for t in range(n + 2 * skew):
        for c in range(MLA_COL_SPLIT):
            if 0 <= t - 2 * skew < n:
                stage_pv(t - 2 * skew, c)
            if 0 <= t - skew < n:
                stage_softmax(t - skew, c)
            if t < n:
                stage_scores(t, c)


def _mla_call(qb, kcat, vt, *, batch, seq):
    tile = MLA_TILE
    n_tiles = seq // tile
    return pl.pallas_call(
        functools.partial(_mla_kernel, tile=tile, n_tiles=n_tiles),
        out_shape=jax.ShapeDtypeStruct((batch * seq, MLA_HEADS * MLA_V_DIM), BF16),
        grid=(batch, MLA_HEADS),
        in_specs=[
            pl.BlockSpec((seq, MLA_QK_PAD), lambda b, h: (b, h)),
            pl.BlockSpec((seq, MLA_QK_PAD), lambda b, h: (b, h)),
            pl.BlockSpec((1, 1, seq // VT_TILE, MLA_VT_ROWS, VT_TILE), lambda b, h: (b, h, 0, 0, 0)),
        ],
        out_specs=pl.BlockSpec((seq, MLA_V_DIM), lambda b, h: (b, h)),
        scratch_shapes=[
            pltpu.VMEM((MLA_SLOTS, tile, tile), F32),
            pltpu.VMEM((MLA_SLOTS, tile, tile), BF16),
            pltpu.VMEM((MLA_SLOTS, 1, tile), F32),
            pltpu.VMEM((2, 1, tile), F32),
            pltpu.VMEM((MLA_VT_ROWS, tile), F32),
        ],
        compiler_params=_params(("parallel", "parallel")),
        name="mla",
    )(qb, kcat, vt)


def _out_kernel(x_ref, oa_ref, ob_ref, mod_ref, wa_ref, wb_ref, x1_ref):
    g1 = mod_ref[0][0:1]
    y = (jnp.dot(oa_ref[...], wa_ref[...], preferred_element_type=F32)
         + jnp.dot(ob_ref[...], wb_ref[...], preferred_element_type=F32))
    x1_ref[...] = x_ref[...] + g1 * y


def _out_call(x2d, oa, ob, mod3, w_out, *, seq):
    tm = TM_OUT
    t = x2d.shape[0]
    tiles_per_seq = seq // tm
    tok = lambda i: (i, 0)
    w_half = lambda half: pl.BlockSpec((SWA_Q_COLS, D_MODEL), lambda i: (half, 0),
                                       pipeline_mode=pl.Buffered(1))
    assert w_out.shape[0] == 2 * SWA_Q_COLS
    return pl.pallas_call(
        _out_kernel,
        out_shape=jax.ShapeDtypeStruct((t, D_MODEL), F32),
        grid=(t // tm,),
        in_specs=[
            pl.BlockSpec((tm, D_MODEL), tok),
            pl.BlockSpec((tm, SWA_Q_COLS), tok),
            pl.BlockSpec((tm, MLA_HEADS * MLA_V_DIM), tok),
            pl.BlockSpec((1, MOD_POST, D_MODEL), lambda i: (i // tiles_per_seq, 0, 0)),
            w_half(0),
            w_half(1),
        ],
        out_specs=pl.BlockSpec((tm, D_MODEL), tok),
        compiler_params=_params(("parallel",)),
        name="out_proj",
    )(x2d, oa, ob, mod3, w_out, w_out)


def _mlp_kernel(x_ref, mod_ref, g_ref, gf_ref, w1_hbm, w2_hbm, o_ref,
                h_ref, w1_buf, w2_buf, sem, *, tf, n_ff):
    i = pl.program_id(0)
    n_steps = pl.num_programs(0)
    mod = mod_ref[0]
    rows = o_ref.shape[0] // ROW_CHUNKS
    chunks = [slice(r * rows, (r + 1) * rows) for r in range(ROW_CHUNKS)]

    def copies(t):
        slot = t % 2
        return (
            pltpu.make_async_copy(w1_hbm.at[:, pl.ds(t * tf, tf)], w1_buf.at[slot],
                                  sem.at[0, slot]),
            pltpu.make_async_copy(w2_hbm.at[pl.ds(t * tf, tf), :], w2_buf.at[slot],
                                  sem.at[1, slot]),
        )

    def start(t):
        for cp in copies(t):
            cp.start()

    def wait(t):
        for cp in copies(t):
            cp.wait()

    def ff_tile(h, t):
        u = jnp.maximum(jnp.dot(h, w1_buf[t % 2], preferred_element_type=F32), 0.0)
        return jnp.dot((u * u).astype(BF16), w2_buf[t % 2], preferred_element_type=F32)

    @pl.when(i == 0)
    def _():
        start(0)

    for t in range(n_ff):
        if t + 1 < n_ff:
            start(t + 1)
        else:
            @pl.when(i + 1 < n_steps)
            def _():
                start(0)
        wait(t)
        if t == 0:
            sh2 = mod[1:2]
            sc2 = mod[2:3]
            for sl in chunks:
                h = ((_rms(x_ref[sl, :]) * g_ref[...]) * (1.0 + sc2) + sh2).astype(BF16)
                h_ref[sl, :] = h
                o_ref[sl, :] = ff_tile(h, t)
        elif t + 1 < n_ff:
            o_ref[...] += ff_tile(h_ref[...], t)
        else:
            g2 = mod[3:4]
            for sl in chunks:
                x2 = x_ref[sl, :] + g2 * (o_ref[sl, :] + ff_tile(h_ref[sl, :], t))
                o_ref[sl, :] = _rms(x2) * gf_ref[...]


def _mlp_call(x1, mod3, g_mlp, g_final, w1, w2, *, seq):
    tm = TM_MLP
    tf = TF_MLP
    n_ff = D_FF // tf
    assert n_ff >= 2
    t = x1.shape[0]
    tiles_per_seq = seq // tm
    return pl.pallas_call(
        functools.partial(_mlp_kernel, tf=tf, n_ff=n_ff),
        out_shape=jax.ShapeDtypeStruct((t, D_MODEL), F32),
        grid=(t // tm,),
        in_specs=[
            pl.BlockSpec((tm, D_MODEL), lambda i: (i, 0)),
            pl.BlockSpec((1, MOD_POST, D_MODEL), lambda i: (i // tiles_per_seq, 0, 0)),
            _const_spec((1, D_MODEL)),
            _const_spec((1, D_MODEL)),
            pl.BlockSpec(memory_space=pl.ANY),
            pl.BlockSpec(memory_space=pl.ANY),
        ],
        out_specs=pl.BlockSpec((tm, D_MODEL), lambda i: (i, 0)),
        scratch_shapes=[
            pltpu.VMEM((tm, D_MODEL), BF16),
            pltpu.VMEM((2, D_MODEL, tf), BF16),
            pltpu.VMEM((2, tf, D_MODEL), BF16),
            pltpu.SemaphoreType.DMA((2, 2)),
        ],
        compiler_params=_params(("arbitrary",)),
        name="mlp",
    )(x1, mod3, g_mlp, g_final, w1, w2)


def _t5_bucket_table():
    q_loc = np.arange(BLOCK)[:, None]
    k_loc = np.arange(2 * BLOCK)[None, :]
    n = np.maximum(q_loc + BLOCK - k_loc, 0)
    max_exact = REL_BUCKETS // 2
    nf = np.maximum(n, 1).astype(np.float64)
    large = max_exact + (np.log(nf / max_exact) / math.log(REL_MAX_DIST / max_exact)
                         * (REL_BUCKETS - max_exact)).astype(np.int32)
    large = np.minimum(large, REL_BUCKETS - 1)
    return np.where(n < max_exact, n, large).astype(np.int32)


def _rope_tables(seq):
    half = MLA_ROPE_DIM // 2
    inv_freq = ROPE_THETA ** (-np.arange(half, dtype=np.float64) / half)
    ang = np.arange(seq, dtype=np.float64)[:, None] * inv_freq[None, :]
    zeros = np.zeros((seq, LANES - MLA_ROPE_DIM))
    cos_tab = np.concatenate([np.cos(ang), np.cos(ang), zeros], axis=1)
    sin_tab = np.concatenate([np.sin(ang), np.sin(ang), zeros], axis=1)
    return cos_tab.astype(np.float32), sin_tab.astype(np.float32)


def _rot_cols(w):
    half = w.shape[-1] // 2
    return jnp.concatenate([-w[..., half:], w[..., :half]], axis=-1)


def kernel(x, c, w_mod, b_mod, attn_norm_g, w_in, swa_sinks, rel_bias, mla_q_norm_g, w_uq,
           mla_kv_norm_g, w_ukv, w_out, mlp_norm_g, w_ff1, w_ff2, final_norm_g):
    batch, seq, _ = x.shape
    depth = w_mod.shape[0]
    assert depth == 1
    t = batch * seq
    x2d = x.reshape(t, D_MODEL)
    l = 0

    w_kr = w_in[l][:, OFF_MLA_KR:OFF_MLA_KR + MLA_ROPE_DIM]
    w_main = jnp.concatenate(
        [w_in[l][:, OFF_SWA_K:OFF_SWA_V], w_in[l][:, OFF_MLA_CQ:OFF_MLA_KR],
         w_kr, _rot_cols(w_kr)], axis=1).astype(BF16)
    w_t = jnp.concatenate(
        [w_in[l][:, :SWA_Q_COLS], w_in[l][:, OFF_SWA_V:OFF_MLA_CQ]], axis=1).T.astype(BF16)

    wq = w_uq[l].reshape(MLA_Q_RANK, MLA_HEADS, MLA_QK_DIM)
    wq_nope = wq[..., :MLA_NOPE_DIM]
    wq_rope = wq[..., MLA_NOPE_DIM:]
    hh = MLA_HEADS // 2
    zq = jnp.zeros((MLA_Q_RANK, hh, LANES - MLA_ROPE_DIM), F32)
    wq_main = jnp.concatenate(
        [jnp.concatenate([wq_nope[:, :hh], wq_rope[:, :hh], zq], axis=-1),
         jnp.concatenate([wq_nope[:, hh:], zq, wq_rope[:, hh:]], axis=-1)], axis=1).reshape(
        MLA_Q_RANK, MLA_HEADS * MLA_QK_PAD).astype(BF16)
    wq_rot_all = _rot_cols(wq_rope)
    wq_rot = jnp.concatenate([wq_rot_all[:, :hh], wq_rot_all[:, hh:]], axis=-1).reshape(
        MLA_Q_RANK, hh * LANES).astype(BF16)

    wkv = w_ukv[l].reshape(MLA_KV_RANK, MLA_HEADS, MLA_NOPE_DIM + MLA_V_DIM)
    w_uk = wkv[..., :MLA_NOPE_DIM].reshape(MLA_KV_RANK, MLA_HEADS * MLA_NOPE_DIM).astype(BF16)
    w_vt = wkv[..., MLA_NOPE_DIM:].reshape(MLA_KV_RANK, MLA_HEADS * MLA_V_DIM).T.astype(BF16)

    cos_tab, sin_tab = _rope_tables(seq)

    assert w_mod.shape[2] == (MOD_PRE + MOD_POST) * D_MODEL
    b_mod2d = b_mod[l].reshape(1, -1)
    mod_pre, bias_tab = _prologue_call(c, w_mod[l], b_mod2d, rel_bias, _t5_bucket_table().T)
    mod_pre = mod_pre.reshape(batch, MOD_PRE, D_MODEL)

    qat, ka, vat, qb, kcat, vt, w_out_b16, w1, w2 = _proj_call(
        x2d, mod_pre, attn_norm_g[l].reshape(1, -1), w_main, w_t,
        mla_q_norm_g[l].reshape(1, -1), mla_kv_norm_g[l].reshape(1, -1),
        wq_main, wq_rot, w_uk, w_vt, cos_tab, sin_tab, w_out[l], w_ff1[l], w_ff2[l],
        batch=batch, seq=seq)

    oa, mod_post = _swa_call(swa_sinks[l], qat, ka, vat, bias_tab, c, w_mod[l], b_mod2d,
                             batch=batch, seq=seq)
    mod_post = mod_post.reshape(batch, MOD_POST, D_MODEL)
    ob = _mla_call(qb, kcat, vt, batch=batch, seq=seq)

    x1 = _out_call(x2d, oa, ob, mod_post, w_out_b16, seq=seq)
    out = _mlp_call(x1, mod_post, mlp_norm_g[l].reshape(1, -1), final_norm_g.reshape(1, -1),
                    w1, w2, seq=seq)
    return out.reshape(batch, seq, D_MODEL)
```

```python
import functools
import math

import jax
import jax.numpy as jnp
import numpy as np
from jax import lax
from jax.experimental import pallas as pl
from jax.experimental.pallas import tpu as pltpu

F32 = jnp.float32
BF16 = jnp.bfloat16

LANES = 128
BF16_ROWS = 16

D_MODEL = 2048
BLOCK = 128
EPS = 1e-6

SWA_HEADS = 16
SWA_KV_HEADS = 2
SWA_HEAD_DIM = 64
SWA_GROUP = SWA_HEADS // SWA_KV_HEADS
WINDOW = 128
REL_BUCKETS = 32
REL_MAX_DIST = 128

MLA_HEADS = 8
MLA_Q_RANK = 384
MLA_KV_RANK = 128
MLA_NOPE_DIM = 128
MLA_ROPE_DIM = 64
MLA_V_DIM = 128
MLA_VT_ROWS = MLA_V_DIM + BF16_ROWS
MLA_QK_DIM = MLA_NOPE_DIM + MLA_ROPE_DIM
MLA_QK_PAD = 256
ROPE_THETA = 10000.0
MLA_Q_SCALE = MLA_QK_DIM ** -0.5 * math.log2(math.e)
D_FF = 4 * D_MODEL

SWA_Q_COLS = SWA_HEADS * SWA_HEAD_DIM
SWA_KV_COLS = SWA_KV_HEADS * SWA_HEAD_DIM
OFF_SWA_K = SWA_Q_COLS
OFF_SWA_V = OFF_SWA_K + SWA_KV_COLS
OFF_MLA_CQ = OFF_SWA_V + SWA_KV_COLS
OFF_MLA_CKV = OFF_MLA_CQ + MLA_Q_RANK
OFF_MLA_KR = OFF_MLA_CKV + MLA_KV_RANK
P_CQ = SWA_KV_COLS
P_CKV = P_CQ + MLA_Q_RANK
P_KR = P_CKV + MLA_KV_RANK
P_COLS = P_KR + LANES
PT_ROWS = SWA_Q_COLS + SWA_KV_COLS
LOG2E = math.log2(math.e)

VMEM_LIMIT_BYTES = 62 * 1024 * 1024

TM_PROJ = 512
MLA_TILE = 512
VT_TILE = min(MLA_TILE, TM_PROJ)
TM_SWA = 512
TM_OUT = 1024
TM_MLP = 512
TF_MLP = 2048
MOD_PRE = 2
MOD_POST = 4
BIAS_HEADS_PER_STEP = 4
SWA_UNIT_HEADS = 4
MLA_SLOTS = 2
MLA_COL_SPLIT = 2
ROW_CHUNKS = 2
PROJ_ROW_CHUNKS = 2


def _params(sem):
    return pltpu.CompilerParams(dimension_semantics=sem, vmem_limit_bytes=VMEM_LIMIT_BYTES)


def _const_spec(shape):
    nd = len(shape)
    return pl.BlockSpec(shape, lambda *_: (0,) * nd, pipeline_mode=pl.Buffered(1))


def _rms(x):
    return x * lax.rsqrt(jnp.mean(x * x, axis=-1, keepdims=True) + EPS)


def _mod_tile(c_ref, w_ref, b_ref):
    c = c_ref[...]
    ca = c * (1.0 / (1.0 + jnp.exp(-c)))
    return jnp.dot(ca.astype(BF16), w_ref[...].astype(BF16),
                   preferred_element_type=F32) + b_ref[...]


def _prologue_kernel(c_ref, w_ref, b_ref, rel_ref, bucket_ref, mod_ref, o_ref):
    mod_ref[...] = _mod_tile(c_ref, w_ref, b_ref)

    bucket = bucket_ref[...]
    k_loc = lax.broadcasted_iota(jnp.int32, bucket.shape, 0)
    q_loc = lax.broadcasted_iota(jnp.int32, bucket.shape, 1)
    dist = q_loc + BLOCK - k_loc
    in_window = (dist >= 0) & (dist < WINDOW)
    for i in range(BIAS_HEADS_PER_STEP):
        h = pl.program_id(0) * BIAS_HEADS_PER_STEP + i
        acc = jnp.zeros(bucket.shape, F32)
        for k in range(REL_BUCKETS):
            acc = jnp.where(bucket == k, rel_ref[k, h], acc)
        acc = acc * LOG2E
        o_ref[0, i] = jnp.where(in_window & (k_loc >= BLOCK), acc, -jnp.inf)
        o_ref[1, i] = jnp.where(in_window, acc, -jnp.inf)


def _prologue_call(c, w_mod, b_mod, rel_bias, bucket_t):
    hs = BIAS_HEADS_PER_STEP
    steps = SWA_HEADS // hs
    n = MOD_PRE * D_MODEL
    tn = n // steps
    assert tn % LANES == 0
    return pl.pallas_call(
        _prologue_kernel,
        out_shape=(jax.ShapeDtypeStruct((c.shape[0], n), F32),
                   jax.ShapeDtypeStruct((2, SWA_HEADS, 2 * BLOCK, BLOCK), F32)),
        grid=(steps,),
        in_specs=[
            pl.BlockSpec(c.shape, lambda s: (0, 0)),
            pl.BlockSpec((D_MODEL, tn), lambda s: (0, s)),
            pl.BlockSpec((1, tn), lambda s: (0, s)),
            pl.BlockSpec(memory_space=pltpu.SMEM),
            pl.BlockSpec((2 * BLOCK, BLOCK), lambda s: (0, 0)),
        ],
        out_specs=(pl.BlockSpec((c.shape[0], tn), lambda s: (0, s)),
                   pl.BlockSpec((2, hs, 2 * BLOCK, BLOCK), lambda s: (0, s, 0, 0))),
        compiler_params=_params(("arbitrary",)),
        name="prologue",
    )(c, w_mod, b_mod, rel_bias, bucket_t)


def _proj_kernel(x_ref, mod_ref, g_ref, win_ref, wt_ref, gq_ref, gkv_ref, wqm_ref, wqr_ref,
                 wuk_ref, wvt_ref, cos_ref, sin_ref, wo_ref, w1_ref, w2_ref,
                 qat_ref, ka_ref, vat_ref, qb_ref, kc_ref, vt_ref, wob_ref, w1b_ref, w2b_ref,
                 *, q_scale):
    wob_ref[...] = wo_ref[...].astype(BF16)
    w1b_ref[...] = w1_ref[...].astype(BF16)
    w2b_ref[...] = w2_ref[...].astype(BF16)

    mod = mod_ref[0]
    sh1 = mod[0:1]
    sc1 = mod[1:2]
    half_heads = MLA_HEADS // 2
    pad_rows = MLA_VT_ROWS - MLA_V_DIM
    ones_rows = jnp.where(lax.broadcasted_iota(jnp.int32, (pad_rows, VT_TILE), 0) == 0,
                          1.0, 0.0).astype(BF16)
    for hd in range(MLA_HEADS):
        for kt in range(vt_ref.shape[2]):
            vt_ref[0, hd, kt, MLA_V_DIM:] = ones_rows

    rows = x_ref.shape[0] // PROJ_ROW_CHUNKS
    assert VT_TILE % rows == 0
    for r in range(PROJ_ROW_CHUNKS):
        rs = slice(r * rows, (r + 1) * rows)
        h = ((_rms(x_ref[rs, :]) * g_ref[...]) * (1.0 + sc1) + sh1).astype(BF16)
        proj = jnp.dot(h, win_ref[...], preferred_element_type=F32)
        proj_t = lax.dot_general(wt_ref[...], h, (((1,), (1,)), ((), ())),
                                 preferred_element_type=F32)

        qat_ref[0, :, rs] = (proj_t[:SWA_Q_COLS] * (SWA_HEAD_DIM ** -0.5 * LOG2E)).astype(BF16)
        vat_ref[0, :, rs] = proj_t[SWA_Q_COLS:].astype(BF16)
        ka_ref[rs, :] = proj[:, :P_CQ].astype(BF16)

        cq = (_rms(proj[:, P_CQ:P_CKV]) * gq_ref[...]).astype(BF16)
        ckv = (_rms(proj[:, P_CKV:P_KR]) * gkv_ref[...]).astype(BF16)
        cos_lo = cos_ref[rs, :]
        sin_lo = sin_ref[rs, :]
        cos_hi = pltpu.roll(cos_lo, MLA_ROPE_DIM, 1)
        sin_hi = pltpu.roll(sin_lo, MLA_ROPE_DIM, 1)
        kr = proj[:, P_KR:P_COLS]
        kr_sw = pltpu.roll(kr, MLA_ROPE_DIM, 1)
        krope_lo = (kr * cos_lo + kr_sw * sin_lo).astype(BF16)
        krope_hi = (kr_sw * cos_hi + kr * sin_hi).astype(BF16)

        qmain = jnp.dot(cq, wqm_ref[...], preferred_element_type=F32)
        qrot = jnp.dot(cq, wqr_ref[...], preferred_element_type=F32)
        knope = jnp.dot(ckv, wuk_ref[...], preferred_element_type=F32)
        vt = lax.dot_general(wvt_ref[...], ckv, (((1,), (1,)), ((), ())),
                             preferred_element_type=F32)

        vt_tile = (r * rows) // VT_TILE
        vt_cols = slice((r * rows) % VT_TILE, (r * rows) % VT_TILE + rows)
        for hd in range(MLA_HEADS):
            lo = hd * MLA_QK_PAD
            mid = lo + MLA_NOPE_DIM
            hi = lo + MLA_QK_PAD
            low = hd < half_heads
            cos_t, sin_t = (cos_lo, sin_lo) if low else (cos_hi, sin_hi)
            rot = qrot[:, (hd % half_heads) * LANES:(hd % half_heads + 1) * LANES]
            qb_ref[rs, lo:mid] = (qmain[:, lo:mid] * q_scale).astype(BF16)
            qb_ref[rs, mid:hi] = ((qmain[:, mid:hi] * cos_t + rot * sin_t) * q_scale).astype(BF16)
            kc_ref[rs, lo:mid] = knope[:, hd * MLA_NOPE_DIM:(hd + 1) * MLA_NOPE_DIM].astype(BF16)
            kc_ref[rs, mid:hi] = krope_lo if low else krope_hi
            vt_ref[0, hd, vt_tile, :MLA_V_DIM, vt_cols] = (
                vt[hd * MLA_V_DIM:(hd + 1) * MLA_V_DIM].astype(BF16))


def _proj_call(x2d, mod3, g_attn, w_main, w_t, gq, gkv, wq_main, wq_rot, w_uk, w_vt,
               cos_tab, sin_tab, w_out, w_ff1, w_ff2, *, batch, seq):
    tm = TM_PROJ
    tiles_per_seq = seq // tm
    t = batch * seq
    steps = t // tm
    slab = lambda w: (w.shape[0] // steps, w.shape[1])
    assert all(w.shape[0] % (steps * BF16_ROWS) == 0 for w in (w_out, w_ff1, w_ff2))
    tok = lambda i: (i, 0)
    tile_t = lambda i: (i, 0, 0)
    pos = lambda i: (i % tiles_per_seq, 0)
    out_shape = (
        jax.ShapeDtypeStruct((t // tm, SWA_Q_COLS, tm), BF16),
        jax.ShapeDtypeStruct((t, SWA_KV_COLS), BF16),
        jax.ShapeDtypeStruct((t // tm, SWA_KV_COLS, tm), BF16),
        jax.ShapeDtypeStruct((t, MLA_HEADS * MLA_QK_PAD), BF16),
        jax.ShapeDtypeStruct((t, MLA_HEADS * MLA_QK_PAD), BF16),
        jax.ShapeDtypeStruct((batch, MLA_HEADS, seq // VT_TILE, MLA_VT_ROWS, VT_TILE), BF16),
        jax.ShapeDtypeStruct(w_out.shape, BF16),
        jax.ShapeDtypeStruct(w_ff1.shape, BF16),
        jax.ShapeDtypeStruct(w_ff2.shape, BF16),
    )
    return pl.pallas_call(
        functools.partial(_proj_kernel, q_scale=MLA_Q_SCALE),
        out_shape=out_shape,
        grid=(t // tm,),
        in_specs=[
            pl.BlockSpec((tm, D_MODEL), tok),
            pl.BlockSpec((1, MOD_PRE, D_MODEL), lambda i: (i // tiles_per_seq, 0, 0)),
            _const_spec((1, D_MODEL)),
            _const_spec((D_MODEL, P_COLS)),
            _const_spec((PT_ROWS, D_MODEL)),
            _const_spec((1, MLA_Q_RANK)),
            _const_spec((1, MLA_KV_RANK)),
            _const_spec((MLA_Q_RANK, MLA_HEADS * MLA_QK_PAD)),
            _const_spec((MLA_Q_RANK, MLA_HEADS // 2 * LANES)),
            _const_spec((MLA_KV_RANK, MLA_HEADS * MLA_NOPE_DIM)),
            _const_spec((MLA_HEADS * MLA_V_DIM, MLA_KV_RANK)),
            pl.BlockSpec((tm, LANES), pos),
            pl.BlockSpec((tm, LANES), pos),
            pl.BlockSpec(slab(w_out), tok),
            pl.BlockSpec(slab(w_ff1), tok),
            pl.BlockSpec(slab(w_ff2), tok),
        ],
        out_specs=(
            pl.BlockSpec((1, SWA_Q_COLS, tm), tile_t),
            pl.BlockSpec((tm, SWA_KV_COLS), tok),
            pl.BlockSpec((1, SWA_KV_COLS, tm), tile_t),
            pl.BlockSpec((tm, MLA_HEADS * MLA_QK_PAD), tok),
            pl.BlockSpec((tm, MLA_HEADS * MLA_QK_PAD), tok),
            pl.BlockSpec((1, MLA_HEADS, tm // VT_TILE, MLA_VT_ROWS, VT_TILE),
                         lambda i: (i // tiles_per_seq, 0, i % tiles_per_seq, 0, 0)),
            pl.BlockSpec(slab(w_out), tok),
            pl.BlockSpec(slab(w_ff1), tok),
            pl.BlockSpec(slab(w_ff2), tok),
        ),
        compiler_params=_params(("parallel",)),
        name="in_proj",
    )(x2d, mod3, g_attn, w_main, w_t, gq, gkv, wq_main, wq_rot, w_uk, w_vt,
      cos_tab, sin_tab, w_out, w_ff1, w_ff2)


def _swa_kernel(sinks_ref, qt_ref, kp_ref, kc_ref, vtp_ref, vtc_ref, bias_ref,
                c_ref, wmod_ref, bmod_ref, o_ref, mod_ref,
                s_ref, p_ref, sink_ref, *, blocks_per_step):
    mod_ref[...] = _mod_tile(c_ref, wmod_ref, bmod_ref)

    first_step = pl.program_id(1) == 0
    k_all = jnp.concatenate([kp_ref[...], kc_ref[...]], axis=0)
    vt_all = jnp.concatenate([vtp_ref[0], vtc_ref[0]], axis=1)
    uh = SWA_UNIT_HEADS
    zeros_q = jnp.zeros((SWA_HEAD_DIM, uh * BLOCK), BF16)
    ones_rows = jnp.where(lax.broadcasted_iota(jnp.int32, (BF16_ROWS, 2 * BLOCK), 0) == 0,
                          1.0, 0.0).astype(BF16)
    units = [(i, g, part) for i in range(blocks_per_step) for g in range(SWA_KV_HEADS)
             for part in range(SWA_GROUP // uh)]

    def stage_scores(u):
        i, g, part = units[u]
        h_first = g * SWA_GROUP + part * uh
        k_band = k_all[i * BLOCK:(i + 2) * BLOCK]
        qt = qt_ref[0, :, i * BLOCK:(i + 1) * BLOCK]
        q_g = jnp.concatenate(
            [qt[(h_first + hh) * SWA_HEAD_DIM:(h_first + hh + 1) * SWA_HEAD_DIM]
             for hh in range(uh)], axis=1)
        rhs = jnp.concatenate([q_g, zeros_q] if g == 0 else [zeros_q, q_g], axis=0)
        s_ref[u % 2] = jnp.dot(k_band, rhs, preferred_element_type=F32)

    def stage_softmax(u):
        i, g, part = units[u]
        variant = jnp.where(first_step, 0, 1) if i == 0 else 1
        for hh in range(uh):
            hd = g * SWA_GROUP + part * uh + hh
            st = s_ref[u % 2, :, hh * BLOCK:(hh + 1) * BLOCK] + bias_ref[variant, hd]
            sink = sinks_ref[hd] * LOG2E
            m = jnp.maximum(jnp.max(st, axis=0, keepdims=True), sink)
            sink_ref[u % 2, hh] = jnp.exp2(sink - m)
            p_ref[u % 2, :, hh * BLOCK:(hh + 1) * BLOCK] = jnp.exp2(st - m).astype(BF16)

    def stage_pv(u):
        i, g, part = units[u]
        vt_band = jnp.concatenate(
            [vt_all[g * SWA_HEAD_DIM:(g + 1) * SWA_HEAD_DIM, i * BLOCK:(i + 2) * BLOCK],
             ones_rows], axis=0)
        ot_g = jnp.dot(vt_band, p_ref[u % 2], preferred_element_type=F32)

        def head_out(hh):
            cols = slice(hh * BLOCK, (hh + 1) * BLOCK)
            denom = ot_g[SWA_HEAD_DIM:SWA_HEAD_DIM + 1, cols] + sink_ref[u % 2, hh]
            return ot_g[:SWA_HEAD_DIM, cols] * (1.0 / denom)

        for pair in range(uh // 2):
            h0 = 2 * pair
            slab = jnp.concatenate([head_out(h0), head_out(h0 + 1)], axis=0)
            col = (g * SWA_GROUP + part * uh + h0) * SWA_HEAD_DIM
            o_ref[i * BLOCK:(i + 1) * BLOCK, col:col + 2 * SWA_HEAD_DIM] = (
                slab.T.astype(BF16))

    n = len(units)
    for t in range(n + 2):
        if 0 <= t - 2 < n:
            stage_pv(t - 2)
        if 0 <= t - 1 < n:
            stage_softmax(t - 1)
        if t < n:
            stage_scores(t)


def _swa_call(sinks, qat, ka, vat, bias_tab, c, w_mod, b_mod, *, batch, seq):
    step = TM_SWA
    blocks_per_step = step // BLOCK
    steps_per_seq = seq // step
    nb = seq // BLOCK
    prev_blk = lambda b, s: b * nb + jnp.maximum(s * blocks_per_step - 1, 0)
    assert step == TM_PROJ
    tile = lambda b, s: (b * steps_per_seq + s, 0, 0)
    prev_tile_last_blk = lambda b, s: (b * steps_per_seq + jnp.maximum(s - 1, 0), 0,
                                       blocks_per_step - 1)
    n_steps = batch * steps_per_seq
    mod_cols = MOD_POST * D_MODEL
    tn = mod_cols // n_steps
    assert tn % LANES == 0 and (MOD_PRE * D_MODEL) % tn == 0
    mod_tile = lambda b, s: (0, b * steps_per_seq + s)
    mod_src_tile = lambda b, s: (0, MOD_PRE * D_MODEL // tn + b * steps_per_seq + s)
    return pl.pallas_call(
        functools.partial(_swa_kernel, blocks_per_step=blocks_per_step),
        out_shape=(jax.ShapeDtypeStruct((batch * seq, SWA_Q_COLS), BF16),
                   jax.ShapeDtypeStruct((c.shape[0], mod_cols), F32)),
        grid=(batch, steps_per_seq),
        in_specs=[
            pl.BlockSpec(memory_space=pltpu.SMEM),
            pl.BlockSpec((1, SWA_Q_COLS, step), tile),
            pl.BlockSpec((BLOCK, SWA_KV_COLS), lambda b, s: (prev_blk(b, s), 0)),
            pl.BlockSpec((step, SWA_KV_COLS), lambda b, s: (b * steps_per_seq + s, 0)),
            pl.BlockSpec((1, SWA_KV_COLS, BLOCK), prev_tile_last_blk),
            pl.BlockSpec((1, SWA_KV_COLS, step), tile),
            _const_spec((2, SWA_HEADS, 2 * BLOCK, BLOCK)),
            _const_spec(c.shape),
            pl.BlockSpec((D_MODEL, tn), mod_src_tile),
            pl.BlockSpec((1, tn), mod_src_tile),
        ],
        out_specs=(
            pl.BlockSpec((step, SWA_Q_COLS), lambda b, s: (b * steps_per_seq + s, 0)),
            pl.BlockSpec((c.shape[0], tn), mod_tile),
        ),
        scratch_shapes=[
            pltpu.VMEM((2, 2 * BLOCK, SWA_UNIT_HEADS * BLOCK), F32),
            pltpu.VMEM((2, 2 * BLOCK, SWA_UNIT_HEADS * BLOCK), BF16),
            pltpu.VMEM((2, SWA_UNIT_HEADS, 1, BLOCK), F32),
        ],
        compiler_params=_params(("parallel", "arbitrary")),
        name="swa",
    )(sinks, qat, ka, ka, vat, vat, bias_tab, c, w_mod, b_mod)


def _mla_kernel(q_ref, k_ref, vt_ref, o_ref, s_ref, p_ref, al_ref, m_ref, acc_ref,
                *, tile, n_tiles):
    pairs = [(qi, j) for qi in range(n_tiles) for j in range(qi + 1)]
    width = tile // MLA_COL_SPLIT

    def geometry(u, c):
        qi, j = pairs[u]
        cols = slice(c * width, (c + 1) * width)
        n_keys = (c + 1) * width if j == qi else tile
        return qi, j, cols, n_keys

    def stage_scores(u, c):
        qi, j, cols, n_keys = geometry(u, c)
        k = k_ref[j * tile:j * tile + n_keys, :]
        q = q_ref[qi * tile + c * width:qi * tile + (c + 1) * width, :]
        s_ref[u % MLA_SLOTS, :n_keys, cols] = lax.dot_general(
            k, q, (((1,), (1,)), ((), ())), preferred_element_type=F32)

    def stage_softmax(u, c):
        qi, j, cols, n_keys = geometry(u, c)
        st = s_ref[u % MLA_SLOTS, :n_keys, cols]
        if j == qi:
            kk = lax.broadcasted_iota(jnp.int32, st.shape, 0)
            qq = lax.broadcasted_iota(jnp.int32, st.shape, 1) + c * width
            st = jnp.where(kk <= qq, st, -jnp.inf)
        cmax = jnp.max(st, axis=0, keepdims=True)
        if j == 0:
            m_new = cmax
        else:
            m_old = m_ref[qi % 2, :, cols]
            m_new = jnp.maximum(m_old, cmax)
            al_ref[u % MLA_SLOTS, :, cols] = jnp.exp2(m_old - m_new)
        m_ref[qi % 2, :, cols] = m_new
        p_ref[u % MLA_SLOTS, :n_keys, cols] = jnp.exp2(st - m_new).astype(BF16)

    def stage_pv(u, c):
        qi, j, cols, n_keys = geometry(u, c)
        vt_tile = vt_ref.shape[-1]
        pv = None
        for lo in range(0, n_keys, vt_tile):
            nk = min(vt_tile, n_keys - lo)
            part = jnp.dot(vt_ref[0, 0, (j * tile + lo) // vt_tile, :, :nk],
                           p_ref[u % MLA_SLOTS, lo:lo + nk, cols], preferred_element_type=F32)
            pv = part if pv is None else pv + part
        if j == 0:
            acc = pv
        else:
            acc = al_ref[u % MLA_SLOTS, :, cols] * acc_ref[:, cols] + pv
        if j == qi:
            rows = slice(qi * tile + c * width, qi * tile + (c + 1) * width)
            denom = acc[MLA_V_DIM:MLA_V_DIM + 1]
            o_ref[rows, :] = (acc[:MLA_V_DIM] / denom).T.astype(BF16)
        else:
            acc_ref[:, cols] = acc

    n = len(pairs)
    skew = MLA_SLOTS - 1
    for t in range(n + 2 * skew):
        for c in range(MLA_COL_SPLIT):
            if 0 <= t - 2 * skew < n:
                stage_pv(t - 2 * skew, c)
            if 0 <= t - skew < n:
                stage_softmax(t - skew, c)
            if t < n:
                stage_scores(t, c)


def _mla_call(qb, kcat, vt, *, batch, seq):
    tile = MLA_TILE
    n_tiles = seq // tile
    return pl.pallas_call(
        functools.partial(_mla_kernel, tile=tile, n_tiles=n_tiles),
        out_shape=jax.ShapeDtypeStruct((batch * seq, MLA_HEADS * MLA_V_DIM), BF16),
        grid=(batch, MLA_HEADS),
        in_specs=[
            pl.BlockSpec((seq, MLA_QK_PAD), lambda b, h: (b, h)),
            pl.BlockSpec((seq, MLA_QK_PAD), lambda b, h: (b, h)),
            pl.BlockSpec((1, 1, seq // VT_TILE, MLA_VT_ROWS, VT_TILE), lambda b, h: (b, h, 0, 0, 0)),
        ],
        out_specs=pl.BlockSpec((seq, MLA_V_DIM), lambda b, h: (b, h)),
        scratch_shapes=[
            pltpu.VMEM((MLA_SLOTS, tile, tile), F32),
            pltpu.VMEM((MLA_SLOTS, tile, tile), BF16),
            pltpu.VMEM((MLA_SLOTS, 1, tile), F32),
            pltpu.VMEM((2, 1, tile), F32),
            pltpu.VMEM((MLA_VT_ROWS, tile), F32),
        ],
        compiler_params=_params(("parallel", "parallel")),
        name="mla",
    )(qb, kcat, vt)


def _out_kernel(x_ref, oa_ref, ob_ref, mod_ref, wa_ref, wb_ref, x1_ref):
    g1 = mod_ref[0][0:1]
    y = (jnp.dot(oa_ref[...], wa_ref[...], preferred_element_type=F32)
         + jnp.dot(ob_ref[...], wb_ref[...], preferred_element_type=F32))
    x1_ref[...] = x_ref[...] + g1 * y


def _out_call(x2d, oa, ob, mod3, w_out, *, seq):
    tm = TM_OUT
    t = x2d.shape[0]
    tiles_per_seq = seq // tm
    tok = lambda i: (i, 0)
    w_half = lambda half: pl.BlockSpec((SWA_Q_COLS, D_MODEL), lambda i: (half, 0),
                                       pipeline_mode=pl.Buffered(1))
    assert w_out.shape[0] == 2 * SWA_Q_COLS
    return pl.pallas_call(
        _out_kernel,
        out_shape=jax.ShapeDtypeStruct((t, D_MODEL), F32),
        grid=(t // tm,),
        in_specs=[
            pl.BlockSpec((tm, D_MODEL), tok),
            pl.BlockSpec((tm, SWA_Q_COLS), tok),
            pl.BlockSpec((tm, MLA_HEADS * MLA_V_DIM), tok),
            pl.BlockSpec((1, MOD_POST, D_MODEL), lambda i: (i // tiles_per_seq, 0, 0)),
            w_half(0),
            w_half(1),
        ],
        out_specs=pl.BlockSpec((tm, D_MODEL), tok),
        compiler_params=_params(("parallel",)),
        name="out_proj",
    )(x2d, oa, ob, mod3, w_out, w_out)


def _mlp_kernel(x_ref, mod_ref, g_ref, gf_ref, w1_hbm, w2_hbm, o_ref,
                h_ref, w1_buf, w2_buf, sem, *, tf, n_ff):
    i = pl.program_id(0)
    n_steps = pl.num_programs(0)
    mod = mod_ref[0]
    rows = o_ref.shape[0] // ROW_CHUNKS
    chunks = [slice(r * rows, (r + 1) * rows) for r in range(ROW_CHUNKS)]

    def copies(t):
        slot = t % 2
        return (
            pltpu.make_async_copy(w1_hbm.at[:, pl.ds(t * tf, tf)], w1_buf.at[slot],
                                  sem.at[0, slot]),
            pltpu.make_async_copy(w2_hbm.at[pl.ds(t * tf, tf), :], w2_buf.at[slot],
                                  sem.at[1, slot]),
        )

    def start(t):
        for cp in copies(t):
            cp.start()

    def wait(t):
        for cp in copies(t):
            cp.wait()

    def ff_tile(h, t):
        u = jnp.maximum(jnp.dot(h, w1_buf[t % 2], preferred_element_type=F32), 0.0)
        return jnp.dot((u * u).astype(BF16), w2_buf[t % 2], preferred_element_type=F32)

    @pl.when(i == 0)
    def _():
        start(0)

    for t in range(n_ff):
        if t + 1 < n_ff:
            start(t + 1)
        else:
            @pl.when(i + 1 < n_steps)
            def _():
                start(0)
        wait(t)
        if t == 0:
            sh2 = mod[1:2]
            sc2 = mod[2:3]
            for sl in chunks:
                h = ((_rms(x_ref[sl, :]) * g_ref[...]) * (1.0 + sc2) + sh2).astype(BF16)
                h_ref[sl, :] = h
                o_ref[sl, :] = ff_tile(h, t)
        elif t + 1 < n_ff:
            o_ref[...] += ff_tile(h_ref[...], t)
        else:
            g2 = mod[3:4]
            for sl in chunks:
                x2 = x_ref[sl, :] + g2 * (o_ref[sl, :] + ff_tile(h_ref[sl, :], t))
                o_ref[sl, :] = _rms(x2) * gf_ref[...]


def _mlp_call(x1, mod3, g_mlp, g_final, w1, w2, *, seq):
    tm = TM_MLP
    tf = TF_MLP
    n_ff = D_FF // tf
    assert n_ff >= 2
    t = x1.shape[0]
    tiles_per_seq = seq // tm
    return pl.pallas_call(
        functools.partial(_mlp_kernel, tf=tf, n_ff=n_ff),
        out_shape=jax.ShapeDtypeStruct((t, D_MODEL), F32),
        grid=(t // tm,),
        in_specs=[
            pl.BlockSpec((tm, D_MODEL), lambda i: (i, 0)),
            pl.BlockSpec((1, MOD_POST, D_MODEL), lambda i: (i // tiles_per_seq, 0, 0)),
            _const_spec((1, D_MODEL)),
            _const_spec((1, D_MODEL)),
            pl.BlockSpec(memory_space=pl.ANY),
            pl.BlockSpec(memory_space=pl.ANY),
        ],
        out_specs=pl.BlockSpec((tm, D_MODEL), lambda i: (i, 0)),
        scratch_shapes=[
            pltpu.VMEM((tm, D_MODEL), BF16),
            pltpu.VMEM((2, D_MODEL, tf), BF16),
            pltpu.VMEM((2, tf, D_MODEL), BF16),
            pltpu.SemaphoreType.DMA((2, 2)),
        ],
        compiler_params=_params(("arbitrary",)),
        name="mlp",
    )(x1, mod3, g_mlp, g_final, w1, w2)


def _tail_kernel(x_ref, oa_ref, ob_ref, mod_ref, g_ref, gf_ref, wo_hbm, w1_hbm, w2_hbm, o_ref,
                 x1_ref, h_ref, w1_buf, w2_buf, sem, *, tf, n_ff):
    i = pl.program_id(0)
    n_steps = pl.num_programs(0)
    n_ph = n_ff + 1
    mod = mod_ref[0]
    half = SWA_Q_COLS
    rows = o_ref.shape[0] // ROW_CHUNKS
    chunks = [slice(r * rows, (r + 1) * rows) for r in range(ROW_CHUNKS)]

    def copies(p, slot):
        if p == 0:
            return (
                pltpu.make_async_copy(wo_hbm.at[pl.ds(0, half), :],
                                      w1_buf.at[slot, pl.ds(0, half), :], sem.at[0, slot]),
                pltpu.make_async_copy(wo_hbm.at[pl.ds(half, half), :],
                                      w2_buf.at[slot, pl.ds(0, half), :], sem.at[1, slot]),
            )
        t = p - 1
        return (
            pltpu.make_async_copy(w1_hbm.at[:, pl.ds(t * tf, tf)], w1_buf.at[slot],
                                  sem.at[0, slot]),
            pltpu.make_async_copy(w2_hbm.at[pl.ds(t * tf, tf), :], w2_buf.at[slot],
                                  sem.at[1, slot]),
        )

    def start(p, slot):
        for cp in copies(p, slot):
            cp.start()

    def wait(p, slot):
        for cp in copies(p, slot):
            cp.wait()

    def ff_tile(h, slot):
        u = jnp.maximum(jnp.dot(h, w1_buf[slot], preferred_element_type=F32), 0.0)
        return jnp.dot((u * u).astype(BF16), w2_buf[slot], preferred_element_type=F32)

    slot_of = lambda p: lax.rem(i + p, 2)

    @pl.when(i == 0)
    def _():
        start(0, slot_of(0))

    for p in range(n_ph):
        slot = slot_of(p)
        if p + 1 < n_ph:
            start(p + 1, slot_of(p + 1))
        else:
            @pl.when(i + 1 < n_steps)
            def _():
                start(0, lax.rem(i + 1, 2))
        wait(p, slot)
        if p == 0:
            g1 = mod[0:1]
            y = (jnp.dot(oa_ref[...], w1_buf[slot, :half, :], preferred_element_type=F32)
                 + jnp.dot(ob_ref[...], w2_buf[slot, :half, :], preferred_element_type=F32))
            x1_ref[...] = x_ref[...] + g1 * y
        elif p == 1:
            sh2 = mod[1:2]
            sc2 = mod[2:3]
            for sl in chunks:
                h = ((_rms(x1_ref[sl, :]) * g_ref[...]) * (1.0 + sc2) + sh2).astype(BF16)
                h_ref[sl, :] = h
                o_ref[sl, :] = ff_tile(h, slot)
        elif p + 1 < n_ph:
            o_ref[...] += ff_tile(h_ref[...], slot)
        else:
            g2 = mod[3:4]
            for sl in chunks:
                x2 = x1_ref[sl, :] + g2 * (o_ref[sl, :] + ff_tile(h_ref[sl, :], slot))
                o_ref[sl, :] = _rms(x2) * gf_ref[...]


def _tail_call(x2d, oa, ob, mod3, g_mlp, g_final, w_out, w1, w2, *, seq):
    tm = TM_MLP
    tf = TF_MLP
    n_ff = D_FF // tf
    assert n_ff >= 2 and tf == D_MODEL and w_out.shape == (2 * SWA_Q_COLS, D_MODEL)
    t = x2d.shape[0]
    tiles_per_seq = seq // tm
    tok = lambda i: (i, 0)
    return pl.pallas_call(
        functools.partial(_tail_kernel, tf=tf, n_ff=n_ff),
        out_shape=jax.ShapeDtypeStruct((t, D_MODEL), F32),
        grid=(t // tm,),
        in_specs=[
            pl.BlockSpec((tm, D_MODEL), tok),
            pl.BlockSpec((tm, SWA_Q_COLS), tok),
            pl.BlockSpec((tm, MLA_HEADS * MLA_V_DIM), tok),
            pl.BlockSpec((1, MOD_POST, D_MODEL), lambda i: (i // tiles_per_seq, 0, 0)),
            _const_spec((1, D_MODEL)),
            _const_spec((1, D_MODEL)),
            pl.BlockSpec(memory_space=pl.ANY),
            pl.BlockSpec(memory_space=pl.ANY),
            pl.BlockSpec(memory_space=pl.ANY),
        ],
        out_specs=pl.BlockSpec((tm, D_MODEL), tok),
        scratch_shapes=[
            pltpu.VMEM((tm, D_MODEL), F32),
            pltpu.VMEM((tm, D_MODEL), BF16),
            pltpu.VMEM((2, D_MODEL, tf), BF16),
            pltpu.VMEM((2, tf, D_MODEL), BF16),
            pltpu.SemaphoreType.DMA((2, 2)),
        ],
        compiler_params=_params(("arbitrary",)),
        name="tail",
    )(x2d, oa, ob, mod3, g_mlp, g_final, w_out, w1, w2)


def _t5_bucket_table():
    q_loc = np.arange(BLOCK)[:, None]
    k_loc = np.arange(2 * BLOCK)[None, :]
    n = np.maximum(q_loc + BLOCK - k_loc, 0)
    max_exact = REL_BUCKETS // 2
    nf = np.maximum(n, 1).astype(np.float64)
    large = max_exact + (np.log(nf / max_exact) / math.log(REL_MAX_DIST / max_exact)
                         * (REL_BUCKETS - max_exact)).astype(np.int32)
    large = np.minimum(large, REL_BUCKETS - 1)
    return np.where(n < max_exact, n, large).astype(np.int32)


def _rope_tables(seq):
    half = MLA_ROPE_DIM // 2
    inv_freq = ROPE_THETA ** (-np.arange(half, dtype=np.float64) / half)
    ang = np.arange(seq, dtype=np.float64)[:, None] * inv_freq[None, :]
    zeros = np.zeros((seq, LANES - MLA_ROPE_DIM))
    cos_tab = np.concatenate([np.cos(ang), np.cos(ang), zeros], axis=1)
    sin_tab = np.concatenate([np.sin(ang), np.sin(ang), zeros], axis=1)
    return cos_tab.astype(np.float32), sin_tab.astype(np.float32)


def _rot_cols(w):
    half = w.shape[-1] // 2
    return jnp.concatenate([-w[..., half:], w[..., :half]], axis=-1)


def kernel(x, c, w_mod, b_mod, attn_norm_g, w_in, swa_sinks, rel_bias, mla_q_norm_g, w_uq,
           mla_kv_norm_g, w_ukv, w_out, mlp_norm_g, w_ff1, w_ff2, final_norm_g):
    batch, seq, _ = x.shape
    depth = w_mod.shape[0]
    assert depth == 1
    t = batch * seq
    x2d = x.reshape(t, D_MODEL)
    l = 0

    w_kr = w_in[l][:, OFF_MLA_KR:OFF_MLA_KR + MLA_ROPE_DIM]
    w_main = jnp.concatenate(
        [w_in[l][:, OFF_SWA_K:OFF_SWA_V], w_in[l][:, OFF_MLA_CQ:OFF_MLA_KR],
         w_kr, _rot_cols(w_kr)], axis=1).astype(BF16)
    w_t = jnp.concatenate(
        [w_in[l][:, :SWA_Q_COLS], w_in[l][:, OFF_SWA_V:OFF_MLA_CQ]], axis=1).T.astype(BF16)

    wq = w_uq[l].reshape(MLA_Q_RANK, MLA_HEADS, MLA_QK_DIM)
    wq_nope = wq[..., :MLA_NOPE_DIM]
    wq_rope = wq[..., MLA_NOPE_DIM:]
    hh = MLA_HEADS // 2
    zq = jnp.zeros((MLA_Q_RANK, hh, LANES - MLA_ROPE_DIM), F32)
    wq_main = jnp.concatenate(
        [jnp.concatenate([wq_nope[:, :hh], wq_rope[:, :hh], zq], axis=-1),
         jnp.concatenate([wq_nope[:, hh:], zq, wq_rope[:, hh:]], axis=-1)], axis=1).reshape(
        MLA_Q_RANK, MLA_HEADS * MLA_QK_PAD).astype(BF16)
    wq_rot_all = _rot_cols(wq_rope)
    wq_rot = jnp.concatenate([wq_rot_all[:, :hh], wq_rot_all[:, hh:]], axis=-1).reshape(
        MLA_Q_RANK, hh * LANES).astype(BF16)

    wkv = w_ukv[l].reshape(MLA_KV_RANK, MLA_HEADS, MLA_NOPE_DIM + MLA_V_DIM)
    w_uk = wkv[..., :MLA_NOPE_DIM].reshape(MLA_KV_RANK, MLA_HEADS * MLA_NOPE_DIM).astype(BF16)
    w_vt = wkv[..., MLA_NOPE_DIM:].reshape(MLA_KV_RANK, MLA_HEADS * MLA_V_DIM).T.astype(BF16)

    cos_tab, sin_tab = _rope_tables(seq)

    assert w_mod.shape[2] == (MOD_PRE + MOD_POST) * D_MODEL
    b_mod2d = b_mod[l].reshape(1, -1)
    mod_pre, bias_tab = _prologue_call(c, w_mod[l], b_mod2d, rel_bias, _t5_bucket_table().T)
    mod_pre = mod_pre.reshape(batch, MOD_PRE, D_MODEL)

    qat, ka, vat, qb, kcat, vt, w_out_b16, w1, w2 = _proj_call(
        x2d, mod_pre, attn_norm_g[l].reshape(1, -1), w_main, w_t,
        mla_q_norm_g[l].reshape(1, -1), mla_kv_norm_g[l].reshape(1, -1),
        wq_main, wq_rot, w_uk, w_vt, cos_tab, sin_tab, w_out[l], w_ff1[l], w_ff2[l],
        batch=batch, seq=seq)

    oa, mod_post = _swa_call(swa_sinks[l], qat, ka, vat, bias_tab, c, w_mod[l], b_mod2d,
                             batch=batch, seq=seq)
    mod_post = mod_post.reshape(batch, MOD_POST, D_MODEL)
    ob = _mla_call(qb, kcat, vt, batch=batch, seq=seq)

    out = _tail_call(x2d, oa, ob, mod_post, mlp_norm_g[l].reshape(1, -1),
                     final_norm_g.reshape(1, -1), w_out_b16, w1, w2, seq=seq)
    return out.reshape(batch, seq, D_MODEL)
```

```python
import functools
import math

import jax
import jax.numpy as jnp
import numpy as np
from jax import lax
from jax.experimental import pallas as pl
from jax.experimental.pallas import tpu as pltpu

F32 = jnp.float32
BF16 = jnp.bfloat16

LANES = 128
BF16_ROWS = 16

D_MODEL = 2048
BLOCK = 128
EPS = 1e-6

SWA_HEADS = 16
SWA_KV_HEADS = 2
SWA_HEAD_DIM = 64
SWA_GROUP = SWA_HEADS // SWA_KV_HEADS
WINDOW = 128
REL_BUCKETS = 32
REL_MAX_DIST = 128

MLA_HEADS = 8
MLA_Q_RANK = 384
MLA_KV_RANK = 128
MLA_NOPE_DIM = 128
MLA_ROPE_DIM = 64
MLA_V_DIM = 128
MLA_VT_ROWS = MLA_V_DIM + BF16_ROWS
MLA_QK_DIM = MLA_NOPE_DIM + MLA_ROPE_DIM
MLA_QK_PAD = 256
ROPE_THETA = 10000.0
MLA_Q_SCALE = MLA_QK_DIM ** -0.5 * math.log2(math.e)
D_FF = 4 * D_MODEL

SWA_Q_COLS = SWA_HEADS * SWA_HEAD_DIM
SWA_KV_COLS = SWA_KV_HEADS * SWA_HEAD_DIM
OFF_SWA_K = SWA_Q_COLS
OFF_SWA_V = OFF_SWA_K + SWA_KV_COLS
OFF_MLA_CQ = OFF_SWA_V + SWA_KV_COLS
OFF_MLA_CKV = OFF_MLA_CQ + MLA_Q_RANK
OFF_MLA_KR = OFF_MLA_CKV + MLA_KV_RANK
P_CQ = SWA_KV_COLS
P_CKV = P_CQ + MLA_Q_RANK
P_KR = P_CKV + MLA_KV_RANK
P_COLS = P_KR + LANES
PT_ROWS = SWA_Q_COLS + SWA_KV_COLS
LOG2E = math.log2(math.e)

VMEM_LIMIT_BYTES = 62 * 1024 * 1024

TM_PROJ = 512
MLA_TILE = 512
VT_TILE = min(MLA_TILE, TM_PROJ)
TM_SWA = 512
TM_OUT = 1024
TM_MLP = 512
TF_MLP = 2048
MOD_PRE = 2
MOD_POST = 4
BIAS_HEADS_PER_STEP = 4
SWA_UNIT_HEADS = 4
MLA_SLOTS = 2
MLA_COL_SPLIT = 2
ROW_CHUNKS = 2
PROJ_ROW_CHUNKS = 2


def _params(sem):
    return pltpu.CompilerParams(dimension_semantics=sem, vmem_limit_bytes=VMEM_LIMIT_BYTES)


def _const_spec(shape):
    nd = len(shape)
    return pl.BlockSpec(shape, lambda *_: (0,) * nd, pipeline_mode=pl.Buffered(1))


def _rms(x):
    return x * lax.rsqrt(jnp.mean(x * x, axis=-1, keepdims=True) + EPS)


def _mod_tile(c_ref, w_ref, b_ref):
    c = c_ref[...]
    ca = c * (1.0 / (1.0 + jnp.exp(-c)))
    return jnp.dot(ca.astype(BF16), w_ref[...].astype(BF16),
                   preferred_element_type=F32) + b_ref[...]


def _prologue_kernel(c_ref, w_ref, b_ref, rel_ref, bucket_ref, mod_ref, o_ref):
    mod_ref[...] = _mod_tile(c_ref, w_ref, b_ref)

    bucket = bucket_ref[...]
    k_loc = lax.broadcasted_iota(jnp.int32, bucket.shape, 0)
    q_loc = lax.broadcasted_iota(jnp.int32, bucket.shape, 1)
    dist = q_loc + BLOCK - k_loc
    in_window = (dist >= 0) & (dist < WINDOW)
    for i in range(BIAS_HEADS_PER_STEP):
        h = pl.program_id(0) * BIAS_HEADS_PER_STEP + i
        acc = jnp.zeros(bucket.shape, F32)
        for k in range(REL_BUCKETS):
            acc = jnp.where(bucket == k, rel_ref[k, h], acc)
        acc = acc * LOG2E
        o_ref[0, i] = jnp.where(in_window & (k_loc >= BLOCK), acc, -jnp.inf)
        o_ref[1, i] = jnp.where(in_window, acc, -jnp.inf)


def _prologue_call(c, w_mod, b_mod, rel_bias, bucket_t):
    hs = BIAS_HEADS_PER_STEP
    steps = SWA_HEADS // hs
    n = MOD_PRE * D_MODEL
    tn = n // steps
    assert tn % LANES == 0
    return pl.pallas_call(
        _prologue_kernel,
        out_shape=(jax.ShapeDtypeStruct((c.shape[0], n), F32),
                   jax.ShapeDtypeStruct((2, SWA_HEADS, 2 * BLOCK, BLOCK), F32)),
        grid=(steps,),
        in_specs=[
            pl.BlockSpec(c.shape, lambda s: (0, 0)),
            pl.BlockSpec((D_MODEL, tn), lambda s: (0, s)),
            pl.BlockSpec((1, tn), lambda s: (0, s)),
            pl.BlockSpec(memory_space=pltpu.SMEM),
            pl.BlockSpec((2 * BLOCK, BLOCK), lambda s: (0, 0)),
        ],
        out_specs=(pl.BlockSpec((c.shape[0], tn), lambda s: (0, s)),
                   pl.BlockSpec((2, hs, 2 * BLOCK, BLOCK), lambda s: (0, s, 0, 0))),
        compiler_params=_params(("arbitrary",)),
        name="prologue",
    )(c, w_mod, b_mod, rel_bias, bucket_t)


def _proj_kernel(x_ref, mod_ref, g_ref, win_ref, wt_ref, gq_ref, gkv_ref, wqm_ref, wqr_ref,
                 wuk_ref, wvt_ref, cos_ref, sin_ref, wo_ref, w1_ref, w2_ref,
                 qat_ref, ka_ref, vat_ref, qb_ref, kc_ref, vt_ref, wob_ref, w1b_ref, w2b_ref,
                 *, q_scale):
    wob_ref[...] = wo_ref[...].astype(BF16)
    w1b_ref[...] = w1_ref[...].astype(BF16)
    w2b_ref[...] = w2_ref[...].astype(BF16)

    mod = mod_ref[0]
    sh1 = mod[0:1]
    sc1 = mod[1:2]
    half_heads = MLA_HEADS // 2
    pad_rows = MLA_VT_ROWS - MLA_V_DIM
    ones_rows = jnp.where(lax.broadcasted_iota(jnp.int32, (pad_rows, VT_TILE), 0) == 0,
                          1.0, 0.0).astype(BF16)
    for hd in range(MLA_HEADS):
        for kt in range(vt_ref.shape[2]):
            vt_ref[0, hd, kt, MLA_V_DIM:] = ones_rows

    rows = x_ref.shape[0] // PROJ_ROW_CHUNKS
    assert VT_TILE % rows == 0
    for r in range(PROJ_ROW_CHUNKS):
        rs = slice(r * rows, (r + 1) * rows)
        h = ((_rms(x_ref[rs, :]) * g_ref[...]) * (1.0 + sc1) + sh1).astype(BF16)
        proj = jnp.dot(h, win_ref[...], preferred_element_type=F32)
        proj_t = lax.dot_general(wt_ref[...], h, (((1,), (1,)), ((), ())),
                                 preferred_element_type=F32)

        qat_ref[0, :, rs] = (proj_t[:SWA_Q_COLS] * (SWA_HEAD_DIM ** -0.5 * LOG2E)).astype(BF16)
        vat_ref[0, :, rs] = proj_t[SWA_Q_COLS:].astype(BF16)
        ka_ref[rs, :] = proj[:, :P_CQ].astype(BF16)

        cq = (_rms(proj[:, P_CQ:P_CKV]) * gq_ref[...]).astype(BF16)
        ckv = (_rms(proj[:, P_CKV:P_KR]) * gkv_ref[...]).astype(BF16)
        cos_lo = cos_ref[rs, :]
        sin_lo = sin_ref[rs, :]
        cos_hi = pltpu.roll(cos_lo, MLA_ROPE_DIM, 1)
        sin_hi = pltpu.roll(sin_lo, MLA_ROPE_DIM, 1)
        kr = proj[:, P_KR:P_COLS]
        kr_sw = pltpu.roll(kr, MLA_ROPE_DIM, 1)
        krope_lo = (kr * cos_lo + kr_sw * sin_lo).astype(BF16)
        krope_hi = (kr_sw * cos_hi + kr * sin_hi).astype(BF16)

        qmain = jnp.dot(cq, wqm_ref[...], preferred_element_type=F32)
        qrot = jnp.dot(cq, wqr_ref[...], preferred_element_type=F32)
        knope = jnp.dot(ckv, wuk_ref[...], preferred_element_type=F32)
        vt = lax.dot_general(wvt_ref[...], ckv, (((1,), (1,)), ((), ())),
                             preferred_element_type=F32)

        vt_tile = (r * rows) // VT_TILE
        vt_cols = slice((r * rows) % VT_TILE, (r * rows) % VT_TILE + rows)
        for hd in range(MLA_HEADS):
            lo = hd * MLA_QK_PAD
            mid = lo + MLA_NOPE_DIM
            hi = lo + MLA_QK_PAD
            low = hd < half_heads
            cos_t, sin_t = (cos_lo, sin_lo) if low else (cos_hi, sin_hi)
            rot = qrot[:, (hd % half_heads) * LANES:(hd % half_heads + 1) * LANES]
            qb_ref[rs, lo:mid] = (qmain[:, lo:mid] * q_scale).astype(BF16)
            qb_ref[rs, mid:hi] = ((qmain[:, mid:hi] * cos_t + rot * sin_t) * q_scale).astype(BF16)
            kc_ref[rs, lo:mid] = knope[:, hd * MLA_NOPE_DIM:(hd + 1) * MLA_NOPE_DIM].astype(BF16)
            kc_ref[rs, mid:hi] = krope_lo if low else krope_hi
            vt_ref[0, hd, vt_tile, :MLA_V_DIM, vt_cols] = (
                vt[hd * MLA_V_DIM:(hd + 1) * MLA_V_DIM].astype(BF16))


def _proj_call(x2d, mod3, g_attn, w_main, w_t, gq, gkv, wq_main, wq_rot, w_uk, w_vt,
               cos_tab, sin_tab, w_out, w_ff1, w_ff2, *, batch, seq):
    tm = TM_PROJ
    tiles_per_seq = seq // tm
    t = batch * seq
    steps = t // tm
    slab = lambda w: (w.shape[0] // steps, w.shape[1])
    assert all(w.shape[0] % (steps * BF16_ROWS) == 0 for w in (w_out, w_ff1, w_ff2))
    tok = lambda i: (i, 0)
    tile_t = lambda i: (i, 0, 0)
    pos = lambda i: (i % tiles_per_seq, 0)
    out_shape = (
        jax.ShapeDtypeStruct((t // tm, SWA_Q_COLS, tm), BF16),
        jax.ShapeDtypeStruct((t, SWA_KV_COLS), BF16),
        jax.ShapeDtypeStruct((t // tm, SWA_KV_COLS, tm), BF16),
        jax.ShapeDtypeStruct((t, MLA_HEADS * MLA_QK_PAD), BF16),
        jax.ShapeDtypeStruct((t, MLA_HEADS * MLA_QK_PAD), BF16),
        jax.ShapeDtypeStruct((batch, MLA_HEADS, seq // VT_TILE, MLA_VT_ROWS, VT_TILE), BF16),
        jax.ShapeDtypeStruct(w_out.shape, BF16),
        jax.ShapeDtypeStruct(w_ff1.shape, BF16),
        jax.ShapeDtypeStruct(w_ff2.shape, BF16),
    )
    return pl.pallas_call(
        functools.partial(_proj_kernel, q_scale=MLA_Q_SCALE),
        out_shape=out_shape,
        grid=(t // tm,),
        in_specs=[
            pl.BlockSpec((tm, D_MODEL), tok),
            pl.BlockSpec((1, MOD_PRE, D_MODEL), lambda i: (i // tiles_per_seq, 0, 0)),
            _const_spec((1, D_MODEL)),
            _const_spec((D_MODEL, P_COLS)),
            _const_spec((PT_ROWS, D_MODEL)),
            _const_spec((1, MLA_Q_RANK)),
            _const_spec((1, MLA_KV_RANK)),
            _const_spec((MLA_Q_RANK, MLA_HEADS * MLA_QK_PAD)),
            _const_spec((MLA_Q_RANK, MLA_HEADS // 2 * LANES)),
            _const_spec((MLA_KV_RANK, MLA_HEADS * MLA_NOPE_DIM)),
            _const_spec((MLA_HEADS * MLA_V_DIM, MLA_KV_RANK)),
            pl.BlockSpec((tm, LANES), pos),
            pl.BlockSpec((tm, LANES), pos),
            pl.BlockSpec(slab(w_out), tok),
            pl.BlockSpec(slab(w_ff1), tok),
            pl.BlockSpec(slab(w_ff2), tok),
        ],
        out_specs=(
            pl.BlockSpec((1, SWA_Q_COLS, tm), tile_t),
            pl.BlockSpec((tm, SWA_KV_COLS), tok),
            pl.BlockSpec((1, SWA_KV_COLS, tm), tile_t),
            pl.BlockSpec((tm, MLA_HEADS * MLA_QK_PAD), tok),
            pl.BlockSpec((tm, MLA_HEADS * MLA_QK_PAD), tok),
            pl.BlockSpec((1, MLA_HEADS, tm // VT_TILE, MLA_VT_ROWS, VT_TILE),
                         lambda i: (i // tiles_per_seq, 0, i % tiles_per_seq, 0, 0)),
            pl.BlockSpec(slab(w_out), tok),
            pl.BlockSpec(slab(w_ff1), tok),
            pl.BlockSpec(slab(w_ff2), tok),
        ),
        compiler_params=_params(("parallel",)),
        name="in_proj",
    )(x2d, mod3, g_attn, w_main, w_t, gq, gkv, wq_main, wq_rot, w_uk, w_vt,
      cos_tab, sin_tab, w_out, w_ff1, w_ff2)


def _swa_kernel(sinks_ref, qt_ref, kp_ref, kc_ref, vtp_ref, vtc_ref, bias_ref,
                c_ref, wmod_ref, bmod_ref, o_ref, mod_ref,
                s_ref, p_ref, sink_ref, *, blocks_per_step):
    mod_ref[...] = _mod_tile(c_ref, wmod_ref, bmod_ref)

    first_step = pl.program_id(1) == 0
    k_all = jnp.concatenate([kp_ref[...], kc_ref[...]], axis=0)
    vt_all = jnp.concatenate([vtp_ref[0], vtc_ref[0]], axis=1)
    uh = SWA_UNIT_HEADS
    zeros_q = jnp.zeros((SWA_HEAD_DIM, uh * BLOCK), BF16)
    ones_rows = jnp.where(lax.broadcasted_iota(jnp.int32, (BF16_ROWS, 2 * BLOCK), 0) == 0,
                          1.0, 0.0).astype(BF16)
    units = [(i, g, part) for i in range(blocks_per_step) for g in range(SWA_KV_HEADS)
             for part in range(SWA_GROUP // uh)]

    def stage_scores(u):
        i, g, part = units[u]
        h_first = g * SWA_GROUP + part * uh
        k_band = k_all[i * BLOCK:(i + 2) * BLOCK]
        qt = qt_ref[0, :, i * BLOCK:(i + 1) * BLOCK]
        q_g = jnp.concatenate(
            [qt[(h_first + hh) * SWA_HEAD_DIM:(h_first + hh + 1) * SWA_HEAD_DIM]
             for hh in range(uh)], axis=1)
        rhs = jnp.concatenate([q_g, zeros_q] if g == 0 else [zeros_q, q_g], axis=0)
        s_ref[u % 2] = jnp.dot(k_band, rhs, preferred_element_type=F32)

    def stage_softmax(u):
        i, g, part = units[u]
        variant = jnp.where(first_step, 0, 1) if i == 0 else 1
        for hh in range(uh):
            hd = g * SWA_GROUP + part * uh + hh
            st = s_ref[u % 2, :, hh * BLOCK:(hh + 1) * BLOCK] + bias_ref[variant, hd]
            sink = sinks_ref[hd] * LOG2E
            m = jnp.maximum(jnp.max(st, axis=0, keepdims=True), sink)
            sink_ref[u % 2, hh] = jnp.exp2(sink - m)
            p_ref[u % 2, :, hh * BLOCK:(hh + 1) * BLOCK] = jnp.exp2(st - m).astype(BF16)

    def stage_pv(u):
        i, g, part = units[u]
        vt_band = jnp.concatenate(
            [vt_all[g * SWA_HEAD_DIM:(g + 1) * SWA_HEAD_DIM, i * BLOCK:(i + 2) * BLOCK],
             ones_rows], axis=0)
        ot_g = jnp.dot(vt_band, p_ref[u % 2], preferred_element_type=F32)

        def head_out(hh):
            cols = slice(hh * BLOCK, (hh + 1) * BLOCK)
            denom = ot_g[SWA_HEAD_DIM:SWA_HEAD_DIM + 1, cols] + sink_ref[u % 2, hh]
            return ot_g[:SWA_HEAD_DIM, cols] * (1.0 / denom)

        for pair in range(uh // 2):
            h0 = 2 * pair
            slab = jnp.concatenate([head_out(h0), head_out(h0 + 1)], axis=0)
            col = (g * SWA_GROUP + part * uh + h0) * SWA_HEAD_DIM
            o_ref[i * BLOCK:(i + 1) * BLOCK, col:col + 2 * SWA_HEAD_DIM] = (
                slab.T.astype(BF16))

    n = len(units)
    for t in range(n + 2):
        if 0 <= t - 2 < n:
            stage_pv(t - 2)
        if 0 <= t - 1 < n:
            stage_softmax(t - 1)
        if t < n:
            stage_scores(t)


def _swa_call(sinks, qat, ka, vat, bias_tab, c, w_mod, b_mod, *, batch, seq):
    step = TM_SWA
    blocks_per_step = step // BLOCK
    steps_per_seq = seq // step
    nb = seq // BLOCK
    prev_blk = lambda b, s: b * nb + jnp.maximum(s * blocks_per_step - 1, 0)
    assert step == TM_PROJ
    tile = lambda b, s: (b * steps_per_seq + s, 0, 0)
    prev_tile_last_blk = lambda b, s: (b * steps_per_seq + jnp.maximum(s - 1, 0), 0,
                                       blocks_per_step - 1)
    n_steps = batch * steps_per_seq
    mod_cols = MOD_POST * D_MODEL
    tn = mod_cols // n_steps
    assert tn % LANES == 0 and (MOD_PRE * D_MODEL) % tn == 0
    mod_tile = lambda b, s: (0, b * steps_per_seq + s)
    mod_src_tile = lambda b, s: (0, MOD_PRE * D_MODEL // tn + b * steps_per_seq + s)
    return pl.pallas_call(
        functools.partial(_swa_kernel, blocks_per_step=blocks_per_step),
        out_shape=(jax.ShapeDtypeStruct((batch * seq, SWA_Q_COLS), BF16),
                   jax.ShapeDtypeStruct((c.shape[0], mod_cols), F32)),
        grid=(batch, steps_per_seq),
        in_specs=[
            pl.BlockSpec(memory_space=pltpu.SMEM),
            pl.BlockSpec((1, SWA_Q_COLS, step), tile),
            pl.BlockSpec((BLOCK, SWA_KV_COLS), lambda b, s: (prev_blk(b, s), 0)),
            pl.BlockSpec((step, SWA_KV_COLS), lambda b, s: (b * steps_per_seq + s, 0)),
            pl.BlockSpec((1, SWA_KV_COLS, BLOCK), prev_tile_last_blk),
            pl.BlockSpec((1, SWA_KV_COLS, step), tile),
            _const_spec((2, SWA_HEADS, 2 * BLOCK, BLOCK)),
            _const_spec(c.shape),
            pl.BlockSpec((D_MODEL, tn), mod_src_tile),
            pl.BlockSpec((1, tn), mod_src_tile),
        ],
        out_specs=(
            pl.BlockSpec((step, SWA_Q_COLS), lambda b, s: (b * steps_per_seq + s, 0)),
            pl.BlockSpec((c.shape[0], tn), mod_tile),
        ),
        scratch_shapes=[
            pltpu.VMEM((2, 2 * BLOCK, SWA_UNIT_HEADS * BLOCK), F32),
            pltpu.VMEM((2, 2 * BLOCK, SWA_UNIT_HEADS * BLOCK), BF16),
            pltpu.VMEM((2, SWA_UNIT_HEADS, 1, BLOCK), F32),
        ],
        compiler_params=_params(("parallel", "arbitrary")),
        name="swa",
    )(sinks, qat, ka, ka, vat, vat, bias_tab, c, w_mod, b_mod)


def _mla_kernel(q_ref, k_ref, vt_ref, o_ref, s_ref, p_ref, al_ref, m_ref, acc_ref,
                *, tile, n_tiles):
    pairs = [(qi, j) for qi in range(n_tiles) for j in range(qi + 1)]
    width = tile // MLA_COL_SPLIT

    def geometry(u, c):
        qi, j = pairs[u]
        cols = slice(c * width, (c + 1) * width)
        n_keys = (c + 1) * width if j == qi else tile
        return qi, j, cols, n_keys

    def stage_scores(u, c):
        qi, j, cols, n_keys = geometry(u, c)
        k = k_ref[j * tile:j * tile + n_keys, :]
        q = q_ref[qi * tile + c * width:qi * tile + (c + 1) * width, :]
        s_ref[u % MLA_SLOTS, :n_keys, cols] = lax.dot_general(
            k, q, (((1,), (1,)), ((), ())), preferred_element_type=F32)

    def stage_softmax(u, c):
        qi, j, cols, n_keys = geometry(u, c)
        st = s_ref[u % MLA_SLOTS, :n_keys, cols]
        if j == qi:
            kk = lax.broadcasted_iota(jnp.int32, st.shape, 0)
            qq = lax.broadcasted_iota(jnp.int32, st.shape, 1) + c * width
            st = jnp.where(kk <= qq, st, -jnp.inf)
        cmax = jnp.max(st, axis=0, keepdims=True)
        if j == 0:
            m_new = cmax
        else:
            m_old = m_ref[qi % 2, :, cols]
            m_new = jnp.maximum(m_old, cmax)
            al_ref[u % MLA_SLOTS, :, cols] = jnp.exp2(m_old - m_new)
        m_ref[qi % 2, :, cols] = m_new
        p_ref[u % MLA_SLOTS, :n_keys, cols] = jnp.exp2(st - m_new).astype(BF16)

    def stage_pv(u, c):
        qi, j, cols, n_keys = geometry(u, c)
        vt_tile = vt_ref.shape[-1]
        pv = None
        for lo in range(0, n_keys, vt_tile):
            nk = min(vt_tile, n_keys - lo)
            part = jnp.dot(vt_ref[0, 0, (j * tile + lo) // vt_tile, :, :nk],
                           p_ref[u % MLA_SLOTS, lo:lo + nk, cols], preferred_element_type=F32)
            pv = part if pv is None else pv + part
        if j == 0:
            acc = pv
        else:
            acc = al_ref[u % MLA_SLOTS, :, cols] * acc_ref[:, cols] + pv
        if j == qi:
            rows = slice(qi * tile + c * width, qi * tile + (c + 1) * width)
            denom = acc[MLA_V_DIM:MLA_V_DIM + 1]
            o_ref[rows, :] = (acc[:MLA_V_DIM] / denom).T.astype(BF16)
        else:
            acc_ref[:, cols] = acc

    n = len(pairs)
    skew = MLA_SLOTS - 1
    for t in range(n + 2 * skew):
        for c in range(MLA_COL_SPLIT):
            if 0 <= t - 2 * skew < n:
                stage_pv(t - 2 * skew, c)
            if 0 <= t - skew < n:
                stage_softmax(t - skew, c)
            if t < n:
                stage_scores(t, c)


def _mla_call(qb, kcat, vt, *, batch, seq):
    tile = MLA_TILE
    n_tiles = seq // tile
    return pl.pallas_call(
        functools.partial(_mla_kernel, tile=tile, n_tiles=n_tiles),
        out_shape=jax.ShapeDtypeStruct((batch * seq, MLA_HEADS * MLA_V_DIM), BF16),
        grid=(batch, MLA_HEADS),
        in_specs=[
            pl.BlockSpec((seq, MLA_QK_PAD), lambda b, h: (b, h)),
            pl.BlockSpec((seq, MLA_QK_PAD), lambda b, h: (b, h)),
            pl.BlockSpec((1, 1, seq // VT_TILE, MLA_VT_ROWS, VT_TILE), lambda b, h: (b, h, 0, 0, 0)),
        ],
        out_specs=pl.BlockSpec((seq, MLA_V_DIM), lambda b, h: (b, h)),
        scratch_shapes=[
            pltpu.VMEM((MLA_SLOTS, tile, tile), F32),
            pltpu.VMEM((MLA_SLOTS, tile, tile), BF16),
            pltpu.VMEM((MLA_SLOTS, 1, tile), F32),
            pltpu.VMEM((2, 1, tile), F32),
            pltpu.VMEM((MLA_VT_ROWS, tile), F32),
        ],
        compiler_params=_params(("parallel", "parallel")),
        name="mla",
    )(qb, kcat, vt)


def _out_kernel(x_ref, oa_ref, ob_ref, mod_ref, wa_ref, wb_ref, x1_ref):
    g1 = mod_ref[0][0:1]
    y = (jnp.dot(oa_ref[...], wa_ref[...], preferred_element_type=F32)
         + jnp.dot(ob_ref[...], wb_ref[...], preferred_element_type=F32))
    x1_ref[...] = x_ref[...] + g1 * y


def _out_call(x2d, oa, ob, mod3, w_out, *, seq):
    tm = TM_OUT
    t = x2d.shape[0]
    tiles_per_seq = seq // tm
    tok = lambda i: (i, 0)
    w_half = lambda half: pl.BlockSpec((SWA_Q_COLS, D_MODEL), lambda i: (half, 0),
                                       pipeline_mode=pl.Buffered(1))
    assert w_out.shape[0] == 2 * SWA_Q_COLS
    return pl.pallas_call(
        _out_kernel,
        out_shape=jax.ShapeDtypeStruct((t, D_MODEL), F32),
        grid=(t // tm,),
        in_specs=[
            pl.BlockSpec((tm, D_MODEL), tok),
            pl.BlockSpec((tm, SWA_Q_COLS), tok),
            pl.BlockSpec((tm, MLA_HEADS * MLA_V_DIM), tok),
            pl.BlockSpec((1, MOD_POST, D_MODEL), lambda i: (i // tiles_per_seq, 0, 0)),
            w_half(0),
            w_half(1),
        ],
        out_specs=pl.BlockSpec((tm, D_MODEL), tok),
        compiler_params=_params(("parallel",)),
        name="out_proj",
    )(x2d, oa, ob, mod3, w_out, w_out)


def _mlp_kernel(x_ref, mod_ref, g_ref, gf_ref, w1_hbm, w2_hbm, o_ref,
                h_ref, w1_buf, w2_buf, sem, *, tf, n_ff):
    i = pl.program_id(0)
    n_steps = pl.num_programs(0)
    mod = mod_ref[0]
    rows = o_ref.shape[0] // ROW_CHUNKS
    chunks = [slice(r * rows, (r + 1) * rows) for r in range(ROW_CHUNKS)]

    def copies(t):
        slot = t % 2
        return (
            pltpu.make_async_copy(w1_hbm.at[:, pl.ds(t * tf, tf)], w1_buf.at[slot],
                                  sem.at[0, slot]),
            pltpu.make_async_copy(w2_hbm.at[pl.ds(t * tf, tf), :], w2_buf.at[slot],
                                  sem.at[1, slot]),
        )

    def start(t):
        for cp in copies(t):
            cp.start()

    def wait(t):
        for cp in copies(t):
            cp.wait()

    def ff_tile(h, t):
        u = jnp.maximum(jnp.dot(h, w1_buf[t % 2], preferred_element_type=F32), 0.0)
        return jnp.dot((u * u).astype(BF16), w2_buf[t % 2], preferred_element_type=F32)

    @pl.when(i == 0)
    def _():
        start(0)

    for t in range(n_ff):
        if t + 1 < n_ff:
            start(t + 1)
        else:
            @pl.when(i + 1 < n_steps)
            def _():
                start(0)
        wait(t)
        if t == 0:
            sh2 = mod[1:2]
            sc2 = mod[2:3]
            for sl in chunks:
                h = ((_rms(x_ref[sl, :]) * g_ref[...]) * (1.0 + sc2) + sh2).astype(BF16)
                h_ref[sl, :] = h
                o_ref[sl, :] = ff_tile(h, t)
        elif t + 1 < n_ff:
            o_ref[...] += ff_tile(h_ref[...], t)
        else:
            g2 = mod[3:4]
            for sl in chunks:
                x2 = x_ref[sl, :] + g2 * (o_ref[sl, :] + ff_tile(h_ref[sl, :], t))
                o_ref[sl, :] = _rms(x2) * gf_ref[...]


def _mlp_call(x1, mod3, g_mlp, g_final, w1, w2, *, seq):
    tm = TM_MLP
    tf = TF_MLP
    n_ff = D_FF // tf
    assert n_ff >= 2
    t = x1.shape[0]
    tiles_per_seq = seq // tm
    return pl.pallas_call(
        functools.partial(_mlp_kernel, tf=tf, n_ff=n_ff),
        out_shape=jax.ShapeDtypeStruct((t, D_MODEL), F32),
        grid=(t // tm,),
        in_specs=[
            pl.BlockSpec((tm, D_MODEL), lambda i: (i, 0)),
            pl.BlockSpec((1, MOD_POST, D_MODEL), lambda i: (i // tiles_per_seq, 0, 0)),
            _const_spec((1, D_MODEL)),
            _const_spec((1, D_MODEL)),
            pl.BlockSpec(memory_space=pl.ANY),
            pl.BlockSpec(memory_space=pl.ANY),
        ],
        out_specs=pl.BlockSpec((tm, D_MODEL), lambda i: (i, 0)),
        scratch_shapes=[
            pltpu.VMEM((tm, D_MODEL), BF16),
            pltpu.VMEM((2, D_MODEL, tf), BF16),
            pltpu.VMEM((2, tf, D_MODEL), BF16),
            pltpu.SemaphoreType.DMA((2, 2)),
        ],
        compiler_params=_params(("arbitrary",)),
        name="mlp",
    )(x1, mod3, g_mlp, g_final, w1, w2)


def _tail_kernel(x_ref, oa_ref, ob_ref, mod_ref, g_ref, gf_ref, wo_hbm, w1_hbm, w2_hbm, o_ref,
                 x1_ref, h_ref, w1_buf, w2_buf, sem, *, tf, n_ff):
    i = pl.program_id(0)
    n_steps = pl.num_programs(0)
    n_ph = n_ff + 2
    assert n_ph % 2 == 0
    mod = mod_ref[0]
    half = SWA_Q_COLS
    rows = o_ref.shape[0] // ROW_CHUNKS
    chunks = [slice(r * rows, (r + 1) * rows) for r in range(ROW_CHUNKS)]

    def copies(p):
        slot = p % 2
        if p < 2:
            return (pltpu.make_async_copy(wo_hbm.at[pl.ds(p * half, half), :],
                                          w1_buf.at[slot, pl.ds(0, half), :], sem.at[0, slot]),)
        t = p - 2
        return (
            pltpu.make_async_copy(w1_hbm.at[:, pl.ds(t * tf, tf)], w1_buf.at[slot],
                                  sem.at[0, slot]),
            pltpu.make_async_copy(w2_hbm.at[pl.ds(t * tf, tf), :], w2_buf.at[slot],
                                  sem.at[1, slot]),
        )

    def start(p):
        for cp in copies(p):
            cp.start()

    def wait(p):
        for cp in copies(p):
            cp.wait()

    def ff_tile(h, p):
        u = jnp.maximum(jnp.dot(h, w1_buf[p % 2], preferred_element_type=F32), 0.0)
        return jnp.dot((u * u).astype(BF16), w2_buf[p % 2], preferred_element_type=F32)

    @pl.when(i == 0)
    def _():
        start(0)

    for p in range(n_ph):
        if p + 1 < n_ph:
            start(p + 1)
        else:
            @pl.when(i + 1 < n_steps)
            def _():
                start(0)
        wait(p)
        if p == 0:
            x1_ref[...] = jnp.dot(oa_ref[...], w1_buf[0, :half, :], preferred_element_type=F32)
        elif p == 1:
            y = x1_ref[...] + jnp.dot(ob_ref[...], w1_buf[1, :half, :],
                                      preferred_element_type=F32)
            x1_ref[...] = x_ref[...] + mod[0:1] * y
        elif p == 2:
            sh2 = mod[1:2]
            sc2 = mod[2:3]
            for sl in chunks:
                h = ((_rms(x1_ref[sl, :]) * g_ref[...]) * (1.0 + sc2) + sh2).astype(BF16)
                h_ref[sl, :] = h
                o_ref[sl, :] = ff_tile(h, p)
        elif p + 1 < n_ph:
            o_ref[...] += ff_tile(h_ref[...], p)
        else:
            g2 = mod[3:4]
            for sl in chunks:
                x2 = x1_ref[sl, :] + g2 * (o_ref[sl, :] + ff_tile(h_ref[sl, :], p))
                o_ref[sl, :] = _rms(x2) * gf_ref[...]


def _tail_call(x2d, oa, ob, mod3, g_mlp, g_final, w_out, w1, w2, *, seq):
    tm = TM_MLP
    tf = TF_MLP
    n_ff = D_FF // tf
    assert n_ff >= 2 and tf == D_MODEL and w_out.shape == (2 * SWA_Q_COLS, D_MODEL)
    t = x2d.shape[0]
    tiles_per_seq = seq // tm
    tok = lambda i: (i, 0)
    return pl.pallas_call(
        functools.partial(_tail_kernel, tf=tf, n_ff=n_ff),
        out_shape=jax.ShapeDtypeStruct((t, D_MODEL), F32),
        grid=(t // tm,),
        in_specs=[
            pl.BlockSpec((tm, D_MODEL), tok),
            pl.BlockSpec((tm, SWA_Q_COLS), tok),
            pl.BlockSpec((tm, MLA_HEADS * MLA_V_DIM), tok),
            pl.BlockSpec((1, MOD_POST, D_MODEL), lambda i: (i // tiles_per_seq, 0, 0)),
            _const_spec((1, D_MODEL)),
            _const_spec((1, D_MODEL)),
            pl.BlockSpec(memory_space=pl.ANY),
            pl.BlockSpec(memory_space=pl.ANY),
            pl.BlockSpec(memory_space=pl.ANY),
        ],
        out_specs=pl.BlockSpec((tm, D_MODEL), tok),
        scratch_shapes=[
            pltpu.VMEM((tm, D_MODEL), F32),
            pltpu.VMEM((tm, D_MODEL), BF16),
            pltpu.VMEM((2, D_MODEL, tf), BF16),
            pltpu.VMEM((2, tf, D_MODEL), BF16),
            pltpu.SemaphoreType.DMA((2, 2)),
        ],
        compiler_params=_params(("arbitrary",)),
        name="tail",
    )(x2d, oa, ob, mod3, g_mlp, g_final, w_out, w1, w2)


def _t5_bucket_table():
    q_loc = np.arange(BLOCK)[:, None]
    k_loc = np.arange(2 * BLOCK)[None, :]
    n = np.maximum(q_loc + BLOCK - k_loc, 0)
    max_exact = REL_BUCKETS // 2
    nf = np.maximum(n, 1).astype(np.float64)
    large = max_exact + (np.log(nf / max_exact) / math.log(REL_MAX_DIST / max_exact)
                         * (REL_BUCKETS - max_exact)).astype(np.int32)
    large = np.minimum(large, REL_BUCKETS - 1)
    return np.where(n < max_exact, n, large).astype(np.int32)


def _rope_tables(seq):
    half = MLA_ROPE_DIM // 2
    inv_freq = ROPE_THETA ** (-np.arange(half, dtype=np.float64) / half)
    ang = np.arange(seq, dtype=np.float64)[:, None] * inv_freq[None, :]
    zeros = np.zeros((seq, LANES - MLA_ROPE_DIM))
    cos_tab = np.concatenate([np.cos(ang), np.cos(ang), zeros], axis=1)
    sin_tab = np.concatenate([np.sin(ang), np.sin(ang), zeros], axis=1)
    return cos_tab.astype(np.float32), sin_tab.astype(np.float32)


def _rot_cols(w):
    half = w.shape[-1] // 2
    return jnp.concatenate([-w[..., half:], w[..., :half]], axis=-1)


def kernel(x, c, w_mod, b_mod, attn_norm_g, w_in, swa_sinks, rel_bias, mla_q_norm_g, w_uq,
           mla_kv_norm_g, w_ukv, w_out, mlp_norm_g, w_ff1, w_ff2, final_norm_g):
    batch, seq, _ = x.shape
    depth = w_mod.shape[0]
    assert depth == 1
    t = batch * seq
    x2d = x.reshape(t, D_MODEL)
    l = 0

    w_kr = w_in[l][:, OFF_MLA_KR:OFF_MLA_KR + MLA_ROPE_DIM]
    w_main = jnp.concatenate(
        [w_in[l][:, OFF_SWA_K:OFF_SWA_V], w_in[l][:, OFF_MLA_CQ:OFF_MLA_KR],
         w_kr, _rot_cols(w_kr)], axis=1).astype(BF16)
    w_t = jnp.concatenate(
        [w_in[l][:, :SWA_Q_COLS], w_in[l][:, OFF_SWA_V:OFF_MLA_CQ]], axis=1).T.astype(BF16)

    wq = w_uq[l].reshape(MLA_Q_RANK, MLA_HEADS, MLA_QK_DIM)
    wq_nope = wq[..., :MLA_NOPE_DIM]
    wq_rope = wq[..., MLA_NOPE_DIM:]
    hh = MLA_HEADS // 2
    zq = jnp.zeros((MLA_Q_RANK, hh, LANES - MLA_ROPE_DIM), F32)
    wq_main = jnp.concatenate(
        [jnp.concatenate([wq_nope[:, :hh], wq_rope[:, :hh], zq], axis=-1),
         jnp.concatenate([wq_nope[:, hh:], zq, wq_rope[:, hh:]], axis=-1)], axis=1).reshape(
        MLA_Q_RANK, MLA_HEADS * MLA_QK_PAD).astype(BF16)
    wq_rot_all = _rot_cols(wq_rope)
    wq_rot = jnp.concatenate([wq_rot_all[:, :hh], wq_rot_all[:, hh:]], axis=-1).reshape(
        MLA_Q_RANK, hh * LANES).astype(BF16)

    wkv = w_ukv[l].reshape(MLA_KV_RANK, MLA_HEADS, MLA_NOPE_DIM + MLA_V_DIM)
    w_uk = wkv[..., :MLA_NOPE_DIM].reshape(MLA_KV_RANK, MLA_HEADS * MLA_NOPE_DIM).astype(BF16)
    w_vt = wkv[..., MLA_NOPE_DIM:].reshape(MLA_KV_RANK, MLA_HEADS * MLA_V_DIM).T.astype(BF16)

    cos_tab, sin_tab = _rope_tables(seq)

    assert w_mod.shape[2] == (MOD_PRE + MOD_POST) * D_MODEL
    b_mod2d = b_mod[l].reshape(1, -1)
    mod_pre, bias_tab = _prologue_call(c, w_mod[l], b_mod2d, rel_bias, _t5_bucket_table().T)
    mod_pre = mod_pre.reshape(batch, MOD_PRE, D_MODEL)

    qat, ka, vat, qb, kcat, vt, w_out_b16, w1, w2 = _proj_call(
        x2d, mod_pre, attn_norm_g[l].reshape(1, -1), w_main, w_t,
        mla_q_norm_g[l].reshape(1, -1), mla_kv_norm_g[l].reshape(1, -1),
        wq_main, wq_rot, w_uk, w_vt, cos_tab, sin_tab, w_out[l], w_ff1[l], w_ff2[l],
        batch=batch, seq=seq)

    oa, mod_post = _swa_call(swa_sinks[l], qat, ka, vat, bias_tab, c, w_mod[l], b_mod2d,
                             batch=batch, seq=seq)
    mod_post = mod_post.reshape(batch, MOD_POST, D_MODEL)
    ob = _mla_call(qb, kcat, vt, batch=batch, seq=seq)

    out = _tail_call(x2d, oa, ob, mod_post, mlp_norm_g[l].reshape(1, -1),
                     final_norm_g.reshape(1, -1), w_out_b16, w1, w2, seq=seq)
    return out.reshape(batch, seq, D_MODEL)
```

```python
import functools
import math

import jax
import jax.numpy as jnp
import numpy as np
from jax import lax
from jax.experimental import pallas as pl
from jax.experimental.pallas import tpu as pltpu

F32 = jnp.float32
BF16 = jnp.bfloat16

LANES = 128
BF16_ROWS = 16

D_MODEL = 2048
BLOCK = 128
EPS = 1e-6

SWA_HEADS = 16
SWA_KV_HEADS = 2
SWA_HEAD_DIM = 64
SWA_GROUP = SWA_HEADS // SWA_KV_HEADS
WINDOW = 128
REL_BUCKETS = 32
REL_MAX_DIST = 128

MLA_HEADS = 8
MLA_Q_RANK = 384
MLA_KV_RANK = 128
MLA_NOPE_DIM = 128
MLA_ROPE_DIM = 64
MLA_V_DIM = 128
MLA_VT_ROWS = MLA_V_DIM + BF16_ROWS
MLA_QK_DIM = MLA_NOPE_DIM + MLA_ROPE_DIM
MLA_QK_PAD = 256
ROPE_THETA = 10000.0
MLA_Q_SCALE = MLA_QK_DIM ** -0.5 * math.log2(math.e)
D_FF = 4 * D_MODEL

SWA_Q_COLS = SWA_HEADS * SWA_HEAD_DIM
SWA_KV_COLS = SWA_KV_HEADS * SWA_HEAD_DIM
OFF_SWA_K = SWA_Q_COLS
OFF_SWA_V = OFF_SWA_K + SWA_KV_COLS
OFF_MLA_CQ = OFF_SWA_V + SWA_KV_COLS
OFF_MLA_CKV = OFF_MLA_CQ + MLA_Q_RANK
OFF_MLA_KR = OFF_MLA_CKV + MLA_KV_RANK
P_CQ = SWA_KV_COLS
P_CKV = P_CQ + MLA_Q_RANK
P_KR = P_CKV + MLA_KV_RANK
P_COLS = P_KR + LANES
PT_ROWS = SWA_Q_COLS + SWA_KV_COLS
LOG2E = math.log2(math.e)

VMEM_LIMIT_BYTES = 62 * 1024 * 1024

TM_PROJ = 512
MLA_TILE = 512
VT_TILE = min(MLA_TILE, TM_PROJ)
TM_SWA = 512
TM_OUT = 1024
TM_MLP = 512
TF_MLP = 2048
MOD_PRE = 2
MOD_POST = 4
BIAS_HEADS_PER_STEP = 4
SWA_UNIT_HEADS = 4
MLA_SLOTS = 2
MLA_COL_SPLIT = 2
ROW_CHUNKS = 2
PROJ_ROW_CHUNKS = 2


def _params(sem):
    return pltpu.CompilerParams(dimension_semantics=sem, vmem_limit_bytes=VMEM_LIMIT_BYTES)


def _const_spec(shape):
    nd = len(shape)
    return pl.BlockSpec(shape, lambda *_: (0,) * nd, pipeline_mode=pl.Buffered(1))


def _rms(x):
    return x * lax.rsqrt(jnp.mean(x * x, axis=-1, keepdims=True) + EPS)


def _mod_tile(c_ref, w_ref, b_ref):
    c = c_ref[...]
    ca = c * (1.0 / (1.0 + jnp.exp(-c)))
    return jnp.dot(ca.astype(BF16), w_ref[...].astype(BF16),
                   preferred_element_type=F32) + b_ref[...]


def _prologue_kernel(c_ref, w_ref, b_ref, rel_ref, bucket_ref, mod_ref, o_ref):
    mod_ref[...] = _mod_tile(c_ref, w_ref, b_ref)

    bucket = bucket_ref[...]
    k_loc = lax.broadcasted_iota(jnp.int32, bucket.shape, 0)
    q_loc = lax.broadcasted_iota(jnp.int32, bucket.shape, 1)
    dist = q_loc + BLOCK - k_loc
    in_window = (dist >= 0) & (dist < WINDOW)
    for i in range(BIAS_HEADS_PER_STEP):
        h = pl.program_id(0) * BIAS_HEADS_PER_STEP + i
        acc = jnp.zeros(bucket.shape, F32)
        for k in range(REL_BUCKETS):
            acc = jnp.where(bucket == k, rel_ref[k, h], acc)
        acc = acc * LOG2E
        o_ref[0, i] = jnp.where(in_window & (k_loc >= BLOCK), acc, -jnp.inf)
        o_ref[1, i] = jnp.where(in_window, acc, -jnp.inf)


def _prologue_call(c, w_mod, b_mod, rel_bias, bucket_t):
    hs = BIAS_HEADS_PER_STEP
    steps = SWA_HEADS // hs
    n = MOD_PRE * D_MODEL
    tn = n // steps
    assert tn % LANES == 0
    return pl.pallas_call(
        _prologue_kernel,
        out_shape=(jax.ShapeDtypeStruct((c.shape[0], n), F32),
                   jax.ShapeDtypeStruct((2, SWA_HEADS, 2 * BLOCK, BLOCK), F32)),
        grid=(steps,),
        in_specs=[
            pl.BlockSpec(c.shape, lambda s: (0, 0)),
            pl.BlockSpec((D_MODEL, tn), lambda s: (0, s)),
            pl.BlockSpec((1, tn), lambda s: (0, s)),
            pl.BlockSpec(memory_space=pltpu.SMEM),
            pl.BlockSpec((2 * BLOCK, BLOCK), lambda s: (0, 0)),
        ],
        out_specs=(pl.BlockSpec((c.shape[0], tn), lambda s: (0, s)),
                   pl.BlockSpec((2, hs, 2 * BLOCK, BLOCK), lambda s: (0, s, 0, 0))),
        compiler_params=_params(("arbitrary",)),
        name="prologue",
    )(c, w_mod, b_mod, rel_bias, bucket_t)


def _proj_kernel(x_ref, mod_ref, g_ref, win_ref, wt_ref, gq_ref, gkv_ref, wqm_ref, wqr_ref,
                 wuk_ref, wvt_ref, cos_ref, sin_ref, wo_ref, w1_ref, w2_ref,
                 qat_ref, ka_ref, vat_ref, qb_ref, kc_ref, vt_ref, wob_ref, w1b_ref, w2b_ref,
                 *, q_scale):
    wob_ref[...] = wo_ref[...].astype(BF16)
    w1b_ref[...] = w1_ref[...].astype(BF16)
    w2b_ref[...] = w2_ref[...].astype(BF16)

    mod = mod_ref[0]
    sh1 = mod[0:1]
    sc1 = mod[1:2]
    half_heads = MLA_HEADS // 2
    pad_rows = MLA_VT_ROWS - MLA_V_DIM
    ones_rows = jnp.where(lax.broadcasted_iota(jnp.int32, (pad_rows, VT_TILE), 0) == 0,
                          1.0, 0.0).astype(BF16)
    for hd in range(MLA_HEADS):
        for kt in range(vt_ref.shape[2]):
            vt_ref[0, hd, kt, MLA_V_DIM:] = ones_rows

    rows = x_ref.shape[0] // PROJ_ROW_CHUNKS
    assert VT_TILE % rows == 0
    for r in range(PROJ_ROW_CHUNKS):
        rs = slice(r * rows, (r + 1) * rows)
        h = ((_rms(x_ref[rs, :]) * g_ref[...]) * (1.0 + sc1) + sh1).astype(BF16)
        proj = jnp.dot(h, win_ref[...], preferred_element_type=F32)
        proj_t = lax.dot_general(wt_ref[...], h, (((1,), (1,)), ((), ())),
                                 preferred_element_type=F32)

        qat_ref[0, :, rs] = (proj_t[:SWA_Q_COLS] * (SWA_HEAD_DIM ** -0.5 * LOG2E)).astype(BF16)
        vat_ref[0, :, rs] = proj_t[SWA_Q_COLS:].astype(BF16)
        ka_ref[rs, :] = proj[:, :P_CQ].astype(BF16)

        cq = (_rms(proj[:, P_CQ:P_CKV]) * gq_ref[...]).astype(BF16)
        ckv = (_rms(proj[:, P_CKV:P_KR]) * gkv_ref[...]).astype(BF16)
        cos_lo = cos_ref[rs, :]
        sin_lo = sin_ref[rs, :]
        cos_hi = pltpu.roll(cos_lo, MLA_ROPE_DIM, 1)
        sin_hi = pltpu.roll(sin_lo, MLA_ROPE_DIM, 1)
        kr = proj[:, P_KR:P_COLS]
        kr_sw = pltpu.roll(kr, MLA_ROPE_DIM, 1)
        krope_lo = (kr * cos_lo + kr_sw * sin_lo).astype(BF16)
        krope_hi = (kr_sw * cos_hi + kr * sin_hi).astype(BF16)

        qmain = jnp.dot(cq, wqm_ref[...], preferred_element_type=F32)
        qrot = jnp.dot(cq, wqr_ref[...], preferred_element_type=F32)
        knope = jnp.dot(ckv, wuk_ref[...], preferred_element_type=F32)
        vt = lax.dot_general(wvt_ref[...], ckv, (((1,), (1,)), ((), ())),
                             preferred_element_type=F32)

        vt_tile = (r * rows) // VT_TILE
        vt_cols = slice((r * rows) % VT_TILE, (r * rows) % VT_TILE + rows)
        for hd in range(MLA_HEADS):
            lo = hd * MLA_QK_PAD
            mid = lo + MLA_NOPE_DIM
            hi = lo + MLA_QK_PAD
            low = hd < half_heads
            cos_t, sin_t = (cos_lo, sin_lo) if low else (cos_hi, sin_hi)
            rot = qrot[:, (hd % half_heads) * LANES:(hd % half_heads + 1) * LANES]
            qb_ref[rs, lo:mid] = (qmain[:, lo:mid] * q_scale).astype(BF16)
            qb_ref[rs, mid:hi] = ((qmain[:, mid:hi] * cos_t + rot * sin_t) * q_scale).astype(BF16)
            kc_ref[rs, lo:mid] = knope[:, hd * MLA_NOPE_DIM:(hd + 1) * MLA_NOPE_DIM].astype(BF16)
            kc_ref[rs, mid:hi] = krope_lo if low else krope_hi
            vt_ref[0, hd, vt_tile, :MLA_V_DIM, vt_cols] = (
                vt[hd * MLA_V_DIM:(hd + 1) * MLA_V_DIM].astype(BF16))


def _proj_call(x2d, mod3, g_attn, w_main, w_t, gq, gkv, wq_main, wq_rot, w_uk, w_vt,
               cos_tab, sin_tab, w_out, w_ff1, w_ff2, *, batch, seq):
    tm = TM_PROJ
    tiles_per_seq = seq // tm
    t = batch * seq
    steps = t // tm
    slab = lambda w: (w.shape[0] // steps, w.shape[1])
    assert all(w.shape[0] % (steps * BF16_ROWS) == 0 for w in (w_out, w_ff1, w_ff2))
    tok = lambda i: (i, 0)
    tile_t = lambda i: (i, 0, 0)
    pos = lambda i: (i % tiles_per_seq, 0)
    out_shape = (
        jax.ShapeDtypeStruct((t // tm, SWA_Q_COLS, tm), BF16),
        jax.ShapeDtypeStruct((t, SWA_KV_COLS), BF16),
        jax.ShapeDtypeStruct((t // tm, SWA_KV_COLS, tm), BF16),
        jax.ShapeDtypeStruct((t, MLA_HEADS * MLA_QK_PAD), BF16),
        jax.ShapeDtypeStruct((t, MLA_HEADS * MLA_QK_PAD), BF16),
        jax.ShapeDtypeStruct((batch, MLA_HEADS, seq // VT_TILE, MLA_VT_ROWS, VT_TILE), BF16),
        jax.ShapeDtypeStruct(w_out.shape, BF16),
        jax.ShapeDtypeStruct(w_ff1.shape, BF16),
        jax.ShapeDtypeStruct(w_ff2.shape, BF16),
    )
    return pl.pallas_call(
        functools.partial(_proj_kernel, q_scale=MLA_Q_SCALE),
        out_shape=out_shape,
        grid=(t // tm,),
        in_specs=[
            pl.BlockSpec((tm, D_MODEL), tok),
            pl.BlockSpec((1, MOD_PRE, D_MODEL), lambda i: (i // tiles_per_seq, 0, 0)),
            _const_spec((1, D_MODEL)),
            _const_spec((D_MODEL, P_COLS)),
            _const_spec((PT_ROWS, D_MODEL)),
            _const_spec((1, MLA_Q_RANK)),
            _const_spec((1, MLA_KV_RANK)),
            _const_spec((MLA_Q_RANK, MLA_HEADS * MLA_QK_PAD)),
            _const_spec((MLA_Q_RANK, MLA_HEADS // 2 * LANES)),
            _const_spec((MLA_KV_RANK, MLA_HEADS * MLA_NOPE_DIM)),
            _const_spec((MLA_HEADS * MLA_V_DIM, MLA_KV_RANK)),
            pl.BlockSpec((tm, LANES), pos),
            pl.BlockSpec((tm, LANES), pos),
            pl.BlockSpec(slab(w_out), tok),
            pl.BlockSpec(slab(w_ff1), tok),
            pl.BlockSpec(slab(w_ff2), tok),
        ],
        out_specs=(
            pl.BlockSpec((1, SWA_Q_COLS, tm), tile_t),
            pl.BlockSpec((tm, SWA_KV_COLS), tok),
            pl.BlockSpec((1, SWA_KV_COLS, tm), tile_t),
            pl.BlockSpec((tm, MLA_HEADS * MLA_QK_PAD), tok),
            pl.BlockSpec((tm, MLA_HEADS * MLA_QK_PAD), tok),
            pl.BlockSpec((1, MLA_HEADS, tm // VT_TILE, MLA_VT_ROWS, VT_TILE),
                         lambda i: (i // tiles_per_seq, 0, i % tiles_per_seq, 0, 0)),
            pl.BlockSpec(slab(w_out), tok),
            pl.BlockSpec(slab(w_ff1), tok),
            pl.BlockSpec(slab(w_ff2), tok),
        ),
        compiler_params=_params(("parallel",)),
        name="in_proj",
    )(x2d, mod3, g_attn, w_main, w_t, gq, gkv, wq_main, wq_rot, w_uk, w_vt,
      cos_tab, sin_tab, w_out, w_ff1, w_ff2)


def _swa_kernel(sinks_ref, qt_ref, kp_ref, kc_ref, vtp_ref, vtc_ref, bias_ref,
                c_ref, wmod_ref, bmod_ref, o_ref, mod_ref,
                s_ref, p_ref, sink_ref, *, blocks_per_step):
    mod_ref[...] = _mod_tile(c_ref, wmod_ref, bmod_ref)

    first_step = pl.program_id(1) == 0
    k_all = jnp.concatenate([kp_ref[...], kc_ref[...]], axis=0)
    vt_all = jnp.concatenate([vtp_ref[0], vtc_ref[0]], axis=1)
    uh = SWA_UNIT_HEADS
    zeros_q = jnp.zeros((SWA_HEAD_DIM, uh * BLOCK), BF16)
    ones_rows = jnp.where(lax.broadcasted_iota(jnp.int32, (BF16_ROWS, 2 * BLOCK), 0) == 0,
                          1.0, 0.0).astype(BF16)
    units = [(i, g, part) for i in range(blocks_per_step) for g in range(SWA_KV_HEADS)
             for part in range(SWA_GROUP // uh)]

    def stage_scores(u):
        i, g, part = units[u]
        h_first = g * SWA_GROUP + part * uh
        k_band = k_all[i * BLOCK:(i + 2) * BLOCK]
        qt = qt_ref[0, :, i * BLOCK:(i + 1) * BLOCK]
        q_g = jnp.concatenate(
            [qt[(h_first + hh) * SWA_HEAD_DIM:(h_first + hh + 1) * SWA_HEAD_DIM]
             for hh in range(uh)], axis=1)
        rhs = jnp.concatenate([q_g, zeros_q] if g == 0 else [zeros_q, q_g], axis=0)
        s_ref[u % 2] = jnp.dot(k_band, rhs, preferred_element_type=F32)

    def stage_softmax(u):
        i, g, part = units[u]
        variant = jnp.where(first_step, 0, 1) if i == 0 else 1
        for hh in range(uh):
            hd = g * SWA_GROUP + part * uh + hh
            st = s_ref[u % 2, :, hh * BLOCK:(hh + 1) * BLOCK] + bias_ref[variant, hd]
            sink = sinks_ref[hd] * LOG2E
            m = jnp.maximum(jnp.max(st, axis=0, keepdims=True), sink)
            sink_ref[u % 2, hh] = jnp.exp2(sink - m)
            p_ref[u % 2, :, hh * BLOCK:(hh + 1) * BLOCK] = jnp.exp2(st - m).astype(BF16)

    def stage_pv(u):
        i, g, part = units[u]
        vt_band = jnp.concatenate(
            [vt_all[g * SWA_HEAD_DIM:(g + 1) * SWA_HEAD_DIM, i * BLOCK:(i + 2) * BLOCK],
             ones_rows], axis=0)
        ot_g = jnp.dot(vt_band, p_ref[u % 2], preferred_element_type=F32)

        def head_out(hh):
            cols = slice(hh * BLOCK, (hh + 1) * BLOCK)
            denom = ot_g[SWA_HEAD_DIM:SWA_HEAD_DIM + 1, cols] + sink_ref[u % 2, hh]
            return ot_g[:SWA_HEAD_DIM, cols] * (1.0 / denom)

        for pair in range(uh // 2):
            h0 = 2 * pair
            slab = jnp.concatenate([head_out(h0), head_out(h0 + 1)], axis=0)
            col = (g * SWA_GROUP + part * uh + h0) * SWA_HEAD_DIM
            o_ref[i * BLOCK:(i + 1) * BLOCK, col:col + 2 * SWA_HEAD_DIM] = (
                slab.T.astype(BF16))

    n = len(units)
    for t in range(n + 2):
        if 0 <= t - 2 < n:
            stage_pv(t - 2)
        if 0 <= t - 1 < n:
            stage_softmax(t - 1)
        if t < n:
            stage_scores(t)


def _swa_call(sinks, qat, ka, vat, bias_tab, c, w_mod, b_mod, *, batch, seq):
    step = TM_SWA
    blocks_per_step = step // BLOCK
    steps_per_seq = seq // step
    nb = seq // BLOCK
    prev_blk = lambda b, s: b * nb + jnp.maximum(s * blocks_per_step - 1, 0)
    assert step == TM_PROJ
    tile = lambda b, s: (b * steps_per_seq + s, 0, 0)
    prev_tile_last_blk = lambda b, s: (b * steps_per_seq + jnp.maximum(s - 1, 0), 0,
                                       blocks_per_step - 1)
    n_steps = batch * steps_per_seq
    mod_cols = MOD_POST * D_MODEL
    tn = mod_cols // n_steps
    assert tn % LANES == 0 and (MOD_PRE * D_MODEL) % tn == 0
    mod_tile = lambda b, s: (0, b * steps_per_seq + s)
    mod_src_tile = lambda b, s: (0, MOD_PRE * D_MODEL // tn + b * steps_per_seq + s)
    return pl.pallas_call(
        functools.partial(_swa_kernel, blocks_per_step=blocks_per_step),
        out_shape=(jax.ShapeDtypeStruct((batch * seq, SWA_Q_COLS), BF16),
                   jax.ShapeDtypeStruct((c.shape[0], mod_cols), F32)),
        grid=(batch, steps_per_seq),
        in_specs=[
            pl.BlockSpec(memory_space=pltpu.SMEM),
            pl.BlockSpec((1, SWA_Q_COLS, step), tile),
            pl.BlockSpec((BLOCK, SWA_KV_COLS), lambda b, s: (prev_blk(b, s), 0)),
            pl.BlockSpec((step, SWA_KV_COLS), lambda b, s: (b * steps_per_seq + s, 0)),
            pl.BlockSpec((1, SWA_KV_COLS, BLOCK), prev_tile_last_blk),
            pl.BlockSpec((1, SWA_KV_COLS, step), tile),
            _const_spec((2, SWA_HEADS, 2 * BLOCK, BLOCK)),
            _const_spec(c.shape),
            pl.BlockSpec((D_MODEL, tn), mod_src_tile),
            pl.BlockSpec((1, tn), mod_src_tile),
        ],
        out_specs=(
            pl.BlockSpec((step, SWA_Q_COLS), lambda b, s: (b * steps_per_seq + s, 0)),
            pl.BlockSpec((c.shape[0], tn), mod_tile),
        ),
        scratch_shapes=[
            pltpu.VMEM((2, 2 * BLOCK, SWA_UNIT_HEADS * BLOCK), F32),
            pltpu.VMEM((2, 2 * BLOCK, SWA_UNIT_HEADS * BLOCK), BF16),
            pltpu.VMEM((2, SWA_UNIT_HEADS, 1, BLOCK), F32),
        ],
        compiler_params=_params(("parallel", "arbitrary")),
        name="swa",
    )(sinks, qat, ka, ka, vat, vat, bias_tab, c, w_mod, b_mod)


def _mla_kernel(q_ref, k_ref, vt_ref, o_ref, s_ref, p_ref, al_ref, m_ref, acc_ref,
                *, tile, n_tiles):
    pairs = [(qi, j) for qi in range(n_tiles) for j in range(qi + 1)]
    width = tile // MLA_COL_SPLIT

    def geometry(u, c):
        qi, j = pairs[u]
        cols = slice(c * width, (c + 1) * width)
        n_keys = (c + 1) * width if j == qi else tile
        return qi, j, cols, n_keys

    def stage_scores(u, c):
        qi, j, cols, n_keys = geometry(u, c)
        k = k_ref[j * tile:j * tile + n_keys, :]
        q = q_ref[qi * tile + c * width:qi * tile + (c + 1) * width, :]
        s_ref[u % MLA_SLOTS, :n_keys, cols] = lax.dot_general(
            k, q, (((1,), (1,)), ((), ())), preferred_element_type=F32)

    def stage_softmax(u, c):
        qi, j, cols, n_keys = geometry(u, c)
        st = s_ref[u % MLA_SLOTS, :n_keys, cols]
        if j == qi:
            kk = lax.broadcasted_iota(jnp.int32, st.shape, 0)
            qq = lax.broadcasted_iota(jnp.int32, st.shape, 1) + c * width
            st = jnp.where(kk <= qq, st, -jnp.inf)
        cmax = jnp.max(st, axis=0, keepdims=True)
        if j == 0:
            m_new = cmax
        else:
            m_old = m_ref[qi % 2, :, cols]
            m_new = jnp.maximum(m_old, cmax)
            al_ref[u % MLA_SLOTS, :, cols] = jnp.exp2(m_old - m_new)
        m_ref[qi % 2, :, cols] = m_new
        p_ref[u % MLA_SLOTS, :n_keys, cols] = jnp.exp2(st - m_new).astype(BF16)

    def stage_pv(u, c):
        qi, j, cols, n_keys = geometry(u, c)
        vt_tile = vt_ref.shape[-1]
        pv = None
        for lo in range(0, n_keys, vt_tile):
            nk = min(vt_tile, n_keys - lo)
            part = jnp.dot(vt_ref[0, 0, (j * tile + lo) // vt_tile, :, :nk],
                           p_ref[u % MLA_SLOTS, lo:lo + nk, cols], preferred_element_type=F32)
            pv = part if pv is None else pv + part
        if j == 0:
            acc = pv
        else:
            acc = al_ref[u % MLA_SLOTS, :, cols] * acc_ref[:, cols] + pv
        if j == qi:
            rows = slice(qi * tile + c * width, qi * tile + (c + 1) * width)
            denom = acc[MLA_V_DIM:MLA_V_DIM + 1]
            o_ref[rows, :] = (acc[:MLA_V_DIM] / denom).T.astype(BF16)
        else:
            acc_ref[:, cols] = acc

    n = len(pairs)
    skew = MLA_SLOTS - 1
    for t in range(n + 2 * skew):
        for c in range(MLA_COL_SPLIT):
            if 0 <= t - 2 * skew < n:
                stage_pv(t - 2 * skew, c)
            if 0 <= t - skew < n:
                stage_softmax(t - skew, c)
            if t < n:
                stage_scores(t, c)


def _mla_call(qb, kcat, vt, *, batch, seq):
    tile = MLA_TILE
    n_tiles = seq // tile
    return pl.pallas_call(
        functools.partial(_mla_kernel, tile=tile, n_tiles=n_tiles),
        out_shape=jax.ShapeDtypeStruct((batch * seq, MLA_HEADS * MLA_V_DIM), BF16),
        grid=(batch, MLA_HEADS),
        in_specs=[
            pl.BlockSpec((seq, MLA_QK_PAD), lambda b, h: (b, h)),
            pl.BlockSpec((seq, MLA_QK_PAD), lambda b, h: (b, h)),
            pl.BlockSpec((1, 1, seq // VT_TILE, MLA_VT_ROWS, VT_TILE), lambda b, h: (b, h, 0, 0, 0)),
        ],
        out_specs=pl.BlockSpec((seq, MLA_V_DIM), lambda b, h: (b, h)),
        scratch_shapes=[
            pltpu.VMEM((MLA_SLOTS, tile, tile), F32),
            pltpu.VMEM((MLA_SLOTS, tile, tile), BF16),
            pltpu.VMEM((MLA_SLOTS, 1, tile), F32),
            pltpu.VMEM((2, 1, tile), F32),
            pltpu.VMEM((MLA_VT_ROWS, tile), F32),
        ],
        compiler_params=_params(("parallel", "parallel")),
        name="mla",
    )(qb, kcat, vt)


def _out_kernel(x_ref, oa_ref, ob_ref, mod_ref, wa_ref, wb_ref, x1_ref):
    g1 = mod_ref[0][0:1]
    y = (jnp.dot(oa_ref[...], wa_ref[...], preferred_element_type=F32)
         + jnp.dot(ob_ref[...], wb_ref[...], preferred_element_type=F32))
    x1_ref[...] = x_ref[...] + g1 * y


def _out_call(x2d, oa, ob, mod3, w_out, *, seq):
    tm = TM_OUT
    t = x2d.shape[0]
    tiles_per_seq = seq // tm
    tok = lambda i: (i, 0)
    w_half = lambda half: pl.BlockSpec((SWA_Q_COLS, D_MODEL), lambda i: (half, 0),
                                       pipeline_mode=pl.Buffered(1))
    assert w_out.shape[0] == 2 * SWA_Q_COLS
    return pl.pallas_call(
        _out_kernel,
        out_shape=jax.ShapeDtypeStruct((t, D_MODEL), F32),
        grid=(t // tm,),
        in_specs=[
            pl.BlockSpec((tm, D_MODEL), tok),
            pl.BlockSpec((tm, SWA_Q_COLS), tok),
            pl.BlockSpec((tm, MLA_HEADS * MLA_V_DIM), tok),
            pl.BlockSpec((1, MOD_POST, D_MODEL), lambda i: (i // tiles_per_seq, 0, 0)),
            w_half(0),
            w_half(1),
        ],
        out_specs=pl.BlockSpec((tm, D_MODEL), tok),
        compiler_params=_params(("parallel",)),
        name="out_proj",
    )(x2d, oa, ob, mod3, w_out, w_out)


def _mlp_kernel(x_ref, mod_ref, g_ref, gf_ref, w1_hbm, w2_hbm, o_ref,
                h_ref, w1_buf, w2_buf, sem, *, tf, n_ff):
    i = pl.program_id(0)
    mod = mod_ref[0]
    rows = o_ref.shape[0] // ROW_CHUNKS
    chunks = [slice(r * rows, (r + 1) * rows) for r in range(ROW_CHUNKS)]

    par = lax.rem(i, 2)

    def slot_of(t):
        return lax.rem(t + par, 2)

    def copies(t):
        slot = slot_of(t)
        off = pl.multiple_of((t + par * (n_ff - 1 - 2 * t)) * tf, tf)
        return (
            pltpu.make_async_copy(w1_hbm.at[:, pl.ds(off, tf)], w1_buf.at[slot],
                                  sem.at[0, slot]),
            pltpu.make_async_copy(w2_hbm.at[pl.ds(off, tf), :], w2_buf.at[slot],
                                  sem.at[1, slot]),
        )

    def start(t):
        for cp in copies(t):
            cp.start()

    def wait(t):
        for cp in copies(t):
            cp.wait()

    def ff_tile(h, t):
        slot = slot_of(t)
        u = jnp.maximum(jnp.dot(h, w1_buf[slot], preferred_element_type=F32), 0.0)
        return jnp.dot((u * u).astype(BF16), w2_buf[slot], preferred_element_type=F32)

    @pl.when(i == 0)
    def _():
        start(0)

    for t in range(n_ff):
        if t + 1 < n_ff:
            start(t + 1)
        if t == 0:
            @pl.when(i == 0)
            def _():
                wait(0)
        else:
            wait(t)
        if t == 0:
            sh2 = mod[1:2]
            sc2 = mod[2:3]
            for sl in chunks:
                h = ((_rms(x_ref[sl, :]) * g_ref[...]) * (1.0 + sc2) + sh2).astype(BF16)
                h_ref[sl, :] = h
                o_ref[sl, :] = ff_tile(h, t)
        elif t + 1 < n_ff:
            o_ref[...] += ff_tile(h_ref[...], t)
        else:
            g2 = mod[3:4]
            for sl in chunks:
                x2 = x_ref[sl, :] + g2 * (o_ref[sl, :] + ff_tile(h_ref[sl, :], t))
                o_ref[sl, :] = _rms(x2) * gf_ref[...]


def _mlp_call(x1, mod3, g_mlp, g_final, w1, w2, *, seq):
    tm = TM_MLP
    tf = TF_MLP
    n_ff = D_FF // tf
    assert n_ff >= 2 and n_ff % 2 == 0
    t = x1.shape[0]
    tiles_per_seq = seq // tm
    return pl.pallas_call(
        functools.partial(_mlp_kernel, tf=tf, n_ff=n_ff),
        out_shape=jax.ShapeDtypeStruct((t, D_MODEL), F32),
        grid=(t // tm,),
        in_specs=[
            pl.BlockSpec((tm, D_MODEL), lambda i: (i, 0)),
            pl.BlockSpec((1, MOD_POST, D_MODEL), lambda i: (i // tiles_per_seq, 0, 0)),
            _const_spec((1, D_MODEL)),
            _const_spec((1, D_MODEL)),
            pl.BlockSpec(memory_space=pl.ANY),
            pl.BlockSpec(memory_space=pl.ANY),
        ],
        out_specs=pl.BlockSpec((tm, D_MODEL), lambda i: (i, 0)),
        scratch_shapes=[
            pltpu.VMEM((tm, D_MODEL), BF16),
            pltpu.VMEM((2, D_MODEL, tf), BF16),
            pltpu.VMEM((2, tf, D_MODEL), BF16),
            pltpu.SemaphoreType.DMA((2, 2)),
        ],
        compiler_params=_params(("arbitrary",)),
        name="mlp",
    )(x1, mod3, g_mlp, g_final, w1, w2)


def _t5_bucket_table():
    q_loc = np.arange(BLOCK)[:, None]
    k_loc = np.arange(2 * BLOCK)[None, :]
    n = np.maximum(q_loc + BLOCK - k_loc, 0)
    max_exact = REL_BUCKETS // 2
    nf = np.maximum(n, 1).astype(np.float64)
    large = max_exact + (np.log(nf / max_exact) / math.log(REL_MAX_DIST / max_exact)
                         * (REL_BUCKETS - max_exact)).astype(np.int32)
    large = np.minimum(large, REL_BUCKETS - 1)
    return np.where(n < max_exact, n, large).astype(np.int32)


def _rope_tables(seq):
    half = MLA_ROPE_DIM // 2
    inv_freq = ROPE_THETA ** (-np.arange(half, dtype=np.float64) / half)
    ang = np.arange(seq, dtype=np.float64)[:, None] * inv_freq[None, :]
    zeros = np.zeros((seq, LANES - MLA_ROPE_DIM))
    cos_tab = np.concatenate([np.cos(ang), np.cos(ang), zeros], axis=1)
    sin_tab = np.concatenate([np.sin(ang), np.sin(ang), zeros], axis=1)
    return cos_tab.astype(np.float32), sin_tab.astype(np.float32)


def _rot_cols(w):
    half = w.shape[-1] // 2
    return jnp.concatenate([-w[..., half:], w[..., :half]], axis=-1)


def kernel(x, c, w_mod, b_mod, attn_norm_g, w_in, swa_sinks, rel_bias, mla_q_norm_g, w_uq,
           mla_kv_norm_g, w_ukv, w_out, mlp_norm_g, w_ff1, w_ff2, final_norm_g):
    batch, seq, _ = x.shape
    depth = w_mod.shape[0]
    assert depth == 1
    t = batch * seq
    x2d = x.reshape(t, D_MODEL)
    l = 0

    w_kr = w_in[l][:, OFF_MLA_KR:OFF_MLA_KR + MLA_ROPE_DIM]
    w_main = jnp.concatenate(
        [w_in[l][:, OFF_SWA_K:OFF_SWA_V], w_in[l][:, OFF_MLA_CQ:OFF_MLA_KR],
         w_kr, _rot_cols(w_kr)], axis=1).astype(BF16)
    w_t = jnp.concatenate(
        [w_in[l][:, :SWA_Q_COLS], w_in[l][:, OFF_SWA_V:OFF_MLA_CQ]], axis=1).T.astype(BF16)

    wq = w_uq[l].reshape(MLA_Q_RANK, MLA_HEADS, MLA_QK_DIM)
    wq_nope = wq[..., :MLA_NOPE_DIM]
    wq_rope = wq[..., MLA_NOPE_DIM:]
    hh = MLA_HEADS // 2
    zq = jnp.zeros((MLA_Q_RANK, hh, LANES - MLA_ROPE_DIM), F32)
    wq_main = jnp.concatenate(
        [jnp.concatenate([wq_nope[:, :hh], wq_rope[:, :hh], zq], axis=-1),
         jnp.concatenate([wq_nope[:, hh:], zq, wq_rope[:, hh:]], axis=-1)], axis=1).reshape(
        MLA_Q_RANK, MLA_HEADS * MLA_QK_PAD).astype(BF16)
    wq_rot_all = _rot_cols(wq_rope)
    wq_rot = jnp.concatenate([wq_rot_all[:, :hh], wq_rot_all[:, hh:]], axis=-1).reshape(
        MLA_Q_RANK, hh * LANES).astype(BF16)

    wkv = w_ukv[l].reshape(MLA_KV_RANK, MLA_HEADS, MLA_NOPE_DIM + MLA_V_DIM)
    w_uk = wkv[..., :MLA_NOPE_DIM].reshape(MLA_KV_RANK, MLA_HEADS * MLA_NOPE_DIM).astype(BF16)
    w_vt = wkv[..., MLA_NOPE_DIM:].reshape(MLA_KV_RANK, MLA_HEADS * MLA_V_DIM).T.astype(BF16)

    cos_tab, sin_tab = _rope_tables(seq)

    assert w_mod.shape[2] == (MOD_PRE + MOD_POST) * D_MODEL
    b_mod2d = b_mod[l].reshape(1, -1)
    mod_pre, bias_tab = _prologue_call(c, w_mod[l], b_mod2d, rel_bias, _t5_bucket_table().T)
    mod_pre = mod_pre.reshape(batch, MOD_PRE, D_MODEL)

    qat, ka, vat, qb, kcat, vt, w_out_b16, w1, w2 = _proj_call(
        x2d, mod_pre, attn_norm_g[l].reshape(1, -1), w_main, w_t,
        mla_q_norm_g[l].reshape(1, -1), mla_kv_norm_g[l].reshape(1, -1),
        wq_main, wq_rot, w_uk, w_vt, cos_tab, sin_tab, w_out[l], w_ff1[l], w_ff2[l],
        batch=batch, seq=seq)

    oa, mod_post = _swa_call(swa_sinks[l], qat, ka, vat, bias_tab, c, w_mod[l], b_mod2d,
                             batch=batch, seq=seq)
    mod_post = mod_post.reshape(batch, MOD_POST, D_MODEL)
    ob = _mla_call(qb, kcat, vt, batch=batch, seq=seq)

    x1 = _out_call(x2d, oa, ob, mod_post, w_out_b16, seq=seq)
    out = _mlp_call(x1, mod_post, mlp_norm_g[l].reshape(1, -1), final_norm_g.reshape(1, -1),
                    w1, w2, seq=seq)
    return out.reshape(batch, seq, D_MODEL)
```
